```python
import math
import jax, jax.numpy as jnp
from jax import lax
import numpy as np

D_MODEL = 1024
BATCH = 2
SEQ = 8192
DEPTH = 1

MIX_WIDTH = D_MODEL
ATTN_WIDTH = MIX_WIDTH // 2
SSM_WIDTH = MIX_WIDTH - ATTN_WIDTH
N_ATTN_HEADS = 4
ATTN_HEAD_DIM = ATTN_WIDTH // (2 * N_ATTN_HEADS)
ATTN_VALUE_DIM = 2 * ATTN_HEAD_DIM
Q_BLOCK = 128
SSM_GROUP = 16
N_SSM_GROUPS = SSM_WIDTH // SSM_GROUP
SSM_STATE = 64
DT_MIN = 1e-3
DT_MAX = 1e-1
N_EXPERT_GROUPS = 4
EXPERTS_PER_GROUP = 8
N_EXPERTS = N_EXPERT_GROUPS * EXPERTS_PER_GROUP
TOP_K_INNER = 2
EXPERT_FF = 512
MOE_BLOCK = 128
IN_COLS = 3 * ATTN_WIDTH + SSM_WIDTH
RMS_EPS = 1e-6

kernel_name = "hymba_diffattn_s5_hiermoe_block"


def rmsnorm(x, gain):
    xf = x.astype(jnp.float32)
    y = xf * lax.rsqrt(jnp.mean(xf * xf, axis=-1, keepdims=True) + RMS_EPS) * gain.astype(jnp.float32)
    return y.astype(x.dtype)


def lambda_init_fn(layer_idx):
    return 0.8 - 0.6 * math.exp(-0.3 * layer_idx)


def alibi_slopes(n_heads):
    return jnp.asarray(2.0 ** (-8.0 * np.arange(1, n_heads + 1) / n_heads), dtype=jnp.float32)


def diff_attention(q, k, v, lam, subln_gain, lambda_init):
    b_, s_ = q.shape[:2]
    n_qblocks = s_ // Q_BLOCK
    scale = ATTN_HEAD_DIM ** -0.5
    qb = (q * scale).reshape(b_, n_qblocks, Q_BLOCK, N_ATTN_HEADS, 2, ATTN_HEAD_DIM)
    qb = qb.transpose(1, 0, 3, 4, 2, 5)
    kt = k.transpose(0, 2, 3, 1, 4)
    vt = v.transpose(0, 2, 1, 3)
    slopes = alibi_slopes(N_ATTN_HEADS)
    kpos = jnp.arange(s_)

    def block(args):
        q_blk, i = args
        qpos = i * Q_BLOCK + jnp.arange(Q_BLOCK)
        dist = (qpos[:, None] - kpos[None, :]).astype(jnp.float32)
        bias = jnp.where(dist >= 0, -slopes[:, None, None] * dist, -jnp.inf)
        scores = jnp.einsum('bhmqd,bhmkd->bhmqk', q_blk, kt).astype(jnp.float32)
        scores = scores + bias[None, :, None]
        p = jax.nn.softmax(scores, axis=-1)
        a = p[:, :, 0] - lam * p[:, :, 1]
        return jnp.einsum('bhqk,bhkv->bhqv', a.astype(vt.dtype), vt)

    o = lax.map(block, (qb, jnp.arange(n_qblocks)))
    o = o.transpose(1, 0, 3, 2, 4).reshape(b_, s_, N_ATTN_HEADS, ATTN_VALUE_DIM)
    o = rmsnorm(o, subln_gain) * (1.0 - lambda_init)
    return o.reshape(b_, s_, ATTN_WIDTH)


def s5_branch(u, lam_re, lam_im, log_dt, b_re, b_im, c_re, c_im, d_skip, w_glu, b_glu, ssm_gain):
    b_, s_ = u.shape[:2]
    f32 = jnp.float32
    uf = u.astype(f32).reshape(b_, s_, N_SSM_GROUPS, SSM_GROUP)
    lam = lax.complex(lam_re.astype(f32), lam_im.astype(f32))
    dt = jnp.exp(log_dt.astype(f32))[:, None]
    lam_bar = jnp.exp(lam * dt)
    b_c = lax.complex(b_re.astype(f32), b_im.astype(f32))
    b_bar = ((lam_bar - 1.0) / lam)[..., None] * b_c
    bu = jnp.einsum('gnc,bsgc->bsgn', b_bar, uf.astype(jnp.complex64))
    a = jnp.broadcast_to(lam_bar, bu.shape)

    def combine(left, right):
        a_l, b_l = left
        a_r, b_r = right
        return a_r * a_l, a_r * b_l + b_r

    _, states = lax.associative_scan(combine, (a, bu), axis=1)
    c_c = lax.complex(c_re.astype(f32), c_im.astype(f32))
    y = jnp.einsum('gcn,bsgn->bsgc', c_c, states).real + d_skip.astype(f32) * uf
    y = jax.nn.gelu(y.reshape(b_, s_, SSM_WIDTH))
    y = y * jax.nn.sigmoid(y @ w_glu.astype(f32) + b_glu.astype(f32))
    return rmsnorm(y, ssm_gain).astype(u.dtype)


def hier_moe(h, w_router_group, b_router_group, w_router_expert, b_router_expert, w_gate, w_up, w_down):
    n_tok = h.shape[0]
    hf = h.astype(jnp.float32)
    g_logits = hf @ w_router_group.astype(jnp.float32) + b_router_group.astype(jnp.float32)
    g_prob = jax.nn.softmax(g_logits, axis=-1)
    g_sel = jnp.argmax(g_logits, axis=-1)
    p_group = jnp.take_along_axis(g_prob, g_sel[:, None], axis=1)
    e_logits = (hf @ w_router_expert.astype(jnp.float32) + b_router_expert.astype(jnp.float32))
    e_logits = e_logits.reshape(n_tok, N_EXPERT_GROUPS, EXPERTS_PER_GROUP)
    e_in = jnp.take_along_axis(e_logits, g_sel[:, None, None], axis=1)[:, 0]
    e_prob = jax.nn.softmax(e_in, axis=-1)
    top_p, top_i = lax.top_k(e_prob, TOP_K_INNER)
    gates = p_group * top_p
    expert_id = g_sel[:, None] * EXPERTS_PER_GROUP + top_i

    n_assign = n_tok * TOP_K_INNER
    flat_e = expert_id.reshape(-1)
    flat_tok = jnp.repeat(jnp.arange(n_tok), TOP_K_INNER)
    flat_g = gates.reshape(-1)
    order = jnp.argsort(flat_e)
    se, stok, sg = flat_e[order], flat_tok[order], flat_g[order]
    counts = jnp.bincount(flat_e, length=N_EXPERTS)
    start = jnp.cumsum(counts) - counts
    padded = ((counts + MOE_BLOCK - 1) // MOE_BLOCK) * MOE_BLOCK
    pend = jnp.cumsum(padded)
    pstart = pend - padded
    dest = pstart[se] + (jnp.arange(n_assign) - start[se])
    n_rows = ((n_assign + N_EXPERTS * (MOE_BLOCK - 1) + MOE_BLOCK - 1) // MOE_BLOCK) * MOE_BLOCK
    n_blocks = n_rows // MOE_BLOCK
    rows = jnp.zeros((n_rows, h.shape[1]), h.dtype).at[dest].set(h[stok])
    block_e = jnp.minimum(jnp.searchsorted(pend, jnp.arange(n_blocks) * MOE_BLOCK, side='right'), N_EXPERTS - 1)

    def expert_block(args):
        xb, e = args
        hid = jax.nn.silu(xb @ w_gate[e]) * (xb @ w_up[e])
        return hid @ w_down[e]

    out = lax.map(expert_block, (rows.reshape(n_blocks, MOE_BLOCK, -1), block_e)).reshape(n_rows, -1)
    return jax.ops.segment_sum(out[dest] * sg[:, None].astype(out.dtype), stok, num_segments=n_tok)


def setup_inputs(seed: int = 0) -> dict:
    key = jax.random.key(seed)
    ks = jax.random.split(key, 32)
    f32 = jnp.float32
    L, D = DEPTH, D_MODEL
    nrm = lambda k, shape, s: jax.random.normal(k, shape, f32) * s
    n_idx = jnp.arange(SSM_STATE, dtype=f32)
    return {
        "x": jax.random.normal(ks[0], (BATCH, SEQ, D), f32),
        "norm_attn": 1.0 + nrm(ks[1], (L, D), 0.02),
        "w_in": nrm(ks[2], (L, D, IN_COLS), D ** -0.5),
        "lambda_q1": nrm(ks[3], (L, ATTN_HEAD_DIM), 0.1),
        "lambda_k1": nrm(ks[4], (L, ATTN_HEAD_DIM), 0.1),
        "lambda_q2": nrm(ks[5], (L, ATTN_HEAD_DIM), 0.1),
        "lambda_k2": nrm(ks[6], (L, ATTN_HEAD_DIM), 0.1),
        "attn_subln": 1.0 + nrm(ks[7], (L, ATTN_VALUE_DIM), 0.02),
        "ssm_lam_re": -0.5 + nrm(ks[8], (L, N_SSM_GROUPS, SSM_STATE), 0.01),
        "ssm_lam_im": math.pi * n_idx + nrm(ks[9], (L, N_SSM_GROUPS, SSM_STATE), 0.01),
        "ssm_log_dt": jax.random.uniform(ks[10], (L, N_SSM_GROUPS), f32, math.log(DT_MIN), math.log(DT_MAX)),
        "ssm_b_re": nrm(ks[11], (L, N_SSM_GROUPS, SSM_STATE, SSM_GROUP), (2 * SSM_GROUP) ** -0.5),
        "ssm_b_im": nrm(ks[12], (L, N_SSM_GROUPS, SSM_STATE, SSM_GROUP), (2 * SSM_GROUP) ** -0.5),
        "ssm_c_re": nrm(ks[13], (L, N_SSM_GROUPS, SSM_GROUP, SSM_STATE), (2 * SSM_STATE) ** -0.5),
        "ssm_c_im": nrm(ks[14], (L, N_SSM_GROUPS, SSM_GROUP, SSM_STATE), (2 * SSM_STATE) ** -0.5),
        "ssm_d": nrm(ks[15], (L, N_SSM_GROUPS, SSM_GROUP), 1.0),
        "w_glu": nrm(ks[16], (L, SSM_WIDTH, SSM_WIDTH), SSM_WIDTH ** -0.5),
        "b_glu": nrm(ks[17], (L, SSM_WIDTH), 0.01),
        "ssm_norm": 1.0 + nrm(ks[18], (L, SSM_WIDTH), 0.02),
        "w_out": nrm(ks[19], (L, MIX_WIDTH, D), MIX_WIDTH ** -0.5),
        "norm_moe": 1.0 + nrm(ks[20], (L, D), 0.02),
        "w_router_group": nrm(ks[21], (L, D, N_EXPERT_GROUPS), D ** -0.5),
        "b_router_group": nrm(ks[22], (L, N_EXPERT_GROUPS), 0.01),
        "w_router_expert": nrm(ks[23], (L, D, N_EXPERTS), D ** -0.5),
        "b_router_expert": nrm(ks[24], (L, N_EXPERTS), 0.01),
        "w_gate": nrm(ks[25], (L, N_EXPERTS, D, EXPERT_FF), D ** -0.5),
        "w_up": nrm(ks[26], (L, N_EXPERTS, D, EXPERT_FF), D ** -0.5),
        "w_down": nrm(ks[27], (L, N_EXPERTS, EXPERT_FF, D), EXPERT_FF ** -0.5),
        "norm_final": 1.0 + nrm(ks[28], (D,), 0.02),
    }


def reference(x, norm_attn, w_in, lambda_q1, lambda_k1, lambda_q2, lambda_k2, attn_subln,
              ssm_lam_re, ssm_lam_im, ssm_log_dt, ssm_b_re, ssm_b_im, ssm_c_re, ssm_c_im, ssm_d,
              w_glu, b_glu, ssm_norm, w_out, norm_moe, w_router_group, b_router_group,
              w_router_expert, b_router_expert, w_gate, w_up, w_down, norm_final):
    b_, s_, d_ = x.shape
    for l in range(DEPTH):
        h = rmsnorm(x, norm_attn[l])
        proj = h @ w_in[l]
        q = proj[..., :ATTN_WIDTH].reshape(b_, s_, N_ATTN_HEADS, 2, ATTN_HEAD_DIM)
        k = proj[..., ATTN_WIDTH:2 * ATTN_WIDTH].reshape(b_, s_, N_ATTN_HEADS, 2, ATTN_HEAD_DIM)
        v = proj[..., 2 * ATTN_WIDTH:3 * ATTN_WIDTH].reshape(b_, s_, N_ATTN_HEADS, ATTN_VALUE_DIM)
        u = proj[..., 3 * ATTN_WIDTH:]
        lam_init = lambda_init_fn(l)
        lam = (jnp.exp(jnp.sum(lambda_q1[l].astype(jnp.float32) * lambda_k1[l].astype(jnp.float32)))
               - jnp.exp(jnp.sum(lambda_q2[l].astype(jnp.float32) * lambda_k2[l].astype(jnp.float32)))
               + lam_init)
        attn_out = diff_attention(q, k, v, lam, attn_subln[l], lam_init)
        ssm_out = s5_branch(u, ssm_lam_re[l], ssm_lam_im[l], ssm_log_dt[l], ssm_b_re[l], ssm_b_im[l],
                            ssm_c_re[l], ssm_c_im[l], ssm_d[l], w_glu[l], b_glu[l], ssm_norm[l])
        x = x + jnp.concatenate([attn_out, ssm_out], axis=-1) @ w_out[l]
        h = rmsnorm(x, norm_moe[l]).reshape(b_ * s_, d_)
        x = x + hier_moe(h, w_router_group[l], b_router_group[l], w_router_expert[l], b_router_expert[l],
                         w_gate[l], w_up[l], w_down[l]).reshape(b_, s_, d_)
    return rmsnorm(x, norm_final)
```

```python
import functools
import math

import jax
import jax.numpy as jnp
import numpy as np
from jax import lax
from jax.experimental import pallas as pl
from jax.experimental.pallas import tpu as pltpu

F32 = jnp.float32
BF16 = jnp.bfloat16

D_MODEL = 1024
N_HEADS = 4
HEAD_DIM = 64
VALUE_DIM = 128
ATTN_WIDTH = 512
SSM_WIDTH = 512
SSM_GROUP = 16
N_GROUPS = 32
SSM_STATE = 64
N_EXPERT_GROUPS = 4
EXPERTS_PER_GROUP = 8
N_EXPERTS = 32
EXPERT_FF = 512
RMS_EPS = 1e-6
LAMBDA_INIT = 0.8 - 0.6 * math.exp(-0.3 * 0)
LOG2E = math.log2(math.e)

CHUNK = 16
HALF_GROUPS = 16
LANES = 128
AUG = 128
IN_COLS_AUG = 2 * N_HEADS * 2 * AUG + ATTN_WIDTH + SSM_WIDTH
ROW_BLOCK = 256
VMEM_LIMIT_CAP = 56 * 1024 * 1024


def _params(dims, vmem_mb):
    return pltpu.CompilerParams(dimension_semantics=dims,
                                vmem_limit_bytes=min(vmem_mb * 1024 * 1024, VMEM_LIMIT_CAP))


def _rms(x, gain):
    return x * lax.rsqrt(jnp.mean(x * x, axis=-1, keepdims=True) + RMS_EPS) * gain


def _split3(val):
    hi = val.astype(BF16).astype(F32)
    r1 = val - hi
    mid = r1.astype(BF16).astype(F32)
    lo = r1 - mid
    return hi, mid, lo


def _inproj_kernel(x_ref, g_ref, w_ref, q_ref, k_ref, v_ref, u_ref, ubuf, *, tile, seq):
    i = pl.program_id(0)
    h = _rms(x_ref[...], g_ref[...]).astype(BF16)
    proj = jnp.dot(h, w_ref[...], preferred_element_type=F32)
    pos0 = lax.rem(i * tile, seq)
    pos = (pos0 + lax.broadcasted_iota(jnp.int32, (tile, AUG), 0)).astype(F32)
    lane = lax.broadcasted_iota(jnp.int32, (tile, AUG), 1)
    qscale = HEAD_DIM ** -0.5 * LOG2E
    kbase = 2 * N_HEADS * AUG
    for hd in range(N_HEADS):
        slope = 2.0 ** (-8.0 * (hd + 1) / N_HEADS) * LOG2E
        hi, mid, lo = _split3(pos * slope)
        q_add = jnp.where(lane < 64, 0.0,
                          jnp.where(lane < 67, 1.0,
                                    jnp.where(lane == 67, -hi, jnp.where(lane == 68, -mid, jnp.where(lane == 69, -lo, 0.0)))))
        k_add = jnp.where(lane == 64, hi,
                          jnp.where(lane == 65, mid,
                                    jnp.where(lane == 66, lo, jnp.where((lane >= 67) & (lane < 70), 1.0, 0.0))))
        for m in range(2):
            c0 = (hd * 2 + m) * AUG
            q_ref[:, c0:c0 + AUG] = (proj[:, c0:c0 + AUG] * qscale + q_add).astype(BF16)
            k_ref[:, c0:c0 + AUG] = (proj[:, kbase + c0:kbase + c0 + AUG] + k_add).astype(BF16)
    vbase = 2 * kbase
    v_ref[...] = proj[:, vbase:vbase + ATTN_WIDTH].astype(BF16)
    ubase = vbase + ATTN_WIDTH
    for cb in range(SSM_WIDTH // LANES):
        ubuf[cb] = proj[:, ubase + cb * LANES:ubase + (cb + 1) * LANES]
    for s in range(CHUNK):
        for cb in range(SSM_WIDTH // LANES):
            u_ref[s, :, cb * LANES:(cb + 1) * LANES] = (
                ubuf[cb, pl.ds(s, tile // CHUNK, stride=CHUNK), :].astype(BF16))


def _in_proj(x2, gain, w_aug, seq, tile=512):
    t = x2.shape[0]
    kern = functools.partial(_inproj_kernel, tile=tile, seq=seq)
    return pl.pallas_call(
        kern, name="in_proj",
        grid=(t // tile,),
        in_specs=[pl.BlockSpec((tile, D_MODEL), lambda i: (i, 0)),
                  pl.BlockSpec((1, D_MODEL), lambda i: (0, 0)),
                  pl.BlockSpec((D_MODEL, IN_COLS_AUG), lambda i: (0, 0))],
        out_specs=[pl.BlockSpec((tile, 2 * N_HEADS * AUG), lambda i: (i, 0)),
                   pl.BlockSpec((tile, 2 * N_HEADS * AUG), lambda i: (i, 0)),
                   pl.BlockSpec((tile, ATTN_WIDTH), lambda i: (i, 0)),
                   pl.BlockSpec((CHUNK, tile // CHUNK, SSM_WIDTH), lambda i: (0, i, 0))],
        out_shape=[jax.ShapeDtypeStruct((t, 2 * N_HEADS * AUG), BF16),
                   jax.ShapeDtypeStruct((t, 2 * N_HEADS * AUG), BF16),
                   jax.ShapeDtypeStruct((t, ATTN_WIDTH), BF16),
                   jax.ShapeDtypeStruct((CHUNK, t // CHUNK, SSM_WIDTH), BF16)],
        scratch_shapes=[pltpu.VMEM((SSM_WIDTH // LANES, tile, LANES), F32)],
        compiler_params=_params(("arbitrary",), 48),
    )(x2, gain, w_aug)


def _attn_kernel(lq1, lk1, lq2, lk2, sub_ref, q_ref, k_ref, v_ref, o_ref, m_sc, l_sc, acc_sc, *, tq):
    qi = pl.program_id(2)
    lam = (jnp.exp(jnp.sum(lq1[...] * lk1[...], axis=-1, keepdims=True))
           - jnp.exp(jnp.sum(lq2[...] * lk2[...], axis=-1, keepdims=True)) + LAMBDA_INIT)
    m_sc[...] = jnp.full(m_sc.shape, -jnp.inf, F32)
    l_sc[...] = jnp.zeros(l_sc.shape, F32)
    acc_sc[...] = jnp.zeros(acc_sc.shape, F32)

    def step(j, masked):
        r0 = pl.multiple_of(j * tq, tq)
        kblk = k_ref[pl.ds(r0, tq), :]
        vblk = v_ref[pl.ds(r0, tq), :]
        for m in range(2):
            s = lax.dot_general(q_ref[:, m * AUG:(m + 1) * AUG], kblk[:, m * AUG:(m + 1) * AUG],
                                (((1,), (1,)), ((), ())), preferred_element_type=F32)
            if masked:
                row = lax.broadcasted_iota(jnp.int32, (tq, tq), 0)
                col = lax.broadcasted_iota(jnp.int32, (tq, tq), 1)
                s = jnp.where(col <= row, s, -jnp.inf)
            m_prev = m_sc[m]
            m_new = jnp.maximum(m_prev, jnp.max(s, axis=-1, keepdims=True))
            p = jnp.exp2(s - m_new)
            alpha = jnp.exp2(m_prev - m_new)
            l_sc[m] = alpha * l_sc[m] + jnp.sum(p, axis=-1, keepdims=True)
            acc_sc[m] = alpha * acc_sc[m] + jnp.dot(p.astype(BF16), vblk, preferred_element_type=F32)
            m_sc[m] = m_new

    def body(j, carry):
        step(j, False)
        return carry

    lax.fori_loop(0, qi, body, 0)
    step(qi, True)
    o = acc_sc[0] / l_sc[0] - lam * (acc_sc[1] / l_sc[1])
    o_ref[...] = (_rms(o, sub_ref[...]) * (1.0 - LAMBDA_INIT)).astype(BF16)


def _attention(q_aug, k_aug, v, lq1, lk1, lq2, lk2, subln, batch, seq, tq=512):
    tq = min(tq, seq)
    nq = seq // tq
    t = batch * seq
    small = pl.BlockSpec((1, HEAD_DIM), lambda b, h, i: (0, 0))
    kern = functools.partial(_attn_kernel, tq=tq)
    return pl.pallas_call(
        kern, name="attention",
        grid=(batch, N_HEADS, nq),
        in_specs=[small, small, small, small,
                  pl.BlockSpec((1, VALUE_DIM), lambda b, h, i: (0, 0)),
                  pl.BlockSpec((tq, 2 * AUG), lambda b, h, i: (b * nq + i, h)),
                  pl.BlockSpec((seq, 2 * AUG), lambda b, h, i: (b, h)),
                  pl.BlockSpec((seq, VALUE_DIM), lambda b, h, i: (b, h))],
        out_specs=pl.BlockSpec((tq, VALUE_DIM), lambda b, h, i: (b * nq + i, h)),
        out_shape=jax.ShapeDtypeStruct((t, ATTN_WIDTH), BF16),
        scratch_shapes=[pltpu.VMEM((2, tq, 1), F32), pltpu.VMEM((2, tq, 1), F32),
                        pltpu.VMEM((2, tq, VALUE_DIM), F32)],
        compiler_params=_params(("arbitrary", "arbitrary", "arbitrary"), 48),
    )(lq1, lk1, lq2, lk2, subln, q_aug, k_aug, v)


def _ssm_state_kernel(u_ref, bre_ref, bim_ref, are_ref, aim_ref, sre_ref, sim_ref, wre, wim):
    i = pl.program_id(1)

    @pl.when(i == 0)
    def _():
        wre[...] = bre_ref[0]
        wim[...] = bim_ref[0]
        sre_ref[...] = jnp.zeros(sre_ref.shape, F32)
        sim_ref[...] = jnp.zeros(sim_ref.shape, F32)

    @pl.when(i > 0)
    def _():
        ar, ai = are_ref[0], aim_ref[0]
        wr, wi = wre[...], wim[...]
        wre[...] = wr * ar - wi * ai
        wim[...] = wr * ai + wi * ar

    u = u_ref[0]
    sre_ref[...] += jnp.dot(u, wre[...].astype(BF16), preferred_element_type=F32)
    sim_ref[...] += jnp.dot(u, wim[...].astype(BF16), preferred_element_type=F32)


def _ssm_state(u3, b_re, b_im, a_row_re, a_row_im):
    nch = u3.shape[1]
    hw = HALF_GROUPS * SSM_GROUP
    sw = HALF_GROUPS * SSM_STATE
    return pl.pallas_call(
        _ssm_state_kernel, name="ssm_state",
        grid=(2, CHUNK),
        in_specs=[pl.BlockSpec((1, nch, hw), lambda hf, i: (CHUNK - 1 - i, 0, hf)),
                  pl.BlockSpec((1, hw, sw), lambda hf, i: (hf, 0, 0)),
                  pl.BlockSpec((1, hw, sw), lambda hf, i: (hf, 0, 0)),
                  pl.BlockSpec((1, 1, sw), lambda hf, i: (hf, 0, 0)),
                  pl.BlockSpec((1, 1, sw), lambda hf, i: (hf, 0, 0))],
        out_specs=[pl.BlockSpec((nch, sw), lambda hf, i: (0, hf)),
                   pl.BlockSpec((nch, sw), lambda hf, i: (0, hf))],
        out_shape=[jax.ShapeDtypeStruct((nch, 2 * sw), F32)] * 2,
        scratch_shapes=[pltpu.VMEM((hw, sw), F32), pltpu.VMEM((hw, sw), F32)],
        compiler_params=_params(("arbitrary", "arbitrary"), 48),
    )(u3, b_re, b_im, a_row_re, a_row_im)


def _ssm_scan_kernel(sre_ref, sim_ref, pre_ref, pim_ref, hre_ref, him_ref, *, nchunk, nsteps):
    hr, hi = sre_ref[...], sim_ref[...]
    row = lax.rem(lax.broadcasted_iota(jnp.int32, hr.shape, 0), nchunk)
    for k in range(nsteps):
        d = 1 << k
        ar, ai = pre_ref[0, k:k + 1, :], pim_ref[0, k:k + 1, :]
        keep = row >= d
        pr = jnp.where(keep, pltpu.roll(hr, d, 0), 0.0)
        pi = jnp.where(keep, pltpu.roll(hi, d, 0), 0.0)
        hr, hi = hr + ar * pr - ai * pi, hi + ar * pi + ai * pr
    keep = row >= 1
    hre_ref[...] = jnp.where(keep, pltpu.roll(hr, 1, 0), 0.0).astype(BF16)
    him_ref[...] = jnp.where(keep, pltpu.roll(hi, 1, 0), 0.0).astype(BF16)


def _ssm_scan(s_re, s_im, p_re, p_im, nchunk, cols=512):
    nrow, width = s_re.shape
    nsteps = p_re.shape[1]
    per_half = (width // 2) // cols
    kern = functools.partial(_ssm_scan_kernel, nchunk=nchunk, nsteps=nsteps)
    blk = pl.BlockSpec((nrow, cols), lambda j: (0, j))
    pblk = pl.BlockSpec((1, nsteps, cols), lambda j: (j // per_half, 0, j % per_half))
    return pl.pallas_call(
        kern, name="ssm_scan",
        grid=(width // cols,),
        in_specs=[blk, blk, pblk, pblk],
        out_specs=[blk, blk],
        out_shape=[jax.ShapeDtypeStruct((nrow, width), BF16)] * 2,
        compiler_params=_params(("arbitrary",), 48),
    )(s_re, s_im, p_re, p_im)


def _ssm_out_kernel(u_ref, kt_ref, hre_ref, him_ref, cre_ref, cim_ref, are_ref, aim_ref, y_ref, wre, wim):
    t = pl.program_id(1)
    ar, ai = are_ref[0], aim_ref[0]

    @pl.when(t == 0)
    def _():
        cr, ci = cre_ref[0], cim_ref[0]
        wre[...] = cr * ar - ci * ai
        wim[...] = cr * ai + ci * ar

    @pl.when(t > 0)
    def _():
        wr, wi = wre[...], wim[...]
        wre[...] = wr * ar - wi * ai
        wim[...] = wr * ai + wi * ar

    y_ref[0] = (jnp.dot(hre_ref[...], wre[...].astype(BF16), preferred_element_type=F32)
                - jnp.dot(him_ref[...], wim[...].astype(BF16), preferred_element_type=F32))

    def body(s, carry):
        y_ref[0] += jnp.dot(u_ref[s], kt_ref[t - s, 0], preferred_element_type=F32)
        return carry

    lax.fori_loop(0, t + 1, body, 0)


def _ssm_out(u3, kt, h_re, h_im, c_re, c_im, a_col_re, a_col_im):
    nch = u3.shape[1]
    hw = HALF_GROUPS * SSM_GROUP
    sw = HALF_GROUPS * SSM_STATE
    return pl.pallas_call(
        _ssm_out_kernel, name="ssm_out",
        grid=(2, CHUNK),
        in_specs=[pl.BlockSpec((CHUNK, nch, hw), lambda hf, t: (0, 0, hf)),
                  pl.BlockSpec((CHUNK, 1, hw, hw), lambda hf, t: (0, hf, 0, 0)),
                  pl.BlockSpec((nch, sw), lambda hf, t: (0, hf)),
                  pl.BlockSpec((nch, sw), lambda hf, t: (0, hf)),
                  pl.BlockSpec((1, sw, hw), lambda hf, t: (hf, 0, 0)),
                  pl.BlockSpec((1, sw, hw), lambda hf, t: (hf, 0, 0)),
                  pl.BlockSpec((1, sw, 1), lambda hf, t: (hf, 0, 0)),
                  pl.BlockSpec((1, sw, 1), lambda hf, t: (hf, 0, 0))],
        out_specs=pl.BlockSpec((1, nch, hw), lambda hf, t: (t, 0, hf)),
        out_shape=jax.ShapeDtypeStruct((CHUNK, nch, SSM_WIDTH), F32),
        scratch_shapes=[pltpu.VMEM((sw, hw), F32), pltpu.VMEM((sw, hw), F32)],
        compiler_params=_params(("arbitrary", "arbitrary"), 48),
    )(u3, kt, h_re, h_im, c_re, c_im, a_col_re, a_col_im)


def _ssm_constants(lam_re, lam_im, log_dt, b_re, b_im, c_re, c_im, d_skip, nchunk):
    lam = lax.complex(lam_re.astype(F32), lam_im.astype(F32))
    dt = jnp.exp(log_dt.astype(F32))[:, None]
    a = jnp.exp(lam * dt)
    bbar = ((a - 1.0) / lam)[..., None] * lax.complex(b_re.astype(F32), b_im.astype(F32))
    c_c = lax.complex(c_re.astype(F32), c_im.astype(F32))
    lags = jnp.arange(CHUNK, dtype=F32)[:, None, None]
    apow = jnp.exp((lam * dt)[None] * lags)
    kt = jnp.real(jnp.einsum('gcn,jgn,gnd->jgdc', c_c, apow, bbar))
    kt = kt.at[0].add(jnp.einsum('gc,dc->gdc', d_skip.astype(F32), jnp.eye(SSM_GROUP, dtype=F32)))
    eye = jnp.eye(HALF_GROUPS, dtype=F32)
    hw, sw = HALF_GROUPS * SSM_GROUP, HALF_GROUPS * SSM_STATE
    kt_t = jnp.einsum('jhgdc,gk->jhgdkc', kt.reshape(CHUNK, 2, HALF_GROUPS, SSM_GROUP, SSM_GROUP), eye)
    kt_t = kt_t.reshape(CHUNK, 2, hw, hw).astype(BF16)

    def b_tiles(part):
        p = part.reshape(2, HALF_GROUPS, SSM_STATE, SSM_GROUP)
        return jnp.einsum('hgnd,gk->hgdkn', p, eye).reshape(2, hw, sw)

    def c_tiles(part):
        p = part.reshape(2, HALF_GROUPS, SSM_GROUP, SSM_STATE)
        return jnp.einsum('hgcn,gk->hgnkc', p, eye).reshape(2, sw, hw)

    nsteps = max(int(math.log2(nchunk)), 1)
    steps = (CHUNK * 2.0 ** jnp.arange(nsteps, dtype=F32))[:, None, None]
    pw = jnp.exp((lam * dt)[None] * steps).reshape(nsteps, 2, sw).transpose(1, 0, 2)
    a_h = a.reshape(2, sw)
    return dict(kt=kt_t, b_re=b_tiles(jnp.real(bbar)), b_im=b_tiles(jnp.imag(bbar)),
                c_re=c_tiles(jnp.real(c_c)), c_im=c_tiles(jnp.imag(c_c)),
                a_row_re=jnp.real(a_h)[:, None, :], a_row_im=jnp.imag(a_h)[:, None, :],
                a_col_re=jnp.real(a_h)[:, :, None], a_col_im=jnp.imag(a_h)[:, :, None],
                p_re=jnp.real(pw), p_im=jnp.imag(pw))


def _mix_kernel(x_ref, attn_ref, y3_ref, wglu_ref, bglu_ref, gssm_ref, woa_ref, wos_ref, gmoe_ref,
                wr_ref, br_ref, x1_ref, h2_ref, info_ref, cnt_ref, ybuf, carry, *, tile):
    i = pl.program_id(0)

    @pl.when(i == 0)
    def _():
        carry[...] = jnp.zeros(carry.shape, F32)

    for s in range(CHUNK):
        for cb in range(SSM_WIDTH // LANES):
            ybuf[cb, pl.ds(s, tile // CHUNK, stride=CHUNK), :] = y3_ref[s, :, cb * LANES:(cb + 1) * LANES]
    y = jax.nn.gelu(jnp.concatenate([ybuf[cb] for cb in range(SSM_WIDTH // LANES)], axis=-1))
    z = jnp.dot(y.astype(BF16), wglu_ref[...], preferred_element_type=F32) + bglu_ref[...]
    y = y * jax.nn.sigmoid(z)
    ssm = _rms(y, gssm_ref[...])
    x1 = (x_ref[...] + jnp.dot(attn_ref[...], woa_ref[...], preferred_element_type=F32)
          + jnp.dot(ssm.astype(BF16), wos_ref[...], preferred_element_type=F32))
    x1_ref[...] = x1
    h2 = _rms(x1, gmoe_ref[...])
    h2_ref[...] = h2

    logits = jnp.dot(h2, wr_ref[...], preferred_element_type=F32, precision=lax.Precision.HIGHEST) + br_ref[...]
    lane = lax.broadcasted_iota(jnp.int32, logits.shape, 1)
    neg = -jnp.inf
    gl = jnp.where(lane < N_EXPERT_GROUPS, logits, neg)
    gmax = jnp.max(gl, axis=-1, keepdims=True)
    gsel = jnp.min(jnp.where(gl == gmax, lane, LANES), axis=-1, keepdims=True)
    p_group = 1.0 / jnp.sum(jnp.exp(gl - gmax), axis=-1, keepdims=True)
    elane = lane - N_EXPERT_GROUPS
    in_grp = (elane >= 0) & (elane < N_EXPERTS) & ((elane >> 3) == gsel)
    el = jnp.where(in_grp, logits, neg)
    m1 = jnp.max(el, axis=-1, keepdims=True)
    i1 = jnp.min(jnp.where(el == m1, lane, LANES), axis=-1, keepdims=True)
    den = jnp.sum(jnp.exp(el - m1), axis=-1, keepdims=True)
    el2 = jnp.where(lane == i1, neg, el)
    m2 = jnp.max(el2, axis=-1, keepdims=True)
    i2 = jnp.min(jnp.where(el2 == m2, lane, LANES), axis=-1, keepdims=True)
    g0 = p_group / den
    g1 = p_group * jnp.exp(m2 - m1) / den
    e0 = i1 - N_EXPERT_GROUPS
    e1 = i2 - N_EXPERT_GROUPS

    hit0 = lane == e0
    hit1 = lane == e1
    onehot = jnp.where(hit0 | hit1, 1.0, 0.0)
    r = lax.broadcasted_iota(jnp.int32, (tile, tile), 0)
    c = lax.broadcasted_iota(jnp.int32, (tile, tile), 1)
    tril = jnp.where(c < r, 1.0, 0.0).astype(BF16)
    before = jnp.dot(tril, onehot.astype(BF16), preferred_element_type=F32) + carry[...]
    rank0 = jnp.sum(jnp.where(hit0, before, 0.0), axis=-1, keepdims=True)
    rank1 = jnp.sum(jnp.where(hit1, before, 0.0), axis=-1, keepdims=True)
    carry[...] += jnp.sum(onehot, axis=0, keepdims=True)
    cnt_ref[...] = carry[...]
    info = jnp.where(lane == 0, e0.astype(F32),
                     jnp.where(lane == 1, e1.astype(F32),
                               jnp.where(lane == 2, g0,
                                         jnp.where(lane == 3, g1,
                                                   jnp.where(lane == 4, rank0, jnp.where(lane == 5, rank1, 0.0))))))
    info_ref[...] = info


def _mix(x2, attn, y3, wglu, bglu, gssm, wo_a, wo_s, gmoe, w_router, b_router, tile=512):
    t = x2.shape[0]
    kern = functools.partial(_mix_kernel, tile=tile)
    full = lambda shape: pl.BlockSpec(shape, lambda i: tuple(0 for _ in shape))
    return pl.pallas_call(
        kern, name="mix",
        grid=(t // tile,),
        in_specs=[pl.BlockSpec((tile, D_MODEL), lambda i: (i, 0)),
                  pl.BlockSpec((tile, ATTN_WIDTH), lambda i: (i, 0)),
                  pl.BlockSpec((CHUNK, tile // CHUNK, SSM_WIDTH), lambda i: (0, i, 0)),
                  full((SSM_WIDTH, SSM_WIDTH)), full((1, SSM_WIDTH)), full((1, SSM_WIDTH)),
                  full((ATTN_WIDTH, D_MODEL)), full((SSM_WIDTH, D_MODEL)), full((1, D_MODEL)),
                  full((D_MODEL, LANES)), full((1, LANES))],
        out_specs=[pl.BlockSpec((tile, D_MODEL), lambda i: (i, 0)),
                   pl.BlockSpec((tile, D_MODEL), lambda i: (i, 0)),
                   pl.BlockSpec((tile, LANES), lambda i: (i, 0)),
                   pl.BlockSpec((1, LANES), lambda i: (0, 0))],
        out_shape=[jax.ShapeDtypeStruct((t, D_MODEL), F32),
                   jax.ShapeDtypeStruct((t, D_MODEL), F32),
                   jax.ShapeDtypeStruct((t, LANES), F32),
                   jax.ShapeDtypeStruct((1, LANES), F32)],
        scratch_shapes=[pltpu.VMEM((SSM_WIDTH // LANES, tile, LANES), F32), pltpu.VMEM((1, LANES), F32)],
        compiler_params=_params(("arbitrary",), 48),
    )(x2, attn, y3, wglu, bglu, gssm, wo_a, wo_s, gmoe, w_router, b_router)


def _row_copy(src, s, dst, d, sem):
    return pltpu.make_async_copy(src.at[pl.ds(s, 1), :], dst.at[pl.ds(d, 1), :], sem)


def _dispatch_kernel(dest_ref, h_ref, rows_in, rows_out, sem, *, tile):
    del rows_in
    base = pl.program_id(0) * tile

    def issue(t, carry):
        for j in range(2):
            _row_copy(h_ref, t, rows_out, dest_ref[(base + t) * 2 + j], sem).start()
        return carry

    def drain(t, carry):
        for j in range(2):
            _row_copy(h_ref, 0, rows_out, 0, sem).wait()
        return carry

    lax.fori_loop(0, tile, issue, 0)
    lax.fori_loop(0, tile, drain, 0)


def _dispatch(dest_flat, h2, rows_zero, tile=256):
    t = h2.shape[0]
    kern = functools.partial(_dispatch_kernel, tile=tile)
    grid_spec = pltpu.PrefetchScalarGridSpec(
        num_scalar_prefetch=1, grid=(t // tile,),
        in_specs=[pl.BlockSpec((tile, D_MODEL), lambda i, d: (i, 0)),
                  pl.BlockSpec(memory_space=pl.ANY)],
        out_specs=pl.BlockSpec(memory_space=pl.ANY),
        scratch_shapes=[pltpu.SemaphoreType.DMA(())])
    return pl.pallas_call(
        kern, name="dispatch", grid_spec=grid_spec,
        out_shape=jax.ShapeDtypeStruct(rows_zero.shape, rows_zero.dtype),
        input_output_aliases={2: 0},
        compiler_params=_params(("arbitrary",), 32),
    )(dest_flat, h2, rows_zero)


def _expert_kernel(be_ref, nu_ref, rows_ref, wg_ref, wu_ref, wd_ref, out_ref, wg_b, wu_b, wd_b):
    i = pl.program_id(0)
    changed = (i == 0) | (be_ref[i] != be_ref[jnp.maximum(i - 1, 0)])

    @pl.when(changed)
    def _():
        wg_b[...] = wg_ref[0].astype(BF16)
        wu_b[...] = wu_ref[0].astype(BF16)
        wd_b[...] = wd_ref[0].astype(BF16)

    @pl.when(i < nu_ref[0])
    def _():
        xb = rows_ref[...].astype(BF16)
        gate = jnp.dot(xb, wg_b[...], preferred_element_type=F32)
        up = jnp.dot(xb, wu_b[...], preferred_element_type=F32)
        hid = (gate * jax.nn.sigmoid(gate) * up).astype(BF16)
        out_ref[...] = jnp.dot(hid, wd_b[...], preferred_element_type=F32)

    @pl.when(i >= nu_ref[0])
    def _():
        out_ref[...] = jnp.zeros(out_ref.shape, F32)


def _experts(block_e, n_used, rows, w_gate, w_up, w_down):
    n_rows = rows.shape[0]
    nb = n_rows // ROW_BLOCK
    last = lambda i, nu: jnp.minimum(i, nu[0] - 1)
    grid_spec = pltpu.PrefetchScalarGridSpec(
        num_scalar_prefetch=2, grid=(nb,),
        in_specs=[pl.BlockSpec((ROW_BLOCK, D_MODEL), lambda i, be, nu: (last(i, nu), 0)),
                  pl.BlockSpec((1, D_MODEL, EXPERT_FF), lambda i, be, nu: (be[i], 0, 0)),
                  pl.BlockSpec((1, D_MODEL, EXPERT_FF), lambda i, be, nu: (be[i], 0, 0)),
                  pl.BlockSpec((1, EXPERT_FF, D_MODEL), lambda i, be, nu: (be[i], 0, 0))],
        out_specs=pl.BlockSpec((ROW_BLOCK, D_MODEL), lambda i, be, nu: (i, 0)),
        scratch_shapes=[pltpu.VMEM((D_MODEL, EXPERT_FF), BF16), pltpu.VMEM((D_MODEL, EXPERT_FF), BF16),
                        pltpu.VMEM((EXPERT_FF, D_MODEL), BF16)])
    return pl.pallas_call(
        _expert_kernel, name="experts", grid_spec=grid_spec,
        out_shape=jax.ShapeDtypeStruct((n_rows, D_MODEL), F32),
        compiler_params=_params(("arbitrary",), 48),
    )(block_e, n_used, rows, w_gate, w_up, w_down)


def _combine_kernel(dest_ref, x1_ref, info_ref, gfin_ref, rows_ref, o_ref, buf0, buf1, sem, *, tile):
    base = pl.program_id(0) * tile

    def issue(t, carry):
        _row_copy(rows_ref, dest_ref[(base + t) * 2], buf0, t, sem).start()
        _row_copy(rows_ref, dest_ref[(base + t) * 2 + 1], buf1, t, sem).start()
        return carry

    def drain(t, carry):
        _row_copy(rows_ref, 0, buf0, 0, sem).wait()
        _row_copy(rows_ref, 0, buf1, 0, sem).wait()
        return carry

    lax.fori_loop(0, tile, issue, 0)
    lax.fori_loop(0, tile, drain, 0)
    info = info_ref[...]
    x2 = x1_ref[...] + info[:, 2:3] * buf0[...] + info[:, 3:4] * buf1[...]
    o_ref[...] = _rms(x2, gfin_ref[...])


def _combine(dest_flat, x1, info, gfin, out_rows, tile=256):
    t = x1.shape[0]
    kern = functools.partial(_combine_kernel, tile=tile)
    grid_spec = pltpu.PrefetchScalarGridSpec(
        num_scalar_prefetch=1, grid=(t // tile,),
        in_specs=[pl.BlockSpec((tile, D_MODEL), lambda i, d: (i, 0)),
                  pl.BlockSpec((tile, LANES), lambda i, d: (i, 0)),
                  pl.BlockSpec((1, D_MODEL), lambda i, d: (0, 0)),
                  pl.BlockSpec(memory_space=pl.ANY)],
        out_specs=pl.BlockSpec((tile, D_MODEL), lambda i, d: (i, 0)),
        scratch_shapes=[pltpu.VMEM((tile, D_MODEL), F32), pltpu.VMEM((tile, D_MODEL), F32),
                        pltpu.SemaphoreType.DMA(())])
    return pl.pallas_call(
        kern, name="combine", grid_spec=grid_spec,
        out_shape=jax.ShapeDtypeStruct((t, D_MODEL), F32),
        compiler_params=_params(("arbitrary",), 32),
    )(dest_flat, x1, info, gfin, out_rows)


def _augment_w_in(w_in):
    qk = w_in[:, :2 * ATTN_WIDTH].reshape(D_MODEL, 2 * N_HEADS * 2, HEAD_DIM)
    qk = jnp.pad(qk, ((0, 0), (0, 0), (0, AUG - HEAD_DIM))).reshape(D_MODEL, 2 * N_HEADS * 2 * AUG)
    return jnp.concatenate([qk, w_in[:, 2 * ATTN_WIDTH:]], axis=1).astype(BF16)


def kernel(x, norm_attn, w_in, lambda_q1, lambda_k1, lambda_q2, lambda_k2, attn_subln, ssm_lam_re, ssm_lam_im, ssm_log_dt, ssm_b_re, ssm_b_im, ssm_c_re, ssm_c_im, ssm_d, w_glu, b_glu, ssm_norm, w_out, norm_moe, w_router_group, b_router_group, w_router_expert, b_router_expert, w_gate, w_up, w_down, norm_final):
    batch, seq, d = x.shape
    t = batch * seq
    nchunk = seq // CHUNK
    x2 = x.reshape(t, d)
    l = 0

    q_aug, k_aug, v, u3 = _in_proj(x2, norm_attn[l][None], _augment_w_in(w_in[l]), seq)
    attn = _attention(q_aug, k_aug, v, lambda_q1[l][None], lambda_k1[l][None], lambda_q2[l][None],
                      lambda_k2[l][None], attn_subln[l][None], batch, seq)

    sc = _ssm_constants(ssm_lam_re[l], ssm_lam_im[l], ssm_log_dt[l], ssm_b_re[l], ssm_b_im[l],
                        ssm_c_re[l], ssm_c_im[l], ssm_d[l], nchunk)
    s_re, s_im = _ssm_state(u3, sc["b_re"], sc["b_im"], sc["a_row_re"], sc["a_row_im"])
    h_re, h_im = _ssm_scan(s_re, s_im, sc["p_re"], sc["p_im"], nchunk)
    y3 = _ssm_out(u3, sc["kt"], h_re, h_im, sc["c_re"], sc["c_im"], sc["a_col_re"], sc["a_col_im"])

    w_router = jnp.concatenate([w_router_group[l], w_router_expert[l]], axis=1).astype(F32)
    w_router = jnp.pad(w_router, ((0, 0), (0, LANES - w_router.shape[1])))
    b_router = jnp.concatenate([b_router_group[l], b_router_expert[l]]).astype(F32)
    b_router = jnp.pad(b_router, (0, LANES - b_router.shape[0]))[None]
    x1, h2, info, cnt = _mix(x2, attn, y3, w_glu[l].astype(BF16), b_glu[l][None], ssm_norm[l][None],
                             w_out[l][:ATTN_WIDTH].astype(BF16), w_out[l][ATTN_WIDTH:].astype(BF16),
                             norm_moe[l][None], w_router, b_router)

    experts = info[:, 0:2].astype(jnp.int32)
    ranks = info[:, 4:6].astype(jnp.int32)
    counts = cnt[0, :N_EXPERTS].astype(jnp.int32)
    padded = ((counts + ROW_BLOCK - 1) // ROW_BLOCK) * ROW_BLOCK
    pend = jnp.cumsum(padded)
    pstart = pend - padded
    dest = (pstart[experts] + ranks).reshape(-1)
    n_rows = ((2 * t + N_EXPERTS * (ROW_BLOCK - 1) + ROW_BLOCK - 1) // ROW_BLOCK) * ROW_BLOCK
    nb = n_rows // ROW_BLOCK
    n_used = (pend[-1] // ROW_BLOCK).astype(jnp.int32)
    blk = jnp.minimum(jnp.arange(nb, dtype=jnp.int32), n_used - 1) * ROW_BLOCK
    block_e = jnp.minimum(jnp.searchsorted(pend, blk, side='right'), N_EXPERTS - 1).astype(jnp.int32)

    rows = _dispatch(dest, h2, jnp.zeros((n_rows, d), F32))
    out_rows = _experts(block_e, n_used[None], rows, w_gate[l], w_up[l], w_down[l])
    out = _combine(dest, x1, info, norm_final[None], out_rows)
    return out.reshape(batch, seq, d)
```

```python
import functools
import math

import jax
import jax.numpy as jnp
import numpy as np
from jax import lax
from jax.experimental import pallas as pl
from jax.experimental.pallas import tpu as pltpu

F32 = jnp.float32
BF16 = jnp.bfloat16

D_MODEL = 1024
N_HEADS = 4
HEAD_DIM = 64
VALUE_DIM = 128
ATTN_WIDTH = 512
SSM_WIDTH = 512
SSM_GROUP = 16
N_GROUPS = 32
SSM_STATE = 64
N_EXPERT_GROUPS = 4
EXPERTS_PER_GROUP = 8
N_EXPERTS = 32
EXPERT_FF = 512
RMS_EPS = 1e-6
LAMBDA_INIT = 0.8 - 0.6 * math.exp(-0.3 * 0)
LOG2E = math.log2(math.e)

CHUNK = 16
HALF_GROUPS = 16
LANES = 128
AUG = 128
ROW_BLOCK = 256
VMEM_LIMIT_CAP = 56 * 1024 * 1024


def _params(dims, vmem_mb):
    return pltpu.CompilerParams(dimension_semantics=dims,
                                vmem_limit_bytes=min(vmem_mb * 1024 * 1024, VMEM_LIMIT_CAP))


def _rms(x, gain):
    return x * lax.rsqrt(jnp.mean(x * x, axis=-1, keepdims=True) + RMS_EPS) * gain


def _split3(val):
    hi = val.astype(BF16).astype(F32)
    r1 = val - hi
    mid = r1.astype(BF16).astype(F32)
    lo = r1 - mid
    return hi, mid, lo


def _inproj_kernel(x_ref, g_ref, wq_ref, wk_ref, wv_ref, wu_ref, q_ref, k_ref, v_ref, u_ref, ubuf, *, tile, seq):
    i = pl.program_id(0)
    h = _rms(x_ref[...], g_ref[...]).astype(BF16)
    nt = (((1,), (1,)), ((), ()))
    qt = lax.dot_general(wq_ref[...], h, nt, preferred_element_type=F32)
    kp = jnp.dot(h, wk_ref[...], preferred_element_type=F32)
    v_ref[...] = lax.dot_general(wv_ref[...], h, nt, preferred_element_type=F32).astype(BF16)
    up = jnp.dot(h, wu_ref[...], preferred_element_type=F32)
    pos0 = lax.rem(i * tile, seq)
    pos_k = (pos0 + lax.broadcasted_iota(jnp.int32, (tile, AUG), 0)).astype(F32)
    pos_q = (pos0 + lax.broadcasted_iota(jnp.int32, (1, tile), 1)).astype(F32)
    lane = lax.broadcasted_iota(jnp.int32, (tile, AUG), 1)
    srow = lax.broadcasted_iota(jnp.int32, (AUG, tile), 0)
    qscale = HEAD_DIM ** -0.5 * LOG2E
    for hd in range(N_HEADS):
        slope = 2.0 ** (-8.0 * (hd + 1) / N_HEADS) * LOG2E
        hi, mid, lo = _split3(pos_k * slope)
        k_add = jnp.where(lane == 64, hi,
                          jnp.where(lane == 65, mid,
                                    jnp.where(lane == 66, lo, jnp.where((lane >= 67) & (lane < 70), 1.0, 0.0))))
        hi, mid, lo = _split3(pos_q * slope)
        q_add = jnp.where(srow < 64, 0.0,
                          jnp.where(srow < 67, 1.0,
                                    jnp.where(srow == 67, -hi, jnp.where(srow == 68, -mid, jnp.where(srow == 69, -lo, 0.0)))))
        for m in range(2):
            c0 = (hd * 2 + m) * AUG
            q_ref[c0:c0 + AUG, :] = (qt[c0:c0 + AUG, :] * qscale + q_add).astype(BF16)
            k_ref[:, c0:c0 + AUG] = (kp[:, c0:c0 + AUG] + k_add).astype(BF16)
    for cb in range(SSM_WIDTH // LANES):
        ubuf[cb] = up[:, cb * LANES:(cb + 1) * LANES]
    for s in range(CHUNK):
        for cb in range(SSM_WIDTH // LANES):
            u_ref[s, :, cb * LANES:(cb + 1) * LANES] = (
                ubuf[cb, pl.ds(s, tile // CHUNK, stride=CHUNK), :].astype(BF16))


def _in_proj(x2, gain, wq_t, wk, wv_t, wu, seq, tile=512):
    t = x2.shape[0]
    qk = 2 * N_HEADS * AUG
    kern = functools.partial(_inproj_kernel, tile=tile, seq=seq)
    full = lambda shape: pl.BlockSpec(shape, lambda i: (0, 0))
    return pl.pallas_call(
        kern, name="in_proj",
        grid=(t // tile,),
        in_specs=[pl.BlockSpec((tile, D_MODEL), lambda i: (i, 0)),
                  full((1, D_MODEL)), full((qk, D_MODEL)), full((D_MODEL, qk)),
                  full((ATTN_WIDTH, D_MODEL)), full((D_MODEL, SSM_WIDTH))],
        out_specs=[pl.BlockSpec((qk, tile), lambda i: (0, i)),
                   pl.BlockSpec((tile, qk), lambda i: (i, 0)),
                   pl.BlockSpec((ATTN_WIDTH, tile), lambda i: (0, i)),
                   pl.BlockSpec((CHUNK, tile // CHUNK, SSM_WIDTH), lambda i: (0, i, 0))],
        out_shape=[jax.ShapeDtypeStruct((qk, t), BF16),
                   jax.ShapeDtypeStruct((t, qk), BF16),
                   jax.ShapeDtypeStruct((ATTN_WIDTH, t), BF16),
                   jax.ShapeDtypeStruct((CHUNK, t // CHUNK, SSM_WIDTH), BF16)],
        scratch_shapes=[pltpu.VMEM((SSM_WIDTH // LANES, tile, LANES), F32)],
        compiler_params=_params(("arbitrary",), 48),
    )(x2, gain, wq_t, wk, wv_t, wu)


def _attn_kernel(lq1, lk1, lq2, lk2, sub_ref, q_ref, k_ref, v_ref, o_ref, m_sc, l_sc, acc_sc, *, tq):
    qi = pl.program_id(2)
    lam = (jnp.exp(jnp.sum(lq1[...] * lk1[...], axis=-1, keepdims=True))
           - jnp.exp(jnp.sum(lq2[...] * lk2[...], axis=-1, keepdims=True)) + LAMBDA_INIT)
    m_sc[...] = jnp.full(m_sc.shape, -jnp.inf, F32)
    l_sc[...] = jnp.zeros(l_sc.shape, F32)
    acc_sc[...] = jnp.zeros(acc_sc.shape, F32)

    def step(j, masked):
        r0 = pl.multiple_of(j * tq, tq)
        kblk = k_ref[pl.ds(r0, tq), :]
        vblk = v_ref[:, pl.ds(r0, tq)]
        scores = [jnp.dot(kblk[:, m * AUG:(m + 1) * AUG], q_ref[m * AUG:(m + 1) * AUG, :],
                          preferred_element_type=F32) for m in range(2)]
        for m in range(2):
            s = scores[m]
            if masked:
                key = lax.broadcasted_iota(jnp.int32, (tq, tq), 0)
                qry = lax.broadcasted_iota(jnp.int32, (tq, tq), 1)
                s = jnp.where(key <= qry, s, -jnp.inf)
            m_prev = m_sc[m]
            m_new = jnp.maximum(m_prev, jnp.max(s, axis=0, keepdims=True))
            p = jnp.exp2(s - m_new)
            alpha = jnp.exp2(m_prev - m_new)
            l_sc[m] = alpha * l_sc[m] + jnp.sum(p.reshape(tq // 8, 8, tq), axis=0)
            acc_sc[m] = alpha * acc_sc[m] + jnp.dot(vblk, p.astype(BF16), preferred_element_type=F32)
            m_sc[m] = m_new

    def body(j, carry):
        step(j, False)
        return carry

    lax.fori_loop(0, qi, body, 0)
    step(qi, True)
    l0 = jnp.sum(l_sc[0], axis=0, keepdims=True)
    l1 = jnp.sum(l_sc[1], axis=0, keepdims=True)
    o = acc_sc[0] / l0 - lam * (acc_sc[1] / l1)
    o = o * lax.rsqrt(jnp.mean(o * o, axis=0, keepdims=True) + RMS_EPS) * sub_ref[...] * (1.0 - LAMBDA_INIT)
    o_ref[...] = o.T.astype(BF16)


def _attention(q_t, k_aug, v_t, lq1, lk1, lq2, lk2, subln_col, batch, seq, tq=512):
    tq = min(tq, seq)
    nq = seq // tq
    t = batch * seq
    small = pl.BlockSpec((1, HEAD_DIM), lambda b, h, i: (0, 0))
    kern = functools.partial(_attn_kernel, tq=tq)
    return pl.pallas_call(
        kern, name="attention",
        grid=(batch, N_HEADS, nq),
        in_specs=[small, small, small, small,
                  pl.BlockSpec((VALUE_DIM, 1), lambda b, h, i: (0, 0)),
                  pl.BlockSpec((2 * AUG, tq), lambda b, h, i: (h, b * nq + i)),
                  pl.BlockSpec((seq, 2 * AUG), lambda b, h, i: (b, h)),
                  pl.BlockSpec((VALUE_DIM, seq), lambda b, h, i: (h, b))],
        out_specs=pl.BlockSpec((tq, VALUE_DIM), lambda b, h, i: (b * nq + i, h)),
        out_shape=jax.ShapeDtypeStruct((t, ATTN_WIDTH), BF16),
        scratch_shapes=[pltpu.VMEM((2, 1, tq), F32), pltpu.VMEM((2, 8, tq), F32),
                        pltpu.VMEM((2, VALUE_DIM, tq), F32)],
        compiler_params=_params(("arbitrary", "arbitrary", "arbitrary"), 48),
    )(lq1, lk1, lq2, lk2, subln_col, q_t, k_aug, v_t)


def _ssm_state_kernel(u_ref, bre_ref, bim_ref, are_ref, aim_ref, sre_ref, sim_ref, wre, wim):
    i = pl.program_id(1)

    @pl.when(i == 0)
    def _():
        wre[...] = bre_ref[0]
        wim[...] = bim_ref[0]
        sre_ref[...] = jnp.zeros(sre_ref.shape, F32)
        sim_ref[...] = jnp.zeros(sim_ref.shape, F32)

    @pl.when(i > 0)
    def _():
        ar, ai = are_ref[0], aim_ref[0]
        wr, wi = wre[...], wim[...]
        wre[...] = wr * ar - wi * ai
        wim[...] = wr * ai + wi * ar

    u = u_ref[0]
    sre_ref[...] += jnp.dot(u, wre[...].astype(BF16), preferred_element_type=F32)
    sim_ref[...] += jnp.dot(u, wim[...].astype(BF16), preferred_element_type=F32)


def _ssm_state(u3, b_re, b_im, a_row_re, a_row_im):
    nch = u3.shape[1]
    hw = HALF_GROUPS * SSM_GROUP
    sw = HALF_GROUPS * SSM_STATE
    return pl.pallas_call(
        _ssm_state_kernel, name="ssm_state",
        grid=(2, CHUNK),
        in_specs=[pl.BlockSpec((1, nch, hw), lambda hf, i: (CHUNK - 1 - i, 0, hf)),
                  pl.BlockSpec((1, hw, sw), lambda hf, i: (hf, 0, 0)),
                  pl.BlockSpec((1, hw, sw), lambda hf, i: (hf, 0, 0)),
                  pl.BlockSpec((1, 1, sw), lambda hf, i: (hf, 0, 0)),
                  pl.BlockSpec((1, 1, sw), lambda hf, i: (hf, 0, 0))],
        out_specs=[pl.BlockSpec((nch, sw), lambda hf, i: (0, hf)),
                   pl.BlockSpec((nch, sw), lambda hf, i: (0, hf))],
        out_shape=[jax.ShapeDtypeStruct((nch, 2 * sw), F32)] * 2,
        scratch_shapes=[pltpu.VMEM((hw, sw), F32), pltpu.VMEM((hw, sw), F32)],
        compiler_params=_params(("arbitrary", "arbitrary"), 48),
    )(u3, b_re, b_im, a_row_re, a_row_im)


def _ssm_scan_kernel(sre_ref, sim_ref, pre_ref, pim_ref, hre_ref, him_ref, *, nchunk, nsteps):
    hr, hi = sre_ref[...], sim_ref[...]
    row = lax.rem(lax.broadcasted_iota(jnp.int32, hr.shape, 0), nchunk)
    for k in range(nsteps):
        d = 1 << k
        ar, ai = pre_ref[0, k:k + 1, :], pim_ref[0, k:k + 1, :]
        keep = row >= d
        pr = jnp.where(keep, pltpu.roll(hr, d, 0), 0.0)
        pi = jnp.where(keep, pltpu.roll(hi, d, 0), 0.0)
        hr, hi = hr + ar * pr - ai * pi, hi + ar * pi + ai * pr
    keep = row >= 1
    hre_ref[...] = jnp.where(keep, pltpu.roll(hr, 1, 0), 0.0).astype(BF16)
    him_ref[...] = jnp.where(keep, pltpu.roll(hi, 1, 0), 0.0).astype(BF16)


def _ssm_scan(s_re, s_im, p_re, p_im, nchunk, cols=512):
    nrow, width = s_re.shape
    nsteps = p_re.shape[1]
    per_half = (width // 2) // cols
    kern = functools.partial(_ssm_scan_kernel, nchunk=nchunk, nsteps=nsteps)
    blk = pl.BlockSpec((nrow, cols), lambda j: (0, j))
    pblk = pl.BlockSpec((1, nsteps, cols), lambda j: (j // per_half, 0, j % per_half))
    return pl.pallas_call(
        kern, name="ssm_scan",
        grid=(width // cols,),
        in_specs=[blk, blk, pblk, pblk],
        out_specs=[blk, blk],
        out_shape=[jax.ShapeDtypeStruct((nrow, width), BF16)] * 2,
        compiler_params=_params(("arbitrary",), 48),
    )(s_re, s_im, p_re, p_im)


def _ssm_out_kernel(u_ref, kt_ref, hre_ref, him_ref, cre_ref, cim_ref, are_ref, aim_ref, y_ref, wre, wim):
    t = pl.program_id(1)
    ar, ai = are_ref[0], aim_ref[0]

    @pl.when(t == 0)
    def _():
        cr, ci = cre_ref[0], cim_ref[0]
        wre[...] = cr * ar - ci * ai
        wim[...] = cr * ai + ci * ar

    @pl.when(t > 0)
    def _():
        wr, wi = wre[...], wim[...]
        wre[...] = wr * ar - wi * ai
        wim[...] = wr * ai + wi * ar

    y_ref[0] = (jnp.dot(hre_ref[...], wre[...].astype(BF16), preferred_element_type=F32)
                - jnp.dot(him_ref[...], wim[...].astype(BF16), preferred_element_type=F32))

    def body(s, carry):
        y_ref[0] += jnp.dot(u_ref[s], kt_ref[t - s, 0], preferred_element_type=F32)
        return carry

    lax.fori_loop(0, t + 1, body, 0)


def _ssm_out(u3, kt, h_re, h_im, c_re, c_im, a_col_re, a_col_im):
    nch = u3.shape[1]
    hw = HALF_GROUPS * SSM_GROUP
    sw = HALF_GROUPS * SSM_STATE
    return pl.pallas_call(
        _ssm_out_kernel, name="ssm_out",
        grid=(2, CHUNK),
        in_specs=[pl.BlockSpec((CHUNK, nch, hw), lambda hf, t: (0, 0, hf)),
                  pl.BlockSpec((CHUNK, 1, hw, hw), lambda hf, t: (0, hf, 0, 0)),
                  pl.BlockSpec((nch, sw), lambda hf, t: (0, hf)),
                  pl.BlockSpec((nch, sw), lambda hf, t: (0, hf)),
                  pl.BlockSpec((1, sw, hw), lambda hf, t: (hf, 0, 0)),
                  pl.BlockSpec((1, sw, hw), lambda hf, t: (hf, 0, 0)),
                  pl.BlockSpec((1, sw, 1), lambda hf, t: (hf, 0, 0)),
                  pl.BlockSpec((1, sw, 1), lambda hf, t: (hf, 0, 0))],
        out_specs=pl.BlockSpec((1, nch, hw), lambda hf, t: (t, 0, hf)),
        out_shape=jax.ShapeDtypeStruct((CHUNK, nch, SSM_WIDTH), F32),
        scratch_shapes=[pltpu.VMEM((sw, hw), F32), pltpu.VMEM((sw, hw), F32)],
        compiler_params=_params(("arbitrary", "arbitrary"), 48),
    )(u3, kt, h_re, h_im, c_re, c_im, a_col_re, a_col_im)


def _ssm_constants(lam_re, lam_im, log_dt, b_re, b_im, c_re, c_im, d_skip, nchunk):
    lam = lax.complex(lam_re.astype(F32), lam_im.astype(F32))
    dt = jnp.exp(log_dt.astype(F32))[:, None]
    a = jnp.exp(lam * dt)
    bbar = ((a - 1.0) / lam)[..., None] * lax.complex(b_re.astype(F32), b_im.astype(F32))
    c_c = lax.complex(c_re.astype(F32), c_im.astype(F32))
    lags = jnp.arange(CHUNK, dtype=F32)[:, None, None]
    apow = jnp.exp((lam * dt)[None] * lags)
    kt = jnp.real(jnp.einsum('gcn,jgn,gnd->jgdc', c_c, apow, bbar))
    kt = kt.at[0].add(jnp.einsum('gc,dc->gdc', d_skip.astype(F32), jnp.eye(SSM_GROUP, dtype=F32)))
    eye = jnp.eye(HALF_GROUPS, dtype=F32)
    hw, sw = HALF_GROUPS * SSM_GROUP, HALF_GROUPS * SSM_STATE
    kt_t = jnp.einsum('jhgdc,gk->jhgdkc', kt.reshape(CHUNK, 2, HALF_GROUPS, SSM_GROUP, SSM_GROUP), eye)
    kt_t = kt_t.reshape(CHUNK, 2, hw, hw).astype(BF16)

    def b_tiles(part):
        p = part.reshape(2, HALF_GROUPS, SSM_STATE, SSM_GROUP)
        return jnp.einsum('hgnd,gk->hgdkn', p, eye).reshape(2, hw, sw)

    def c_tiles(part):
        p = part.reshape(2, HALF_GROUPS, SSM_GROUP, SSM_STATE)
        return jnp.einsum('hgcn,gk->hgnkc', p, eye).reshape(2, sw, hw)

    nsteps = max(int(math.log2(nchunk)), 1)
    steps = (CHUNK * 2.0 ** jnp.arange(nsteps, dtype=F32))[:, None, None]
    pw = jnp.exp((lam * dt)[None] * steps).reshape(nsteps, 2, sw).transpose(1, 0, 2)
    a_h = a.reshape(2, sw)
    return dict(kt=kt_t, b_re=b_tiles(jnp.real(bbar)), b_im=b_tiles(jnp.imag(bbar)),
                c_re=c_tiles(jnp.real(c_c)), c_im=c_tiles(jnp.imag(c_c)),
                a_row_re=jnp.real(a_h)[:, None, :], a_row_im=jnp.imag(a_h)[:, None, :],
                a_col_re=jnp.real(a_h)[:, :, None], a_col_im=jnp.imag(a_h)[:, :, None],
                p_re=jnp.real(pw), p_im=jnp.imag(pw))


def _mix_kernel(x_ref, attn_ref, y3_ref, wglu_ref, bglu_ref, gssm_ref, woa_ref, wos_ref, gmoe_ref,
                wr_ref, br_ref, x1_ref, h2_ref, info_ref, cnt_ref, ybuf, carry, *, tile):
    i = pl.program_id(0)

    @pl.when(i == 0)
    def _():
        carry[...] = jnp.zeros(carry.shape, F32)

    for s in range(CHUNK):
        for cb in range(SSM_WIDTH // LANES):
            ybuf[cb, pl.ds(s, tile // CHUNK, stride=CHUNK), :] = y3_ref[s, :, cb * LANES:(cb + 1) * LANES]
    y = jax.nn.gelu(jnp.concatenate([ybuf[cb] for cb in range(SSM_WIDTH // LANES)], axis=-1))
    z = jnp.dot(y.astype(BF16), wglu_ref[...], preferred_element_type=F32) + bglu_ref[...]
    y = y * jax.nn.sigmoid(z)
    ssm = _rms(y, gssm_ref[...])
    x1 = (x_ref[...] + jnp.dot(attn_ref[...], woa_ref[...], preferred_element_type=F32)
          + jnp.dot(ssm.astype(BF16), wos_ref[...], preferred_element_type=F32))
    x1_ref[...] = x1
    h2 = _rms(x1, gmoe_ref[...])
    h2_ref[...] = h2

    logits = jnp.dot(h2, wr_ref[...], preferred_element_type=F32, precision=lax.Precision.HIGHEST) + br_ref[...]
    lane = lax.broadcasted_iota(jnp.int32, logits.shape, 1)
    neg = -jnp.inf
    gl = jnp.where(lane < N_EXPERT_GROUPS, logits, neg)
    gmax = jnp.max(gl, axis=-1, keepdims=True)
    gsel = jnp.min(jnp.where(gl == gmax, lane, LANES), axis=-1, keepdims=True)
    p_group = 1.0 / jnp.sum(jnp.exp(gl - gmax), axis=-1, keepdims=True)
    elane = lane - N_EXPERT_GROUPS
    in_grp = (elane >= 0) & (elane < N_EXPERTS) & ((elane >> 3) == gsel)
    el = jnp.where(in_grp, logits, neg)
    m1 = jnp.max(el, axis=-1, keepdims=True)
    i1 = jnp.min(jnp.where(el == m1, lane, LANES), axis=-1, keepdims=True)
    den = jnp.sum(jnp.exp(el - m1), axis=-1, keepdims=True)
    el2 = jnp.where(lane == i1, neg, el)
    m2 = jnp.max(el2, axis=-1, keepdims=True)
    i2 = jnp.min(jnp.where(el2 == m2, lane, LANES), axis=-1, keepdims=True)
    g0 = p_group / den
    g1 = p_group * jnp.exp(m2 - m1) / den
    e0 = i1 - N_EXPERT_GROUPS
    e1 = i2 - N_EXPERT_GROUPS

    hit0 = lane == e0
    hit1 = lane == e1
    onehot = jnp.where(hit0 | hit1, 1.0, 0.0)
    r = lax.broadcasted_iota(jnp.int32, (tile, tile), 0)
    c = lax.broadcasted_iota(jnp.int32, (tile, tile), 1)
    tril = jnp.where(c < r, 1.0, 0.0).astype(BF16)
    before = jnp.dot(tril, onehot.astype(BF16), preferred_element_type=F32) + carry[...]
    rank0 = jnp.sum(jnp.where(hit0, before, 0.0), axis=-1, keepdims=True)
    rank1 = jnp.sum(jnp.where(hit1, before, 0.0), axis=-1, keepdims=True)
    carry[...] += jnp.sum(onehot, axis=0, keepdims=True)
    cnt_ref[...] = carry[...]
    info = jnp.where(lane == 0, e0.astype(F32),
                     jnp.where(lane == 1, e1.astype(F32),
                               jnp.where(lane == 2, g0,
                                         jnp.where(lane == 3, g1,
                                                   jnp.where(lane == 4, rank0, jnp.where(lane == 5, rank1, 0.0))))))
    info_ref[...] = info


def _mix(x2, attn, y3, wglu, bglu, gssm, wo_a, wo_s, gmoe, w_router, b_router, tile=512):
    t = x2.shape[0]
    kern = functools.partial(_mix_kernel, tile=tile)
    full = lambda shape: pl.BlockSpec(shape, lambda i: tuple(0 for _ in shape))
    return pl.pallas_call(
        kern, name="mix",
        grid=(t // tile,),
        in_specs=[pl.BlockSpec((tile, D_MODEL), lambda i: (i, 0)),
                  pl.BlockSpec((tile, ATTN_WIDTH), lambda i: (i, 0)),
                  pl.BlockSpec((CHUNK, tile // CHUNK, SSM_WIDTH), lambda i: (0, i, 0)),
                  full((SSM_WIDTH, SSM_WIDTH)), full((1, SSM_WIDTH)), full((1, SSM_WIDTH)),
                  full((ATTN_WIDTH, D_MODEL)), full((SSM_WIDTH, D_MODEL)), full((1, D_MODEL)),
                  full((D_MODEL, LANES)), full((1, LANES))],
        out_specs=[pl.BlockSpec((tile, D_MODEL), lambda i: (i, 0)),
                   pl.BlockSpec((tile, D_MODEL), lambda i: (i, 0)),
                   pl.BlockSpec((tile, LANES), lambda i: (i, 0)),
                   pl.BlockSpec((1, LANES), lambda i: (0, 0))],
        out_shape=[jax.ShapeDtypeStruct((t, D_MODEL), F32),
                   jax.ShapeDtypeStruct((t, D_MODEL), F32),
                   jax.ShapeDtypeStruct((t, LANES), F32),
                   jax.ShapeDtypeStruct((1, LANES), F32)],
        scratch_shapes=[pltpu.VMEM((SSM_WIDTH // LANES, tile, LANES), F32), pltpu.VMEM((1, LANES), F32)],
        compiler_params=_params(("arbitrary",), 48),
    )(x2, attn, y3, wglu, bglu, gssm, wo_a, wo_s, gmoe, w_router, b_router)


def _row_copy(src, s, dst, d, sem):
    return pltpu.make_async_copy(src.at[pl.ds(s, 1), :], dst.at[pl.ds(d, 1), :], sem)


def _dispatch_kernel(dest_ref, h_ref, rows_in, rows_out, sem, *, tile):
    del rows_in
    base = pl.program_id(0) * tile

    def issue(t, carry):
        for j in range(2):
            _row_copy(h_ref, t, rows_out, dest_ref[(base + t) * 2 + j], sem).start()
        return carry

    def drain(t, carry):
        for j in range(2):
            _row_copy(h_ref, 0, rows_out, 0, sem).wait()
        return carry

    lax.fori_loop(0, tile, issue, 0)
    lax.fori_loop(0, tile, drain, 0)


def _dispatch(dest_flat, h2, rows_zero, tile=256):
    t = h2.shape[0]
    kern = functools.partial(_dispatch_kernel, tile=tile)
    grid_spec = pltpu.PrefetchScalarGridSpec(
        num_scalar_prefetch=1, grid=(t // tile,),
        in_specs=[pl.BlockSpec((tile, D_MODEL), lambda i, d: (i, 0)),
                  pl.BlockSpec(memory_space=pl.ANY)],
        out_specs=pl.BlockSpec(memory_space=pl.ANY),
        scratch_shapes=[pltpu.SemaphoreType.DMA(())])
    return pl.pallas_call(
        kern, name="dispatch", grid_spec=grid_spec,
        out_shape=jax.ShapeDtypeStruct(rows_zero.shape, rows_zero.dtype),
        input_output_aliases={2: 0},
        compiler_params=_params(("arbitrary",), 32),
    )(dest_flat, h2, rows_zero)


def _expert_kernel(be_ref, nu_ref, rows_ref, wg_ref, wu_ref, wd_ref, out_ref, wg_b, wu_b, wd_b):
    i = pl.program_id(0)
    changed = (i == 0) | (be_ref[i] != be_ref[jnp.maximum(i - 1, 0)])

    @pl.when(changed)
    def _():
        wg_b[...] = wg_ref[0].astype(BF16)
        wu_b[...] = wu_ref[0].astype(BF16)
        wd_b[...] = wd_ref[0].astype(BF16)

    @pl.when(i < nu_ref[0])
    def _():
        xb = rows_ref[...].astype(BF16)
        gate = jnp.dot(xb, wg_b[...], preferred_element_type=F32)
        up = jnp.dot(xb, wu_b[...], preferred_element_type=F32)
        hid = (gate * jax.nn.sigmoid(gate) * up).astype(BF16)
        out_ref[...] = jnp.dot(hid, wd_b[...], preferred_element_type=F32)

    @pl.when(i >= nu_ref[0])
    def _():
        out_ref[...] = jnp.zeros(out_ref.shape, F32)


def _experts(block_e, n_used, rows, w_gate, w_up, w_down):
    n_rows = rows.shape[0]
    nb = n_rows // ROW_BLOCK
    last = lambda i, nu: jnp.minimum(i, nu[0] - 1)
    grid_spec = pltpu.PrefetchScalarGridSpec(
        num_scalar_prefetch=2, grid=(nb,),
        in_specs=[pl.BlockSpec((ROW_BLOCK, D_MODEL), lambda i, be, nu: (last(i, nu), 0)),
                  pl.BlockSpec((1, D_MODEL, EXPERT_FF), lambda i, be, nu: (be[i], 0, 0)),
                  pl.BlockSpec((1, D_MODEL, EXPERT_FF), lambda i, be, nu: (be[i], 0, 0)),
                  pl.BlockSpec((1, EXPERT_FF, D_MODEL), lambda i, be, nu: (be[i], 0, 0))],
        out_specs=pl.BlockSpec((ROW_BLOCK, D_MODEL), lambda i, be, nu: (i, 0)),
        scratch_shapes=[pltpu.VMEM((D_MODEL, EXPERT_FF), BF16), pltpu.VMEM((D_MODEL, EXPERT_FF), BF16),
                        pltpu.VMEM((EXPERT_FF, D_MODEL), BF16)])
    return pl.pallas_call(
        _expert_kernel, name="experts", grid_spec=grid_spec,
        out_shape=jax.ShapeDtypeStruct((n_rows, D_MODEL), F32),
        compiler_params=_params(("arbitrary",), 48),
    )(block_e, n_used, rows, w_gate, w_up, w_down)


def _combine_kernel(dest_ref, x1_ref, info_ref, gfin_ref, rows_ref, o_ref, buf0, buf1, sem, *, tile):
    base = pl.program_id(0) * tile

    def issue(t, carry):
        _row_copy(rows_ref, dest_ref[(base + t) * 2], buf0, t, sem).start()
        _row_copy(rows_ref, dest_ref[(base + t) * 2 + 1], buf1, t, sem).start()
        return carry

    def drain(t, carry):
        _row_copy(rows_ref, 0, buf0, 0, sem).wait()
        _row_copy(rows_ref, 0, buf1, 0, sem).wait()
        return carry

    lax.fori_loop(0, tile, issue, 0)
    lax.fori_loop(0, tile, drain, 0)
    info = info_ref[...]
    x2 = x1_ref[...] + info[:, 2:3] * buf0[...] + info[:, 3:4] * buf1[...]
    o_ref[...] = _rms(x2, gfin_ref[...])


def _combine(dest_flat, x1, info, gfin, out_rows, tile=256):
    t = x1.shape[0]
    kern = functools.partial(_combine_kernel, tile=tile)
    grid_spec = pltpu.PrefetchScalarGridSpec(
        num_scalar_prefetch=1, grid=(t // tile,),
        in_specs=[pl.BlockSpec((tile, D_MODEL), lambda i, d: (i, 0)),
                  pl.BlockSpec((tile, LANES), lambda i, d: (i, 0)),
                  pl.BlockSpec((1, D_MODEL), lambda i, d: (0, 0)),
                  pl.BlockSpec(memory_space=pl.ANY)],
        out_specs=pl.BlockSpec((tile, D_MODEL), lambda i, d: (i, 0)),
        scratch_shapes=[pltpu.VMEM((tile, D_MODEL), F32), pltpu.VMEM((tile, D_MODEL), F32),
                        pltpu.SemaphoreType.DMA(())])
    return pl.pallas_call(
        kern, name="combine", grid_spec=grid_spec,
        out_shape=jax.ShapeDtypeStruct((t, D_MODEL), F32),
        compiler_params=_params(("arbitrary",), 32),
    )(dest_flat, x1, info, gfin, out_rows)


def _split_w_in(w_in):
    def pad_blocks(w):
        w = w.reshape(D_MODEL, N_HEADS * 2, HEAD_DIM)
        return jnp.pad(w, ((0, 0), (0, 0), (0, AUG - HEAD_DIM))).reshape(D_MODEL, N_HEADS * 2 * AUG)
    wq_t = pad_blocks(w_in[:, :ATTN_WIDTH]).T.astype(BF16)
    wk = pad_blocks(w_in[:, ATTN_WIDTH:2 * ATTN_WIDTH]).astype(BF16)
    wv_t = w_in[:, 2 * ATTN_WIDTH:3 * ATTN_WIDTH].T.astype(BF16)
    wu = w_in[:, 3 * ATTN_WIDTH:].astype(BF16)
    return wq_t, wk, wv_t, wu


def kernel(x, norm_attn, w_in, lambda_q1, lambda_k1, lambda_q2, lambda_k2, attn_subln, ssm_lam_re, ssm_lam_im, ssm_log_dt, ssm_b_re, ssm_b_im, ssm_c_re, ssm_c_im, ssm_d, w_glu, b_glu, ssm_norm, w_out, norm_moe, w_router_group, b_router_group, w_router_expert, b_router_expert, w_gate, w_up, w_down, norm_final):
    batch, seq, d = x.shape
    t = batch * seq
    nchunk = seq // CHUNK
    x2 = x.reshape(t, d)
    l = 0

    q_t, k_aug, v_t, u3 = _in_proj(x2, norm_attn[l][None], *_split_w_in(w_in[l]), seq)
    attn = _attention(q_t, k_aug, v_t, lambda_q1[l][None], lambda_k1[l][None], lambda_q2[l][None],
                      lambda_k2[l][None], attn_subln[l][:, None], batch, seq)

    sc = _ssm_constants(ssm_lam_re[l], ssm_lam_im[l], ssm_log_dt[l], ssm_b_re[l], ssm_b_im[l],
                        ssm_c_re[l], ssm_c_im[l], ssm_d[l], nchunk)
    s_re, s_im = _ssm_state(u3, sc["b_re"], sc["b_im"], sc["a_row_re"], sc["a_row_im"])
    h_re, h_im = _ssm_scan(s_re, s_im, sc["p_re"], sc["p_im"], nchunk)
    y3 = _ssm_out(u3, sc["kt"], h_re, h_im, sc["c_re"], sc["c_im"], sc["a_col_re"], sc["a_col_im"])

    w_router = jnp.concatenate([w_router_group[l], w_router_expert[l]], axis=1).astype(F32)
    w_router = jnp.pad(w_router, ((0, 0), (0, LANES - w_router.shape[1])))
    b_router = jnp.concatenate([b_router_group[l], b_router_expert[l]]).astype(F32)
    b_router = jnp.pad(b_router, (0, LANES - b_router.shape[0]))[None]
    x1, h2, info, cnt = _mix(x2, attn, y3, w_glu[l].astype(BF16), b_glu[l][None], ssm_norm[l][None],
                             w_out[l][:ATTN_WIDTH].astype(BF16), w_out[l][ATTN_WIDTH:].astype(BF16),
                             norm_moe[l][None], w_router, b_router)

    experts = info[:, 0:2].astype(jnp.int32)
    ranks = info[:, 4:6].astype(jnp.int32)
    counts = cnt[0, :N_EXPERTS].astype(jnp.int32)
    padded = ((counts + ROW_BLOCK - 1) // ROW_BLOCK) * ROW_BLOCK
    pend = jnp.cumsum(padded)
    pstart = pend - padded
    dest = (pstart[experts] + ranks).reshape(-1)
    n_rows = ((2 * t + N_EXPERTS * (ROW_BLOCK - 1) + ROW_BLOCK - 1) // ROW_BLOCK) * ROW_BLOCK
    nb = n_rows // ROW_BLOCK
    n_used = (pend[-1] // ROW_BLOCK).astype(jnp.int32)
    blk = jnp.minimum(jnp.arange(nb, dtype=jnp.int32), n_used - 1) * ROW_BLOCK
    block_e = jnp.minimum(jnp.searchsorted(pend, blk, side='right'), N_EXPERTS - 1).astype(jnp.int32)

    rows = _dispatch(dest, h2, jnp.zeros((n_rows, d), F32))
    out_rows = _experts(block_e, n_used[None], rows, w_gate[l], w_up[l], w_down[l])
    out = _combine(dest, x1, info, norm_final[None], out_rows)
    return out.reshape(batch, seq, d)
```

```python
import functools
import math

import jax
import jax.numpy as jnp
import numpy as np
from jax import lax
from jax.experimental import pallas as pl
from jax.experimental.pallas import tpu as pltpu

F32 = jnp.float32
BF16 = jnp.bfloat16

D_MODEL = 1024
N_HEADS = 4
HEAD_DIM = 64
VALUE_DIM = 128
ATTN_WIDTH = 512
SSM_WIDTH = 512
SSM_GROUP = 16
N_GROUPS = 32
SSM_STATE = 64
N_EXPERT_GROUPS = 4
EXPERTS_PER_GROUP = 8
N_EXPERTS = 32
EXPERT_FF = 512
RMS_EPS = 1e-6
LAMBDA_INIT = 0.8 - 0.6 * math.exp(-0.3 * 0)
LOG2E = math.log2(math.e)

CHUNK = 16
HALF_GROUPS = 16
LANES = 128
AUG = 128
ROW_BLOCK = 256
VMEM_LIMIT_CAP = 56 * 1024 * 1024


def _params(dims, vmem_mb):
    return pltpu.CompilerParams(dimension_semantics=dims,
                                vmem_limit_bytes=min(vmem_mb * 1024 * 1024, VMEM_LIMIT_CAP))


def _rms(x, gain):
    return x * lax.rsqrt(jnp.mean(x * x, axis=-1, keepdims=True) + RMS_EPS) * gain


def _split3(val):
    hi = val.astype(BF16).astype(F32)
    r1 = val - hi
    mid = r1.astype(BF16).astype(F32)
    lo = r1 - mid
    return hi, mid, lo


def _inproj_kernel(x_ref, g_ref, wq_ref, wk_ref, wv_ref, wu_ref, q_ref, k_ref, v_ref, u_ref, ubuf, *, tile, seq):
    i = pl.program_id(0)
    h = _rms(x_ref[...], g_ref[...]).astype(BF16)
    nt = (((1,), (1,)), ((), ()))
    qt = lax.dot_general(wq_ref[...], h, nt, preferred_element_type=F32)
    kp = jnp.dot(h, wk_ref[...], preferred_element_type=F32)
    v_ref[...] = lax.dot_general(wv_ref[...], h, nt, preferred_element_type=F32).astype(BF16)
    up = jnp.dot(h, wu_ref[...], preferred_element_type=F32)
    pos0 = lax.rem(i * tile, seq)
    pos_k = (pos0 + lax.broadcasted_iota(jnp.int32, (tile, AUG), 0)).astype(F32)
    pos_q = (pos0 + lax.broadcasted_iota(jnp.int32, (1, tile), 1)).astype(F32)
    lane = lax.broadcasted_iota(jnp.int32, (tile, AUG), 1)
    srow = lax.broadcasted_iota(jnp.int32, (AUG, tile), 0)
    qscale = HEAD_DIM ** -0.5 * LOG2E
    for hd in range(N_HEADS):
        slope = 2.0 ** (-8.0 * (hd + 1) / N_HEADS) * LOG2E
        hi, mid, lo = _split3(pos_k * slope)
        k_add = jnp.where(lane == 64, hi,
                          jnp.where(lane == 65, mid,
                                    jnp.where(lane == 66, lo, jnp.where((lane >= 67) & (lane < 70), 1.0, 0.0))))
        hi, mid, lo = _split3(pos_q * slope)
        q_add = jnp.where(srow < 64, 0.0,
                          jnp.where(srow < 67, 1.0,
                                    jnp.where(srow == 67, -hi, jnp.where(srow == 68, -mid, jnp.where(srow == 69, -lo, 0.0)))))
        for m in range(2):
            c0 = (hd * 2 + m) * AUG
            q_ref[c0:c0 + AUG, :] = (qt[c0:c0 + AUG, :] * qscale + q_add).astype(BF16)
            k_ref[:, c0:c0 + AUG] = (kp[:, c0:c0 + AUG] + k_add).astype(BF16)
    for cb in range(SSM_WIDTH // LANES):
        ubuf[cb] = up[:, cb * LANES:(cb + 1) * LANES]
    for s in range(CHUNK):
        for cb in range(SSM_WIDTH // LANES):
            u_ref[s, :, cb * LANES:(cb + 1) * LANES] = (
                ubuf[cb, pl.ds(s, tile // CHUNK, stride=CHUNK), :].astype(BF16))


def _in_proj(x2, gain, wq_t, wk, wv_t, wu, seq, tile=512):
    t = x2.shape[0]
    qk = 2 * N_HEADS * AUG
    kern = functools.partial(_inproj_kernel, tile=tile, seq=seq)
    full = lambda shape: pl.BlockSpec(shape, lambda i: (0, 0))
    return pl.pallas_call(
        kern, name="in_proj",
        grid=(t // tile,),
        in_specs=[pl.BlockSpec((tile, D_MODEL), lambda i: (i, 0)),
                  full((1, D_MODEL)), full((qk, D_MODEL)), full((D_MODEL, qk)),
                  full((ATTN_WIDTH, D_MODEL)), full((D_MODEL, SSM_WIDTH))],
        out_specs=[pl.BlockSpec((qk, tile), lambda i: (0, i)),
                   pl.BlockSpec((tile, qk), lambda i: (i, 0)),
                   pl.BlockSpec((ATTN_WIDTH, tile), lambda i: (0, i)),
                   pl.BlockSpec((CHUNK, tile // CHUNK, SSM_WIDTH), lambda i: (0, i, 0))],
        out_shape=[jax.ShapeDtypeStruct((qk, t), BF16),
                   jax.ShapeDtypeStruct((t, qk), BF16),
                   jax.ShapeDtypeStruct((ATTN_WIDTH, t), BF16),
                   jax.ShapeDtypeStruct((CHUNK, t // CHUNK, SSM_WIDTH), BF16)],
        scratch_shapes=[pltpu.VMEM((SSM_WIDTH // LANES, tile, LANES), F32)],
        compiler_params=_params(("arbitrary",), 48),
    )(x2, gain, wq_t, wk, wv_t, wu)


def _attn_kernel(lq1, lk1, lq2, lk2, sub_ref, q_ref, k_ref, v_ref, o_ref, m_sc, l_sc, acc_sc, *, tq):
    qi = pl.program_id(2)
    lam = (jnp.exp(jnp.sum(lq1[...] * lk1[...], axis=-1, keepdims=True))
           - jnp.exp(jnp.sum(lq2[...] * lk2[...], axis=-1, keepdims=True)) + LAMBDA_INIT)
    m_sc[...] = jnp.full(m_sc.shape, -jnp.inf, F32)
    l_sc[...] = jnp.zeros(l_sc.shape, F32)
    acc_sc[...] = jnp.zeros(acc_sc.shape, F32)

    def step(j, masked):
        r0 = pl.multiple_of(j * tq, tq)
        kblk = k_ref[pl.ds(r0, tq), :]
        vblk = v_ref[:, pl.ds(r0, tq)]
        scores = [jnp.dot(kblk[:, m * AUG:(m + 1) * AUG], q_ref[m * AUG:(m + 1) * AUG, :],
                          preferred_element_type=F32) for m in range(2)]
        for m in range(2):
            s = scores[m]
            if masked:
                key = lax.broadcasted_iota(jnp.int32, (tq, tq), 0)
                qry = lax.broadcasted_iota(jnp.int32, (tq, tq), 1)
                s = jnp.where(key <= qry, s, -jnp.inf)
            m_prev = m_sc[m]
            m_new = jnp.maximum(m_prev, jnp.max(s, axis=0, keepdims=True))
            p = jnp.exp2(s - m_new)
            alpha = jnp.exp2(m_prev - m_new)
            l_sc[m] = alpha * l_sc[m] + jnp.sum(p.reshape(tq // 8, 8, tq), axis=0)
            acc_sc[m] = alpha * acc_sc[m] + jnp.dot(vblk, p.astype(BF16), preferred_element_type=F32)
            m_sc[m] = m_new

    def body(j, carry):
        step(j, False)
        return carry

    lax.fori_loop(0, qi, body, 0)
    step(qi, True)
    l0 = jnp.sum(l_sc[0], axis=0, keepdims=True)
    l1 = jnp.sum(l_sc[1], axis=0, keepdims=True)
    o = acc_sc[0] / l0 - lam * (acc_sc[1] / l1)
    o = o * lax.rsqrt(jnp.mean(o * o, axis=0, keepdims=True) + RMS_EPS) * sub_ref[...] * (1.0 - LAMBDA_INIT)
    o_ref[...] = o.T.astype(BF16)


def _attention(q_t, k_aug, v_t, lq1, lk1, lq2, lk2, subln_col, batch, seq, tq=512):
    tq = min(tq, seq)
    nq = seq // tq
    t = batch * seq
    small = pl.BlockSpec((1, HEAD_DIM), lambda b, h, i: (0, 0))
    kern = functools.partial(_attn_kernel, tq=tq)
    return pl.pallas_call(
        kern, name="attention",
        grid=(batch, N_HEADS, nq),
        in_specs=[small, small, small, small,
                  pl.BlockSpec((VALUE_DIM, 1), lambda b, h, i: (0, 0)),
                  pl.BlockSpec((2 * AUG, tq), lambda b, h, i: (h, b * nq + i)),
                  pl.BlockSpec((seq, 2 * AUG), lambda b, h, i: (b, h)),
                  pl.BlockSpec((VALUE_DIM, seq), lambda b, h, i: (h, b))],
        out_specs=pl.BlockSpec((tq, VALUE_DIM), lambda b, h, i: (b * nq + i, h)),
        out_shape=jax.ShapeDtypeStruct((t, ATTN_WIDTH), BF16),
        scratch_shapes=[pltpu.VMEM((2, 1, tq), F32), pltpu.VMEM((2, 8, tq), F32),
                        pltpu.VMEM((2, VALUE_DIM, tq), F32)],
        compiler_params=_params(("arbitrary", "arbitrary", "arbitrary"), 48),
    )(lq1, lk1, lq2, lk2, subln_col, q_t, k_aug, v_t)


def _ssm_state_kernel(u_ref, bre_ref, bim_ref, are_ref, aim_ref, sre_ref, sim_ref, wre, wim):
    i = pl.program_id(1)

    @pl.when(i == 0)
    def _():
        wre[...] = bre_ref[0]
        wim[...] = bim_ref[0]
        sre_ref[...] = jnp.zeros(sre_ref.shape, F32)
        sim_ref[...] = jnp.zeros(sim_ref.shape, F32)

    @pl.when(i > 0)
    def _():
        ar, ai = are_ref[0], aim_ref[0]
        wr, wi = wre[...], wim[...]
        wre[...] = wr * ar - wi * ai
        wim[...] = wr * ai + wi * ar

    u = u_ref[0]
    sre_ref[...] += jnp.dot(u, wre[...].astype(BF16), preferred_element_type=F32)
    sim_ref[...] += jnp.dot(u, wim[...].astype(BF16), preferred_element_type=F32)


def _ssm_state(u3, b_re, b_im, a_row_re, a_row_im):
    nch = u3.shape[1]
    hw = HALF_GROUPS * SSM_GROUP
    sw = HALF_GROUPS * SSM_STATE
    return pl.pallas_call(
        _ssm_state_kernel, name="ssm_state",
        grid=(2, CHUNK),
        in_specs=[pl.BlockSpec((1, nch, hw), lambda hf, i: (CHUNK - 1 - i, 0, hf)),
                  pl.BlockSpec((1, hw, sw), lambda hf, i: (hf, 0, 0)),
                  pl.BlockSpec((1, hw, sw), lambda hf, i: (hf, 0, 0)),
                  pl.BlockSpec((1, 1, sw), lambda hf, i: (hf, 0, 0)),
                  pl.BlockSpec((1, 1, sw), lambda hf, i: (hf, 0, 0))],
        out_specs=[pl.BlockSpec((nch, sw), lambda hf, i: (0, hf)),
                   pl.BlockSpec((nch, sw), lambda hf, i: (0, hf))],
        out_shape=[jax.ShapeDtypeStruct((nch, 2 * sw), F32)] * 2,
        scratch_shapes=[pltpu.VMEM((hw, sw), F32), pltpu.VMEM((hw, sw), F32)],
        compiler_params=_params(("arbitrary", "arbitrary"), 48),
    )(u3, b_re, b_im, a_row_re, a_row_im)


def _ssm_scan_kernel(sre_ref, sim_ref, pre_ref, pim_ref, hre_ref, him_ref, *, nchunk, nsteps):
    hr, hi = sre_ref[...], sim_ref[...]
    row = lax.rem(lax.broadcasted_iota(jnp.int32, hr.shape, 0), nchunk)
    for k in range(nsteps):
        d = 1 << k
        ar, ai = pre_ref[0, k:k + 1, :], pim_ref[0, k:k + 1, :]
        keep = row >= d
        pr = jnp.where(keep, pltpu.roll(hr, d, 0), 0.0)
        pi = jnp.where(keep, pltpu.roll(hi, d, 0), 0.0)
        hr, hi = hr + ar * pr - ai * pi, hi + ar * pi + ai * pr
    keep = row >= 1
    hre_ref[...] = jnp.where(keep, pltpu.roll(hr, 1, 0), 0.0).astype(BF16)
    him_ref[...] = jnp.where(keep, pltpu.roll(hi, 1, 0), 0.0).astype(BF16)


def _ssm_scan(s_re, s_im, p_re, p_im, nchunk, cols=512):
    nrow, width = s_re.shape
    nsteps = p_re.shape[1]
    per_half = (width // 2) // cols
    kern = functools.partial(_ssm_scan_kernel, nchunk=nchunk, nsteps=nsteps)
    blk = pl.BlockSpec((nrow, cols), lambda j: (0, j))
    pblk = pl.BlockSpec((1, nsteps, cols), lambda j: (j // per_half, 0, j % per_half))
    return pl.pallas_call(
        kern, name="ssm_scan",
        grid=(width // cols,),
        in_specs=[blk, blk, pblk, pblk],
        out_specs=[blk, blk],
        out_shape=[jax.ShapeDtypeStruct((nrow, width), BF16)] * 2,
        compiler_params=_params(("arbitrary",), 48),
    )(s_re, s_im, p_re, p_im)


def _ssm_out_kernel(u_ref, kt_ref, hre_ref, him_ref, cre_ref, cim_ref, are_ref, aim_ref, y_ref, wre, wim):
    t = pl.program_id(1)
    ar, ai = are_ref[0], aim_ref[0]

    @pl.when(t == 0)
    def _():
        cr, ci = cre_ref[0], cim_ref[0]
        wre[...] = cr * ar - ci * ai
        wim[...] = cr * ai + ci * ar

    @pl.when(t > 0)
    def _():
        wr, wi = wre[...], wim[...]
        wre[...] = wr * ar - wi * ai
        wim[...] = wr * ai + wi * ar

    y_ref[0] = (jnp.dot(hre_ref[...], wre[...].astype(BF16), preferred_element_type=F32)
                - jnp.dot(him_ref[...], wim[...].astype(BF16), preferred_element_type=F32))

    def body(s, carry):
        y_ref[0] += jnp.dot(u_ref[s], kt_ref[t - s, 0], preferred_element_type=F32)
        return carry

    lax.fori_loop(0, t + 1, body, 0)


def _ssm_out(u3, kt, h_re, h_im, c_re, c_im, a_col_re, a_col_im):
    nch = u3.shape[1]
    hw = HALF_GROUPS * SSM_GROUP
    sw = HALF_GROUPS * SSM_STATE
    return pl.pallas_call(
        _ssm_out_kernel, name="ssm_out",
        grid=(2, CHUNK),
        in_specs=[pl.BlockSpec((CHUNK, nch, hw), lambda hf, t: (0, 0, hf)),
                  pl.BlockSpec((CHUNK, 1, hw, hw), lambda hf, t: (0, hf, 0, 0)),
                  pl.BlockSpec((nch, sw), lambda hf, t: (0, hf)),
                  pl.BlockSpec((nch, sw), lambda hf, t: (0, hf)),
                  pl.BlockSpec((1, sw, hw), lambda hf, t: (hf, 0, 0)),
                  pl.BlockSpec((1, sw, hw), lambda hf, t: (hf, 0, 0)),
                  pl.BlockSpec((1, sw, 1), lambda hf, t: (hf, 0, 0)),
                  pl.BlockSpec((1, sw, 1), lambda hf, t: (hf, 0, 0))],
        out_specs=pl.BlockSpec((1, nch, hw), lambda hf, t: (t, 0, hf)),
        out_shape=jax.ShapeDtypeStruct((CHUNK, nch, SSM_WIDTH), F32),
        scratch_shapes=[pltpu.VMEM((sw, hw), F32), pltpu.VMEM((sw, hw), F32)],
        compiler_params=_params(("arbitrary", "arbitrary"), 48),
    )(u3, kt, h_re, h_im, c_re, c_im, a_col_re, a_col_im)


def _ssm_constants(lam_re, lam_im, log_dt, b_re, b_im, c_re, c_im, d_skip, nchunk):
    lr, li = lam_re.astype(F32), lam_im.astype(F32)
    dt = jnp.exp(log_dt.astype(F32))[:, None]

    def lam_bar_pow(k):
        mag = jnp.exp(k * lr * dt)
        return mag * jnp.cos(k * li * dt), mag * jnp.sin(k * li * dt)

    a_re, a_im = lam_bar_pow(1.0)
    den = lr * lr + li * li
    coef_re = ((a_re - 1.0) * lr + a_im * li) / den
    coef_im = (a_im * lr - (a_re - 1.0) * li) / den
    bb_re = coef_re[..., None] * b_re.astype(F32) - coef_im[..., None] * b_im.astype(F32)
    bb_im = coef_re[..., None] * b_im.astype(F32) + coef_im[..., None] * b_re.astype(F32)
    cc_re, cc_im = c_re.astype(F32), c_im.astype(F32)
    lags = jnp.arange(CHUNK, dtype=F32)[:, None, None]
    pw_re, pw_im = lam_bar_pow(lags)
    pb_re = pw_re[..., None] * bb_re[None] - pw_im[..., None] * bb_im[None]
    pb_im = pw_re[..., None] * bb_im[None] + pw_im[..., None] * bb_re[None]
    kt = jnp.einsum('gcn,jgnd->jgdc', cc_re, pb_re) - jnp.einsum('gcn,jgnd->jgdc', cc_im, pb_im)
    skip = jnp.einsum('gc,dc->gdc', d_skip.astype(F32), jnp.eye(SSM_GROUP, dtype=F32))
    kt = jnp.concatenate([kt[:1] + skip[None], kt[1:]], axis=0)
    eye = jnp.eye(HALF_GROUPS, dtype=F32)
    hw, sw = HALF_GROUPS * SSM_GROUP, HALF_GROUPS * SSM_STATE
    kt_t = jnp.einsum('jhgdc,gk->jhgdkc', kt.reshape(CHUNK, 2, HALF_GROUPS, SSM_GROUP, SSM_GROUP), eye)
    kt_t = kt_t.reshape(CHUNK, 2, hw, hw).astype(BF16)

    def b_tiles(part):
        p = part.reshape(2, HALF_GROUPS, SSM_STATE, SSM_GROUP)
        return jnp.einsum('hgnd,gk->hgdkn', p, eye).reshape(2, hw, sw)

    def c_tiles(part):
        p = part.reshape(2, HALF_GROUPS, SSM_GROUP, SSM_STATE)
        return jnp.einsum('hgcn,gk->hgnkc', p, eye).reshape(2, sw, hw)

    nsteps = max(int(math.log2(nchunk)), 1)
    steps = (CHUNK * 2.0 ** jnp.arange(nsteps, dtype=F32))[:, None, None]
    st_re, st_im = lam_bar_pow(steps)
    by_half = lambda p: p.reshape(nsteps, 2, sw).transpose(1, 0, 2)
    ar_h, ai_h = a_re.reshape(2, sw), a_im.reshape(2, sw)
    return dict(kt=kt_t, b_re=b_tiles(bb_re), b_im=b_tiles(bb_im),
                c_re=c_tiles(cc_re), c_im=c_tiles(cc_im),
                a_row_re=ar_h[:, None, :], a_row_im=ai_h[:, None, :],
                a_col_re=ar_h[:, :, None], a_col_im=ai_h[:, :, None],
                p_re=by_half(st_re), p_im=by_half(st_im))


def _mix_kernel(x_ref, attn_ref, y3_ref, wglu_ref, bglu_ref, gssm_ref, woa_ref, wos_ref, gmoe_ref,
                wr_ref, br_ref, x1_ref, h2_ref, info_ref, rt_ref, cnt_ref, ybuf, carry, *, tile):
    i = pl.program_id(0)

    @pl.when(i == 0)
    def _():
        carry[...] = jnp.zeros(carry.shape, F32)

    for s in range(CHUNK):
        for cb in range(SSM_WIDTH // LANES):
            ybuf[cb, pl.ds(s, tile // CHUNK, stride=CHUNK), :] = y3_ref[s, :, cb * LANES:(cb + 1) * LANES]
    y = jax.nn.gelu(jnp.concatenate([ybuf[cb] for cb in range(SSM_WIDTH // LANES)], axis=-1))
    z = jnp.dot(y.astype(BF16), wglu_ref[...], preferred_element_type=F32) + bglu_ref[...]
    y = y * jax.nn.sigmoid(z)
    ssm = _rms(y, gssm_ref[...])
    x1 = (x_ref[...] + jnp.dot(attn_ref[...], woa_ref[...], preferred_element_type=F32)
          + jnp.dot(ssm.astype(BF16), wos_ref[...], preferred_element_type=F32))
    x1_ref[...] = x1
    h2 = _rms(x1, gmoe_ref[...])
    h2_ref[...] = h2

    h_hi = h2.astype(BF16)
    h_lo = (h2 - h_hi.astype(F32)).astype(BF16)
    wr = wr_ref[...]
    both = jnp.dot(h_hi, wr, preferred_element_type=F32)
    logits = (both[:, :LANES] + both[:, LANES:]
              + jnp.dot(h_lo, wr[:, :LANES], preferred_element_type=F32) + br_ref[...])
    lane = lax.broadcasted_iota(jnp.int32, logits.shape, 1)
    neg = -jnp.inf
    gl = jnp.where(lane < N_EXPERT_GROUPS, logits, neg)
    gmax = jnp.max(gl, axis=-1, keepdims=True)
    gsel = jnp.min(jnp.where(gl == gmax, lane, LANES), axis=-1, keepdims=True)
    p_group = 1.0 / jnp.sum(jnp.exp(gl - gmax), axis=-1, keepdims=True)
    elane = lane - N_EXPERT_GROUPS
    in_grp = (elane >= 0) & (elane < N_EXPERTS) & ((elane >> 3) == gsel)
    el = jnp.where(in_grp, logits, neg)
    m1 = jnp.max(el, axis=-1, keepdims=True)
    i1 = jnp.min(jnp.where(el == m1, lane, LANES), axis=-1, keepdims=True)
    den = jnp.sum(jnp.exp(el - m1), axis=-1, keepdims=True)
    el2 = jnp.where(lane == i1, neg, el)
    m2 = jnp.max(el2, axis=-1, keepdims=True)
    i2 = jnp.min(jnp.where(el2 == m2, lane, LANES), axis=-1, keepdims=True)
    g0 = p_group / den
    g1 = p_group * jnp.exp(m2 - m1) / den
    e0 = i1 - N_EXPERT_GROUPS
    e1 = i2 - N_EXPERT_GROUPS

    hit0 = lane == e0
    hit1 = lane == e1
    onehot = jnp.where(hit0 | hit1, 1.0, 0.0)
    r = lax.broadcasted_iota(jnp.int32, (tile, tile), 0)
    c = lax.broadcasted_iota(jnp.int32, (tile, tile), 1)
    tril = jnp.where(c < r, 1.0, 0.0).astype(BF16)
    before = jnp.dot(tril, onehot.astype(BF16), preferred_element_type=F32) + carry[...]
    rank0 = jnp.sum(jnp.where(hit0, before, 0.0), axis=-1, keepdims=True)
    rank1 = jnp.sum(jnp.where(hit1, before, 0.0), axis=-1, keepdims=True)
    carry[...] += jnp.sum(onehot, axis=0, keepdims=True)
    cnt_ref[...] = carry[...]
    info = jnp.where(lane == 0, e0.astype(F32),
                     jnp.where(lane == 1, e1.astype(F32),
                               jnp.where(lane == 2, g0,
                                         jnp.where(lane == 3, g1,
                                                   jnp.where(lane == 4, rank0, jnp.where(lane == 5, rank1, 0.0))))))
    info_ref[...] = info
    rt_ref[...] = info.T[:8]


def _mix(x2, attn, y3, wglu, bglu, gssm, wo_a, wo_s, gmoe, w_router, b_router, tile=512):
    t = x2.shape[0]
    kern = functools.partial(_mix_kernel, tile=tile)
    full = lambda shape: pl.BlockSpec(shape, lambda i: tuple(0 for _ in shape))
    return pl.pallas_call(
        kern, name="mix",
        grid=(t // tile,),
        in_specs=[pl.BlockSpec((tile, D_MODEL), lambda i: (i, 0)),
                  pl.BlockSpec((tile, ATTN_WIDTH), lambda i: (i, 0)),
                  pl.BlockSpec((CHUNK, tile // CHUNK, SSM_WIDTH), lambda i: (0, i, 0)),
                  full((SSM_WIDTH, SSM_WIDTH)), full((1, SSM_WIDTH)), full((1, SSM_WIDTH)),
                  full((ATTN_WIDTH, D_MODEL)), full((SSM_WIDTH, D_MODEL)), full((1, D_MODEL)),
                  full((D_MODEL, 2 * LANES)), full((1, LANES))],
        out_specs=[pl.BlockSpec((tile, D_MODEL), lambda i: (i, 0)),
                   pl.BlockSpec((tile, D_MODEL), lambda i: (i, 0)),
                   pl.BlockSpec((tile, LANES), lambda i: (i, 0)),
                   pl.BlockSpec((8, tile), lambda i: (0, i)),
                   pl.BlockSpec((1, LANES), lambda i: (0, 0))],
        out_shape=[jax.ShapeDtypeStruct((t, D_MODEL), F32),
                   jax.ShapeDtypeStruct((t, D_MODEL), F32),
                   jax.ShapeDtypeStruct((t, LANES), F32),
                   jax.ShapeDtypeStruct((8, t), F32),
                   jax.ShapeDtypeStruct((1, LANES), F32)],
        scratch_shapes=[pltpu.VMEM((SSM_WIDTH // LANES, tile, LANES), F32), pltpu.VMEM((1, LANES), F32)],
        compiler_params=_params(("arbitrary",), 48),
    )(x2, attn, y3, wglu, bglu, gssm, wo_a, wo_s, gmoe, w_router, b_router)


def _row_copy(src, s, dst, d, sem):
    return pltpu.make_async_copy(src.at[pl.ds(s, 1), :], dst.at[pl.ds(d, 1), :], sem)


UNROLL = 4
SUBLANES = 8
PAD_SIZES = tuple(1 << b for b in reversed(range(3, ROW_BLOCK.bit_length() - 1)))


def _dispatch_kernel(dest_ref, pad_start_ref, pad_len_ref, h_ref, rows_out, zbuf, sem, zsem, *, tile, n_tok,
                     n_blocks):
    i = pl.program_id(0)
    base = i * tile

    def zero_fill(e, start):
        first, rem = pad_start_ref[e], pad_len_ref[e]
        for k in range(SUBLANES - 1):
            @pl.when(k < (rem & (SUBLANES - 1)))
            def _():
                cp = _row_copy(zbuf, 0, rows_out, first + k, zsem)
                cp.start() if start else cp.wait()
        end = first + rem
        for size in PAD_SIZES:
            end = end - (rem & size)

            @pl.when((rem & size) != 0)
            def _():
                off = pl.multiple_of(end, SUBLANES)
                cp = pltpu.make_async_copy(zbuf.at[pl.ds(0, size), :], rows_out.at[pl.ds(off, size), :], zsem)
                cp.start() if start else cp.wait()

    def zero_tail(start):
        used = (pad_start_ref[N_EXPERTS - 1] + pad_len_ref[N_EXPERTS - 1]) // ROW_BLOCK

        def blk(b, carry):
            for half in range(ROW_BLOCK // PAD_SIZES[0]):
                off = pl.multiple_of(b * ROW_BLOCK + half * PAD_SIZES[0], SUBLANES)
                cp = pltpu.make_async_copy(zbuf, rows_out.at[pl.ds(off, PAD_SIZES[0]), :], zsem)
                cp.start() if start else cp.wait()
            return carry

        lax.fori_loop(used, n_blocks, blk, 0)

    @pl.when(i == 0)
    def _():
        zbuf[...] = jnp.zeros(zbuf.shape, F32)
        lax.fori_loop(0, N_EXPERTS, lambda e, c: (zero_fill(e, True), c)[1], 0)
        zero_tail(True)

    def issue(tb, carry):
        for k in range(UNROLL):
            t = tb * UNROLL + k
            for j in range(2):
                _row_copy(h_ref, t, rows_out, dest_ref[j * n_tok + base + t], sem).start()
        return carry

    def drain(tb, carry):
        for _ in range(2 * UNROLL):
            _row_copy(h_ref, 0, rows_out, 0, sem).wait()
        return carry

    lax.fori_loop(0, tile // UNROLL, issue, 0)
    lax.fori_loop(0, tile // UNROLL, drain, 0)

    @pl.when(i == 0)
    def _():
        lax.fori_loop(0, N_EXPERTS, lambda e, c: (zero_fill(e, False), c)[1], 0)
        zero_tail(False)


def _dispatch(dest_flat, pad_start, pad_len, h2, n_rows, tile=256):
    t = h2.shape[0]
    kern = functools.partial(_dispatch_kernel, tile=tile, n_tok=t, n_blocks=n_rows // ROW_BLOCK)
    grid_spec = pltpu.PrefetchScalarGridSpec(
        num_scalar_prefetch=3, grid=(t // tile,),
        in_specs=[pl.BlockSpec((tile, D_MODEL), lambda i, *_: (i, 0))],
        out_specs=pl.BlockSpec(memory_space=pl.ANY),
        scratch_shapes=[pltpu.VMEM((PAD_SIZES[0], D_MODEL), F32),
                        pltpu.SemaphoreType.DMA(()), pltpu.SemaphoreType.DMA(())])
    return pl.pallas_call(
        kern, name="dispatch", grid_spec=grid_spec,
        out_shape=jax.ShapeDtypeStruct((n_rows, D_MODEL), F32),
        compiler_params=_params(("arbitrary",), 32),
    )(dest_flat, pad_start, pad_len, h2)


def _expert_kernel(be_ref, nu_ref, rows_ref, wg_ref, wu_ref, wd_ref, out_ref, wg_b, wu_b, wd_b):
    i = pl.program_id(0)
    changed = (i == 0) | (be_ref[i] != be_ref[jnp.maximum(i - 1, 0)])

    @pl.when(changed)
    def _():
        wg_b[...] = wg_ref[0].astype(BF16)
        wu_b[...] = wu_ref[0].astype(BF16)
        wd_b[...] = wd_ref[0].astype(BF16)

    @pl.when(i < nu_ref[0])
    def _():
        xb = rows_ref[...].astype(BF16)
        gate = jnp.dot(xb, wg_b[...], preferred_element_type=F32)
        up = jnp.dot(xb, wu_b[...], preferred_element_type=F32)
        hid = (gate * jax.nn.sigmoid(gate) * up).astype(BF16)
        out_ref[...] = jnp.dot(hid, wd_b[...], preferred_element_type=F32)

    @pl.when(i >= nu_ref[0])
    def _():
        out_ref[...] = jnp.zeros(out_ref.shape, F32)


def _experts(block_e, n_used, rows, w_gate, w_up, w_down):
    n_rows = rows.shape[0]
    nb = n_rows // ROW_BLOCK
    last = lambda i, nu: jnp.maximum(jnp.minimum(i, nu[0] - 1), 0)
    grid_spec = pltpu.PrefetchScalarGridSpec(
        num_scalar_prefetch=2, grid=(nb,),
        in_specs=[pl.BlockSpec((ROW_BLOCK, D_MODEL), lambda i, be, nu: (last(i, nu), 0)),
                  pl.BlockSpec((1, D_MODEL, EXPERT_FF), lambda i, be, nu: (be[i], 0, 0)),
                  pl.BlockSpec((1, D_MODEL, EXPERT_FF), lambda i, be, nu: (be[i], 0, 0)),
                  pl.BlockSpec((1, EXPERT_FF, D_MODEL), lambda i, be, nu: (be[i], 0, 0))],
        out_specs=pl.BlockSpec((ROW_BLOCK, D_MODEL), lambda i, be, nu: (i, 0)),
        scratch_shapes=[pltpu.VMEM((D_MODEL, EXPERT_FF), BF16), pltpu.VMEM((D_MODEL, EXPERT_FF), BF16),
                        pltpu.VMEM((EXPERT_FF, D_MODEL), BF16)])
    return pl.pallas_call(
        _expert_kernel, name="experts", grid_spec=grid_spec,
        out_shape=jax.ShapeDtypeStruct((n_rows, D_MODEL), F32),
        compiler_params=_params(("arbitrary",), 48),
    )(block_e, n_used, rows, w_gate, w_up, w_down)


def _combine_kernel(dest_ref, x1_ref, info_ref, gfin_ref, rows_ref, o_ref, buf0, buf1, sem, *, tile, n_tok):
    base = pl.program_id(0) * tile

    def issue(tb, carry):
        for k in range(UNROLL):
            t = tb * UNROLL + k
            _row_copy(rows_ref, dest_ref[base + t], buf0, t, sem).start()
            _row_copy(rows_ref, dest_ref[n_tok + base + t], buf1, t, sem).start()
        return carry

    def drain(tb, carry):
        for _ in range(UNROLL):
            _row_copy(rows_ref, 0, buf0, 0, sem).wait()
            _row_copy(rows_ref, 0, buf1, 0, sem).wait()
        return carry

    lax.fori_loop(0, tile // UNROLL, issue, 0)
    lax.fori_loop(0, tile // UNROLL, drain, 0)
    info = info_ref[...]
    x2 = x1_ref[...] + info[:, 2:3] * buf0[...] + info[:, 3:4] * buf1[...]
    o_ref[...] = _rms(x2, gfin_ref[...])


def _combine(dest_flat, x1, info, gfin, out_rows, tile=256):
    t = x1.shape[0]
    kern = functools.partial(_combine_kernel, tile=tile, n_tok=t)
    grid_spec = pltpu.PrefetchScalarGridSpec(
        num_scalar_prefetch=1, grid=(t // tile,),
        in_specs=[pl.BlockSpec((tile, D_MODEL), lambda i, d: (i, 0)),
                  pl.BlockSpec((tile, LANES), lambda i, d: (i, 0)),
                  pl.BlockSpec((1, D_MODEL), lambda i, d: (0, 0)),
                  pl.BlockSpec(memory_space=pl.ANY)],
        out_specs=pl.BlockSpec((tile, D_MODEL), lambda i, d: (i, 0)),
        scratch_shapes=[pltpu.VMEM((tile, D_MODEL), F32), pltpu.VMEM((tile, D_MODEL), F32),
                        pltpu.SemaphoreType.DMA(())])
    return pl.pallas_call(
        kern, name="combine", grid_spec=grid_spec,
        out_shape=jax.ShapeDtypeStruct((t, D_MODEL), F32),
        compiler_params=_params(("arbitrary",), 32),
    )(dest_flat, x1, info, gfin, out_rows)


def _split_w_in(w_in):
    def pad_blocks(w):
        w = w.reshape(D_MODEL, N_HEADS * 2, HEAD_DIM)
        return jnp.pad(w, ((0, 0), (0, 0), (0, AUG - HEAD_DIM))).reshape(D_MODEL, N_HEADS * 2 * AUG)
    wq_t = pad_blocks(w_in[:, :ATTN_WIDTH]).T.astype(BF16)
    wk = pad_blocks(w_in[:, ATTN_WIDTH:2 * ATTN_WIDTH]).astype(BF16)
    wv_t = w_in[:, 2 * ATTN_WIDTH:3 * ATTN_WIDTH].T.astype(BF16)
    wu = w_in[:, 3 * ATTN_WIDTH:].astype(BF16)
    return wq_t, wk, wv_t, wu


def kernel(x, norm_attn, w_in, lambda_q1, lambda_k1, lambda_q2, lambda_k2, attn_subln, ssm_lam_re, ssm_lam_im, ssm_log_dt, ssm_b_re, ssm_b_im, ssm_c_re, ssm_c_im, ssm_d, w_glu, b_glu, ssm_norm, w_out, norm_moe, w_router_group, b_router_group, w_router_expert, b_router_expert, w_gate, w_up, w_down, norm_final):
    batch, seq, d = x.shape
    t = batch * seq
    nchunk = seq // CHUNK
    x2 = x.reshape(t, d)
    l = 0

    q_t, k_aug, v_t, u3 = _in_proj(x2, norm_attn[l][None], *_split_w_in(w_in[l]), seq)
    attn = _attention(q_t, k_aug, v_t, lambda_q1[l][None], lambda_k1[l][None], lambda_q2[l][None],
                      lambda_k2[l][None], attn_subln[l][:, None], batch, seq)

    sc = _ssm_constants(ssm_lam_re[l], ssm_lam_im[l], ssm_log_dt[l], ssm_b_re[l], ssm_b_im[l],
                        ssm_c_re[l], ssm_c_im[l], ssm_d[l], nchunk)
    s_re, s_im = _ssm_state(u3, sc["b_re"], sc["b_im"], sc["a_row_re"], sc["a_row_im"])
    h_re, h_im = _ssm_scan(s_re, s_im, sc["p_re"], sc["p_im"], nchunk)
    y3 = _ssm_out(u3, sc["kt"], h_re, h_im, sc["c_re"], sc["c_im"], sc["a_col_re"], sc["a_col_im"])

    w_router = jnp.concatenate([w_router_group[l], w_router_expert[l]], axis=1).astype(F32)
    w_router = jnp.pad(w_router, ((0, 0), (0, LANES - w_router.shape[1])))
    w_router_hi = w_router.astype(BF16)
    w_router = jnp.concatenate([w_router_hi, (w_router - w_router_hi.astype(F32)).astype(BF16)], axis=1)
    b_router = jnp.concatenate([b_router_group[l], b_router_expert[l]]).astype(F32)
    b_router = jnp.pad(b_router, (0, LANES - b_router.shape[0]))[None]
    x1, h2, info, route_t, cnt = _mix(x2, attn, y3, w_glu[l].astype(BF16), b_glu[l][None], ssm_norm[l][None],
                                      w_out[l][:ATTN_WIDTH].astype(BF16), w_out[l][ATTN_WIDTH:].astype(BF16),
                                      norm_moe[l][None], w_router, b_router)

    experts = route_t[0:2].astype(jnp.int32)
    ranks = route_t[4:6].astype(jnp.int32)
    counts = cnt[0, :N_EXPERTS].astype(jnp.int32)
    padded = ((counts + ROW_BLOCK - 1) // ROW_BLOCK) * ROW_BLOCK
    ids = jnp.arange(N_EXPERTS, dtype=jnp.int32)
    pend = jnp.sum(jnp.where(ids[None, :] <= ids[:, None], padded[None, :], 0), axis=1)
    pstart = pend - padded
    dest = ranks
    for e in range(N_EXPERTS):
        dest = dest + jnp.where(experts == e, pstart[e], 0)
    dest = dest.reshape(-1)
    n_rows = ((2 * t + N_EXPERTS * (ROW_BLOCK - 1) + ROW_BLOCK - 1) // ROW_BLOCK) * ROW_BLOCK
    nb = n_rows // ROW_BLOCK
    n_used = (pend[-1] // ROW_BLOCK).astype(jnp.int32)
    blk = jnp.minimum(jnp.arange(nb, dtype=jnp.int32), n_used - 1) * ROW_BLOCK
    block_e = jnp.minimum(jnp.sum((pend[None, :] <= blk[:, None]).astype(jnp.int32), axis=1), N_EXPERTS - 1)

    rows = _dispatch(dest, pstart + counts, padded - counts, h2, n_rows)
    out_rows = _experts(block_e, n_used[None], rows, w_gate[l], w_up[l], w_down[l])
    out = _combine(dest, x1, info, norm_final[None], out_rows)
    return out.reshape(batch, seq, d)
```

```python
import functools
import math

import jax
import jax.numpy as jnp
import numpy as np
from jax import lax
from jax.experimental import pallas as pl
from jax.experimental.pallas import tpu as pltpu

F32 = jnp.float32
BF16 = jnp.bfloat16

D_MODEL = 1024
N_HEADS = 4
HEAD_DIM = 64
VALUE_DIM = 128
ATTN_WIDTH = 512
SSM_WIDTH = 512
SSM_GROUP = 16
N_GROUPS = 32
SSM_STATE = 64
N_EXPERT_GROUPS = 4
EXPERTS_PER_GROUP = 8
N_EXPERTS = 32
EXPERT_FF = 512
RMS_EPS = 1e-6
LAMBDA_INIT = 0.8 - 0.6 * math.exp(-0.3 * 0)
LOG2E = math.log2(math.e)

CHUNK = 16
HALF_GROUPS = 16
LANES = 128
AUG = 128
NORM_LANE = 70
V_ROWS = VALUE_DIM + 16
UNDERFLOW_LOG2 = 152.0
ROW_BLOCK = 256
VMEM_LIMIT_CAP = 56 * 1024 * 1024


def _params(dims, vmem_mb):
    return pltpu.CompilerParams(dimension_semantics=dims,
                                vmem_limit_bytes=min(vmem_mb * 1024 * 1024, VMEM_LIMIT_CAP))


def _rms(x, gain):
    return x * lax.rsqrt(jnp.mean(x * x, axis=-1, keepdims=True) + RMS_EPS) * gain


def _split3(val):
    hi = val.astype(BF16).astype(F32)
    r1 = val - hi
    mid = r1.astype(BF16).astype(F32)
    lo = r1 - mid
    return hi, mid, lo


def _inproj_kernel(x_ref, g_ref, wq_ref, wk_ref, wv_ref, wu_ref, q_ref, k_ref, v_ref, u_ref, ubuf, *, tile, seq):
    i = pl.program_id(0)
    h = _rms(x_ref[...], g_ref[...]).astype(BF16)
    nt = (((1,), (1,)), ((), ()))
    qt = lax.dot_general(wq_ref[...], h, nt, preferred_element_type=F32)
    kp = jnp.dot(h, wk_ref[...], preferred_element_type=F32)
    vt = lax.dot_general(wv_ref[...], h, nt, preferred_element_type=F32)
    ones_row = jnp.where(lax.broadcasted_iota(jnp.int32, (V_ROWS - VALUE_DIM, tile), 0) == 0, 1.0, 0.0)
    for hd in range(N_HEADS):
        v_ref[hd * V_ROWS:hd * V_ROWS + VALUE_DIM, :] = vt[hd * VALUE_DIM:(hd + 1) * VALUE_DIM, :].astype(BF16)
        v_ref[hd * V_ROWS + VALUE_DIM:(hd + 1) * V_ROWS, :] = ones_row.astype(BF16)
    up = jnp.dot(h, wu_ref[...], preferred_element_type=F32)
    pos0 = lax.rem(i * tile, seq)
    pos_k = (pos0 + lax.broadcasted_iota(jnp.int32, (tile, AUG), 0)).astype(F32)
    pos_q = (pos0 + lax.broadcasted_iota(jnp.int32, (1, tile), 1)).astype(F32)
    lane = lax.broadcasted_iota(jnp.int32, (tile, AUG), 1)
    srow = lax.broadcasted_iota(jnp.int32, (AUG, tile), 0)
    qscale = HEAD_DIM ** -0.5 * LOG2E
    for hd in range(N_HEADS):
        slope = 2.0 ** (-8.0 * (hd + 1) / N_HEADS) * LOG2E
        hi, mid, lo = _split3(pos_k * slope)
        k_add = jnp.where(lane == 64, hi,
                          jnp.where(lane == 65, mid,
                                    jnp.where(lane == 66, lo, jnp.where((lane >= 67) & (lane < 70), 1.0, 0.0))))
        hi, mid, lo = _split3(pos_q * slope)
        q_add = jnp.where(srow < 64, 0.0,
                          jnp.where(srow < 67, 1.0,
                                    jnp.where(srow == 67, -hi, jnp.where(srow == 68, -mid, jnp.where(srow == 69, -lo, 0.0)))))
        for m in range(2):
            c0 = (hd * 2 + m) * AUG
            q_ref[c0:c0 + AUG, :] = (qt[c0:c0 + AUG, :] * qscale + q_add).astype(BF16)
            kb = kp[:, c0:c0 + AUG]
            kr = kb.astype(BF16).astype(F32)
            norm2 = jnp.sum(kr * kr, axis=-1, keepdims=True) * (1.0 + 2.0 ** -6)
            k_ref[:, c0:c0 + AUG] = (kb + k_add + jnp.where(lane == NORM_LANE, norm2, 0.0)).astype(BF16)
    for cb in range(SSM_WIDTH // LANES):
        ubuf[cb] = up[:, cb * LANES:(cb + 1) * LANES]
    for s in range(CHUNK):
        for cb in range(SSM_WIDTH // LANES):
            u_ref[s, :, cb * LANES:(cb + 1) * LANES] = (
                ubuf[cb, pl.ds(s, tile // CHUNK, stride=CHUNK), :].astype(BF16))


def _in_proj(x2, gain, wq_t, wk, wv_t, wu, seq, tile=512):
    t = x2.shape[0]
    qk = 2 * N_HEADS * AUG
    kern = functools.partial(_inproj_kernel, tile=tile, seq=seq)
    full = lambda shape: pl.BlockSpec(shape, lambda i: (0, 0))
    return pl.pallas_call(
        kern, name="in_proj",
        grid=(t // tile,),
        in_specs=[pl.BlockSpec((tile, D_MODEL), lambda i: (i, 0)),
                  full((1, D_MODEL)), full((qk, D_MODEL)), full((D_MODEL, qk)),
                  full((ATTN_WIDTH, D_MODEL)), full((D_MODEL, SSM_WIDTH))],
        out_specs=[pl.BlockSpec((qk, tile), lambda i: (0, i)),
                   pl.BlockSpec((tile, qk), lambda i: (i, 0)),
                   pl.BlockSpec((N_HEADS * V_ROWS, tile), lambda i: (0, i)),
                   pl.BlockSpec((CHUNK, tile // CHUNK, SSM_WIDTH), lambda i: (0, i, 0))],
        out_shape=[jax.ShapeDtypeStruct((qk, t), BF16),
                   jax.ShapeDtypeStruct((t, qk), BF16),
                   jax.ShapeDtypeStruct((N_HEADS * V_ROWS, t), BF16),
                   jax.ShapeDtypeStruct((CHUNK, t // CHUNK, SSM_WIDTH), BF16)],
        scratch_shapes=[pltpu.VMEM((SSM_WIDTH // LANES, tile, LANES), F32)],
        compiler_params=_params(("arbitrary",), 48),
    )(x2, gain, wq_t, wk, wv_t, wu)


def _attn_kernel(lq1, lk1, lq2, lk2, sub_ref, q_ref, k_ref, v_ref, o_ref, m_sc, acc_sc, kn_sc, *, tq):
    hd = pl.program_id(1)
    qi = pl.program_id(2)
    lam = (jnp.exp(jnp.sum(lq1[...] * lk1[...], axis=-1, keepdims=True))
           - jnp.exp(jnp.sum(lq2[...] * lk2[...], axis=-1, keepdims=True)) + LAMBDA_INIT)
    m_sc[...] = jnp.full(m_sc.shape, -jnp.inf, F32)
    acc_sc[...] = jnp.zeros(acc_sc.shape, F32)

    @pl.when(qi == 0)
    def _():
        for m in range(2):
            kn_sc[m] = jnp.max(k_ref[:, m * AUG:(m + 1) * AUG].astype(F32), axis=0, keepdims=True)

    def step(start, nkeys, masked):
        r0 = pl.multiple_of(start, tq)
        kblk = k_ref[pl.ds(r0, nkeys), :]
        vblk = v_ref[:, pl.ds(r0, nkeys)]
        scores = [jnp.dot(kblk[:, m * AUG:(m + 1) * AUG], q_ref[m * AUG:(m + 1) * AUG, :],
                          preferred_element_type=F32) for m in range(2)]
        for m in range(2):
            s = scores[m]
            if masked:
                key = lax.broadcasted_iota(jnp.int32, (nkeys, tq), 0)
                qry = lax.broadcasted_iota(jnp.int32, (nkeys, tq), 1)
                s = jnp.where(key <= qry, s, -jnp.inf)
            m_prev = m_sc[m]
            m_new = jnp.maximum(m_prev, jnp.max(s, axis=0, keepdims=True))
            p = jnp.exp2(s - m_new)
            alpha = jnp.exp2(m_prev - m_new)
            acc_sc[m] = alpha * acc_sc[m] + jnp.dot(vblk, p.astype(BF16), preferred_element_type=F32)
            m_sc[m] = m_new

    step(qi * tq, tq, True)

    lane = lax.broadcasted_iota(jnp.int32, (1, AUG), 1)
    slope = LOG2E * jnp.exp2(-2.0 * (jnp.full((1, 1), hd, jnp.int32) + 1).astype(F32))
    reach = jnp.zeros((1, 1), F32)
    for m in range(2):
        qf = q_ref[m * AUG:m * AUG + HEAD_DIM, :].astype(F32)
        q2 = jnp.max(jnp.sum(qf * qf, axis=0, keepdims=True), axis=-1, keepdims=True)
        k2 = jnp.max(jnp.where(lane == NORM_LANE, kn_sc[m], 0.0), axis=-1, keepdims=True)
        m_min = jnp.min(m_sc[m], axis=-1, keepdims=True)
        reach = jnp.maximum(reach, (jnp.sqrt(q2 * k2) * 1.001 + (UNDERFLOW_LOG2 + 0.5) - m_min) / slope)
    need = jnp.ceil((reach + (tq - 1)) / tq).astype(jnp.int32) - 1
    n_below = jnp.minimum(jnp.max(jnp.maximum(need, 0)), qi)

    def pair(i, carry):
        step((qi - 2 - 2 * i) * tq, 2 * tq, False)
        return carry

    lax.fori_loop(0, n_below // 2, pair, 0)

    @pl.when(n_below % 2 == 1)
    def _():
        step((qi - n_below) * tq, tq, False)

    l0 = acc_sc[0, VALUE_DIM:VALUE_DIM + 1, :]
    l1 = acc_sc[1, VALUE_DIM:VALUE_DIM + 1, :]
    o = acc_sc[0, :VALUE_DIM, :] / l0 - lam * (acc_sc[1, :VALUE_DIM, :] / l1)
    o = o * lax.rsqrt(jnp.mean(o * o, axis=0, keepdims=True) + RMS_EPS) * sub_ref[...] * (1.0 - LAMBDA_INIT)
    o_ref[...] = o.T.astype(BF16)


def _attention(q_t, k_aug, v_t, lq1, lk1, lq2, lk2, subln_col, batch, seq, tq=512):
    tq = min(tq, seq)
    nq = seq // tq
    t = batch * seq
    small = pl.BlockSpec((1, HEAD_DIM), lambda b, h, i: (0, 0))
    kern = functools.partial(_attn_kernel, tq=tq)
    return pl.pallas_call(
        kern, name="attention",
        grid=(batch, N_HEADS, nq),
        in_specs=[small, small, small, small,
                  pl.BlockSpec((VALUE_DIM, 1), lambda b, h, i: (0, 0)),
                  pl.BlockSpec((2 * AUG, tq), lambda b, h, i: (h, b * nq + i)),
                  pl.BlockSpec((seq, 2 * AUG), lambda b, h, i: (b, h)),
                  pl.BlockSpec((V_ROWS, seq), lambda b, h, i: (h, b))],
        out_specs=pl.BlockSpec((tq, VALUE_DIM), lambda b, h, i: (b * nq + i, h)),
        out_shape=jax.ShapeDtypeStruct((t, ATTN_WIDTH), BF16),
        scratch_shapes=[pltpu.VMEM((2, 1, tq), F32), pltpu.VMEM((2, V_ROWS, tq), F32),
                        pltpu.VMEM((2, 1, AUG), F32)],
        compiler_params=_params(("arbitrary", "arbitrary", "arbitrary"), 48),
    )(lq1, lk1, lq2, lk2, subln_col, q_t, k_aug, v_t)


def _ssm_state_kernel(u_ref, bre_ref, bim_ref, are_ref, aim_ref, sre_ref, sim_ref, wre, wim):
    i = pl.program_id(1)

    @pl.when(i == 0)
    def _():
        wre[...] = bre_ref[0]
        wim[...] = bim_ref[0]
        sre_ref[...] = jnp.zeros(sre_ref.shape, F32)
        sim_ref[...] = jnp.zeros(sim_ref.shape, F32)

    @pl.when(i > 0)
    def _():
        ar, ai = are_ref[0], aim_ref[0]
        wr, wi = wre[...], wim[...]
        wre[...] = wr * ar - wi * ai
        wim[...] = wr * ai + wi * ar

    u = u_ref[0]
    sre_ref[...] += jnp.dot(u, wre[...].astype(BF16), preferred_element_type=F32)
    sim_ref[...] += jnp.dot(u, wim[...].astype(BF16), preferred_element_type=F32)


def _ssm_state(u3, b_re, b_im, a_row_re, a_row_im):
    nch = u3.shape[1]
    hw = HALF_GROUPS * SSM_GROUP
    sw = HALF_GROUPS * SSM_STATE
    return pl.pallas_call(
        _ssm_state_kernel, name="ssm_state",
        grid=(2, CHUNK),
        in_specs=[pl.BlockSpec((1, nch, hw), lambda hf, i: (CHUNK - 1 - i, 0, hf)),
                  pl.BlockSpec((1, hw, sw), lambda hf, i: (hf, 0, 0)),
                  pl.BlockSpec((1, hw, sw), lambda hf, i: (hf, 0, 0)),
                  pl.BlockSpec((1, 1, sw), lambda hf, i: (hf, 0, 0)),
                  pl.BlockSpec((1, 1, sw), lambda hf, i: (hf, 0, 0))],
        out_specs=[pl.BlockSpec((nch, sw), lambda hf, i: (0, hf)),
                   pl.BlockSpec((nch, sw), lambda hf, i: (0, hf))],
        out_shape=[jax.ShapeDtypeStruct((nch, 2 * sw), F32)] * 2,
        scratch_shapes=[pltpu.VMEM((hw, sw), F32), pltpu.VMEM((hw, sw), F32)],
        compiler_params=_params(("arbitrary", "arbitrary"), 48),
    )(u3, b_re, b_im, a_row_re, a_row_im)


def _ssm_scan_kernel(sre_ref, sim_ref, pre_ref, pim_ref, hre_ref, him_ref, *, nchunk, nsteps):
    hr, hi = sre_ref[...], sim_ref[...]
    row = lax.rem(lax.broadcasted_iota(jnp.int32, hr.shape, 0), nchunk)
    for k in range(nsteps):
        d = 1 << k
        ar, ai = pre_ref[0, k:k + 1, :], pim_ref[0, k:k + 1, :]
        keep = row >= d
        pr = jnp.where(keep, pltpu.roll(hr, d, 0), 0.0)
        pi = jnp.where(keep, pltpu.roll(hi, d, 0), 0.0)
        hr, hi = hr + ar * pr - ai * pi, hi + ar * pi + ai * pr
    keep = row >= 1
    hre_ref[...] = jnp.where(keep, pltpu.roll(hr, 1, 0), 0.0).astype(BF16)
    him_ref[...] = jnp.where(keep, pltpu.roll(hi, 1, 0), 0.0).astype(BF16)


def _ssm_scan(s_re, s_im, p_re, p_im, nchunk, cols=512):
    nrow, width = s_re.shape
    nsteps = p_re.shape[1]
    per_half = (width // 2) // cols
    kern = functools.partial(_ssm_scan_kernel, nchunk=nchunk, nsteps=nsteps)
    blk = pl.BlockSpec((nrow, cols), lambda j: (0, j))
    pblk = pl.BlockSpec((1, nsteps, cols), lambda j: (j // per_half, 0, j % per_half))
    return pl.pallas_call(
        kern, name="ssm_scan",
        grid=(width // cols,),
        in_specs=[blk, blk, pblk, pblk],
        out_specs=[blk, blk],
        out_shape=[jax.ShapeDtypeStruct((nrow, width), BF16)] * 2,
        compiler_params=_params(("arbitrary",), 48),
    )(s_re, s_im, p_re, p_im)


def _ssm_out_kernel(u_ref, kt_ref, hre_ref, him_ref, cre_ref, cim_ref, are_ref, aim_ref, y_ref, wre, wim):
    t = pl.program_id(1)
    ar, ai = are_ref[0], aim_ref[0]

    @pl.when(t == 0)
    def _():
        cr, ci = cre_ref[0], cim_ref[0]
        wre[...] = cr * ar - ci * ai
        wim[...] = cr * ai + ci * ar

    @pl.when(t > 0)
    def _():
        wr, wi = wre[...], wim[...]
        wre[...] = wr * ar - wi * ai
        wim[...] = wr * ai + wi * ar

    y_ref[0] = (jnp.dot(hre_ref[...], wre[...].astype(BF16), preferred_element_type=F32)
                - jnp.dot(him_ref[...], wim[...].astype(BF16), preferred_element_type=F32))

    def body(s, carry):
        y_ref[0] += jnp.dot(u_ref[s], kt_ref[t - s, 0], preferred_element_type=F32)
        return carry

    lax.fori_loop(0, t + 1, body, 0)


def _ssm_out(u3, kt, h_re, h_im, c_re, c_im, a_col_re, a_col_im):
    nch = u3.shape[1]
    hw = HALF_GROUPS * SSM_GROUP
    sw = HALF_GROUPS * SSM_STATE
    return pl.pallas_call(
        _ssm_out_kernel, name="ssm_out",
        grid=(2, CHUNK),
        in_specs=[pl.BlockSpec((CHUNK, nch, hw), lambda hf, t: (0, 0, hf)),
                  pl.BlockSpec((CHUNK, 1, hw, hw), lambda hf, t: (0, hf, 0, 0)),
                  pl.BlockSpec((nch, sw), lambda hf, t: (0, hf)),
                  pl.BlockSpec((nch, sw), lambda hf, t: (0, hf)),
                  pl.BlockSpec((1, sw, hw), lambda hf, t: (hf, 0, 0)),
                  pl.BlockSpec((1, sw, hw), lambda hf, t: (hf, 0, 0)),
                  pl.BlockSpec((1, sw, 1), lambda hf, t: (hf, 0, 0)),
                  pl.BlockSpec((1, sw, 1), lambda hf, t: (hf, 0, 0))],
        out_specs=pl.BlockSpec((1, nch, hw), lambda hf, t: (t, 0, hf)),
        out_shape=jax.ShapeDtypeStruct((CHUNK, nch, SSM_WIDTH), F32),
        scratch_shapes=[pltpu.VMEM((sw, hw), F32), pltpu.VMEM((sw, hw), F32)],
        compiler_params=_params(("arbitrary", "arbitrary"), 48),
    )(u3, kt, h_re, h_im, c_re, c_im, a_col_re, a_col_im)


def _ssm_constants(lam_re, lam_im, log_dt, b_re, b_im, c_re, c_im, d_skip, nchunk):
    lr, li = lam_re.astype(F32), lam_im.astype(F32)
    dt = jnp.exp(log_dt.astype(F32))[:, None]

    def lam_bar_pow(k):
        mag = jnp.exp(k * lr * dt)
        return mag * jnp.cos(k * li * dt), mag * jnp.sin(k * li * dt)

    a_re, a_im = lam_bar_pow(1.0)
    den = lr * lr + li * li
    coef_re = ((a_re - 1.0) * lr + a_im * li) / den
    coef_im = (a_im * lr - (a_re - 1.0) * li) / den
    bb_re = coef_re[..., None] * b_re.astype(F32) - coef_im[..., None] * b_im.astype(F32)
    bb_im = coef_re[..., None] * b_im.astype(F32) + coef_im[..., None] * b_re.astype(F32)
    cc_re, cc_im = c_re.astype(F32), c_im.astype(F32)
    lags = jnp.arange(CHUNK, dtype=F32)[:, None, None]
    pw_re, pw_im = lam_bar_pow(lags)
    pb_re = pw_re[..., None] * bb_re[None] - pw_im[..., None] * bb_im[None]
    pb_im = pw_re[..., None] * bb_im[None] + pw_im[..., None] * bb_re[None]
    kt = jnp.einsum('gcn,jgnd->jgdc', cc_re, pb_re) - jnp.einsum('gcn,jgnd->jgdc', cc_im, pb_im)
    skip = jnp.einsum('gc,dc->gdc', d_skip.astype(F32), jnp.eye(SSM_GROUP, dtype=F32))
    kt = jnp.concatenate([kt[:1] + skip[None], kt[1:]], axis=0)
    eye = jnp.eye(HALF_GROUPS, dtype=F32)
    hw, sw = HALF_GROUPS * SSM_GROUP, HALF_GROUPS * SSM_STATE
    kt_t = jnp.einsum('jhgdc,gk->jhgdkc', kt.reshape(CHUNK, 2, HALF_GROUPS, SSM_GROUP, SSM_GROUP), eye)
    kt_t = kt_t.reshape(CHUNK, 2, hw, hw).astype(BF16)

    def b_tiles(part):
        p = part.reshape(2, HALF_GROUPS, SSM_STATE, SSM_GROUP)
        return jnp.einsum('hgnd,gk->hgdkn', p, eye).reshape(2, hw, sw)

    def c_tiles(part):
        p = part.reshape(2, HALF_GROUPS, SSM_GROUP, SSM_STATE)
        return jnp.einsum('hgcn,gk->hgnkc', p, eye).reshape(2, sw, hw)

    nsteps = max(int(math.log2(nchunk)), 1)
    steps = (CHUNK * 2.0 ** jnp.arange(nsteps, dtype=F32))[:, None, None]
    st_re, st_im = lam_bar_pow(steps)
    by_half = lambda p: p.reshape(nsteps, 2, sw).transpose(1, 0, 2)
    ar_h, ai_h = a_re.reshape(2, sw), a_im.reshape(2, sw)
    return dict(kt=kt_t, b_re=b_tiles(bb_re), b_im=b_tiles(bb_im),
                c_re=c_tiles(cc_re), c_im=c_tiles(cc_im),
                a_row_re=ar_h[:, None, :], a_row_im=ai_h[:, None, :],
                a_col_re=ar_h[:, :, None], a_col_im=ai_h[:, :, None],
                p_re=by_half(st_re), p_im=by_half(st_im))


def _mix_kernel(x_ref, attn_ref, y3_ref, wglu_ref, bglu_ref, gssm_ref, woa_ref, wos_ref, gmoe_ref,
                wr_ref, br_ref, x1_ref, h2_ref, info_ref, rt_ref, cnt_ref, ybuf, carry, *, tile):
    i = pl.program_id(0)

    @pl.when(i == 0)
    def _():
        carry[...] = jnp.zeros(carry.shape, F32)

    for s in range(CHUNK):
        for cb in range(SSM_WIDTH // LANES):
            ybuf[cb, pl.ds(s, tile // CHUNK, stride=CHUNK), :] = y3_ref[s, :, cb * LANES:(cb + 1) * LANES]
    y = jax.nn.gelu(jnp.concatenate([ybuf[cb] for cb in range(SSM_WIDTH // LANES)], axis=-1))
    z = jnp.dot(y.astype(BF16), wglu_ref[...], preferred_element_type=F32) + bglu_ref[...]
    y = y * jax.nn.sigmoid(z)
    ssm = _rms(y, gssm_ref[...])
    x1 = (x_ref[...] + jnp.dot(attn_ref[...], woa_ref[...], preferred_element_type=F32)
          + jnp.dot(ssm.astype(BF16), wos_ref[...], preferred_element_type=F32))
    x1_ref[...] = x1
    h2 = _rms(x1, gmoe_ref[...])
    h2_ref[...] = h2

    h_hi = h2.astype(BF16)
    h_lo = (h2 - h_hi.astype(F32)).astype(BF16)
    wr = wr_ref[...]
    both = jnp.dot(h_hi, wr, preferred_element_type=F32)
    logits = (both[:, :LANES] + both[:, LANES:]
              + jnp.dot(h_lo, wr[:, :LANES], preferred_element_type=F32) + br_ref[...])
    lane = lax.broadcasted_iota(jnp.int32, logits.shape, 1)
    neg = -jnp.inf
    gl = jnp.where(lane < N_EXPERT_GROUPS, logits, neg)
    gmax = jnp.max(gl, axis=-1, keepdims=True)
    gsel = jnp.min(jnp.where(gl == gmax, lane, LANES), axis=-1, keepdims=True)
    p_group = 1.0 / jnp.sum(jnp.exp(gl - gmax), axis=-1, keepdims=True)
    elane = lane - N_EXPERT_GROUPS
    in_grp = (elane >= 0) & (elane < N_EXPERTS) & ((elane >> 3) == gsel)
    el = jnp.where(in_grp, logits, neg)
    m1 = jnp.max(el, axis=-1, keepdims=True)
    i1 = jnp.min(jnp.where(el == m1, lane, LANES), axis=-1, keepdims=True)
    den = jnp.sum(jnp.exp(el - m1), axis=-1, keepdims=True)
    el2 = jnp.where(lane == i1, neg, el)
    m2 = jnp.max(el2, axis=-1, keepdims=True)
    i2 = jnp.min(jnp.where(el2 == m2, lane, LANES), axis=-1, keepdims=True)
    g0 = p_group / den
    g1 = p_group * jnp.exp(m2 - m1) / den
    e0 = i1 - N_EXPERT_GROUPS
    e1 = i2 - N_EXPERT_GROUPS

    hit0 = lane == e0
    hit1 = lane == e1
    onehot = jnp.where(hit0 | hit1, 1.0, 0.0)
    r = lax.broadcasted_iota(jnp.int32, (tile, tile), 0)
    c = lax.broadcasted_iota(jnp.int32, (tile, tile), 1)
    tril = jnp.where(c < r, 1.0, 0.0).astype(BF16)
    before = jnp.dot(tril, onehot.astype(BF16), preferred_element_type=F32) + carry[...]
    rank0 = jnp.sum(jnp.where(hit0, before, 0.0), axis=-1, keepdims=True)
    rank1 = jnp.sum(jnp.where(hit1, before, 0.0), axis=-1, keepdims=True)
    carry[...] += jnp.sum(onehot, axis=0, keepdims=True)
    cnt_ref[...] = carry[...]
    info = jnp.where(lane == 0, e0.astype(F32),
                     jnp.where(lane == 1, e1.astype(F32),
                               jnp.where(lane == 2, g0,
                                         jnp.where(lane == 3, g1,
                                                   jnp.where(lane == 4, rank0, jnp.where(lane == 5, rank1, 0.0))))))
    info_ref[...] = info
    rt_ref[...] = info.T[:8]


def _mix(x2, attn, y3, wglu, bglu, gssm, wo_a, wo_s, gmoe, w_router, b_router, tile=512):
    t = x2.shape[0]
    kern = functools.partial(_mix_kernel, tile=tile)
    full = lambda shape: pl.BlockSpec(shape, lambda i: tuple(0 for _ in shape))
    return pl.pallas_call(
        kern, name="mix",
        grid=(t // tile,),
        in_specs=[pl.BlockSpec((tile, D_MODEL), lambda i: (i, 0)),
                  pl.BlockSpec((tile, ATTN_WIDTH), lambda i: (i, 0)),
                  pl.BlockSpec((CHUNK, tile // CHUNK, SSM_WIDTH), lambda i: (0, i, 0)),
                  full((SSM_WIDTH, SSM_WIDTH)), full((1, SSM_WIDTH)), full((1, SSM_WIDTH)),
                  full((ATTN_WIDTH, D_MODEL)), full((SSM_WIDTH, D_MODEL)), full((1, D_MODEL)),
                  full((D_MODEL, 2 * LANES)), full((1, LANES))],
        out_specs=[pl.BlockSpec((tile, D_MODEL), lambda i: (i, 0)),
                   pl.BlockSpec((tile, D_MODEL), lambda i: (i, 0)),
                   pl.BlockSpec((tile, LANES), lambda i: (i, 0)),
                   pl.BlockSpec((8, tile), lambda i: (0, i)),
                   pl.BlockSpec((1, LANES), lambda i: (0, 0))],
        out_shape=[jax.ShapeDtypeStruct((t, D_MODEL), F32),
                   jax.ShapeDtypeStruct((t, D_MODEL), F32),
                   jax.ShapeDtypeStruct((t, LANES), F32),
                   jax.ShapeDtypeStruct((8, t), F32),
                   jax.ShapeDtypeStruct((1, LANES), F32)],
        scratch_shapes=[pltpu.VMEM((SSM_WIDTH // LANES, tile, LANES), F32), pltpu.VMEM((1, LANES), F32)],
        compiler_params=_params(("arbitrary",), 48),
    )(x2, attn, y3, wglu, bglu, gssm, wo_a, wo_s, gmoe, w_router, b_router)


def _row_copy(src, s, dst, d, sem):
    return pltpu.make_async_copy(src.at[pl.ds(s, 1), :], dst.at[pl.ds(d, 1), :], sem)


UNROLL = 4
SUBLANES = 8
PAD_SIZES = tuple(1 << b for b in reversed(range(3, ROW_BLOCK.bit_length() - 1)))


def _dispatch_kernel(dest_ref, pad_start_ref, pad_len_ref, h_ref, rows_out, zbuf, sem, zsem, *, tile, n_tok,
                     n_blocks):
    i = pl.program_id(0)
    base = i * tile

    def zero_fill(e, start):
        first, rem = pad_start_ref[e], pad_len_ref[e]
        for k in range(SUBLANES - 1):
            @pl.when(k < (rem & (SUBLANES - 1)))
            def _():
                cp = _row_copy(zbuf, 0, rows_out, first + k, zsem)
                cp.start() if start else cp.wait()
        end = first + rem
        for size in PAD_SIZES:
            end = end - (rem & size)

            @pl.when((rem & size) != 0)
            def _():
                off = pl.multiple_of(end, SUBLANES)
                cp = pltpu.make_async_copy(zbuf.at[pl.ds(0, size), :], rows_out.at[pl.ds(off, size), :], zsem)
                cp.start() if start else cp.wait()

    def zero_tail(start):
        used = (pad_start_ref[N_EXPERTS - 1] + pad_len_ref[N_EXPERTS - 1]) // ROW_BLOCK

        def blk(b, carry):
            for half in range(ROW_BLOCK // PAD_SIZES[0]):
                off = pl.multiple_of(b * ROW_BLOCK + half * PAD_SIZES[0], SUBLANES)
                cp = pltpu.make_async_copy(zbuf, rows_out.at[pl.ds(off, PAD_SIZES[0]), :], zsem)
                cp.start() if start else cp.wait()
            return carry

        lax.fori_loop(used, n_blocks, blk, 0)

    @pl.when(i == 0)
    def _():
        zbuf[...] = jnp.zeros(zbuf.shape, F32)
        lax.fori_loop(0, N_EXPERTS, lambda e, c: (zero_fill(e, True), c)[1], 0)
        zero_tail(True)

    def issue(tb, carry):
        for k in range(UNROLL):
            t = tb * UNROLL + k
            for j in range(2):
                _row_copy(h_ref, t, rows_out, dest_ref[j * n_tok + base + t], sem).start()
        return carry

    def drain(tb, carry):
        for _ in range(2 * UNROLL):
            _row_copy(h_ref, 0, rows_out, 0, sem).wait()
        return carry

    lax.fori_loop(0, tile // UNROLL, issue, 0)
    lax.fori_loop(0, tile // UNROLL, drain, 0)

    @pl.when(i == 0)
    def _():
        lax.fori_loop(0, N_EXPERTS, lambda e, c: (zero_fill(e, False), c)[1], 0)
        zero_tail(False)


def _dispatch(dest_flat, pad_start, pad_len, h2, n_rows, tile=256):
    t = h2.shape[0]
    kern = functools.partial(_dispatch_kernel, tile=tile, n_tok=t, n_blocks=n_rows // ROW_BLOCK)
    grid_spec = pltpu.PrefetchScalarGridSpec(
        num_scalar_prefetch=3, grid=(t // tile,),
        in_specs=[pl.BlockSpec((tile, D_MODEL), lambda i, *_: (i, 0))],
        out_specs=pl.BlockSpec(memory_space=pl.ANY),
        scratch_shapes=[pltpu.VMEM((PAD_SIZES[0], D_MODEL), F32),
                        pltpu.SemaphoreType.DMA(()), pltpu.SemaphoreType.DMA(())])
    return pl.pallas_call(
        kern, name="dispatch", grid_spec=grid_spec,
        out_shape=jax.ShapeDtypeStruct((n_rows, D_MODEL), F32),
        compiler_params=_params(("arbitrary",), 32),
    )(dest_flat, pad_start, pad_len, h2)


def _expert_kernel(be_ref, nu_ref, rows_ref, wg_ref, wu_ref, wd_ref, out_ref, wg_b, wu_b, wd_b):
    i = pl.program_id(0)
    changed = (i == 0) | (be_ref[i] != be_ref[jnp.maximum(i - 1, 0)])

    @pl.when(changed)
    def _():
        wg_b[...] = wg_ref[0].astype(BF16)
        wu_b[...] = wu_ref[0].astype(BF16)
        wd_b[...] = wd_ref[0].astype(BF16)

    @pl.when(i < nu_ref[0])
    def _():
        xb = rows_ref[...].astype(BF16)
        gate = jnp.dot(xb, wg_b[...], preferred_element_type=F32)
        up = jnp.dot(xb, wu_b[...], preferred_element_type=F32)
        hid = (gate * jax.nn.sigmoid(gate) * up).astype(BF16)
        out_ref[...] = jnp.dot(hid, wd_b[...], preferred_element_type=F32)

    @pl.when(i >= nu_ref[0])
    def _():
        out_ref[...] = jnp.zeros(out_ref.shape, F32)


def _experts(block_e, n_used, rows, w_gate, w_up, w_down):
    n_rows = rows.shape[0]
    nb = n_rows // ROW_BLOCK
    last = lambda i, nu: jnp.maximum(jnp.minimum(i, nu[0] - 1), 0)
    grid_spec = pltpu.PrefetchScalarGridSpec(
        num_scalar_prefetch=2, grid=(nb,),
        in_specs=[pl.BlockSpec((ROW_BLOCK, D_MODEL), lambda i, be, nu: (last(i, nu), 0)),
                  pl.BlockSpec((1, D_MODEL, EXPERT_FF), lambda i, be, nu: (be[i], 0, 0)),
                  pl.BlockSpec((1, D_MODEL, EXPERT_FF), lambda i, be, nu: (be[i], 0, 0)),
                  pl.BlockSpec((1, EXPERT_FF, D_MODEL), lambda i, be, nu: (be[i], 0, 0))],
        out_specs=pl.BlockSpec((ROW_BLOCK, D_MODEL), lambda i, be, nu: (i, 0)),
        scratch_shapes=[pltpu.VMEM((D_MODEL, EXPERT_FF), BF16), pltpu.VMEM((D_MODEL, EXPERT_FF), BF16),
                        pltpu.VMEM((EXPERT_FF, D_MODEL), BF16)])
    return pl.pallas_call(
        _expert_kernel, name="experts", grid_spec=grid_spec,
        out_shape=jax.ShapeDtypeStruct((n_rows, D_MODEL), F32),
        compiler_params=_params(("arbitrary",), 48),
    )(block_e, n_used, rows, w_gate, w_up, w_down)


def _combine_kernel(dest_ref, x1_ref, info_ref, gfin_ref, rows_ref, o_ref, buf0, buf1, sem, *, tile, n_tok):
    base = pl.program_id(0) * tile

    def issue(tb, carry):
        for k in range(UNROLL):
            t = tb * UNROLL + k
            _row_copy(rows_ref, dest_ref[base + t], buf0, t, sem).start()
            _row_copy(rows_ref, dest_ref[n_tok + base + t], buf1, t, sem).start()
        return carry

    def drain(tb, carry):
        for _ in range(UNROLL):
            _row_copy(rows_ref, 0, buf0, 0, sem).wait()
            _row_copy(rows_ref, 0, buf1, 0, sem).wait()
        return carry

    lax.fori_loop(0, tile // UNROLL, issue, 0)
    lax.fori_loop(0, tile // UNROLL, drain, 0)
    info = info_ref[...]
    x2 = x1_ref[...] + info[:, 2:3] * buf0[...] + info[:, 3:4] * buf1[...]
    o_ref[...] = _rms(x2, gfin_ref[...])


def _combine(dest_flat, x1, info, gfin, out_rows, tile=256):
    t = x1.shape[0]
    kern = functools.partial(_combine_kernel, tile=tile, n_tok=t)
    grid_spec = pltpu.PrefetchScalarGridSpec(
        num_scalar_prefetch=1, grid=(t // tile,),
        in_specs=[pl.BlockSpec((tile, D_MODEL), lambda i, d: (i, 0)),
                  pl.BlockSpec((tile, LANES), lambda i, d: (i, 0)),
                  pl.BlockSpec((1, D_MODEL), lambda i, d: (0, 0)),
                  pl.BlockSpec(memory_space=pl.ANY)],
        out_specs=pl.BlockSpec((tile, D_MODEL), lambda i, d: (i, 0)),
        scratch_shapes=[pltpu.VMEM((tile, D_MODEL), F32), pltpu.VMEM((tile, D_MODEL), F32),
                        pltpu.SemaphoreType.DMA(())])
    return pl.pallas_call(
        kern, name="combine", grid_spec=grid_spec,
        out_shape=jax.ShapeDtypeStruct((t, D_MODEL), F32),
        compiler_params=_params(("arbitrary",), 32),
    )(dest_flat, x1, info, gfin, out_rows)


def _split_w_in(w_in):
    def pad_blocks(w):
        w = w.reshape(D_MODEL, N_HEADS * 2, HEAD_DIM)
        return jnp.pad(w, ((0, 0), (0, 0), (0, AUG - HEAD_DIM))).reshape(D_MODEL, N_HEADS * 2 * AUG)
    wq_t = pad_blocks(w_in[:, :ATTN_WIDTH]).T.astype(BF16)
    wk = pad_blocks(w_in[:, ATTN_WIDTH:2 * ATTN_WIDTH]).astype(BF16)
    wv_t = w_in[:, 2 * ATTN_WIDTH:3 * ATTN_WIDTH].T.astype(BF16)
    wu = w_in[:, 3 * ATTN_WIDTH:].astype(BF16)
    return wq_t, wk, wv_t, wu


def kernel(x, norm_attn, w_in, lambda_q1, lambda_k1, lambda_q2, lambda_k2, attn_subln, ssm_lam_re, ssm_lam_im, ssm_log_dt, ssm_b_re, ssm_b_im, ssm_c_re, ssm_c_im, ssm_d, w_glu, b_glu, ssm_norm, w_out, norm_moe, w_router_group, b_router_group, w_router_expert, b_router_expert, w_gate, w_up, w_down, norm_final):
    batch, seq, d = x.shape
    t = batch * seq
    nchunk = seq // CHUNK
    x2 = x.reshape(t, d)
    l = 0

    q_t, k_aug, v_t, u3 = _in_proj(x2, norm_attn[l][None], *_split_w_in(w_in[l]), seq)
    attn = _attention(q_t, k_aug, v_t, lambda_q1[l][None], lambda_k1[l][None], lambda_q2[l][None],
                      lambda_k2[l][None], attn_subln[l][:, None], batch, seq)

    sc = _ssm_constants(ssm_lam_re[l], ssm_lam_im[l], ssm_log_dt[l], ssm_b_re[l], ssm_b_im[l],
                        ssm_c_re[l], ssm_c_im[l], ssm_d[l], nchunk)
    s_re, s_im = _ssm_state(u3, sc["b_re"], sc["b_im"], sc["a_row_re"], sc["a_row_im"])
    h_re, h_im = _ssm_scan(s_re, s_im, sc["p_re"], sc["p_im"], nchunk)
    y3 = _ssm_out(u3, sc["kt"], h_re, h_im, sc["c_re"], sc["c_im"], sc["a_col_re"], sc["a_col_im"])

    w_router = jnp.concatenate([w_router_group[l], w_router_expert[l]], axis=1).astype(F32)
    w_router = jnp.pad(w_router, ((0, 0), (0, LANES - w_router.shape[1])))
    w_router_hi = w_router.astype(BF16)
    w_router = jnp.concatenate([w_router_hi, (w_router - w_router_hi.astype(F32)).astype(BF16)], axis=1)
    b_router = jnp.concatenate([b_router_group[l], b_router_expert[l]]).astype(F32)
    b_router = jnp.pad(b_router, (0, LANES - b_router.shape[0]))[None]
    x1, h2, info, route_t, cnt = _mix(x2, attn, y3, w_glu[l].astype(BF16), b_glu[l][None], ssm_norm[l][None],
                                      w_out[l][:ATTN_WIDTH].astype(BF16), w_out[l][ATTN_WIDTH:].astype(BF16),
                                      norm_moe[l][None], w_router, b_router)

    experts = route_t[0:2].astype(jnp.int32)
    ranks = route_t[4:6].astype(jnp.int32)
    counts = cnt[0, :N_EXPERTS].astype(jnp.int32)
    padded = ((counts + ROW_BLOCK - 1) // ROW_BLOCK) * ROW_BLOCK
    ids = jnp.arange(N_EXPERTS, dtype=jnp.int32)
    pend = jnp.sum(jnp.where(ids[None, :] <= ids[:, None], padded[None, :], 0), axis=1)
    pstart = pend - padded
    dest = ranks
    for e in range(N_EXPERTS):
        dest = dest + jnp.where(experts == e, pstart[e], 0)
    dest = dest.reshape(-1)
    n_rows = ((2 * t + N_EXPERTS * (ROW_BLOCK - 1) + ROW_BLOCK - 1) // ROW_BLOCK) * ROW_BLOCK
    nb = n_rows // ROW_BLOCK
    n_used = (pend[-1] // ROW_BLOCK).astype(jnp.int32)
    blk = jnp.minimum(jnp.arange(nb, dtype=jnp.int32), n_used - 1) * ROW_BLOCK
    block_e = jnp.minimum(jnp.sum((pend[None, :] <= blk[:, None]).astype(jnp.int32), axis=1), N_EXPERTS - 1)

    rows = _dispatch(dest, pstart + counts, padded - counts, h2, n_rows)
    out_rows = _experts(block_e, n_used[None], rows, w_gate[l], w_up[l], w_down[l])
    out = _combine(dest, x1, info, norm_final[None], out_rows)
    return out.reshape(batch, seq, d)
```

```python
import functools
import math

import jax
import jax.numpy as jnp
import numpy as np
from jax import lax
from jax.experimental import pallas as pl
from jax.experimental.pallas import tpu as pltpu

F32 = jnp.float32
BF16 = jnp.bfloat16

D_MODEL = 1024
N_HEADS = 4
HEAD_DIM = 64
VALUE_DIM = 128
ATTN_WIDTH = 512
SSM_WIDTH = 512
SSM_GROUP = 16
N_GROUPS = 32
SSM_STATE = 64
N_EXPERT_GROUPS = 4
EXPERTS_PER_GROUP = 8
N_EXPERTS = 32
EXPERT_FF = 512
RMS_EPS = 1e-6
LAMBDA_INIT = 0.8 - 0.6 * math.exp(-0.3 * 0)
LOG2E = math.log2(math.e)

CHUNK = 16
HALF_GROUPS = 16
LANES = 128
AUG = 128
NORM_LANE = 70
V_ROWS = VALUE_DIM + 16
UNDERFLOW_LOG2 = 152.0
ROW_BLOCK = 256
ROW_TILE = D_MODEL // 128
VMEM_LIMIT_CAP = 56 * 1024 * 1024


def _params(dims, vmem_mb):
    return pltpu.CompilerParams(dimension_semantics=dims,
                                vmem_limit_bytes=min(vmem_mb * 1024 * 1024, VMEM_LIMIT_CAP))


def _rms(x, gain):
    return x * lax.rsqrt(jnp.mean(x * x, axis=-1, keepdims=True) + RMS_EPS) * gain


def _store_tile_rows(ref, val):
    n = val.shape[0]
    for c in range(ROW_TILE):
        ref[pl.ds(c, n, stride=ROW_TILE), :] = val[:, c * LANES:(c + 1) * LANES]


def _load_tile_rows(ref):
    n = ref.shape[0] // ROW_TILE
    return jnp.concatenate([ref[pl.ds(c, n, stride=ROW_TILE), :] for c in range(ROW_TILE)], axis=-1)


def _split3(val):
    hi = val.astype(BF16).astype(F32)
    r1 = val - hi
    mid = r1.astype(BF16).astype(F32)
    lo = r1 - mid
    return hi, mid, lo


def _inproj_kernel(x_ref, g_ref, wq_ref, wk_ref, wv_ref, wu_ref, q_ref, k_ref, v_ref, u_ref, ubuf, *, tile, seq):
    i = pl.program_id(0)
    h = _rms(x_ref[...], g_ref[...]).astype(BF16)
    nt = (((1,), (1,)), ((), ()))
    qt = lax.dot_general(wq_ref[...], h, nt, preferred_element_type=F32)
    kp = jnp.dot(h, wk_ref[...], preferred_element_type=F32)
    vt = lax.dot_general(wv_ref[...], h, nt, preferred_element_type=F32)
    ones_row = jnp.where(lax.broadcasted_iota(jnp.int32, (V_ROWS - VALUE_DIM, tile), 0) == 0, 1.0, 0.0)
    for hd in range(N_HEADS):
        v_ref[hd * V_ROWS:hd * V_ROWS + VALUE_DIM, :] = vt[hd * VALUE_DIM:(hd + 1) * VALUE_DIM, :].astype(BF16)
        v_ref[hd * V_ROWS + VALUE_DIM:(hd + 1) * V_ROWS, :] = ones_row.astype(BF16)
    up = jnp.dot(h, wu_ref[...], preferred_element_type=F32)
    pos0 = lax.rem(i * tile, seq)
    pos_k = (pos0 + lax.broadcasted_iota(jnp.int32, (tile, AUG), 0)).astype(F32)
    pos_q = (pos0 + lax.broadcasted_iota(jnp.int32, (1, tile), 1)).astype(F32)
    lane = lax.broadcasted_iota(jnp.int32, (tile, AUG), 1)
    srow = lax.broadcasted_iota(jnp.int32, (AUG, tile), 0)
    qscale = HEAD_DIM ** -0.5 * LOG2E
    for hd in range(N_HEADS):
        slope = 2.0 ** (-8.0 * (hd + 1) / N_HEADS) * LOG2E
        hi, mid, lo = _split3(pos_k * slope)
        k_add = jnp.where(lane == 64, hi,
                          jnp.where(lane == 65, mid,
                                    jnp.where(lane == 66, lo, jnp.where((lane >= 67) & (lane < 70), 1.0, 0.0))))
        hi, mid, lo = _split3(pos_q * slope)
        q_add = jnp.where(srow < 64, 0.0,
                          jnp.where(srow < 67, 1.0,
                                    jnp.where(srow == 67, -hi, jnp.where(srow == 68, -mid, jnp.where(srow == 69, -lo, 0.0)))))
        for m in range(2):
            c0 = (hd * 2 + m) * AUG
            q_ref[c0:c0 + AUG, :] = (qt[c0:c0 + AUG, :] * qscale + q_add).astype(BF16)
            kb = kp[:, c0:c0 + AUG]
            kr = kb.astype(BF16).astype(F32)
            norm2 = jnp.sum(kr * kr, axis=-1, keepdims=True) * (1.0 + 2.0 ** -6)
            k_ref[:, c0:c0 + AUG] = (kb + k_add + jnp.where(lane == NORM_LANE, norm2, 0.0)).astype(BF16)
    for cb in range(SSM_WIDTH // LANES):
        ubuf[cb] = up[:, cb * LANES:(cb + 1) * LANES]
    for s in range(CHUNK):
        for cb in range(SSM_WIDTH // LANES):
            u_ref[s, :, cb * LANES:(cb + 1) * LANES] = (
                ubuf[cb, pl.ds(s, tile // CHUNK, stride=CHUNK), :].astype(BF16))


def _in_proj(x2, gain, wq_t, wk, wv_t, wu, seq, tile=512):
    t = x2.shape[0]
    qk = 2 * N_HEADS * AUG
    kern = functools.partial(_inproj_kernel, tile=tile, seq=seq)
    full = lambda shape: pl.BlockSpec(shape, lambda i: (0, 0))
    return pl.pallas_call(
        kern, name="in_proj",
        grid=(t // tile,),
        in_specs=[pl.BlockSpec((tile, D_MODEL), lambda i: (i, 0)),
                  full((1, D_MODEL)), full((qk, D_MODEL)), full((D_MODEL, qk)),
                  full((ATTN_WIDTH, D_MODEL)), full((D_MODEL, SSM_WIDTH))],
        out_specs=[pl.BlockSpec((qk, tile), lambda i: (0, i)),
                   pl.BlockSpec((tile, qk), lambda i: (i, 0)),
                   pl.BlockSpec((N_HEADS * V_ROWS, tile), lambda i: (0, i)),
                   pl.BlockSpec((CHUNK, tile // CHUNK, SSM_WIDTH), lambda i: (0, i, 0))],
        out_shape=[jax.ShapeDtypeStruct((qk, t), BF16),
                   jax.ShapeDtypeStruct((t, qk), BF16),
                   jax.ShapeDtypeStruct((N_HEADS * V_ROWS, t), BF16),
                   jax.ShapeDtypeStruct((CHUNK, t // CHUNK, SSM_WIDTH), BF16)],
        scratch_shapes=[pltpu.VMEM((SSM_WIDTH // LANES, tile, LANES), F32)],
        compiler_params=_params(("arbitrary",), 48),
    )(x2, gain, wq_t, wk, wv_t, wu)


def _attn_kernel(lq1, lk1, lq2, lk2, sub_ref, q_ref, k_ref, v_ref, o_ref, m_sc, acc_sc, kn_sc, *, tq):
    hd = pl.program_id(1)
    qi = pl.program_id(2)
    lam = (jnp.exp(jnp.sum(lq1[...] * lk1[...], axis=-1, keepdims=True))
           - jnp.exp(jnp.sum(lq2[...] * lk2[...], axis=-1, keepdims=True)) + LAMBDA_INIT)
    m_sc[...] = jnp.full(m_sc.shape, -jnp.inf, F32)
    acc_sc[...] = jnp.zeros(acc_sc.shape, F32)

    @pl.when(qi == 0)
    def _():
        for m in range(2):
            kn_sc[m] = jnp.max(k_ref[:, m * AUG:(m + 1) * AUG].astype(F32), axis=0, keepdims=True)

    def step(start, nkeys, masked):
        r0 = pl.multiple_of(start, tq)
        kblk = k_ref[pl.ds(r0, nkeys), :]
        vblk = v_ref[:, pl.ds(r0, nkeys)]
        scores = [jnp.dot(kblk[:, m * AUG:(m + 1) * AUG], q_ref[m * AUG:(m + 1) * AUG, :],
                          preferred_element_type=F32) for m in range(2)]
        for m in range(2):
            s = scores[m]
            if masked:
                key = lax.broadcasted_iota(jnp.int32, (nkeys, tq), 0)
                qry = lax.broadcasted_iota(jnp.int32, (nkeys, tq), 1)
                s = jnp.where(key <= qry, s, -jnp.inf)
            m_prev = m_sc[m]
            m_new = jnp.maximum(m_prev, jnp.max(s, axis=0, keepdims=True))
            p = jnp.exp2(s - m_new)
            alpha = jnp.exp2(m_prev - m_new)
            acc_sc[m] = alpha * acc_sc[m] + jnp.dot(vblk, p.astype(BF16), preferred_element_type=F32)
            m_sc[m] = m_new

    step(qi * tq, tq, True)

    lane = lax.broadcasted_iota(jnp.int32, (1, AUG), 1)
    slope = LOG2E * jnp.exp2(-2.0 * (jnp.full((1, 1), hd, jnp.int32) + 1).astype(F32))
    reach = jnp.zeros((1, 1), F32)
    for m in range(2):
        qf = q_ref[m * AUG:m * AUG + HEAD_DIM, :].astype(F32)
        q2 = jnp.max(jnp.sum(qf * qf, axis=0, keepdims=True), axis=-1, keepdims=True)
        k2 = jnp.max(jnp.where(lane == NORM_LANE, kn_sc[m], 0.0), axis=-1, keepdims=True)
        m_min = jnp.min(m_sc[m], axis=-1, keepdims=True)
        reach = jnp.maximum(reach, (jnp.sqrt(q2 * k2) * 1.001 + (UNDERFLOW_LOG2 + 0.5) - m_min) / slope)
    need = jnp.ceil((reach + (tq - 1)) / tq).astype(jnp.int32) - 1
    n_below = jnp.minimum(jnp.max(jnp.maximum(need, 0)), qi)

    def pair(i, carry):
        step((qi - 2 - 2 * i) * tq, 2 * tq, False)
        return carry

    lax.fori_loop(0, n_below // 2, pair, 0)

    @pl.when(n_below % 2 == 1)
    def _():
        step((qi - n_below) * tq, tq, False)

    l0 = acc_sc[0, VALUE_DIM:VALUE_DIM + 1, :]
    l1 = acc_sc[1, VALUE_DIM:VALUE_DIM + 1, :]
    o = acc_sc[0, :VALUE_DIM, :] / l0 - lam * (acc_sc[1, :VALUE_DIM, :] / l1)
    o = o * lax.rsqrt(jnp.mean(o * o, axis=0, keepdims=True) + RMS_EPS) * sub_ref[...] * (1.0 - LAMBDA_INIT)
    o_ref[...] = o.T.astype(BF16)


def _attention(q_t, k_aug, v_t, lq1, lk1, lq2, lk2, subln_col, batch, seq, tq=512):
    tq = min(tq, seq)
    nq = seq // tq
    t = batch * seq
    small = pl.BlockSpec((1, HEAD_DIM), lambda b, h, i: (0, 0))
    kern = functools.partial(_attn_kernel, tq=tq)
    return pl.pallas_call(
        kern, name="attention",
        grid=(batch, N_HEADS, nq),
        in_specs=[small, small, small, small,
                  pl.BlockSpec((VALUE_DIM, 1), lambda b, h, i: (0, 0)),
                  pl.BlockSpec((2 * AUG, tq), lambda b, h, i: (h, b * nq + i)),
                  pl.BlockSpec((seq, 2 * AUG), lambda b, h, i: (b, h)),
                  pl.BlockSpec((V_ROWS, seq), lambda b, h, i: (h, b))],
        out_specs=pl.BlockSpec((tq, VALUE_DIM), lambda b, h, i: (b * nq + i, h)),
        out_shape=jax.ShapeDtypeStruct((t, ATTN_WIDTH), BF16),
        scratch_shapes=[pltpu.VMEM((2, 1, tq), F32), pltpu.VMEM((2, V_ROWS, tq), F32),
                        pltpu.VMEM((2, 1, AUG), F32)],
        compiler_params=_params(("arbitrary", "arbitrary", "arbitrary"), 48),
    )(lq1, lk1, lq2, lk2, subln_col, q_t, k_aug, v_t)


def _ssm_state_kernel(u_ref, bre_ref, bim_ref, are_ref, aim_ref, sre_ref, sim_ref, wre, wim):
    i = pl.program_id(1)

    @pl.when(i == 0)
    def _():
        wre[...] = bre_ref[0]
        wim[...] = bim_ref[0]
        sre_ref[...] = jnp.zeros(sre_ref.shape, F32)
        sim_ref[...] = jnp.zeros(sim_ref.shape, F32)

    @pl.when(i > 0)
    def _():
        ar, ai = are_ref[0], aim_ref[0]
        wr, wi = wre[...], wim[...]
        wre[...] = wr * ar - wi * ai
        wim[...] = wr * ai + wi * ar

    u = u_ref[0]
    sre_ref[...] += jnp.dot(u, wre[...].astype(BF16), preferred_element_type=F32)
    sim_ref[...] += jnp.dot(u, wim[...].astype(BF16), preferred_element_type=F32)


def _ssm_state(u3, b_re, b_im, a_row_re, a_row_im):
    nch = u3.shape[1]
    hw = HALF_GROUPS * SSM_GROUP
    sw = HALF_GROUPS * SSM_STATE
    return pl.pallas_call(
        _ssm_state_kernel, name="ssm_state",
        grid=(2, CHUNK),
        in_specs=[pl.BlockSpec((1, nch, hw), lambda hf, i: (CHUNK - 1 - i, 0, hf)),
                  pl.BlockSpec((1, hw, sw), lambda hf, i: (hf, 0, 0)),
                  pl.BlockSpec((1, hw, sw), lambda hf, i: (hf, 0, 0)),
                  pl.BlockSpec((1, 1, sw), lambda hf, i: (hf, 0, 0)),
                  pl.BlockSpec((1, 1, sw), lambda hf, i: (hf, 0, 0))],
        out_specs=[pl.BlockSpec((nch, sw), lambda hf, i: (0, hf)),
                   pl.BlockSpec((nch, sw), lambda hf, i: (0, hf))],
        out_shape=[jax.ShapeDtypeStruct((nch, 2 * sw), F32)] * 2,
        scratch_shapes=[pltpu.VMEM((hw, sw), F32), pltpu.VMEM((hw, sw), F32)],
        compiler_params=_params(("arbitrary", "arbitrary"), 48),
    )(u3, b_re, b_im, a_row_re, a_row_im)


def _ssm_scan_kernel(sre_ref, sim_ref, pre_ref, pim_ref, hre_ref, him_ref, *, nchunk, nsteps):
    hr, hi = sre_ref[...], sim_ref[...]
    row = lax.rem(lax.broadcasted_iota(jnp.int32, hr.shape, 0), nchunk)
    for k in range(nsteps):
        d = 1 << k
        ar, ai = pre_ref[0, k:k + 1, :], pim_ref[0, k:k + 1, :]
        keep = row >= d
        pr = jnp.where(keep, pltpu.roll(hr, d, 0), 0.0)
        pi = jnp.where(keep, pltpu.roll(hi, d, 0), 0.0)
        hr, hi = hr + ar * pr - ai * pi, hi + ar * pi + ai * pr
    keep = row >= 1
    hre_ref[...] = jnp.where(keep, pltpu.roll(hr, 1, 0), 0.0).astype(BF16)
    him_ref[...] = jnp.where(keep, pltpu.roll(hi, 1, 0), 0.0).astype(BF16)


def _ssm_scan(s_re, s_im, p_re, p_im, nchunk, cols=512):
    nrow, width = s_re.shape
    nsteps = p_re.shape[1]
    per_half = (width // 2) // cols
    kern = functools.partial(_ssm_scan_kernel, nchunk=nchunk, nsteps=nsteps)
    blk = pl.BlockSpec((nrow, cols), lambda j: (0, j))
    pblk = pl.BlockSpec((1, nsteps, cols), lambda j: (j // per_half, 0, j % per_half))
    return pl.pallas_call(
        kern, name="ssm_scan",
        grid=(width // cols,),
        in_specs=[blk, blk, pblk, pblk],
        out_specs=[blk, blk],
        out_shape=[jax.ShapeDtypeStruct((nrow, width), BF16)] * 2,
        compiler_params=_params(("arbitrary",), 48),
    )(s_re, s_im, p_re, p_im)


def _ssm_out_kernel(u_ref, kt_ref, hre_ref, him_ref, cre_ref, cim_ref, are_ref, aim_ref, y_ref, wre, wim):
    t = pl.program_id(1)
    ar, ai = are_ref[0], aim_ref[0]

    @pl.when(t == 0)
    def _():
        cr, ci = cre_ref[0], cim_ref[0]
        wre[...] = cr * ar - ci * ai
        wim[...] = cr * ai + ci * ar

    @pl.when(t > 0)
    def _():
        wr, wi = wre[...], wim[...]
        wre[...] = wr * ar - wi * ai
        wim[...] = wr * ai + wi * ar

    y_ref[0] = (jnp.dot(hre_ref[...], wre[...].astype(BF16), preferred_element_type=F32)
                - jnp.dot(him_ref[...], wim[...].astype(BF16), preferred_element_type=F32))

    def body(s, carry):
        y_ref[0] += jnp.dot(u_ref[s], kt_ref[t - s, 0], preferred_element_type=F32)
        return carry

    lax.fori_loop(0, t + 1, body, 0)


def _ssm_out(u3, kt, h_re, h_im, c_re, c_im, a_col_re, a_col_im):
    nch = u3.shape[1]
    hw = HALF_GROUPS * SSM_GROUP
    sw = HALF_GROUPS * SSM_STATE
    return pl.pallas_call(
        _ssm_out_kernel, name="ssm_out",
        grid=(2, CHUNK),
        in_specs=[pl.BlockSpec((CHUNK, nch, hw), lambda hf, t: (0, 0, hf)),
                  pl.BlockSpec((CHUNK, 1, hw, hw), lambda hf, t: (0, hf, 0, 0)),
                  pl.BlockSpec((nch, sw), lambda hf, t: (0, hf)),
                  pl.BlockSpec((nch, sw), lambda hf, t: (0, hf)),
                  pl.BlockSpec((1, sw, hw), lambda hf, t: (hf, 0, 0)),
                  pl.BlockSpec((1, sw, hw), lambda hf, t: (hf, 0, 0)),
                  pl.BlockSpec((1, sw, 1), lambda hf, t: (hf, 0, 0)),
                  pl.BlockSpec((1, sw, 1), lambda hf, t: (hf, 0, 0))],
        out_specs=pl.BlockSpec((1, nch, hw), lambda hf, t: (t, 0, hf)),
        out_shape=jax.ShapeDtypeStruct((CHUNK, nch, SSM_WIDTH), F32),
        scratch_shapes=[pltpu.VMEM((sw, hw), F32), pltpu.VMEM((sw, hw), F32)],
        compiler_params=_params(("arbitrary", "arbitrary"), 48),
    )(u3, kt, h_re, h_im, c_re, c_im, a_col_re, a_col_im)


def _ssm_constants(lam_re, lam_im, log_dt, b_re, b_im, c_re, c_im, d_skip, nchunk):
    lr, li = lam_re.astype(F32), lam_im.astype(F32)
    dt = jnp.exp(log_dt.astype(F32))[:, None]

    def lam_bar_pow(k):
        mag = jnp.exp(k * lr * dt)
        return mag * jnp.cos(k * li * dt), mag * jnp.sin(k * li * dt)

    a_re, a_im = lam_bar_pow(1.0)
    den = lr * lr + li * li
    coef_re = ((a_re - 1.0) * lr + a_im * li) / den
    coef_im = (a_im * lr - (a_re - 1.0) * li) / den
    bb_re = coef_re[..., None] * b_re.astype(F32) - coef_im[..., None] * b_im.astype(F32)
    bb_im = coef_re[..., None] * b_im.astype(F32) + coef_im[..., None] * b_re.astype(F32)
    cc_re, cc_im = c_re.astype(F32), c_im.astype(F32)
    lags = jnp.arange(CHUNK, dtype=F32)[:, None, None]
    pw_re, pw_im = lam_bar_pow(lags)
    pb_re = pw_re[..., None] * bb_re[None] - pw_im[..., None] * bb_im[None]
    pb_im = pw_re[..., None] * bb_im[None] + pw_im[..., None] * bb_re[None]
    kt = jnp.einsum('gcn,jgnd->jgdc', cc_re, pb_re) - jnp.einsum('gcn,jgnd->jgdc', cc_im, pb_im)
    skip = jnp.einsum('gc,dc->gdc', d_skip.astype(F32), jnp.eye(SSM_GROUP, dtype=F32))
    kt = jnp.concatenate([kt[:1] + skip[None], kt[1:]], axis=0)
    eye = jnp.eye(HALF_GROUPS, dtype=F32)
    hw, sw = HALF_GROUPS * SSM_GROUP, HALF_GROUPS * SSM_STATE
    kt_t = jnp.einsum('jhgdc,gk->jhgdkc', kt.reshape(CHUNK, 2, HALF_GROUPS, SSM_GROUP, SSM_GROUP), eye)
    kt_t = kt_t.reshape(CHUNK, 2, hw, hw).astype(BF16)

    def b_tiles(part):
        p = part.reshape(2, HALF_GROUPS, SSM_STATE, SSM_GROUP)
        return jnp.einsum('hgnd,gk->hgdkn', p, eye).reshape(2, hw, sw)

    def c_tiles(part):
        p = part.reshape(2, HALF_GROUPS, SSM_GROUP, SSM_STATE)
        return jnp.einsum('hgcn,gk->hgnkc', p, eye).reshape(2, sw, hw)

    nsteps = max(int(math.log2(nchunk)), 1)
    steps = (CHUNK * 2.0 ** jnp.arange(nsteps, dtype=F32))[:, None, None]
    st_re, st_im = lam_bar_pow(steps)
    by_half = lambda p: p.reshape(nsteps, 2, sw).transpose(1, 0, 2)
    ar_h, ai_h = a_re.reshape(2, sw), a_im.reshape(2, sw)
    return dict(kt=kt_t, b_re=b_tiles(bb_re), b_im=b_tiles(bb_im),
                c_re=c_tiles(cc_re), c_im=c_tiles(cc_im),
                a_row_re=ar_h[:, None, :], a_row_im=ai_h[:, None, :],
                a_col_re=ar_h[:, :, None], a_col_im=ai_h[:, :, None],
                p_re=by_half(st_re), p_im=by_half(st_im))


def _mix_kernel(x_ref, attn_ref, y3_ref, wglu_ref, bglu_ref, gssm_ref, woa_ref, wos_ref, gmoe_ref,
                wr_ref, br_ref, x1_ref, h2_ref, info_ref, rt_ref, cnt_ref, ybuf, carry, *, tile):
    i = pl.program_id(0)

    @pl.when(i == 0)
    def _():
        carry[...] = jnp.zeros(carry.shape, F32)

    for s in range(CHUNK):
        for cb in range(SSM_WIDTH // LANES):
            ybuf[cb, pl.ds(s, tile // CHUNK, stride=CHUNK), :] = y3_ref[s, :, cb * LANES:(cb + 1) * LANES]
    y = jax.nn.gelu(jnp.concatenate([ybuf[cb] for cb in range(SSM_WIDTH // LANES)], axis=-1))
    z = jnp.dot(y.astype(BF16), wglu_ref[...], preferred_element_type=F32) + bglu_ref[...]
    y = y * jax.nn.sigmoid(z)
    ssm = _rms(y, gssm_ref[...])
    x1 = (x_ref[...] + jnp.dot(attn_ref[...], woa_ref[...], preferred_element_type=F32)
          + jnp.dot(ssm.astype(BF16), wos_ref[...], preferred_element_type=F32))
    x1_ref[...] = x1
    h2 = _rms(x1, gmoe_ref[...])
    _store_tile_rows(h2_ref, h2)

    h_hi = h2.astype(BF16)
    h_lo = (h2 - h_hi.astype(F32)).astype(BF16)
    wr = wr_ref[...]
    both = jnp.dot(h_hi, wr, preferred_element_type=F32)
    logits = (both[:, :LANES] + both[:, LANES:]
              + jnp.dot(h_lo, wr[:, :LANES], preferred_element_type=F32) + br_ref[...])
    lane = lax.broadcasted_iota(jnp.int32, logits.shape, 1)
    neg = -jnp.inf
    gl = jnp.where(lane < N_EXPERT_GROUPS, logits, neg)
    gmax = jnp.max(gl, axis=-1, keepdims=True)
    gsel = jnp.min(jnp.where(gl == gmax, lane, LANES), axis=-1, keepdims=True)
    p_group = 1.0 / jnp.sum(jnp.exp(gl - gmax), axis=-1, keepdims=True)
    elane = lane - N_EXPERT_GROUPS
    in_grp = (elane >= 0) & (elane < N_EXPERTS) & ((elane >> 3) == gsel)
    el = jnp.where(in_grp, logits, neg)
    m1 = jnp.max(el, axis=-1, keepdims=True)
    i1 = jnp.min(jnp.where(el == m1, lane, LANES), axis=-1, keepdims=True)
    den = jnp.sum(jnp.exp(el - m1), axis=-1, keepdims=True)
    el2 = jnp.where(lane == i1, neg, el)
    m2 = jnp.max(el2, axis=-1, keepdims=True)
    i2 = jnp.min(jnp.where(el2 == m2, lane, LANES), axis=-1, keepdims=True)
    g0 = p_group / den
    g1 = p_group * jnp.exp(m2 - m1) / den
    e0 = i1 - N_EXPERT_GROUPS
    e1 = i2 - N_EXPERT_GROUPS

    hit0 = lane == e0
    hit1 = lane == e1
    onehot = jnp.where(hit0 | hit1, 1.0, 0.0)
    r = lax.broadcasted_iota(jnp.int32, (tile, tile), 0)
    c = lax.broadcasted_iota(jnp.int32, (tile, tile), 1)
    tril = jnp.where(c < r, 1.0, 0.0).astype(BF16)
    before = jnp.dot(tril, onehot.astype(BF16), preferred_element_type=F32) + carry[...]
    rank0 = jnp.sum(jnp.where(hit0, before, 0.0), axis=-1, keepdims=True)
    rank1 = jnp.sum(jnp.where(hit1, before, 0.0), axis=-1, keepdims=True)
    carry[...] += jnp.sum(onehot, axis=0, keepdims=True)
    cnt_ref[...] = carry[...]
    info = jnp.where(lane == 0, e0.astype(F32),
                     jnp.where(lane == 1, e1.astype(F32),
                               jnp.where(lane == 2, g0,
                                         jnp.where(lane == 3, g1,
                                                   jnp.where(lane == 4, rank0, jnp.where(lane == 5, rank1, 0.0))))))
    info_ref[...] = info
    rt_ref[...] = info.T[:8]


def _mix(x2, attn, y3, wglu, bglu, gssm, wo_a, wo_s, gmoe, w_router, b_router, tile=512):
    t = x2.shape[0]
    kern = functools.partial(_mix_kernel, tile=tile)
    full = lambda shape: pl.BlockSpec(shape, lambda i: tuple(0 for _ in shape))
    return pl.pallas_call(
        kern, name="mix",
        grid=(t // tile,),
        in_specs=[pl.BlockSpec((tile, D_MODEL), lambda i: (i, 0)),
                  pl.BlockSpec((tile, ATTN_WIDTH), lambda i: (i, 0)),
                  pl.BlockSpec((CHUNK, tile // CHUNK, SSM_WIDTH), lambda i: (0, i, 0)),
                  full((SSM_WIDTH, SSM_WIDTH)), full((1, SSM_WIDTH)), full((1, SSM_WIDTH)),
                  full((ATTN_WIDTH, D_MODEL)), full((SSM_WIDTH, D_MODEL)), full((1, D_MODEL)),
                  full((D_MODEL, 2 * LANES)), full((1, LANES))],
        out_specs=[pl.BlockSpec((tile, D_MODEL), lambda i: (i, 0)),
                   pl.BlockSpec((tile * ROW_TILE, LANES), lambda i: (i, 0)),
                   pl.BlockSpec((tile, LANES), lambda i: (i, 0)),
                   pl.BlockSpec((8, tile), lambda i: (0, i)),
                   pl.BlockSpec((1, LANES), lambda i: (0, 0))],
        out_shape=[jax.ShapeDtypeStruct((t, D_MODEL), F32),
                   jax.ShapeDtypeStruct((t * ROW_TILE, LANES), F32),
                   jax.ShapeDtypeStruct((t, LANES), F32),
                   jax.ShapeDtypeStruct((8, t), F32),
                   jax.ShapeDtypeStruct((1, LANES), F32)],
        scratch_shapes=[pltpu.VMEM((SSM_WIDTH // LANES, tile, LANES), F32), pltpu.VMEM((1, LANES), F32)],
        compiler_params=_params(("arbitrary",), 48),
    )(x2, attn, y3, wglu, bglu, gssm, wo_a, wo_s, gmoe, w_router, b_router)


def _rows_at(ref, row, n_rows=1):
    return ref.at[pl.ds(pl.multiple_of(row * ROW_TILE, ROW_TILE), n_rows * ROW_TILE), :]


def _row_copy(src, s, dst, d, sem):
    return pltpu.make_async_copy(_rows_at(src, s), _rows_at(dst, d), sem)


UNROLL = 4
PAD_SIZES = tuple(1 << b for b in reversed(range(ROW_BLOCK.bit_length() - 1)))


def _dispatch_kernel(dest_ref, pad_start_ref, pad_len_ref, h_ref, rows_out, zbuf, sem, zsem, *, tile, n_tok,
                     n_blocks):
    i = pl.program_id(0)
    base = i * tile

    def zero_fill(e, start):
        off, rem = pad_start_ref[e], pad_len_ref[e]
        for size in PAD_SIZES:
            @pl.when((rem & size) != 0)
            def _():
                cp = pltpu.make_async_copy(_rows_at(zbuf, 0, size), _rows_at(rows_out, off, size), zsem)
                cp.start() if start else cp.wait()
            off = off + (rem & size)

    def zero_tail(start):
        used = (pad_start_ref[N_EXPERTS - 1] + pad_len_ref[N_EXPERTS - 1]) // ROW_BLOCK

        def blk(b, carry):
            for half in range(ROW_BLOCK // PAD_SIZES[0]):
                cp = pltpu.make_async_copy(zbuf, _rows_at(rows_out, b * ROW_BLOCK + half * PAD_SIZES[0],
                                                          PAD_SIZES[0]), zsem)
                cp.start() if start else cp.wait()
            return carry

        lax.fori_loop(used, n_blocks, blk, 0)

    @pl.when(i == 0)
    def _():
        zbuf[...] = jnp.zeros(zbuf.shape, F32)
        lax.fori_loop(0, N_EXPERTS, lambda e, c: (zero_fill(e, True), c)[1], 0)
        zero_tail(True)

    def issue(tb, carry):
        for k in range(UNROLL):
            t = tb * UNROLL + k
            for j in range(2):
                _row_copy(h_ref, t, rows_out, dest_ref[j * n_tok + base + t], sem).start()
        return carry

    def drain(tb, carry):
        for _ in range(2 * UNROLL):
            _row_copy(h_ref, 0, rows_out, 0, sem).wait()
        return carry

    lax.fori_loop(0, tile // UNROLL, issue, 0)
    lax.fori_loop(0, tile // UNROLL, drain, 0)

    @pl.when(i == 0)
    def _():
        lax.fori_loop(0, N_EXPERTS, lambda e, c: (zero_fill(e, False), c)[1], 0)
        zero_tail(False)


def _dispatch(dest_flat, pad_start, pad_len, h2, n_rows, tile=512):
    t = h2.shape[0] // ROW_TILE
    kern = functools.partial(_dispatch_kernel, tile=tile, n_tok=t, n_blocks=n_rows // ROW_BLOCK)
    grid_spec = pltpu.PrefetchScalarGridSpec(
        num_scalar_prefetch=3, grid=(t // tile,),
        in_specs=[pl.BlockSpec((tile * ROW_TILE, LANES), lambda i, *_: (i, 0))],
        out_specs=pl.BlockSpec(memory_space=pl.ANY),
        scratch_shapes=[pltpu.VMEM((PAD_SIZES[0] * ROW_TILE, LANES), F32),
                        pltpu.SemaphoreType.DMA(()), pltpu.SemaphoreType.DMA(())])
    return pl.pallas_call(
        kern, name="dispatch", grid_spec=grid_spec,
        out_shape=jax.ShapeDtypeStruct((n_rows * ROW_TILE, LANES), F32),
        compiler_params=_params(("arbitrary",), 32),
    )(dest_flat, pad_start, pad_len, h2)


def _expert_kernel(be_ref, nu_ref, rows_ref, wg_ref, wu_ref, wd_ref, out_ref, wg_b, wu_b, wd_b):
    i = pl.program_id(0)
    changed = (i == 0) | (be_ref[i] != be_ref[jnp.maximum(i - 1, 0)])

    @pl.when(changed)
    def _():
        wg_b[...] = wg_ref[0].astype(BF16)
        wu_b[...] = wu_ref[0].astype(BF16)
        wd_b[...] = wd_ref[0].astype(BF16)

    @pl.when(i < nu_ref[0])
    def _():
        xb = _load_tile_rows(rows_ref).astype(BF16)
        gate = jnp.dot(xb, wg_b[...], preferred_element_type=F32)
        up = jnp.dot(xb, wu_b[...], preferred_element_type=F32)
        hid = (gate * jax.nn.sigmoid(gate) * up).astype(BF16)
        _store_tile_rows(out_ref, jnp.dot(hid, wd_b[...], preferred_element_type=F32))

    @pl.when(i >= nu_ref[0])
    def _():
        out_ref[...] = jnp.zeros(out_ref.shape, F32)


def _experts(block_e, n_used, rows, w_gate, w_up, w_down):
    nb = rows.shape[0] // (ROW_BLOCK * ROW_TILE)
    last = lambda i, nu: jnp.maximum(jnp.minimum(i, nu[0] - 1), 0)
    grid_spec = pltpu.PrefetchScalarGridSpec(
        num_scalar_prefetch=2, grid=(nb,),
        in_specs=[pl.BlockSpec((ROW_BLOCK * ROW_TILE, LANES), lambda i, be, nu: (last(i, nu), 0)),
                  pl.BlockSpec((1, D_MODEL, EXPERT_FF), lambda i, be, nu: (be[i], 0, 0)),
                  pl.BlockSpec((1, D_MODEL, EXPERT_FF), lambda i, be, nu: (be[i], 0, 0)),
                  pl.BlockSpec((1, EXPERT_FF, D_MODEL), lambda i, be, nu: (be[i], 0, 0))],
        out_specs=pl.BlockSpec((ROW_BLOCK * ROW_TILE, LANES), lambda i, be, nu: (i, 0)),
        scratch_shapes=[pltpu.VMEM((D_MODEL, EXPERT_FF), BF16), pltpu.VMEM((D_MODEL, EXPERT_FF), BF16),
                        pltpu.VMEM((EXPERT_FF, D_MODEL), BF16)])
    return pl.pallas_call(
        _expert_kernel, name="experts", grid_spec=grid_spec,
        out_shape=jax.ShapeDtypeStruct(rows.shape, F32),
        compiler_params=_params(("arbitrary",), 48),
    )(block_e, n_used, rows, w_gate, w_up, w_down)


def _combine_kernel(dest_ref, x1_ref, info_ref, gfin_ref, rows_ref, o_ref, buf0, buf1, sems, *, tile, n_tok):
    i = pl.program_id(0)
    slot = i % 2

    def gather(step, to_slot):
        base = step * tile
        b0, b1, sem = buf0.at[to_slot], buf1.at[to_slot], sems.at[to_slot]

        def issue(tb, carry):
            for k in range(UNROLL):
                t = tb * UNROLL + k
                _row_copy(rows_ref, dest_ref[base + t], b0, t, sem).start()
                _row_copy(rows_ref, dest_ref[n_tok + base + t], b1, t, sem).start()
            return carry

        lax.fori_loop(0, tile // UNROLL, issue, 0)

    @pl.when(i == 0)
    def _():
        gather(0, 0)

    @pl.when(i + 1 < pl.num_programs(0))
    def _():
        gather(i + 1, 1 - slot)

    b0, b1, sem = buf0.at[slot], buf1.at[slot], sems.at[slot]

    def drain(tb, carry):
        for _ in range(UNROLL):
            _row_copy(rows_ref, 0, b0, 0, sem).wait()
            _row_copy(rows_ref, 0, b1, 0, sem).wait()
        return carry

    lax.fori_loop(0, tile // UNROLL, drain, 0)
    info = info_ref[...]
    x2 = x1_ref[...] + info[:, 2:3] * _load_tile_rows(b0) + info[:, 3:4] * _load_tile_rows(b1)
    o_ref[...] = _rms(x2, gfin_ref[...])


def _combine(dest_flat, x1, info, gfin, out_rows, tile=256):
    t = x1.shape[0]
    kern = functools.partial(_combine_kernel, tile=tile, n_tok=t)
    grid_spec = pltpu.PrefetchScalarGridSpec(
        num_scalar_prefetch=1, grid=(t // tile,),
        in_specs=[pl.BlockSpec((tile, D_MODEL), lambda i, d: (i, 0)),
                  pl.BlockSpec((tile, LANES), lambda i, d: (i, 0)),
                  pl.BlockSpec((1, D_MODEL), lambda i, d: (0, 0)),
                  pl.BlockSpec(memory_space=pl.ANY)],
        out_specs=pl.BlockSpec((tile, D_MODEL), lambda i, d: (i, 0)),
        scratch_shapes=[pltpu.VMEM((2, tile * ROW_TILE, LANES), F32), pltpu.VMEM((2, tile * ROW_TILE, LANES), F32),
                        pltpu.SemaphoreType.DMA((2,))])
    return pl.pallas_call(
        kern, name="combine", grid_spec=grid_spec,
        out_shape=jax.ShapeDtypeStruct((t, D_MODEL), F32),
        compiler_params=_params(("arbitrary",), 32),
    )(dest_flat, x1, info, gfin, out_rows)


def _split_w_in(w_in):
    def pad_blocks(w):
        w = w.reshape(D_MODEL, N_HEADS * 2, HEAD_DIM)
        return jnp.pad(w, ((0, 0), (0, 0), (0, AUG - HEAD_DIM))).reshape(D_MODEL, N_HEADS * 2 * AUG)
    wq_t = pad_blocks(w_in[:, :ATTN_WIDTH]).T.astype(BF16)
    wk = pad_blocks(w_in[:, ATTN_WIDTH:2 * ATTN_WIDTH]).astype(BF16)
    wv_t = w_in[:, 2 * ATTN_WIDTH:3 * ATTN_WIDTH].T.astype(BF16)
    wu = w_in[:, 3 * ATTN_WIDTH:].astype(BF16)
    return wq_t, wk, wv_t, wu


def kernel(x, norm_attn, w_in, lambda_q1, lambda_k1, lambda_q2, lambda_k2, attn_subln, ssm_lam_re, ssm_lam_im, ssm_log_dt, ssm_b_re, ssm_b_im, ssm_c_re, ssm_c_im, ssm_d, w_glu, b_glu, ssm_norm, w_out, norm_moe, w_router_group, b_router_group, w_router_expert, b_router_expert, w_gate, w_up, w_down, norm_final):
    batch, seq, d = x.shape
    t = batch * seq
    nchunk = seq // CHUNK
    x2 = x.reshape(t, d)
    l = 0

    q_t, k_aug, v_t, u3 = _in_proj(x2, norm_attn[l][None], *_split_w_in(w_in[l]), seq)
    attn = _attention(q_t, k_aug, v_t, lambda_q1[l][None], lambda_k1[l][None], lambda_q2[l][None],
                      lambda_k2[l][None], attn_subln[l][:, None], batch, seq)

    sc = _ssm_constants(ssm_lam_re[l], ssm_lam_im[l], ssm_log_dt[l], ssm_b_re[l], ssm_b_im[l],
                        ssm_c_re[l], ssm_c_im[l], ssm_d[l], nchunk)
    s_re, s_im = _ssm_state(u3, sc["b_re"], sc["b_im"], sc["a_row_re"], sc["a_row_im"])
    h_re, h_im = _ssm_scan(s_re, s_im, sc["p_re"], sc["p_im"], nchunk)
    y3 = _ssm_out(u3, sc["kt"], h_re, h_im, sc["c_re"], sc["c_im"], sc["a_col_re"], sc["a_col_im"])

    w_router = jnp.concatenate([w_router_group[l], w_router_expert[l]], axis=1).astype(F32)
    w_router = jnp.pad(w_router, ((0, 0), (0, LANES - w_router.shape[1])))
    w_router_hi = w_router.astype(BF16)
    w_router = jnp.concatenate([w_router_hi, (w_router - w_router_hi.astype(F32)).astype(BF16)], axis=1)
    b_router = jnp.concatenate([b_router_group[l], b_router_expert[l]]).astype(F32)
    b_router = jnp.pad(b_router, (0, LANES - b_router.shape[0]))[None]
    x1, h2, info, route_t, cnt = _mix(x2, attn, y3, w_glu[l].astype(BF16), b_glu[l][None], ssm_norm[l][None],
                                      w_out[l][:ATTN_WIDTH].astype(BF16), w_out[l][ATTN_WIDTH:].astype(BF16),
                                      norm_moe[l][None], w_router, b_router)

    experts = route_t[0:2].astype(jnp.int32)
    ranks = route_t[4:6].astype(jnp.int32)
    counts = cnt[0, :N_EXPERTS].astype(jnp.int32)
    padded = ((counts + ROW_BLOCK - 1) // ROW_BLOCK) * ROW_BLOCK
    ids = jnp.arange(N_EXPERTS, dtype=jnp.int32)
    pend = jnp.sum(jnp.where(ids[None, :] <= ids[:, None], padded[None, :], 0), axis=1)
    pstart = pend - padded
    dest = ranks
    for e in range(N_EXPERTS):
        dest = dest + jnp.where(experts == e, pstart[e], 0)
    dest = dest.reshape(-1)
    n_rows = ((2 * t + N_EXPERTS * (ROW_BLOCK - 1) + ROW_BLOCK - 1) // ROW_BLOCK) * ROW_BLOCK
    nb = n_rows // ROW_BLOCK
    n_used = (pend[-1] // ROW_BLOCK).astype(jnp.int32)
    blk = jnp.minimum(jnp.arange(nb, dtype=jnp.int32), n_used - 1) * ROW_BLOCK
    block_e = jnp.minimum(jnp.sum((pend[None, :] <= blk[:, None]).astype(jnp.int32), axis=1), N_EXPERTS - 1)

    rows = _dispatch(dest, pstart + counts, padded - counts, h2, n_rows)
    out_rows = _experts(block_e, n_used[None], rows, w_gate[l], w_up[l], w_down[l])
    out = _combine(dest, x1, info, norm_final[None], out_rows)
    return out.reshape(batch, seq, d)
```

```python
import functools
import math

import jax
import jax.numpy as jnp
import numpy as np
from jax import lax
from jax.experimental import pallas as pl
from jax.experimental.pallas import tpu as pltpu

F32 = jnp.float32
BF16 = jnp.bfloat16

D_MODEL = 1024
N_HEADS = 4
HEAD_DIM = 64
VALUE_DIM = 128
ATTN_WIDTH = 512
SSM_WIDTH = 512
SSM_GROUP = 16
N_GROUPS = 32
SSM_STATE = 64
N_EXPERT_GROUPS = 4
EXPERTS_PER_GROUP = 8
N_EXPERTS = 32
EXPERT_FF = 512
RMS_EPS = 1e-6
LAMBDA_INIT = 0.8 - 0.6 * math.exp(-0.3 * 0)
LOG2E = math.log2(math.e)

CHUNK = 16
HALF_GROUPS = 16
TOEP_SPAN = 4
LANES = 128
AUG = 128
NORM_LANE = 70
V_ROWS = VALUE_DIM + 16
UNDERFLOW_LOG2 = 152.0
ROW_BLOCK = 256
ROW_TILE = D_MODEL // 128
VMEM_LIMIT_CAP = 56 * 1024 * 1024


def _params(dims, vmem_mb):
    return pltpu.CompilerParams(dimension_semantics=dims,
                                vmem_limit_bytes=min(vmem_mb * 1024 * 1024, VMEM_LIMIT_CAP))


def _rms(x, gain):
    return x * lax.rsqrt(jnp.mean(x * x, axis=-1, keepdims=True) + RMS_EPS) * gain


def _store_tile_rows(ref, val):
    n = val.shape[0]
    for c in range(ROW_TILE):
        ref[pl.ds(c, n, stride=ROW_TILE), :] = val[:, c * LANES:(c + 1) * LANES]


def _load_tile_rows(ref):
    n = ref.shape[0] // ROW_TILE
    return jnp.concatenate([ref[pl.ds(c, n, stride=ROW_TILE), :] for c in range(ROW_TILE)], axis=-1)


def _split3(val):
    hi = val.astype(BF16).astype(F32)
    r1 = val - hi
    mid = r1.astype(BF16).astype(F32)
    lo = r1 - mid
    return hi, mid, lo


def _inproj_kernel(x_ref, g_ref, wq_ref, wk_ref, wv_ref, wu_ref, q_ref, k_ref, v_ref, u_ref, ubuf, *, tile, seq):
    i = pl.program_id(0)
    h = _rms(x_ref[...], g_ref[...]).astype(BF16)
    nt = (((1,), (1,)), ((), ()))
    qt = lax.dot_general(wq_ref[...], h, nt, preferred_element_type=F32)
    kp = jnp.dot(h, wk_ref[...], preferred_element_type=F32)
    vt = lax.dot_general(wv_ref[...], h, nt, preferred_element_type=F32)
    ones_row = jnp.where(lax.broadcasted_iota(jnp.int32, (V_ROWS - VALUE_DIM, tile), 0) == 0, 1.0, 0.0)
    for hd in range(N_HEADS):
        v_ref[hd * V_ROWS:hd * V_ROWS + VALUE_DIM, :] = vt[hd * VALUE_DIM:(hd + 1) * VALUE_DIM, :].astype(BF16)
        v_ref[hd * V_ROWS + VALUE_DIM:(hd + 1) * V_ROWS, :] = ones_row.astype(BF16)
    up = jnp.dot(h, wu_ref[...], preferred_element_type=F32)
    pos0 = lax.rem(i * tile, seq)
    pos_k = (pos0 + lax.broadcasted_iota(jnp.int32, (tile, AUG), 0)).astype(F32)
    pos_q = (pos0 + lax.broadcasted_iota(jnp.int32, (1, tile), 1)).astype(F32)
    lane = lax.broadcasted_iota(jnp.int32, (tile, AUG), 1)
    srow = lax.broadcasted_iota(jnp.int32, (AUG, tile), 0)
    qscale = HEAD_DIM ** -0.5 * LOG2E
    for hd in range(N_HEADS):
        slope = 2.0 ** (-8.0 * (hd + 1) / N_HEADS) * LOG2E
        hi, mid, lo = _split3(pos_k * slope)
        k_add = jnp.where(lane == 64, hi,
                          jnp.where(lane == 65, mid,
                                    jnp.where(lane == 66, lo, jnp.where((lane >= 67) & (lane < 70), 1.0, 0.0))))
        hi, mid, lo = _split3(pos_q * slope)
        q_add = jnp.where(srow < 64, 0.0,
                          jnp.where(srow < 67, 1.0,
                                    jnp.where(srow == 67, -hi, jnp.where(srow == 68, -mid, jnp.where(srow == 69, -lo, 0.0)))))
        for m in range(2):
            c0 = (hd * 2 + m) * AUG
            q_ref[c0:c0 + AUG, :] = (qt[c0:c0 + AUG, :] * qscale + q_add).astype(BF16)
            kb = kp[:, c0:c0 + AUG]
            kr = kb.astype(BF16).astype(F32)
            norm2 = jnp.sum(kr * kr, axis=-1, keepdims=True) * (1.0 + 2.0 ** -6)
            k_ref[:, c0:c0 + AUG] = (kb + k_add + jnp.where(lane == NORM_LANE, norm2, 0.0)).astype(BF16)
    for cb in range(SSM_WIDTH // LANES):
        ubuf[cb] = up[:, cb * LANES:(cb + 1) * LANES]
    per_half = HALF_GROUPS * SSM_GROUP // LANES
    for s in range(CHUNK):
        for cb in range(SSM_WIDTH // LANES):
            c0 = ((cb // per_half) * CHUNK + s) * HALF_GROUPS * SSM_GROUP + (cb % per_half) * LANES
            u_ref[:, c0:c0 + LANES] = ubuf[cb, pl.ds(s, tile // CHUNK, stride=CHUNK), :].astype(BF16)


def _in_proj(x2, gain, wq_t, wk, wv_t, wu, seq, tile=512):
    t = x2.shape[0]
    qk = 2 * N_HEADS * AUG
    kern = functools.partial(_inproj_kernel, tile=tile, seq=seq)
    full = lambda shape: pl.BlockSpec(shape, lambda i: (0, 0))
    return pl.pallas_call(
        kern, name="in_proj",
        grid=(t // tile,),
        in_specs=[pl.BlockSpec((tile, D_MODEL), lambda i: (i, 0)),
                  full((1, D_MODEL)), full((qk, D_MODEL)), full((D_MODEL, qk)),
                  full((ATTN_WIDTH, D_MODEL)), full((D_MODEL, SSM_WIDTH))],
        out_specs=[pl.BlockSpec((qk, tile), lambda i: (0, i)),
                   pl.BlockSpec((tile, qk), lambda i: (i, 0)),
                   pl.BlockSpec((N_HEADS * V_ROWS, tile), lambda i: (0, i)),
                   pl.BlockSpec((tile // CHUNK, CHUNK * SSM_WIDTH), lambda i: (i, 0))],
        out_shape=[jax.ShapeDtypeStruct((qk, t), BF16),
                   jax.ShapeDtypeStruct((t, qk), BF16),
                   jax.ShapeDtypeStruct((N_HEADS * V_ROWS, t), BF16),
                   jax.ShapeDtypeStruct((t // CHUNK, CHUNK * SSM_WIDTH), BF16)],
        scratch_shapes=[pltpu.VMEM((SSM_WIDTH // LANES, tile, LANES), F32)],
        compiler_params=_params(("arbitrary",), 48),
    )(x2, gain, wq_t, wk, wv_t, wu)


def _attn_kernel(lq1, lk1, lq2, lk2, sub_ref, q_ref, k_ref, v_ref, o_ref, m_sc, acc_sc, kn_sc, *, tq):
    hd = pl.program_id(1)
    qi = pl.program_id(2)
    lam = (jnp.exp(jnp.sum(lq1[...] * lk1[...], axis=-1, keepdims=True))
           - jnp.exp(jnp.sum(lq2[...] * lk2[...], axis=-1, keepdims=True)) + LAMBDA_INIT)
    m_sc[...] = jnp.full(m_sc.shape, -jnp.inf, F32)
    acc_sc[...] = jnp.zeros(acc_sc.shape, F32)

    @pl.when(qi == 0)
    def _():
        for m in range(2):
            kn_sc[m] = jnp.max(k_ref[:, m * AUG:(m + 1) * AUG].astype(F32), axis=0, keepdims=True)

    def step(start, nkeys, masked):
        r0 = pl.multiple_of(start, tq)
        kblk = k_ref[pl.ds(r0, nkeys), :]
        vblk = v_ref[:, pl.ds(r0, nkeys)]
        scores = [jnp.dot(kblk[:, m * AUG:(m + 1) * AUG], q_ref[m * AUG:(m + 1) * AUG, :],
                          preferred_element_type=F32) for m in range(2)]
        for m in range(2):
            s = scores[m]
            if masked:
                key = lax.broadcasted_iota(jnp.int32, (nkeys, tq), 0)
                qry = lax.broadcasted_iota(jnp.int32, (nkeys, tq), 1)
                s = jnp.where(key <= qry, s, -jnp.inf)
            m_prev = m_sc[m]
            m_new = jnp.maximum(m_prev, jnp.max(s, axis=0, keepdims=True))
            p = jnp.exp2(s - m_new)
            alpha = jnp.exp2(m_prev - m_new)
            acc_sc[m] = alpha * acc_sc[m] + jnp.dot(vblk, p.astype(BF16), preferred_element_type=F32)
            m_sc[m] = m_new

    step(qi * tq, tq, True)

    lane = lax.broadcasted_iota(jnp.int32, (1, AUG), 1)
    slope = LOG2E * jnp.exp2(-2.0 * (jnp.full((1, 1), hd, jnp.int32) + 1).astype(F32))
    reach = jnp.zeros((1, 1), F32)
    for m in range(2):
        qf = q_ref[m * AUG:m * AUG + HEAD_DIM, :].astype(F32)
        q2 = jnp.max(jnp.sum(qf * qf, axis=0, keepdims=True), axis=-1, keepdims=True)
        k2 = jnp.max(jnp.where(lane == NORM_LANE, kn_sc[m], 0.0), axis=-1, keepdims=True)
        m_min = jnp.min(m_sc[m], axis=-1, keepdims=True)
        reach = jnp.maximum(reach, (jnp.sqrt(q2 * k2) * 1.001 + (UNDERFLOW_LOG2 + 0.5) - m_min) / slope)
    need = jnp.ceil((reach + (tq - 1)) / tq).astype(jnp.int32) - 1
    n_below = jnp.minimum(jnp.max(jnp.maximum(need, 0)), qi)

    def pair(i, carry):
        step((qi - 2 - 2 * i) * tq, 2 * tq, False)
        return carry

    lax.fori_loop(0, n_below // 2, pair, 0)

    @pl.when(n_below % 2 == 1)
    def _():
        step((qi - n_below) * tq, tq, False)

    l0 = acc_sc[0, VALUE_DIM:VALUE_DIM + 1, :]
    l1 = acc_sc[1, VALUE_DIM:VALUE_DIM + 1, :]
    o = acc_sc[0, :VALUE_DIM, :] / l0 - lam * (acc_sc[1, :VALUE_DIM, :] / l1)
    o = o * lax.rsqrt(jnp.mean(o * o, axis=0, keepdims=True) + RMS_EPS) * sub_ref[...] * (1.0 - LAMBDA_INIT)
    o_ref[...] = o.T.astype(BF16)


def _attention(q_t, k_aug, v_t, lq1, lk1, lq2, lk2, subln_col, batch, seq, tq=512):
    tq = min(tq, seq)
    nq = seq // tq
    t = batch * seq
    small = pl.BlockSpec((1, HEAD_DIM), lambda b, h, i: (0, 0))
    kern = functools.partial(_attn_kernel, tq=tq)
    return pl.pallas_call(
        kern, name="attention",
        grid=(batch, N_HEADS, nq),
        in_specs=[small, small, small, small,
                  pl.BlockSpec((VALUE_DIM, 1), lambda b, h, i: (0, 0)),
                  pl.BlockSpec((2 * AUG, tq), lambda b, h, i: (h, b * nq + i)),
                  pl.BlockSpec((seq, 2 * AUG), lambda b, h, i: (b, h)),
                  pl.BlockSpec((V_ROWS, seq), lambda b, h, i: (h, b))],
        out_specs=pl.BlockSpec((tq, VALUE_DIM), lambda b, h, i: (b * nq + i, h)),
        out_shape=jax.ShapeDtypeStruct((t, ATTN_WIDTH), BF16),
        scratch_shapes=[pltpu.VMEM((2, 1, tq), F32), pltpu.VMEM((2, V_ROWS, tq), F32),
                        pltpu.VMEM((2, 1, AUG), F32)],
        compiler_params=_params(("arbitrary", "arbitrary", "arbitrary"), 48),
    )(lq1, lk1, lq2, lk2, subln_col, q_t, k_aug, v_t)


def _ssm_state_kernel(u_ref, bre_ref, bim_ref, are_ref, aim_ref, sre_ref, sim_ref, wre, wim):
    i = pl.program_id(1)

    @pl.when(i == 0)
    def _():
        wre[...] = bre_ref[0]
        wim[...] = bim_ref[0]
        sre_ref[...] = jnp.zeros(sre_ref.shape, F32)
        sim_ref[...] = jnp.zeros(sim_ref.shape, F32)

    @pl.when(i > 0)
    def _():
        ar, ai = are_ref[0], aim_ref[0]
        wr, wi = wre[...], wim[...]
        wre[...] = wr * ar - wi * ai
        wim[...] = wr * ai + wi * ar

    u = u_ref[...]
    sre_ref[...] += jnp.dot(u, wre[...].astype(BF16), preferred_element_type=F32)
    sim_ref[...] += jnp.dot(u, wim[...].astype(BF16), preferred_element_type=F32)


def _ssm_state(u3, b_re, b_im, a_row_re, a_row_im):
    nch = u3.shape[0]
    hw = HALF_GROUPS * SSM_GROUP
    sw = HALF_GROUPS * SSM_STATE
    return pl.pallas_call(
        _ssm_state_kernel, name="ssm_state",
        grid=(2, CHUNK),
        in_specs=[pl.BlockSpec((nch, hw), lambda hf, i: (0, hf * CHUNK + CHUNK - 1 - i)),
                  pl.BlockSpec((1, hw, sw), lambda hf, i: (hf, 0, 0)),
                  pl.BlockSpec((1, hw, sw), lambda hf, i: (hf, 0, 0)),
                  pl.BlockSpec((1, 1, sw), lambda hf, i: (hf, 0, 0)),
                  pl.BlockSpec((1, 1, sw), lambda hf, i: (hf, 0, 0))],
        out_specs=[pl.BlockSpec((nch, sw), lambda hf, i: (0, hf)),
                   pl.BlockSpec((nch, sw), lambda hf, i: (0, hf))],
        out_shape=[jax.ShapeDtypeStruct((nch, 2 * sw), F32)] * 2,
        scratch_shapes=[pltpu.VMEM((hw, sw), F32), pltpu.VMEM((hw, sw), F32)],
        compiler_params=_params(("arbitrary", "arbitrary"), 48),
    )(u3, b_re, b_im, a_row_re, a_row_im)


def _ssm_scan_kernel(sre_ref, sim_ref, pre_ref, pim_ref, hre_ref, him_ref, *, nchunk, nsteps):
    hr, hi = sre_ref[...], sim_ref[...]
    row = lax.rem(lax.broadcasted_iota(jnp.int32, hr.shape, 0), nchunk)
    for k in range(nsteps):
        d = 1 << k
        ar, ai = pre_ref[0, k:k + 1, :], pim_ref[0, k:k + 1, :]
        keep = row >= d
        pr = jnp.where(keep, pltpu.roll(hr, d, 0), 0.0)
        pi = jnp.where(keep, pltpu.roll(hi, d, 0), 0.0)
        hr, hi = hr + ar * pr - ai * pi, hi + ar * pi + ai * pr
    keep = row >= 1
    hre_ref[...] = jnp.where(keep, pltpu.roll(hr, 1, 0), 0.0).astype(BF16)
    him_ref[...] = jnp.where(keep, pltpu.roll(hi, 1, 0), 0.0).astype(BF16)


def _ssm_scan(s_re, s_im, p_re, p_im, nchunk, cols=512):
    nrow, width = s_re.shape
    nsteps = p_re.shape[1]
    per_half = (width // 2) // cols
    kern = functools.partial(_ssm_scan_kernel, nchunk=nchunk, nsteps=nsteps)
    blk = pl.BlockSpec((nrow, cols), lambda j: (0, j))
    pblk = pl.BlockSpec((1, nsteps, cols), lambda j: (j // per_half, 0, j % per_half))
    return pl.pallas_call(
        kern, name="ssm_scan",
        grid=(width // cols,),
        in_specs=[blk, blk, pblk, pblk],
        out_specs=[blk, blk],
        out_shape=[jax.ShapeDtypeStruct((nrow, width), BF16)] * 2,
        compiler_params=_params(("arbitrary",), 48),
    )(s_re, s_im, p_re, p_im)


def _ssm_out_kernel(u_ref, kt_ref, hre_ref, him_ref, cre_ref, cim_ref, are_ref, aim_ref, y_ref, wre, wim, toep):
    t = pl.program_id(1)
    ar, ai = are_ref[0], aim_ref[0]

    @pl.when(t == 0)
    def _():
        cr, ci = cre_ref[0], cim_ref[0]
        wre[...] = cr * ar - ci * ai
        wim[...] = cr * ai + ci * ar

    @pl.when(t > 0)
    def _():
        wr, wi = wre[...], wim[...]
        wre[...] = wr * ar - wi * ai
        wim[...] = wr * ai + wi * ar

    hw = kt_ref.shape[-1]
    for s in range(CHUNK):
        tile = kt_ref[jnp.maximum(t - s, 0), 0]
        toep[s * hw:(s + 1) * hw, :] = jnp.where(s <= t, tile, jnp.zeros_like(tile))

    span = TOEP_SPAN * hw
    y_ref[0] = (jnp.dot(u_ref[:, :span], toep[:span, :], preferred_element_type=F32)
                + jnp.dot(hre_ref[...], wre[...].astype(BF16), preferred_element_type=F32)
                - jnp.dot(him_ref[...], wim[...].astype(BF16), preferred_element_type=F32))
    for piece in range(1, CHUNK // TOEP_SPAN):
        @pl.when(t >= piece * TOEP_SPAN)
        def _():
            lo = piece * span
            y_ref[0] += jnp.dot(u_ref[:, lo:lo + span], toep[lo:lo + span, :], preferred_element_type=F32)


def _ssm_out(u3, kt, h_re, h_im, c_re, c_im, a_col_re, a_col_im):
    nch = u3.shape[0]
    hw = HALF_GROUPS * SSM_GROUP
    sw = HALF_GROUPS * SSM_STATE
    return pl.pallas_call(
        _ssm_out_kernel, name="ssm_out",
        grid=(2, CHUNK),
        in_specs=[pl.BlockSpec((nch, CHUNK * hw), lambda hf, t: (0, hf)),
                  pl.BlockSpec((CHUNK, 1, hw, hw), lambda hf, t: (0, hf, 0, 0)),
                  pl.BlockSpec((nch, sw), lambda hf, t: (0, hf)),
                  pl.BlockSpec((nch, sw), lambda hf, t: (0, hf)),
                  pl.BlockSpec((1, sw, hw), lambda hf, t: (hf, 0, 0)),
                  pl.BlockSpec((1, sw, hw), lambda hf, t: (hf, 0, 0)),
                  pl.BlockSpec((1, sw, 1), lambda hf, t: (hf, 0, 0)),
                  pl.BlockSpec((1, sw, 1), lambda hf, t: (hf, 0, 0))],
        out_specs=pl.BlockSpec((1, nch, hw), lambda hf, t: (t, 0, hf)),
        out_shape=jax.ShapeDtypeStruct((CHUNK, nch, SSM_WIDTH), F32),
        scratch_shapes=[pltpu.VMEM((sw, hw), F32), pltpu.VMEM((sw, hw), F32), pltpu.VMEM((CHUNK * hw, hw), BF16)],
        compiler_params=_params(("arbitrary", "arbitrary"), 48),
    )(u3, kt, h_re, h_im, c_re, c_im, a_col_re, a_col_im)


def _ssm_constants(lam_re, lam_im, log_dt, b_re, b_im, c_re, c_im, d_skip, nchunk):
    lr, li = lam_re.astype(F32), lam_im.astype(F32)
    dt = jnp.exp(log_dt.astype(F32))[:, None]

    def lam_bar_pow(k):
        mag = jnp.exp(k * lr * dt)
        return mag * jnp.cos(k * li * dt), mag * jnp.sin(k * li * dt)

    a_re, a_im = lam_bar_pow(1.0)
    den = lr * lr + li * li
    coef_re = ((a_re - 1.0) * lr + a_im * li) / den
    coef_im = (a_im * lr - (a_re - 1.0) * li) / den
    bb_re = coef_re[..., None] * b_re.astype(F32) - coef_im[..., None] * b_im.astype(F32)
    bb_im = coef_re[..., None] * b_im.astype(F32) + coef_im[..., None] * b_re.astype(F32)
    cc_re, cc_im = c_re.astype(F32), c_im.astype(F32)
    lags = jnp.arange(CHUNK, dtype=F32)[:, None, None]
    pw_re, pw_im = lam_bar_pow(lags)
    pb_re = pw_re[..., None] * bb_re[None] - pw_im[..., None] * bb_im[None]
    pb_im = pw_re[..., None] * bb_im[None] + pw_im[..., None] * bb_re[None]
    kt = jnp.einsum('gcn,jgnd->jgdc', cc_re, pb_re) - jnp.einsum('gcn,jgnd->jgdc', cc_im, pb_im)
    skip = jnp.einsum('gc,dc->gdc', d_skip.astype(F32), jnp.eye(SSM_GROUP, dtype=F32))
    kt = jnp.concatenate([kt[:1] + skip[None], kt[1:]], axis=0)
    hw, sw = HALF_GROUPS * SSM_GROUP, HALF_GROUPS * SSM_STATE

    def block_diag(rows, row_group, col_group):
        wide = jnp.tile(rows, (1,) * (rows.ndim - 1) + (HALF_GROUPS,))
        r = lax.broadcasted_iota(jnp.int32, wide.shape, wide.ndim - 2) // row_group
        c = lax.broadcasted_iota(jnp.int32, wide.shape, wide.ndim - 1) // col_group
        return jnp.where(r == c, wide, 0.0)

    kt_t = block_diag(kt.reshape(CHUNK, 2, hw, SSM_GROUP), SSM_GROUP, SSM_GROUP).astype(BF16)

    def b_tiles(part):
        p = part.reshape(2, HALF_GROUPS, SSM_STATE, SSM_GROUP).transpose(0, 1, 3, 2)
        return block_diag(p.reshape(2, hw, SSM_STATE), SSM_GROUP, SSM_STATE)

    def c_tiles(part):
        p = part.reshape(2, HALF_GROUPS, SSM_GROUP, SSM_STATE).transpose(0, 1, 3, 2)
        return block_diag(p.reshape(2, sw, SSM_GROUP), SSM_STATE, SSM_GROUP)

    nsteps = max(int(math.log2(nchunk)), 1)
    steps = (CHUNK * 2.0 ** jnp.arange(nsteps, dtype=F32))[:, None, None]
    st_re, st_im = lam_bar_pow(steps)
    by_half = lambda p: p.reshape(nsteps, 2, sw).transpose(1, 0, 2)
    ar_h, ai_h = a_re.reshape(2, sw), a_im.reshape(2, sw)
    return dict(kt=kt_t, b_re=b_tiles(bb_re), b_im=b_tiles(bb_im),
                c_re=c_tiles(cc_re), c_im=c_tiles(cc_im),
                a_row_re=ar_h[:, None, :], a_row_im=ai_h[:, None, :],
                a_col_re=ar_h[:, :, None], a_col_im=ai_h[:, :, None],
                p_re=by_half(st_re), p_im=by_half(st_im))


def _mix_kernel(x_ref, attn_ref, y3_ref, wglu_ref, bglu_ref, gssm_ref, woa_ref, wos_ref, gmoe_ref,
                wr_ref, br_ref, x1_ref, h2_ref, info_ref, rt_ref, cnt_ref, ybuf, carry, *, tile):
    i = pl.program_id(0)

    @pl.when(i == 0)
    def _():
        carry[...] = jnp.zeros(carry.shape, F32)

    for s in range(CHUNK):
        for cb in range(SSM_WIDTH // LANES):
            ybuf[cb, pl.ds(s, tile // CHUNK, stride=CHUNK), :] = y3_ref[s, :, cb * LANES:(cb + 1) * LANES]
    y = jax.nn.gelu(jnp.concatenate([ybuf[cb] for cb in range(SSM_WIDTH // LANES)], axis=-1))
    z = jnp.dot(y.astype(BF16), wglu_ref[...], preferred_element_type=F32) + bglu_ref[...]
    y = y * jax.nn.sigmoid(z)
    ssm = _rms(y, gssm_ref[...])
    x1 = (x_ref[...] + jnp.dot(attn_ref[...], woa_ref[...], preferred_element_type=F32)
          + jnp.dot(ssm.astype(BF16), wos_ref[...], preferred_element_type=F32))
    x1_ref[...] = x1
    h2 = _rms(x1, gmoe_ref[...])
    _store_tile_rows(h2_ref, h2)

    h_hi = h2.astype(BF16)
    h_lo = (h2 - h_hi.astype(F32)).astype(BF16)
    wr = wr_ref[...]
    both = jnp.dot(h_hi, wr, preferred_element_type=F32)
    logits = (both[:, :LANES] + both[:, LANES:]
              + jnp.dot(h_lo, wr[:, :LANES], preferred_element_type=F32) + br_ref[...])
    lane = lax.broadcasted_iota(jnp.int32, logits.shape, 1)
    neg = -jnp.inf
    gl = jnp.where(lane < N_EXPERT_GROUPS, logits, neg)
    gmax = jnp.max(gl, axis=-1, keepdims=True)
    gsel = jnp.min(jnp.where(gl == gmax, lane, LANES), axis=-1, keepdims=True)
    p_group = 1.0 / jnp.sum(jnp.exp(gl - gmax), axis=-1, keepdims=True)
    elane = lane - N_EXPERT_GROUPS
    in_grp = (elane >= 0) & (elane < N_EXPERTS) & ((elane >> 3) == gsel)
    el = jnp.where(in_grp, logits, neg)
    m1 = jnp.max(el, axis=-1, keepdims=True)
    i1 = jnp.min(jnp.where(el == m1, lane, LANES), axis=-1, keepdims=True)
    den = jnp.sum(jnp.exp(el - m1), axis=-1, keepdims=True)
    el2 = jnp.where(lane == i1, neg, el)
    m2 = jnp.max(el2, axis=-1, keepdims=True)
    i2 = jnp.min(jnp.where(el2 == m2, lane, LANES), axis=-1, keepdims=True)
    g0 = p_group / den
    g1 = p_group * jnp.exp(m2 - m1) / den
    e0 = i1 - N_EXPERT_GROUPS
    e1 = i2 - N_EXPERT_GROUPS

    hit0 = lane == e0
    hit1 = lane == e1
    onehot = jnp.where(hit0 | hit1, 1.0, 0.0)
    r = lax.broadcasted_iota(jnp.int32, (tile, tile), 0)
    c = lax.broadcasted_iota(jnp.int32, (tile, tile), 1)
    tril = jnp.where(c < r, 1.0, 0.0).astype(BF16)
    before = jnp.dot(tril, onehot.astype(BF16), preferred_element_type=F32) + carry[...]
    rank0 = jnp.sum(jnp.where(hit0, before, 0.0), axis=-1, keepdims=True)
    rank1 = jnp.sum(jnp.where(hit1, before, 0.0), axis=-1, keepdims=True)
    carry[...] += jnp.sum(onehot, axis=0, keepdims=True)
    cnt_ref[...] = carry[...]
    info = jnp.where(lane == 0, e0.astype(F32),
                     jnp.where(lane == 1, e1.astype(F32),
                               jnp.where(lane == 2, g0,
                                         jnp.where(lane == 3, g1,
                                                   jnp.where(lane == 4, rank0, jnp.where(lane == 5, rank1, 0.0))))))
    info_ref[...] = info
    rt_ref[...] = info.T[:8]


def _mix(x2, attn, y3, wglu, bglu, gssm, wo_a, wo_s, gmoe, w_router, b_router, tile=512):
    t = x2.shape[0]
    kern = functools.partial(_mix_kernel, tile=tile)
    full = lambda shape: pl.BlockSpec(shape, lambda i: tuple(0 for _ in shape))
    return pl.pallas_call(
        kern, name="mix",
        grid=(t // tile,),
        in_specs=[pl.BlockSpec((tile, D_MODEL), lambda i: (i, 0)),
                  pl.BlockSpec((tile, ATTN_WIDTH), lambda i: (i, 0)),
                  pl.BlockSpec((CHUNK, tile // CHUNK, SSM_WIDTH), lambda i: (0, i, 0)),
                  full((SSM_WIDTH, SSM_WIDTH)), full((1, SSM_WIDTH)), full((1, SSM_WIDTH)),
                  full((ATTN_WIDTH, D_MODEL)), full((SSM_WIDTH, D_MODEL)), full((1, D_MODEL)),
                  full((D_MODEL, 2 * LANES)), full((1, LANES))],
        out_specs=[pl.BlockSpec((tile, D_MODEL), lambda i: (i, 0)),
                   pl.BlockSpec((tile * ROW_TILE, LANES), lambda i: (i, 0)),
                   pl.BlockSpec((tile, LANES), lambda i: (i, 0)),
                   pl.BlockSpec((8, tile), lambda i: (0, i)),
                   pl.BlockSpec((1, LANES), lambda i: (0, 0))],
        out_shape=[jax.ShapeDtypeStruct((t, D_MODEL), F32),
                   jax.ShapeDtypeStruct((t * ROW_TILE, LANES), F32),
                   jax.ShapeDtypeStruct((t, LANES), F32),
                   jax.ShapeDtypeStruct((8, t), F32),
                   jax.ShapeDtypeStruct((1, LANES), F32)],
        scratch_shapes=[pltpu.VMEM((SSM_WIDTH // LANES, tile, LANES), F32), pltpu.VMEM((1, LANES), F32)],
        compiler_params=_params(("arbitrary",), 48),
    )(x2, attn, y3, wglu, bglu, gssm, wo_a, wo_s, gmoe, w_router, b_router)


def _rows_at(ref, row, n_rows=1):
    return ref.at[pl.ds(pl.multiple_of(row * ROW_TILE, ROW_TILE), n_rows * ROW_TILE), :]


def _row_copy(src, s, dst, d, sem):
    return pltpu.make_async_copy(_rows_at(src, s), _rows_at(dst, d), sem)


UNROLL = 4
PAD_SIZES = tuple(1 << b for b in reversed(range(ROW_BLOCK.bit_length() - 1)))


def _dispatch_kernel(dest_ref, pad_start_ref, pad_len_ref, h_ref, rows_out, zbuf, sem, zsem, *, tile, n_tok,
                     n_blocks):
    i = pl.program_id(0)
    base = i * tile

    def zero_fill(e, start):
        off, rem = pad_start_ref[e], pad_len_ref[e]
        for size in PAD_SIZES:
            @pl.when((rem & size) != 0)
            def _():
                cp = pltpu.make_async_copy(_rows_at(zbuf, 0, size), _rows_at(rows_out, off, size), zsem)
                cp.start() if start else cp.wait()
            off = off + (rem & size)

    def zero_tail(start):
        used = (pad_start_ref[N_EXPERTS - 1] + pad_len_ref[N_EXPERTS - 1]) // ROW_BLOCK

        def blk(b, carry):
            for half in range(ROW_BLOCK // PAD_SIZES[0]):
                cp = pltpu.make_async_copy(zbuf, _rows_at(rows_out, b * ROW_BLOCK + half * PAD_SIZES[0],
                                                          PAD_SIZES[0]), zsem)
                cp.start() if start else cp.wait()
            return carry

        lax.fori_loop(used, n_blocks, blk, 0)

    @pl.when(i == 0)
    def _():
        zbuf[...] = jnp.zeros(zbuf.shape, F32)
        lax.fori_loop(0, N_EXPERTS, lambda e, c: (zero_fill(e, True), c)[1], 0)
        zero_tail(True)

    def issue(tb, carry):
        for k in range(UNROLL):
            t = tb * UNROLL + k
            for j in range(2):
                _row_copy(h_ref, t, rows_out, dest_ref[j * n_tok + base + t], sem).start(priority=j)
        return carry

    def drain(tb, carry):
        for _ in range(2 * UNROLL):
            _row_copy(h_ref, 0, rows_out, 0, sem).wait()
        return carry

    lax.fori_loop(0, tile // UNROLL, issue, 0)
    lax.fori_loop(0, tile // UNROLL, drain, 0)

    @pl.when(i == 0)
    def _():
        lax.fori_loop(0, N_EXPERTS, lambda e, c: (zero_fill(e, False), c)[1], 0)
        zero_tail(False)


def _dispatch(dest_flat, pad_start, pad_len, h2, n_rows, tile=512):
    t = h2.shape[0] // ROW_TILE
    kern = functools.partial(_dispatch_kernel, tile=tile, n_tok=t, n_blocks=n_rows // ROW_BLOCK)
    grid_spec = pltpu.PrefetchScalarGridSpec(
        num_scalar_prefetch=3, grid=(t // tile,),
        in_specs=[pl.BlockSpec((tile * ROW_TILE, LANES), lambda i, *_: (i, 0))],
        out_specs=pl.BlockSpec(memory_space=pl.ANY),
        scratch_shapes=[pltpu.VMEM((PAD_SIZES[0] * ROW_TILE, LANES), F32),
                        pltpu.SemaphoreType.DMA(()), pltpu.SemaphoreType.DMA(())])
    return pl.pallas_call(
        kern, name="dispatch", grid_spec=grid_spec,
        out_shape=jax.ShapeDtypeStruct((n_rows * ROW_TILE, LANES), F32),
        compiler_params=_params(("arbitrary",), 32),
    )(dest_flat, pad_start, pad_len, h2)


def _expert_kernel(be_ref, nu_ref, rows_ref, wg_ref, wu_ref, wd_ref, out_ref, wg_b, wu_b, wd_b):
    i = pl.program_id(0)
    changed = (i == 0) | (be_ref[i] != be_ref[jnp.maximum(i - 1, 0)])

    @pl.when(changed)
    def _():
        wg_b[...] = wg_ref[0].astype(BF16)
        wu_b[...] = wu_ref[0].astype(BF16)
        wd_b[...] = wd_ref[0].astype(BF16)

    @pl.when(i < nu_ref[0])
    def _():
        xb = _load_tile_rows(rows_ref).astype(BF16)
        gate = jnp.dot(xb, wg_b[...], preferred_element_type=F32)
        up = jnp.dot(xb, wu_b[...], preferred_element_type=F32)
        hid = (gate * jax.nn.sigmoid(gate) * up).astype(BF16)
        _store_tile_rows(out_ref, jnp.dot(hid, wd_b[...], preferred_element_type=F32))

    @pl.when(i >= nu_ref[0])
    def _():
        out_ref[...] = jnp.zeros(out_ref.shape, F32)


def _experts(block_e, n_used, rows, w_gate, w_up, w_down):
    nb = rows.shape[0] // (ROW_BLOCK * ROW_TILE)
    last = lambda i, nu: jnp.maximum(jnp.minimum(i, nu[0] - 1), 0)
    grid_spec = pltpu.PrefetchScalarGridSpec(
        num_scalar_prefetch=2, grid=(nb,),
        in_specs=[pl.BlockSpec((ROW_BLOCK * ROW_TILE, LANES), lambda i, be, nu: (last(i, nu), 0)),
                  pl.BlockSpec((1, D_MODEL, EXPERT_FF), lambda i, be, nu: (be[i], 0, 0)),
                  pl.BlockSpec((1, D_MODEL, EXPERT_FF), lambda i, be, nu: (be[i], 0, 0)),
                  pl.BlockSpec((1, EXPERT_FF, D_MODEL), lambda i, be, nu: (be[i], 0, 0))],
        out_specs=pl.BlockSpec((ROW_BLOCK * ROW_TILE, LANES), lambda i, be, nu: (i, 0)),
        scratch_shapes=[pltpu.VMEM((D_MODEL, EXPERT_FF), BF16), pltpu.VMEM((D_MODEL, EXPERT_FF), BF16),
                        pltpu.VMEM((EXPERT_FF, D_MODEL), BF16)])
    return pl.pallas_call(
        _expert_kernel, name="experts", grid_spec=grid_spec,
        out_shape=jax.ShapeDtypeStruct(rows.shape, F32),
        compiler_params=_params(("arbitrary",), 48),
    )(block_e, n_used, rows, w_gate, w_up, w_down)


def _combine_kernel(dest_ref, x1_ref, info_ref, gfin_ref, rows_ref, o_ref, buf0, buf1, sems, *, tile, n_tok):
    i = pl.program_id(0)
    slot = i % 2

    def gather(step, to_slot):
        base = step * tile
        b0, b1, sem = buf0.at[to_slot], buf1.at[to_slot], sems.at[to_slot]

        def issue(tb, carry):
            for k in range(UNROLL):
                t = tb * UNROLL + k
                _row_copy(rows_ref, dest_ref[base + t], b0, t, sem).start(priority=0)
                _row_copy(rows_ref, dest_ref[n_tok + base + t], b1, t, sem).start(priority=1)
            return carry

        lax.fori_loop(0, tile // UNROLL, issue, 0)

    @pl.when(i == 0)
    def _():
        gather(0, 0)

    @pl.when(i + 1 < pl.num_programs(0))
    def _():
        gather(i + 1, 1 - slot)

    b0, b1, sem = buf0.at[slot], buf1.at[slot], sems.at[slot]

    def drain(tb, carry):
        for _ in range(UNROLL):
            _row_copy(rows_ref, 0, b0, 0, sem).wait()
            _row_copy(rows_ref, 0, b1, 0, sem).wait()
        return carry

    lax.fori_loop(0, tile // UNROLL, drain, 0)
    info = info_ref[...]
    x2 = x1_ref[...] + info[:, 2:3] * _load_tile_rows(b0) + info[:, 3:4] * _load_tile_rows(b1)
    o_ref[...] = _rms(x2, gfin_ref[...])


def _combine(dest_flat, x1, info, gfin, out_rows, tile=256):
    t = x1.shape[0]
    kern = functools.partial(_combine_kernel, tile=tile, n_tok=t)
    grid_spec = pltpu.PrefetchScalarGridSpec(
        num_scalar_prefetch=1, grid=(t // tile,),
        in_specs=[pl.BlockSpec((tile, D_MODEL), lambda i, d: (i, 0)),
                  pl.BlockSpec((tile, LANES), lambda i, d: (i, 0)),
                  pl.BlockSpec((1, D_MODEL), lambda i, d: (0, 0)),
                  pl.BlockSpec(memory_space=pl.ANY)],
        out_specs=pl.BlockSpec((tile, D_MODEL), lambda i, d: (i, 0)),
        scratch_shapes=[pltpu.VMEM((2, tile * ROW_TILE, LANES), F32), pltpu.VMEM((2, tile * ROW_TILE, LANES), F32),
                        pltpu.SemaphoreType.DMA((2,))])
    return pl.pallas_call(
        kern, name="combine", grid_spec=grid_spec,
        out_shape=jax.ShapeDtypeStruct((t, D_MODEL), F32),
        compiler_params=_params(("arbitrary",), 32),
    )(dest_flat, x1, info, gfin, out_rows)


def _split_w_in(w_in):
    def pad_blocks(w):
        w = w.reshape(D_MODEL, N_HEADS * 2, HEAD_DIM)
        return jnp.pad(w, ((0, 0), (0, 0), (0, AUG - HEAD_DIM))).reshape(D_MODEL, N_HEADS * 2 * AUG)
    wq_t = pad_blocks(w_in[:, :ATTN_WIDTH]).T.astype(BF16)
    wk = pad_blocks(w_in[:, ATTN_WIDTH:2 * ATTN_WIDTH]).astype(BF16)
    wv_t = w_in[:, 2 * ATTN_WIDTH:3 * ATTN_WIDTH].T.astype(BF16)
    wu = w_in[:, 3 * ATTN_WIDTH:].astype(BF16)
    return wq_t, wk, wv_t, wu


def kernel(x, norm_attn, w_in, lambda_q1, lambda_k1, lambda_q2, lambda_k2, attn_subln, ssm_lam_re, ssm_lam_im, ssm_log_dt, ssm_b_re, ssm_b_im, ssm_c_re, ssm_c_im, ssm_d, w_glu, b_glu, ssm_norm, w_out, norm_moe, w_router_group, b_router_group, w_router_expert, b_router_expert, w_gate, w_up, w_down, norm_final):
    batch, seq, d = x.shape
    t = batch * seq
    nchunk = seq // CHUNK
    x2 = x.reshape(t, d)
    l = 0

    q_t, k_aug, v_t, u3 = _in_proj(x2, norm_attn[l][None], *_split_w_in(w_in[l]), seq)
    attn = _attention(q_t, k_aug, v_t, lambda_q1[l][None], lambda_k1[l][None], lambda_q2[l][None],
                      lambda_k2[l][None], attn_subln[l][:, None], batch, seq)

    sc = _ssm_constants(ssm_lam_re[l], ssm_lam_im[l], ssm_log_dt[l], ssm_b_re[l], ssm_b_im[l],
                        ssm_c_re[l], ssm_c_im[l], ssm_d[l], nchunk)
    s_re, s_im = _ssm_state(u3, sc["b_re"], sc["b_im"], sc["a_row_re"], sc["a_row_im"])
    h_re, h_im = _ssm_scan(s_re, s_im, sc["p_re"], sc["p_im"], nchunk)
    y3 = _ssm_out(u3, sc["kt"], h_re, h_im, sc["c_re"], sc["c_im"], sc["a_col_re"], sc["a_col_im"])

    w_router = jnp.concatenate([w_router_group[l], w_router_expert[l]], axis=1).astype(F32)
    w_router = jnp.pad(w_router, ((0, 0), (0, LANES - w_router.shape[1])))
    w_router_hi = w_router.astype(BF16)
    w_router = jnp.concatenate([w_router_hi, (w_router - w_router_hi.astype(F32)).astype(BF16)], axis=1)
    b_router = jnp.concatenate([b_router_group[l], b_router_expert[l]]).astype(F32)
    b_router = jnp.pad(b_router, (0, LANES - b_router.shape[0]))[None]
    x1, h2, info, route_t, cnt = _mix(x2, attn, y3, w_glu[l].astype(BF16), b_glu[l][None], ssm_norm[l][None],
                                      w_out[l][:ATTN_WIDTH].astype(BF16), w_out[l][ATTN_WIDTH:].astype(BF16),
                                      norm_moe[l][None], w_router, b_router)

    experts = route_t[0:2].astype(jnp.int32)
    ranks = route_t[4:6].astype(jnp.int32)
    counts = cnt[0, :N_EXPERTS].astype(jnp.int32)
    padded = ((counts + ROW_BLOCK - 1) // ROW_BLOCK) * ROW_BLOCK
    ids = jnp.arange(N_EXPERTS, dtype=jnp.int32)
    pend = jnp.sum(jnp.where(ids[None, :] <= ids[:, None], padded[None, :], 0), axis=1)
    pstart = pend - padded
    dest = ranks
    for e in range(N_EXPERTS):
        dest = dest + jnp.where(experts == e, pstart[e], 0)
    dest = dest.reshape(-1)
    n_rows = ((2 * t + N_EXPERTS * (ROW_BLOCK - 1) + ROW_BLOCK - 1) // ROW_BLOCK) * ROW_BLOCK
    nb = n_rows // ROW_BLOCK
    n_used = (pend[-1] // ROW_BLOCK).astype(jnp.int32)
    blk = jnp.minimum(jnp.arange(nb, dtype=jnp.int32), n_used - 1) * ROW_BLOCK
    block_e = jnp.minimum(jnp.sum((pend[None, :] <= blk[:, None]).astype(jnp.int32), axis=1), N_EXPERTS - 1)

    rows = _dispatch(dest, pstart + counts, padded - counts, h2, n_rows)
    out_rows = _experts(block_e, n_used[None], rows, w_gate[l], w_up[l], w_down[l])
    out = _combine(dest, x1, info, norm_final[None], out_rows)
    return out.reshape(batch, seq, d)
```

```python
import functools
import math

import jax
import jax.numpy as jnp
import numpy as np
from jax import lax
from jax.experimental import pallas as pl
from jax.experimental.pallas import tpu as pltpu

F32 = jnp.float32
BF16 = jnp.bfloat16

D_MODEL = 1024
N_HEADS = 4
HEAD_DIM = 64
VALUE_DIM = 128
ATTN_WIDTH = 512
SSM_WIDTH = 512
SSM_GROUP = 16
N_GROUPS = 32
SSM_STATE = 64
N_EXPERT_GROUPS = 4
EXPERTS_PER_GROUP = 8
N_EXPERTS = 32
EXPERT_FF = 512
RMS_EPS = 1e-6
LAMBDA_INIT = 0.8 - 0.6 * math.exp(-0.3 * 0)
LOG2E = math.log2(math.e)

CHUNK = 16
HALF_GROUPS = 16
TOEP_SPAN = 4
LANES = 128
AUG = 128
NORM_LANE = 70
QNORM_ROW = 71
V_ROWS = VALUE_DIM + 16
UNDERFLOW_LOG2 = 152.0
ROW_BLOCK = 256
ROW_TILE = D_MODEL // 128
VMEM_LIMIT_CAP = 56 * 1024 * 1024


def _params(dims, vmem_mb):
    return pltpu.CompilerParams(dimension_semantics=dims,
                                vmem_limit_bytes=min(vmem_mb * 1024 * 1024, VMEM_LIMIT_CAP))


def _rms(x, gain):
    return x * lax.rsqrt(jnp.mean(x * x, axis=-1, keepdims=True) + RMS_EPS) * gain


def _store_tile_rows(ref, val):
    n = val.shape[0]
    for c in range(ROW_TILE):
        ref[pl.ds(c, n, stride=ROW_TILE), :] = val[:, c * LANES:(c + 1) * LANES]


def _load_tile_rows(ref):
    n = ref.shape[0] // ROW_TILE
    return jnp.concatenate([ref[pl.ds(c, n, stride=ROW_TILE), :] for c in range(ROW_TILE)], axis=-1)


def _split3(val):
    hi = val.astype(BF16).astype(F32)
    r1 = val - hi
    mid = r1.astype(BF16).astype(F32)
    lo = r1 - mid
    return hi, mid, lo


def _inproj_kernel(x_ref, g_ref, wq_ref, wk_ref, wv_ref, wu_ref, q_ref, k_ref, v_ref, u_ref, ubuf, *, tile, seq):
    i = pl.program_id(0)
    h = _rms(x_ref[...], g_ref[...]).astype(BF16)
    nt = (((1,), (1,)), ((), ()))
    qt = lax.dot_general(wq_ref[...], h, nt, preferred_element_type=F32)
    kp = jnp.dot(h, wk_ref[...], preferred_element_type=F32)
    vt = lax.dot_general(wv_ref[...], h, nt, preferred_element_type=F32)
    ones_row = jnp.where(lax.broadcasted_iota(jnp.int32, (V_ROWS - VALUE_DIM, tile), 0) == 0, 1.0, 0.0)
    for hd in range(N_HEADS):
        v_ref[hd * V_ROWS:hd * V_ROWS + VALUE_DIM, :] = vt[hd * VALUE_DIM:(hd + 1) * VALUE_DIM, :].astype(BF16)
        v_ref[hd * V_ROWS + VALUE_DIM:(hd + 1) * V_ROWS, :] = ones_row.astype(BF16)
    up = jnp.dot(h, wu_ref[...], preferred_element_type=F32)
    pos0 = lax.rem(i * tile, seq)
    pos_k = (pos0 + lax.broadcasted_iota(jnp.int32, (tile, AUG), 0)).astype(F32)
    pos_q = (pos0 + lax.broadcasted_iota(jnp.int32, (1, tile), 1)).astype(F32)
    lane = lax.broadcasted_iota(jnp.int32, (tile, AUG), 1)
    srow = lax.broadcasted_iota(jnp.int32, (AUG - HEAD_DIM, tile), 0)
    qscale = HEAD_DIM ** -0.5 * LOG2E
    for hd in range(N_HEADS):
        slope = 2.0 ** (-8.0 * (hd + 1) / N_HEADS) * LOG2E
        hi, mid, lo = _split3(pos_k * slope)
        k_add = jnp.where(lane == 64, hi,
                          jnp.where(lane == 65, mid,
                                    jnp.where(lane == 66, lo, jnp.where((lane >= 67) & (lane < 70), 1.0, 0.0))))
        hi, mid, lo = _split3(pos_q * slope)
        q_add = jnp.where(srow < 3, 1.0,
                          jnp.where(srow == 3, -hi, jnp.where(srow == 4, -mid, jnp.where(srow == 5, -lo, 0.0))))
        for m in range(2):
            c0 = (hd * 2 + m) * AUG
            qb = (qt[c0 // 2:c0 // 2 + HEAD_DIM, :] * qscale).astype(BF16)
            q_ref[c0:c0 + HEAD_DIM, :] = qb
            qr = qb.astype(F32)
            qnorm2 = jnp.sum(qr * qr, axis=0, keepdims=True) * (1.0 + 2.0 ** -6)
            q_ref[c0 + HEAD_DIM:c0 + AUG, :] = (
                q_add + jnp.where(srow == QNORM_ROW - HEAD_DIM, qnorm2, 0.0)).astype(BF16)
            kb = kp[:, c0:c0 + AUG]
            kr = kb.astype(BF16).astype(F32)
            norm2 = jnp.sum(kr * kr, axis=-1, keepdims=True) * (1.0 + 2.0 ** -6)
            k_ref[:, c0:c0 + AUG] = (kb + k_add + jnp.where(lane == NORM_LANE, norm2, 0.0)).astype(BF16)
    for cb in range(SSM_WIDTH // LANES):
        ubuf[cb] = up[:, cb * LANES:(cb + 1) * LANES]
    per_half = HALF_GROUPS * SSM_GROUP // LANES
    for s in range(CHUNK):
        for cb in range(SSM_WIDTH // LANES):
            c0 = ((cb // per_half) * CHUNK + s) * HALF_GROUPS * SSM_GROUP + (cb % per_half) * LANES
            u_ref[:, c0:c0 + LANES] = ubuf[cb, pl.ds(s, tile // CHUNK, stride=CHUNK), :].astype(BF16)


def _in_proj(x2, gain, wq_t, wk, wv_t, wu, seq, tile=512):
    t = x2.shape[0]
    qk = 2 * N_HEADS * AUG
    kern = functools.partial(_inproj_kernel, tile=tile, seq=seq)
    full = lambda shape: pl.BlockSpec(shape, lambda i: (0, 0))
    return pl.pallas_call(
        kern, name="in_proj",
        grid=(t // tile,),
        in_specs=[pl.BlockSpec((tile, D_MODEL), lambda i: (i, 0)),
                  full((1, D_MODEL)), full((ATTN_WIDTH, D_MODEL)), full((D_MODEL, qk)),
                  full((ATTN_WIDTH, D_MODEL)), full((D_MODEL, SSM_WIDTH))],
        out_specs=[pl.BlockSpec((qk, tile), lambda i: (0, i)),
                   pl.BlockSpec((tile, qk), lambda i: (i, 0)),
                   pl.BlockSpec((N_HEADS * V_ROWS, tile), lambda i: (0, i)),
                   pl.BlockSpec((tile // CHUNK, CHUNK * SSM_WIDTH), lambda i: (i, 0))],
        out_shape=[jax.ShapeDtypeStruct((qk, t), BF16),
                   jax.ShapeDtypeStruct((t, qk), BF16),
                   jax.ShapeDtypeStruct((N_HEADS * V_ROWS, t), BF16),
                   jax.ShapeDtypeStruct((t // CHUNK, CHUNK * SSM_WIDTH), BF16)],
        scratch_shapes=[pltpu.VMEM((SSM_WIDTH // LANES, tile, LANES), F32)],
        compiler_params=_params(("arbitrary",), 48),
    )(x2, gain, wq_t, wk, wv_t, wu)


def _attn_kernel(lq1, lk1, lq2, lk2, sub_ref, q_ref, k_ref, v_ref, o_ref, m_sc, acc_sc, kn_sc, *, tq):
    hd = pl.program_id(1)
    qi = pl.program_id(2)
    lam = (jnp.exp(jnp.sum(lq1[...] * lk1[...], axis=-1, keepdims=True))
           - jnp.exp(jnp.sum(lq2[...] * lk2[...], axis=-1, keepdims=True)) + LAMBDA_INIT)

    @pl.when(qi == 0)
    def _():
        for m in range(2):
            kn_sc[m] = jnp.max(k_ref[:, m * AUG:(m + 1) * AUG].astype(F32), axis=0, keepdims=True)

    def step(start, nkeys, align, q_lo=0, nq=tq, key_lo=None, first=False):
        r0 = pl.multiple_of(start, align)
        kblk = k_ref[pl.ds(r0, nkeys), :]
        vblk = v_ref[:, pl.ds(r0, nkeys)]
        scores = [jnp.dot(kblk[:, m * AUG:(m + 1) * AUG], q_ref[m * AUG:(m + 1) * AUG, q_lo:q_lo + nq],
                          preferred_element_type=F32) for m in range(2)]
        for m in range(2):
            s = scores[m]
            if key_lo is not None:
                key = lax.broadcasted_iota(jnp.int32, (nkeys, nq), 0) + key_lo
                qry = lax.broadcasted_iota(jnp.int32, (nkeys, nq), 1) + q_lo
                s = jnp.where(key <= qry, s, -jnp.inf)
            if first:
                m_new = jnp.max(s, axis=0, keepdims=True)
                pv = jnp.dot(vblk, jnp.exp2(s - m_new).astype(BF16), preferred_element_type=F32)
            else:
                m_prev = m_sc[m, :, q_lo:q_lo + nq]
                m_new = jnp.maximum(m_prev, jnp.max(s, axis=0, keepdims=True))
                pv = (jnp.exp2(m_prev - m_new) * acc_sc[m, :, q_lo:q_lo + nq]
                      + jnp.dot(vblk, jnp.exp2(s - m_new).astype(BF16), preferred_element_type=F32))
            acc_sc[m, :, q_lo:q_lo + nq] = pv
            m_sc[m, :, q_lo:q_lo + nq] = m_new

    half = tq // 2
    step(qi * tq, half, half, key_lo=0, first=True)
    step(qi * tq + half, half, half, q_lo=half, nq=half, key_lo=half)

    lane = lax.broadcasted_iota(jnp.int32, (1, AUG), 1)
    sub = lax.broadcasted_iota(jnp.int32, (16, 1), 0)
    slope = LOG2E * jnp.exp2(-2.0 * (jnp.full((1, 1), hd, jnp.int32) + 1).astype(F32))
    reach = jnp.zeros((1, 1), F32)
    for m in range(2):
        qrows = jnp.max(q_ref[m * AUG + HEAD_DIM:m * AUG + HEAD_DIM + 16, :].astype(F32), axis=-1, keepdims=True)
        q2 = jnp.max(jnp.where(sub == QNORM_ROW - HEAD_DIM, qrows, 0.0), axis=0, keepdims=True)
        k2 = jnp.max(jnp.where(lane == NORM_LANE, kn_sc[m], 0.0), axis=-1, keepdims=True)
        m_min = jnp.min(m_sc[m], axis=-1, keepdims=True)
        reach = jnp.maximum(reach, (jnp.sqrt(q2 * k2) * 1.001 + (UNDERFLOW_LOG2 + 0.5) - m_min) / slope)
    need = jnp.ceil((reach + (tq - 1)) / tq).astype(jnp.int32) - 1
    n_below = jnp.minimum(jnp.max(jnp.maximum(need, 0)), qi)

    def quad(i, carry):
        step((qi - 2 - 4 * i) * tq, 2 * tq, tq)
        step((qi - 4 - 4 * i) * tq, 2 * tq, tq)
        return carry

    lax.fori_loop(0, n_below // 4, quad, 0)
    rest = n_below % 4
    done = n_below - rest

    @pl.when(rest >= 2)
    def _():
        step((qi - done - 2) * tq, 2 * tq, tq)

    @pl.when(rest % 2 == 1)
    def _():
        step((qi - n_below) * tq, tq, tq)

    l0 = acc_sc[0, VALUE_DIM:VALUE_DIM + 1, :]
    l1 = acc_sc[1, VALUE_DIM:VALUE_DIM + 1, :]
    o = acc_sc[0, :VALUE_DIM, :] / l0 - lam * (acc_sc[1, :VALUE_DIM, :] / l1)
    o = o * lax.rsqrt(jnp.mean(o * o, axis=0, keepdims=True) + RMS_EPS) * sub_ref[...] * (1.0 - LAMBDA_INIT)
    o_ref[...] = o.T.astype(BF16)


def _attention(q_t, k_aug, v_t, lq1, lk1, lq2, lk2, subln_col, batch, seq, tq=512):
    tq = min(tq, seq)
    nq = seq // tq
    t = batch * seq
    small = pl.BlockSpec((1, HEAD_DIM), lambda b, h, i: (0, 0))
    kern = functools.partial(_attn_kernel, tq=tq)
    return pl.pallas_call(
        kern, name="attention",
        grid=(batch, N_HEADS, nq),
        in_specs=[small, small, small, small,
                  pl.BlockSpec((VALUE_DIM, 1), lambda b, h, i: (0, 0)),
                  pl.BlockSpec((2 * AUG, tq), lambda b, h, i: (h, b * nq + i)),
                  pl.BlockSpec((seq, 2 * AUG), lambda b, h, i: (b, h)),
                  pl.BlockSpec((V_ROWS, seq), lambda b, h, i: (h, b))],
        out_specs=pl.BlockSpec((tq, VALUE_DIM), lambda b, h, i: (b * nq + i, h)),
        out_shape=jax.ShapeDtypeStruct((t, ATTN_WIDTH), BF16),
        scratch_shapes=[pltpu.VMEM((2, 1, tq), F32), pltpu.VMEM((2, V_ROWS, tq), F32),
                        pltpu.VMEM((2, 1, AUG), F32)],
        compiler_params=_params(("arbitrary", "arbitrary", "arbitrary"), 48),
    )(lq1, lk1, lq2, lk2, subln_col, q_t, k_aug, v_t)


def _ssm_state_kernel(u_ref, bre_ref, bim_ref, are_ref, aim_ref, sre_ref, sim_ref, wre, wim):
    i = pl.program_id(1)

    @pl.when(i == 0)
    def _():
        wre[...] = bre_ref[0]
        wim[...] = bim_ref[0]
        sre_ref[...] = jnp.zeros(sre_ref.shape, F32)
        sim_ref[...] = jnp.zeros(sim_ref.shape, F32)

    @pl.when(i > 0)
    def _():
        ar, ai = are_ref[0], aim_ref[0]
        wr, wi = wre[...], wim[...]
        wre[...] = wr * ar - wi * ai
        wim[...] = wr * ai + wi * ar

    u = u_ref[...]
    sre_ref[...] += jnp.dot(u, wre[...].astype(BF16), preferred_element_type=F32)
    sim_ref[...] += jnp.dot(u, wim[...].astype(BF16), preferred_element_type=F32)


def _ssm_state(u3, b_re, b_im, a_row_re, a_row_im):
    nch = u3.shape[0]
    hw = HALF_GROUPS * SSM_GROUP
    sw = HALF_GROUPS * SSM_STATE
    return pl.pallas_call(
        _ssm_state_kernel, name="ssm_state",
        grid=(2, CHUNK),
        in_specs=[pl.BlockSpec((nch, hw), lambda hf, i: (0, hf * CHUNK + CHUNK - 1 - i)),
                  pl.BlockSpec((1, hw, sw), lambda hf, i: (hf, 0, 0)),
                  pl.BlockSpec((1, hw, sw), lambda hf, i: (hf, 0, 0)),
                  pl.BlockSpec((1, 1, sw), lambda hf, i: (hf, 0, 0)),
                  pl.BlockSpec((1, 1, sw), lambda hf, i: (hf, 0, 0))],
        out_specs=[pl.BlockSpec((nch, sw), lambda hf, i: (0, hf)),
                   pl.BlockSpec((nch, sw), lambda hf, i: (0, hf))],
        out_shape=[jax.ShapeDtypeStruct((nch, 2 * sw), F32)] * 2,
        scratch_shapes=[pltpu.VMEM((hw, sw), F32), pltpu.VMEM((hw, sw), F32)],
        compiler_params=_params(("arbitrary", "arbitrary"), 48),
    )(u3, b_re, b_im, a_row_re, a_row_im)


def _ssm_scan_kernel(sre_ref, sim_ref, pre_ref, pim_ref, hre_ref, him_ref, *, nchunk, nsteps):
    hr, hi = sre_ref[...], sim_ref[...]
    row = lax.rem(lax.broadcasted_iota(jnp.int32, hr.shape, 0), nchunk)
    for k in range(nsteps):
        d = 1 << k
        ar, ai = pre_ref[0, k:k + 1, :], pim_ref[0, k:k + 1, :]
        keep = row >= d
        pr = jnp.where(keep, pltpu.roll(hr, d, 0), 0.0)
        pi = jnp.where(keep, pltpu.roll(hi, d, 0), 0.0)
        hr, hi = hr + ar * pr - ai * pi, hi + ar * pi + ai * pr
    keep = row >= 1
    hre_ref[...] = jnp.where(keep, pltpu.roll(hr, 1, 0), 0.0).astype(BF16)
    him_ref[...] = jnp.where(keep, pltpu.roll(hi, 1, 0), 0.0).astype(BF16)


def _ssm_scan(s_re, s_im, p_re, p_im, nchunk, cols=512):
    nrow, width = s_re.shape
    nsteps = p_re.shape[1]
    per_half = (width // 2) // cols
    kern = functools.partial(_ssm_scan_kernel, nchunk=nchunk, nsteps=nsteps)
    blk = pl.BlockSpec((nrow, cols), lambda j: (0, j))
    pblk = pl.BlockSpec((1, nsteps, cols), lambda j: (j // per_half, 0, j % per_half))
    return pl.pallas_call(
        kern, name="ssm_scan",
        grid=(width // cols,),
        in_specs=[blk, blk, pblk, pblk],
        out_specs=[blk, blk],
        out_shape=[jax.ShapeDtypeStruct((nrow, width), BF16)] * 2,
        compiler_params=_params(("arbitrary",), 48),
    )(s_re, s_im, p_re, p_im)


def _ssm_out_kernel(u_ref, kt_ref, hre_ref, him_ref, cre_ref, cim_ref, are_ref, aim_ref, y_ref, wre, wim, toep):
    t = pl.program_id(1)
    ar, ai = are_ref[0], aim_ref[0]

    @pl.when(t == 0)
    def _():
        cr, ci = cre_ref[0], cim_ref[0]
        wre[...] = cr * ar - ci * ai
        wim[...] = cr * ai + ci * ar

    @pl.when(t > 0)
    def _():
        wr, wi = wre[...], wim[...]
        wre[...] = wr * ar - wi * ai
        wim[...] = wr * ai + wi * ar

    hw = kt_ref.shape[-1]
    for s in range(CHUNK):
        tile = kt_ref[jnp.maximum(t - s, 0), 0]
        toep[s * hw:(s + 1) * hw, :] = jnp.where(s <= t, tile, jnp.zeros_like(tile))

    span = TOEP_SPAN * hw
    y_ref[0] = (jnp.dot(u_ref[:, :span], toep[:span, :], preferred_element_type=F32)
                + jnp.dot(hre_ref[...], wre[...].astype(BF16), preferred_element_type=F32)
                - jnp.dot(him_ref[...], wim[...].astype(BF16), preferred_element_type=F32))
    for piece in range(1, CHUNK // TOEP_SPAN):
        @pl.when(t >= piece * TOEP_SPAN)
        def _():
            lo = piece * span
            y_ref[0] += jnp.dot(u_ref[:, lo:lo + span], toep[lo:lo + span, :], preferred_element_type=F32)


def _ssm_out(u3, kt, h_re, h_im, c_re, c_im, a_col_re, a_col_im):
    nch = u3.shape[0]
    hw = HALF_GROUPS * SSM_GROUP
    sw = HALF_GROUPS * SSM_STATE
    return pl.pallas_call(
        _ssm_out_kernel, name="ssm_out",
        grid=(2, CHUNK),
        in_specs=[pl.BlockSpec((nch, CHUNK * hw), lambda hf, t: (0, hf)),
                  pl.BlockSpec((CHUNK, 1, hw, hw), lambda hf, t: (0, hf, 0, 0)),
                  pl.BlockSpec((nch, sw), lambda hf, t: (0, hf)),
                  pl.BlockSpec((nch, sw), lambda hf, t: (0, hf)),
                  pl.BlockSpec((1, sw, hw), lambda hf, t: (hf, 0, 0)),
                  pl.BlockSpec((1, sw, hw), lambda hf, t: (hf, 0, 0)),
                  pl.BlockSpec((1, sw, 1), lambda hf, t: (hf, 0, 0)),
                  pl.BlockSpec((1, sw, 1), lambda hf, t: (hf, 0, 0))],
        out_specs=pl.BlockSpec((1, nch, hw), lambda hf, t: (t, 0, hf)),
        out_shape=jax.ShapeDtypeStruct((CHUNK, nch, SSM_WIDTH), F32),
        scratch_shapes=[pltpu.VMEM((sw, hw), F32), pltpu.VMEM((sw, hw), F32), pltpu.VMEM((CHUNK * hw, hw), BF16)],
        compiler_params=_params(("arbitrary", "arbitrary"), 48),
    )(u3, kt, h_re, h_im, c_re, c_im, a_col_re, a_col_im)


def _ssm_constants(lam_re, lam_im, log_dt, b_re, b_im, c_re, c_im, d_skip, nchunk):
    lr, li = lam_re.astype(F32), lam_im.astype(F32)
    dt = jnp.exp(log_dt.astype(F32))[:, None]

    def lam_bar_pow(k):
        mag = jnp.exp(k * lr * dt)
        return mag * jnp.cos(k * li * dt), mag * jnp.sin(k * li * dt)

    a_re, a_im = lam_bar_pow(1.0)
    den = lr * lr + li * li
    coef_re = ((a_re - 1.0) * lr + a_im * li) / den
    coef_im = (a_im * lr - (a_re - 1.0) * li) / den
    bb_re = coef_re[..., None] * b_re.astype(F32) - coef_im[..., None] * b_im.astype(F32)
    bb_im = coef_re[..., None] * b_im.astype(F32) + coef_im[..., None] * b_re.astype(F32)
    cc_re, cc_im = c_re.astype(F32), c_im.astype(F32)
    lags = jnp.arange(CHUNK, dtype=F32)[:, None, None]
    pw_re, pw_im = lam_bar_pow(lags)
    pb_re = pw_re[..., None] * bb_re[None] - pw_im[..., None] * bb_im[None]
    pb_im = pw_re[..., None] * bb_im[None] + pw_im[..., None] * bb_re[None]
    kt = jnp.einsum('gcn,jgnd->jgdc', cc_re, pb_re) - jnp.einsum('gcn,jgnd->jgdc', cc_im, pb_im)
    skip = jnp.einsum('gc,dc->gdc', d_skip.astype(F32), jnp.eye(SSM_GROUP, dtype=F32))
    kt = jnp.concatenate([kt[:1] + skip[None], kt[1:]], axis=0)
    hw, sw = HALF_GROUPS * SSM_GROUP, HALF_GROUPS * SSM_STATE

    def block_diag(rows, row_group, col_group):
        wide = jnp.tile(rows, (1,) * (rows.ndim - 1) + (HALF_GROUPS,))
        r = lax.broadcasted_iota(jnp.int32, wide.shape, wide.ndim - 2) // row_group
        c = lax.broadcasted_iota(jnp.int32, wide.shape, wide.ndim - 1) // col_group
        return jnp.where(r == c, wide, 0.0)

    kt_t = block_diag(kt.reshape(CHUNK, 2, hw, SSM_GROUP), SSM_GROUP, SSM_GROUP).astype(BF16)

    def b_tiles(part):
        p = part.reshape(2, HALF_GROUPS, SSM_STATE, SSM_GROUP).transpose(0, 1, 3, 2)
        return block_diag(p.reshape(2, hw, SSM_STATE), SSM_GROUP, SSM_STATE)

    def c_tiles(part):
        p = part.reshape(2, HALF_GROUPS, SSM_GROUP, SSM_STATE).transpose(0, 1, 3, 2)
        return block_diag(p.reshape(2, sw, SSM_GROUP), SSM_STATE, SSM_GROUP)

    nsteps = max(int(math.log2(nchunk)), 1)
    steps = (CHUNK * 2.0 ** jnp.arange(nsteps, dtype=F32))[:, None, None]
    st_re, st_im = lam_bar_pow(steps)
    by_half = lambda p: p.reshape(nsteps, 2, sw).transpose(1, 0, 2)
    ar_h, ai_h = a_re.reshape(2, sw), a_im.reshape(2, sw)
    return dict(kt=kt_t, b_re=b_tiles(bb_re), b_im=b_tiles(bb_im),
                c_re=c_tiles(cc_re), c_im=c_tiles(cc_im),
                a_row_re=ar_h[:, None, :], a_row_im=ai_h[:, None, :],
                a_col_re=ar_h[:, :, None], a_col_im=ai_h[:, :, None],
                p_re=by_half(st_re), p_im=by_half(st_im))


def _mix_kernel(x_ref, attn_ref, y3_ref, wglu_ref, bglu_ref, gssm_ref, woa_ref, wos_ref, gmoe_ref,
                wr_ref, br_ref, x1_ref, h2_ref, info_ref, rt_ref, cnt_ref, ybuf, carry, *, tile):
    i = pl.program_id(0)

    @pl.when(i == 0)
    def _():
        carry[...] = jnp.zeros(carry.shape, F32)

    for s in range(CHUNK):
        for cb in range(SSM_WIDTH // LANES):
            ybuf[cb, pl.ds(s, tile // CHUNK, stride=CHUNK), :] = y3_ref[s, :, cb * LANES:(cb + 1) * LANES]
    y = jax.nn.gelu(jnp.concatenate([ybuf[cb] for cb in range(SSM_WIDTH // LANES)], axis=-1))
    z = jnp.dot(y.astype(BF16), wglu_ref[...], preferred_element_type=F32) + bglu_ref[...]
    y = y * jax.nn.sigmoid(z)
    ssm = _rms(y, gssm_ref[...])
    x1 = (x_ref[...] + jnp.dot(attn_ref[...], woa_ref[...], preferred_element_type=F32)
          + jnp.dot(ssm.astype(BF16), wos_ref[...], preferred_element_type=F32))
    x1_ref[...] = x1
    h2 = _rms(x1, gmoe_ref[...])
    _store_tile_rows(h2_ref, h2)

    h_hi = h2.astype(BF16)
    h_lo = (h2 - h_hi.astype(F32)).astype(BF16)
    wr = wr_ref[...]
    both = jnp.dot(h_hi, wr, preferred_element_type=F32)
    logits = (both[:, :LANES] + both[:, LANES:]
              + jnp.dot(h_lo, wr[:, :LANES], preferred_element_type=F32) + br_ref[...])
    lane = lax.broadcasted_iota(jnp.int32, logits.shape, 1)
    neg = -jnp.inf
    gl = jnp.where(lane < N_EXPERT_GROUPS, logits, neg)
    gmax = jnp.max(gl, axis=-1, keepdims=True)
    gsel = jnp.min(jnp.where(gl == gmax, lane, LANES), axis=-1, keepdims=True)
    p_group = 1.0 / jnp.sum(jnp.exp(gl - gmax), axis=-1, keepdims=True)
    elane = lane - N_EXPERT_GROUPS
    in_grp = (elane >= 0) & (elane < N_EXPERTS) & ((elane >> 3) == gsel)
    el = jnp.where(in_grp, logits, neg)
    m1 = jnp.max(el, axis=-1, keepdims=True)
    i1 = jnp.min(jnp.where(el == m1, lane, LANES), axis=-1, keepdims=True)
    den = jnp.sum(jnp.exp(el - m1), axis=-1, keepdims=True)
    el2 = jnp.where(lane == i1, neg, el)
    m2 = jnp.max(el2, axis=-1, keepdims=True)
    i2 = jnp.min(jnp.where(el2 == m2, lane, LANES), axis=-1, keepdims=True)
    g0 = p_group / den
    g1 = p_group * jnp.exp(m2 - m1) / den
    e0 = i1 - N_EXPERT_GROUPS
    e1 = i2 - N_EXPERT_GROUPS

    hit0 = lane == e0
    hit1 = lane == e1
    onehot = jnp.where(hit0 | hit1, 1.0, 0.0)
    r = lax.broadcasted_iota(jnp.int32, (tile, tile), 0)
    c = lax.broadcasted_iota(jnp.int32, (tile, tile), 1)
    tril = jnp.where(c < r, 1.0, 0.0).astype(BF16)
    before = jnp.dot(tril, onehot.astype(BF16), preferred_element_type=F32) + carry[...]
    rank0 = jnp.sum(jnp.where(hit0, before, 0.0), axis=-1, keepdims=True)
    rank1 = jnp.sum(jnp.where(hit1, before, 0.0), axis=-1, keepdims=True)
    carry[...] += jnp.sum(onehot, axis=0, keepdims=True)
    cnt_ref[...] = carry[...]
    info = jnp.where(lane == 0, e0.astype(F32),
                     jnp.where(lane == 1, e1.astype(F32),
                               jnp.where(lane == 2, g0,
                                         jnp.where(lane == 3, g1,
                                                   jnp.where(lane == 4, rank0, jnp.where(lane == 5, rank1, 0.0))))))
    info_ref[...] = info
    rt_ref[...] = info.T[:8]


def _mix(x2, attn, y3, wglu, bglu, gssm, wo_a, wo_s, gmoe, w_router, b_router, tile=512):
    t = x2.shape[0]
    kern = functools.partial(_mix_kernel, tile=tile)
    full = lambda shape: pl.BlockSpec(shape, lambda i: tuple(0 for _ in shape))
    return pl.pallas_call(
        kern, name="mix",
        grid=(t // tile,),
        in_specs=[pl.BlockSpec((tile, D_MODEL), lambda i: (i, 0)),
                  pl.BlockSpec((tile, ATTN_WIDTH), lambda i: (i, 0)),
                  pl.BlockSpec((CHUNK, tile // CHUNK, SSM_WIDTH), lambda i: (0, i, 0)),
                  full((SSM_WIDTH, SSM_WIDTH)), full((1, SSM_WIDTH)), full((1, SSM_WIDTH)),
                  full((ATTN_WIDTH, D_MODEL)), full((SSM_WIDTH, D_MODEL)), full((1, D_MODEL)),
                  full((D_MODEL, 2 * LANES)), full((1, LANES))],
        out_specs=[pl.BlockSpec((tile, D_MODEL), lambda i: (i, 0)),
                   pl.BlockSpec((tile * ROW_TILE, LANES), lambda i: (i, 0)),
                   pl.BlockSpec((tile, LANES), lambda i: (i, 0)),
                   pl.BlockSpec((8, tile), lambda i: (0, i)),
                   pl.BlockSpec((1, LANES), lambda i: (0, 0))],
        out_shape=[jax.ShapeDtypeStruct((t, D_MODEL), F32),
                   jax.ShapeDtypeStruct((t * ROW_TILE, LANES), F32),
                   jax.ShapeDtypeStruct((t, LANES), F32),
                   jax.ShapeDtypeStruct((8, t), F32),
                   jax.ShapeDtypeStruct((1, LANES), F32)],
        scratch_shapes=[pltpu.VMEM((SSM_WIDTH // LANES, tile, LANES), F32), pltpu.VMEM((1, LANES), F32)],
        compiler_params=_params(("arbitrary",), 48),
    )(x2, attn, y3, wglu, bglu, gssm, wo_a, wo_s, gmoe, w_router, b_router)


def _rows_at(ref, row, n_rows=1):
    return ref.at[pl.ds(pl.multiple_of(row * ROW_TILE, ROW_TILE), n_rows * ROW_TILE), :]


def _row_copy(src, s, dst, d, sem):
    return pltpu.make_async_copy(_rows_at(src, s), _rows_at(dst, d), sem)


UNROLL = 4
PAD_SIZES = tuple(1 << b for b in reversed(range(ROW_BLOCK.bit_length() - 1)))


def _dispatch_kernel(dest_ref, pad_start_ref, pad_len_ref, h_ref, rows_out, zbuf, sem, zsem, *, tile, n_tok,
                     n_blocks):
    i = pl.program_id(0)
    base = i * tile

    def zero_fill(e, start):
        off, rem = pad_start_ref[e], pad_len_ref[e]
        for size in PAD_SIZES:
            @pl.when((rem & size) != 0)
            def _():
                cp = pltpu.make_async_copy(_rows_at(zbuf, 0, size), _rows_at(rows_out, off, size), zsem)
                cp.start() if start else cp.wait()
            off = off + (rem & size)

    def zero_tail(start):
        used = (pad_start_ref[N_EXPERTS - 1] + pad_len_ref[N_EXPERTS - 1]) // ROW_BLOCK

        def blk(b, carry):
            for half in range(ROW_BLOCK // PAD_SIZES[0]):
                cp = pltpu.make_async_copy(zbuf, _rows_at(rows_out, b * ROW_BLOCK + half * PAD_SIZES[0],
                                                          PAD_SIZES[0]), zsem)
                cp.start() if start else cp.wait()
            return carry

        lax.fori_loop(used, n_blocks, blk, 0)

    @pl.when(i == 0)
    def _():
        zbuf[...] = jnp.zeros(zbuf.shape, F32)
        lax.fori_loop(0, N_EXPERTS, lambda e, c: (zero_fill(e, True), c)[1], 0)
        zero_tail(True)

    def issue(tb, carry):
        for k in range(UNROLL):
            t = tb * UNROLL + k
            for j in range(2):
                _row_copy(h_ref, t, rows_out, dest_ref[j * n_tok + base + t], sem).start(priority=j)
        return carry

    def drain(tb, carry):
        for _ in range(2 * UNROLL):
            _row_copy(h_ref, 0, rows_out, 0, sem).wait()
        return carry

    lax.fori_loop(0, tile // UNROLL, issue, 0)
    lax.fori_loop(0, tile // UNROLL, drain, 0)

    @pl.when(i == 0)
    def _():
        lax.fori_loop(0, N_EXPERTS, lambda e, c: (zero_fill(e, False), c)[1], 0)
        zero_tail(False)


def _dispatch(dest_flat, pad_start, pad_len, h2, n_rows, tile=512):
    t = h2.shape[0] // ROW_TILE
    kern = functools.partial(_dispatch_kernel, tile=tile, n_tok=t, n_blocks=n_rows // ROW_BLOCK)
    grid_spec = pltpu.PrefetchScalarGridSpec(
        num_scalar_prefetch=3, grid=(t // tile,),
        in_specs=[pl.BlockSpec((tile * ROW_TILE, LANES), lambda i, *_: (i, 0))],
        out_specs=pl.BlockSpec(memory_space=pl.ANY),
        scratch_shapes=[pltpu.VMEM((PAD_SIZES[0] * ROW_TILE, LANES), F32),
                        pltpu.SemaphoreType.DMA(()), pltpu.SemaphoreType.DMA(())])
    return pl.pallas_call(
        kern, name="dispatch", grid_spec=grid_spec,
        out_shape=jax.ShapeDtypeStruct((n_rows * ROW_TILE, LANES), F32),
        compiler_params=_params(("arbitrary",), 32),
    )(dest_flat, pad_start, pad_len, h2)


def _expert_kernel(be_ref, nu_ref, rows_ref, wg_ref, wu_ref, wd_ref, out_ref, wg_b, wu_b, wd_b):
    i = pl.program_id(0)
    changed = (i == 0) | (be_ref[i] != be_ref[jnp.maximum(i - 1, 0)])

    @pl.when(changed)
    def _():
        wg_b[...] = wg_ref[0].astype(BF16)
        wu_b[...] = wu_ref[0].astype(BF16)
        wd_b[...] = wd_ref[0].astype(BF16)

    @pl.when(i < nu_ref[0])
    def _():
        xb = _load_tile_rows(rows_ref).astype(BF16)
        gate = jnp.dot(xb, wg_b[...], preferred_element_type=F32)
        up = jnp.dot(xb, wu_b[...], preferred_element_type=F32)
        hid = (gate * jax.nn.sigmoid(gate) * up).astype(BF16)
        _store_tile_rows(out_ref, jnp.dot(hid, wd_b[...], preferred_element_type=F32))

    @pl.when(i >= nu_ref[0])
    def _():
        out_ref[...] = jnp.zeros(out_ref.shape, F32)


def _experts(block_e, n_used, rows, w_gate, w_up, w_down):
    nb = rows.shape[0] // (ROW_BLOCK * ROW_TILE)
    last = lambda i, nu: jnp.maximum(jnp.minimum(i, nu[0] - 1), 0)
    grid_spec = pltpu.PrefetchScalarGridSpec(
        num_scalar_prefetch=2, grid=(nb,),
        in_specs=[pl.BlockSpec((ROW_BLOCK * ROW_TILE, LANES), lambda i, be, nu: (last(i, nu), 0)),
                  pl.BlockSpec((1, D_MODEL, EXPERT_FF), lambda i, be, nu: (be[i], 0, 0)),
                  pl.BlockSpec((1, D_MODEL, EXPERT_FF), lambda i, be, nu: (be[i], 0, 0)),
                  pl.BlockSpec((1, EXPERT_FF, D_MODEL), lambda i, be, nu: (be[i], 0, 0))],
        out_specs=pl.BlockSpec((ROW_BLOCK * ROW_TILE, LANES), lambda i, be, nu: (i, 0)),
        scratch_shapes=[pltpu.VMEM((D_MODEL, EXPERT_FF), BF16), pltpu.VMEM((D_MODEL, EXPERT_FF), BF16),
                        pltpu.VMEM((EXPERT_FF, D_MODEL), BF16)])
    return pl.pallas_call(
        _expert_kernel, name="experts", grid_spec=grid_spec,
        out_shape=jax.ShapeDtypeStruct(rows.shape, F32),
        compiler_params=_params(("arbitrary",), 48),
    )(block_e, n_used, rows, w_gate, w_up, w_down)


def _combine_kernel(dest_ref, x1_ref, info_ref, gfin_ref, rows_ref, o_ref, buf0, buf1, sems, *, tile, n_tok):
    i = pl.program_id(0)
    slot = i % 2

    def gather(step, to_slot):
        base = step * tile
        b0, b1, sem = buf0.at[to_slot], buf1.at[to_slot], sems.at[to_slot]

        def issue(tb, carry):
            for k in range(UNROLL):
                t = tb * UNROLL + k
                _row_copy(rows_ref, dest_ref[base + t], b0, t, sem).start(priority=0)
                _row_copy(rows_ref, dest_ref[n_tok + base + t], b1, t, sem).start(priority=1)
            return carry

        lax.fori_loop(0, tile // UNROLL, issue, 0)

    @pl.when(i == 0)
    def _():
        gather(0, 0)

    @pl.when(i + 1 < pl.num_programs(0))
    def _():
        gather(i + 1, 1 - slot)

    b0, b1, sem = buf0.at[slot], buf1.at[slot], sems.at[slot]

    def drain(tb, carry):
        for _ in range(UNROLL):
            _row_copy(rows_ref, 0, b0, 0, sem).wait()
            _row_copy(rows_ref, 0, b1, 0, sem).wait()
        return carry

    lax.fori_loop(0, tile // UNROLL, drain, 0)
    info = info_ref[...]
    x2 = x1_ref[...] + info[:, 2:3] * _load_tile_rows(b0) + info[:, 3:4] * _load_tile_rows(b1)
    o_ref[...] = _rms(x2, gfin_ref[...])


def _combine(dest_flat, x1, info, gfin, out_rows, tile=256):
    t = x1.shape[0]
    kern = functools.partial(_combine_kernel, tile=tile, n_tok=t)
    grid_spec = pltpu.PrefetchScalarGridSpec(
        num_scalar_prefetch=1, grid=(t // tile,),
        in_specs=[pl.BlockSpec((tile, D_MODEL), lambda i, d: (i, 0)),
                  pl.BlockSpec((tile, LANES), lambda i, d: (i, 0)),
                  pl.BlockSpec((1, D_MODEL), lambda i, d: (0, 0)),
                  pl.BlockSpec(memory_space=pl.ANY)],
        out_specs=pl.BlockSpec((tile, D_MODEL), lambda i, d: (i, 0)),
        scratch_shapes=[pltpu.VMEM((2, tile * ROW_TILE, LANES), F32), pltpu.VMEM((2, tile * ROW_TILE, LANES), F32),
                        pltpu.SemaphoreType.DMA((2,))])
    return pl.pallas_call(
        kern, name="combine", grid_spec=grid_spec,
        out_shape=jax.ShapeDtypeStruct((t, D_MODEL), F32),
        compiler_params=_params(("arbitrary",), 32),
    )(dest_flat, x1, info, gfin, out_rows)


def _split_w_in(w_in):
    def pad_blocks(w):
        w = w.reshape(D_MODEL, N_HEADS * 2, HEAD_DIM)
        return jnp.pad(w, ((0, 0), (0, 0), (0, AUG - HEAD_DIM))).reshape(D_MODEL, N_HEADS * 2 * AUG)
    wq_t = w_in[:, :ATTN_WIDTH].T.astype(BF16)
    wk = pad_blocks(w_in[:, ATTN_WIDTH:2 * ATTN_WIDTH]).astype(BF16)
    wv_t = w_in[:, 2 * ATTN_WIDTH:3 * ATTN_WIDTH].T.astype(BF16)
    wu = w_in[:, 3 * ATTN_WIDTH:].astype(BF16)
    return wq_t, wk, wv_t, wu


def kernel(x, norm_attn, w_in, lambda_q1, lambda_k1, lambda_q2, lambda_k2, attn_subln, ssm_lam_re, ssm_lam_im, ssm_log_dt, ssm_b_re, ssm_b_im, ssm_c_re, ssm_c_im, ssm_d, w_glu, b_glu, ssm_norm, w_out, norm_moe, w_router_group, b_router_group, w_router_expert, b_router_expert, w_gate, w_up, w_down, norm_final):
    batch, seq, d = x.shape
    t = batch * seq
    nchunk = seq // CHUNK
    x2 = x.reshape(t, d)
    l = 0

    q_t, k_aug, v_t, u3 = _in_proj(x2, norm_attn[l][None], *_split_w_in(w_in[l]), seq)
    attn = _attention(q_t, k_aug, v_t, lambda_q1[l][None], lambda_k1[l][None], lambda_q2[l][None],
                      lambda_k2[l][None], attn_subln[l][:, None], batch, seq)

    sc = _ssm_constants(ssm_lam_re[l], ssm_lam_im[l], ssm_log_dt[l], ssm_b_re[l], ssm_b_im[l],
                        ssm_c_re[l], ssm_c_im[l], ssm_d[l], nchunk)
    s_re, s_im = _ssm_state(u3, sc["b_re"], sc["b_im"], sc["a_row_re"], sc["a_row_im"])
    h_re, h_im = _ssm_scan(s_re, s_im, sc["p_re"], sc["p_im"], nchunk)
    y3 = _ssm_out(u3, sc["kt"], h_re, h_im, sc["c_re"], sc["c_im"], sc["a_col_re"], sc["a_col_im"])

    w_router = jnp.concatenate([w_router_group[l], w_router_expert[l]], axis=1).astype(F32)
    w_router = jnp.pad(w_router, ((0, 0), (0, LANES - w_router.shape[1])))
    w_router_hi = w_router.astype(BF16)
    w_router = jnp.concatenate([w_router_hi, (w_router - w_router_hi.astype(F32)).astype(BF16)], axis=1)
    b_router = jnp.concatenate([b_router_group[l], b_router_expert[l]]).astype(F32)
    b_router = jnp.pad(b_router, (0, LANES - b_router.shape[0]))[None]
    x1, h2, info, route_t, cnt = _mix(x2, attn, y3, w_glu[l].astype(BF16), b_glu[l][None], ssm_norm[l][None],
                                      w_out[l][:ATTN_WIDTH].astype(BF16), w_out[l][ATTN_WIDTH:].astype(BF16),
                                      norm_moe[l][None], w_router, b_router)

    experts = route_t[0:2].astype(jnp.int32)
    ranks = route_t[4:6].astype(jnp.int32)
    counts = cnt[0, :N_EXPERTS].astype(jnp.int32)
    padded = ((counts + ROW_BLOCK - 1) // ROW_BLOCK) * ROW_BLOCK
    ids = jnp.arange(N_EXPERTS, dtype=jnp.int32)
    pend = jnp.sum(jnp.where(ids[None, :] <= ids[:, None], padded[None, :], 0), axis=1)
    pstart = pend - padded
    dest = ranks
    for e in range(N_EXPERTS):
        dest = dest + jnp.where(experts == e, pstart[e], 0)
    dest = dest.reshape(-1)
    n_rows = ((2 * t + N_EXPERTS * (ROW_BLOCK - 1) + ROW_BLOCK - 1) // ROW_BLOCK) * ROW_BLOCK
    nb = n_rows // ROW_BLOCK
    n_used = (pend[-1] // ROW_BLOCK).astype(jnp.int32)
    blk = jnp.minimum(jnp.arange(nb, dtype=jnp.int32), n_used - 1) * ROW_BLOCK
    block_e = jnp.minimum(jnp.sum((pend[None, :] <= blk[:, None]).astype(jnp.int32), axis=1), N_EXPERTS - 1)

    rows = _dispatch(dest, pstart + counts, padded - counts, h2, n_rows)
    out_rows = _experts(block_e, n_used[None], rows, w_gate[l], w_up[l], w_down[l])
    out = _combine(dest, x1, info, norm_final[None], out_rows)
    return out.reshape(batch, seq, d)
```

```python
import functools
import math

import jax
import jax.numpy as jnp
import numpy as np
from jax import lax
from jax.experimental import pallas as pl
from jax.experimental.pallas import tpu as pltpu

F32 = jnp.float32
BF16 = jnp.bfloat16

D_MODEL = 1024
N_HEADS = 4
HEAD_DIM = 64
VALUE_DIM = 128
ATTN_WIDTH = 512
SSM_WIDTH = 512
SSM_GROUP = 16
N_GROUPS = 32
SSM_STATE = 64
N_EXPERT_GROUPS = 4
EXPERTS_PER_GROUP = 8
N_EXPERTS = 32
EXPERT_FF = 512
RMS_EPS = 1e-6
LAMBDA_INIT = 0.8 - 0.6 * math.exp(-0.3 * 0)
LOG2E = math.log2(math.e)

CHUNK = 16
HALF_GROUPS = 16
TOEP_SPAN = 4
LANES = 128
AUG = 128
NORM_LANE = 70
QNORM_ROW = 71
V_ROWS = VALUE_DIM + 16
UNDERFLOW_LOG2 = 152.0
ROW_BLOCK = 256
ROW_TILE = D_MODEL // 128
VMEM_LIMIT_CAP = 56 * 1024 * 1024


def _params(dims, vmem_mb):
    return pltpu.CompilerParams(dimension_semantics=dims,
                                vmem_limit_bytes=min(vmem_mb * 1024 * 1024, VMEM_LIMIT_CAP))


def _rms(x, gain):
    return x * lax.rsqrt(jnp.mean(x * x, axis=-1, keepdims=True) + RMS_EPS) * gain


def _store_tile_rows(ref, val):
    n = val.shape[0]
    for c in range(ROW_TILE):
        ref[pl.ds(c, n, stride=ROW_TILE), :] = val[:, c * LANES:(c + 1) * LANES]


def _load_tile_rows(ref, first=0, n=None):
    n = ref.shape[0] // ROW_TILE if n is None else n
    return jnp.concatenate([ref[pl.ds(first * ROW_TILE + c, n, stride=ROW_TILE), :] for c in range(ROW_TILE)],
                           axis=-1)


def _split3(val):
    hi = val.astype(BF16).astype(F32)
    r1 = val - hi
    mid = r1.astype(BF16).astype(F32)
    lo = r1 - mid
    return hi, mid, lo


def _inproj_kernel(x_ref, g_ref, wq_ref, wk_ref, wv_ref, wu_ref, q_ref, k_ref, v_ref, u_ref, ubuf, *, tile, seq):
    i = pl.program_id(0)
    h = _rms(x_ref[...], g_ref[...]).astype(BF16)
    nt = (((1,), (1,)), ((), ()))
    qt = lax.dot_general(wq_ref[...], h, nt, preferred_element_type=F32)
    kp = jnp.dot(h, wk_ref[...], preferred_element_type=F32)
    vt = lax.dot_general(wv_ref[...], h, nt, preferred_element_type=F32)
    ones_row = jnp.where(lax.broadcasted_iota(jnp.int32, (V_ROWS - VALUE_DIM, tile), 0) == 0, 1.0, 0.0)
    for hd in range(N_HEADS):
        v_ref[hd * V_ROWS:hd * V_ROWS + VALUE_DIM, :] = vt[hd * VALUE_DIM:(hd + 1) * VALUE_DIM, :].astype(BF16)
        v_ref[hd * V_ROWS + VALUE_DIM:(hd + 1) * V_ROWS, :] = ones_row.astype(BF16)
    up = jnp.dot(h, wu_ref[...], preferred_element_type=F32)
    pos0 = lax.rem(i * tile, seq)
    pos_k = (pos0 + lax.broadcasted_iota(jnp.int32, (tile, AUG), 0)).astype(F32)
    pos_q = (pos0 + lax.broadcasted_iota(jnp.int32, (1, tile), 1)).astype(F32)
    lane = lax.broadcasted_iota(jnp.int32, (tile, AUG), 1)
    srow = lax.broadcasted_iota(jnp.int32, (AUG - HEAD_DIM, tile), 0)
    qscale = HEAD_DIM ** -0.5 * LOG2E
    for hd in range(N_HEADS):
        slope = 2.0 ** (-8.0 * (hd + 1) / N_HEADS) * LOG2E
        hi, mid, lo = _split3(pos_k * slope)
        k_add = jnp.where(lane == 64, hi,
                          jnp.where(lane == 65, mid,
                                    jnp.where(lane == 66, lo, jnp.where((lane >= 67) & (lane < 70), 1.0, 0.0))))
        hi, mid, lo = _split3(pos_q * slope)
        q_add = jnp.where(srow < 3, 1.0,
                          jnp.where(srow == 3, -hi, jnp.where(srow == 4, -mid, jnp.where(srow == 5, -lo, 0.0))))
        for m in range(2):
            c0 = (hd * 2 + m) * AUG
            qb = (qt[c0 // 2:c0 // 2 + HEAD_DIM, :] * qscale).astype(BF16)
            q_ref[c0:c0 + HEAD_DIM, :] = qb
            qr = qb.astype(F32)
            qnorm2 = jnp.sum(qr * qr, axis=0, keepdims=True) * (1.0 + 2.0 ** -6)
            q_ref[c0 + HEAD_DIM:c0 + AUG, :] = (
                q_add + jnp.where(srow == QNORM_ROW - HEAD_DIM, qnorm2, 0.0)).astype(BF16)
            kb = kp[:, c0:c0 + AUG]
            kr = kb.astype(BF16).astype(F32)
            norm2 = jnp.sum(kr * kr, axis=-1, keepdims=True) * (1.0 + 2.0 ** -6)
            k_ref[:, c0:c0 + AUG] = (kb + k_add + jnp.where(lane == NORM_LANE, norm2, 0.0)).astype(BF16)
    for cb in range(SSM_WIDTH // LANES):
        ubuf[cb] = up[:, cb * LANES:(cb + 1) * LANES]
    per_half = HALF_GROUPS * SSM_GROUP // LANES
    for s in range(CHUNK):
        for cb in range(SSM_WIDTH // LANES):
            c0 = ((cb // per_half) * CHUNK + s) * HALF_GROUPS * SSM_GROUP + (cb % per_half) * LANES
            u_ref[:, c0:c0 + LANES] = ubuf[cb, pl.ds(s, tile // CHUNK, stride=CHUNK), :].astype(BF16)


def _in_proj(x2, gain, wq_t, wk, wv_t, wu, seq, tile=512):
    t = x2.shape[0]
    qk = 2 * N_HEADS * AUG
    kern = functools.partial(_inproj_kernel, tile=tile, seq=seq)
    full = lambda shape: pl.BlockSpec(shape, lambda i: (0, 0))
    return pl.pallas_call(
        kern, name="in_proj",
        grid=(t // tile,),
        in_specs=[pl.BlockSpec((tile, D_MODEL), lambda i: (i, 0)),
                  full((1, D_MODEL)), full((ATTN_WIDTH, D_MODEL)), full((D_MODEL, qk)),
                  full((ATTN_WIDTH, D_MODEL)), full((D_MODEL, SSM_WIDTH))],
        out_specs=[pl.BlockSpec((qk, tile), lambda i: (0, i)),
                   pl.BlockSpec((tile, qk), lambda i: (i, 0)),
                   pl.BlockSpec((N_HEADS * V_ROWS, tile), lambda i: (0, i)),
                   pl.BlockSpec((tile // CHUNK, CHUNK * SSM_WIDTH), lambda i: (i, 0))],
        out_shape=[jax.ShapeDtypeStruct((qk, t), BF16),
                   jax.ShapeDtypeStruct((t, qk), BF16),
                   jax.ShapeDtypeStruct((N_HEADS * V_ROWS, t), BF16),
                   jax.ShapeDtypeStruct((t // CHUNK, CHUNK * SSM_WIDTH), BF16)],
        scratch_shapes=[pltpu.VMEM((SSM_WIDTH // LANES, tile, LANES), F32)],
        compiler_params=_params(("arbitrary",), 48),
    )(x2, gain, wq_t, wk, wv_t, wu)


def _attn_kernel(lq1, lk1, lq2, lk2, sub_ref, q_ref, k_ref, v_ref, o_ref, m_sc, acc_sc, kn_sc, *, tq):
    hd = pl.program_id(1)
    qi = pl.program_id(2)
    lam = (jnp.exp(jnp.sum(lq1[...] * lk1[...], axis=-1, keepdims=True))
           - jnp.exp(jnp.sum(lq2[...] * lk2[...], axis=-1, keepdims=True)) + LAMBDA_INIT)

    @pl.when(qi == 0)
    def _():
        for m in range(2):
            kn_sc[m] = jnp.max(k_ref[:, m * AUG:(m + 1) * AUG].astype(F32), axis=0, keepdims=True)

    def step(start, nkeys, align, q_lo=0, nq=tq, key_lo=None, first=False):
        r0 = pl.multiple_of(start, align)
        kblk = k_ref[pl.ds(r0, nkeys), :]
        vblk = v_ref[:, pl.ds(r0, nkeys)]
        scores = [jnp.dot(kblk[:, m * AUG:(m + 1) * AUG], q_ref[m * AUG:(m + 1) * AUG, q_lo:q_lo + nq],
                          preferred_element_type=F32) for m in range(2)]
        for m in range(2):
            s = scores[m]
            if key_lo is not None:
                key = lax.broadcasted_iota(jnp.int32, (nkeys, nq), 0) + key_lo
                qry = lax.broadcasted_iota(jnp.int32, (nkeys, nq), 1) + q_lo
                s = jnp.where(key <= qry, s, -jnp.inf)
            if first:
                m_new = jnp.max(s, axis=0, keepdims=True)
                pv = jnp.dot(vblk, jnp.exp2(s - m_new).astype(BF16), preferred_element_type=F32)
            else:
                m_prev = m_sc[m, :, q_lo:q_lo + nq]
                m_new = jnp.maximum(m_prev, jnp.max(s, axis=0, keepdims=True))
                pv = (jnp.exp2(m_prev - m_new) * acc_sc[m, :, q_lo:q_lo + nq]
                      + jnp.dot(vblk, jnp.exp2(s - m_new).astype(BF16), preferred_element_type=F32))
            acc_sc[m, :, q_lo:q_lo + nq] = pv
            m_sc[m, :, q_lo:q_lo + nq] = m_new

    half = tq // 2
    step(qi * tq, half, half, key_lo=0, first=True)
    step(qi * tq + half, half, half, q_lo=half, nq=half, key_lo=half)

    lane = lax.broadcasted_iota(jnp.int32, (1, AUG), 1)
    sub = lax.broadcasted_iota(jnp.int32, (16, 1), 0)
    slope = LOG2E * jnp.exp2(-2.0 * (jnp.full((1, 1), hd, jnp.int32) + 1).astype(F32))
    reach = jnp.zeros((1, 1), F32)
    for m in range(2):
        qrows = jnp.max(q_ref[m * AUG + HEAD_DIM:m * AUG + HEAD_DIM + 16, :].astype(F32), axis=-1, keepdims=True)
        q2 = jnp.max(jnp.where(sub == QNORM_ROW - HEAD_DIM, qrows, 0.0), axis=0, keepdims=True)
        k2 = jnp.max(jnp.where(lane == NORM_LANE, kn_sc[m], 0.0), axis=-1, keepdims=True)
        m_min = jnp.min(m_sc[m], axis=-1, keepdims=True)
        reach = jnp.maximum(reach, (jnp.sqrt(q2 * k2) * 1.001 + (UNDERFLOW_LOG2 + 0.5) - m_min) / slope)
    need = jnp.ceil((reach + (tq - 1)) / tq).astype(jnp.int32) - 1
    n_below = jnp.minimum(jnp.max(jnp.maximum(need, 0)), qi)

    def quad(i, carry):
        step((qi - 2 - 4 * i) * tq, 2 * tq, tq)
        step((qi - 4 - 4 * i) * tq, 2 * tq, tq)
        return carry

    lax.fori_loop(0, n_below // 4, quad, 0)
    rest = n_below % 4
    done = n_below - rest

    @pl.when(rest >= 2)
    def _():
        step((qi - done - 2) * tq, 2 * tq, tq)

    @pl.when(rest % 2 == 1)
    def _():
        step((qi - n_below) * tq, tq, tq)

    l0 = acc_sc[0, VALUE_DIM:VALUE_DIM + 1, :]
    l1 = acc_sc[1, VALUE_DIM:VALUE_DIM + 1, :]
    o = acc_sc[0, :VALUE_DIM, :] / l0 - lam * (acc_sc[1, :VALUE_DIM, :] / l1)
    o = o * lax.rsqrt(jnp.mean(o * o, axis=0, keepdims=True) + RMS_EPS) * sub_ref[...] * (1.0 - LAMBDA_INIT)
    o_ref[...] = o.T.astype(BF16)


def _attention(q_t, k_aug, v_t, lq1, lk1, lq2, lk2, subln_col, batch, seq, tq=512):
    tq = min(tq, seq)
    nq = seq // tq
    t = batch * seq
    small = pl.BlockSpec((1, HEAD_DIM), lambda b, h, i: (0, 0))
    kern = functools.partial(_attn_kernel, tq=tq)
    return pl.pallas_call(
        kern, name="attention",
        grid=(batch, N_HEADS, nq),
        in_specs=[small, small, small, small,
                  pl.BlockSpec((VALUE_DIM, 1), lambda b, h, i: (0, 0)),
                  pl.BlockSpec((2 * AUG, tq), lambda b, h, i: (h, b * nq + i)),
                  pl.BlockSpec((seq, 2 * AUG), lambda b, h, i: (b, h)),
                  pl.BlockSpec((V_ROWS, seq), lambda b, h, i: (h, b))],
        out_specs=pl.BlockSpec((tq, VALUE_DIM), lambda b, h, i: (b * nq + i, h)),
        out_shape=jax.ShapeDtypeStruct((t, ATTN_WIDTH), BF16),
        scratch_shapes=[pltpu.VMEM((2, 1, tq), F32), pltpu.VMEM((2, V_ROWS, tq), F32),
                        pltpu.VMEM((2, 1, AUG), F32)],
        compiler_params=_params(("arbitrary", "arbitrary", "arbitrary"), 48),
    )(lq1, lk1, lq2, lk2, subln_col, q_t, k_aug, v_t)


def _ssm_state_kernel(u_ref, bre_ref, bim_ref, are_ref, aim_ref, sre_ref, sim_ref, wre, wim):
    i = pl.program_id(1)

    @pl.when(i == 0)
    def _():
        wre[...] = bre_ref[0]
        wim[...] = bim_ref[0]
        sre_ref[...] = jnp.zeros(sre_ref.shape, F32)
        sim_ref[...] = jnp.zeros(sim_ref.shape, F32)

    @pl.when(i > 0)
    def _():
        ar, ai = are_ref[0], aim_ref[0]
        wr, wi = wre[...], wim[...]
        wre[...] = wr * ar - wi * ai
        wim[...] = wr * ai + wi * ar

    u = u_ref[...]
    sre_ref[...] += jnp.dot(u, wre[...].astype(BF16), preferred_element_type=F32)
    sim_ref[...] += jnp.dot(u, wim[...].astype(BF16), preferred_element_type=F32)


def _ssm_state(u3, b_re, b_im, a_row_re, a_row_im):
    nch = u3.shape[0]
    hw = HALF_GROUPS * SSM_GROUP
    sw = HALF_GROUPS * SSM_STATE
    return pl.pallas_call(
        _ssm_state_kernel, name="ssm_state",
        grid=(2, CHUNK),
        in_specs=[pl.BlockSpec((nch, hw), lambda hf, i: (0, hf * CHUNK + CHUNK - 1 - i)),
                  pl.BlockSpec((1, hw, sw), lambda hf, i: (hf, 0, 0)),
                  pl.BlockSpec((1, hw, sw), lambda hf, i: (hf, 0, 0)),
                  pl.BlockSpec((1, 1, sw), lambda hf, i: (hf, 0, 0)),
                  pl.BlockSpec((1, 1, sw), lambda hf, i: (hf, 0, 0))],
        out_specs=[pl.BlockSpec((nch, sw), lambda hf, i: (0, hf)),
                   pl.BlockSpec((nch, sw), lambda hf, i: (0, hf))],
        out_shape=[jax.ShapeDtypeStruct((nch, 2 * sw), F32)] * 2,
        scratch_shapes=[pltpu.VMEM((hw, sw), F32), pltpu.VMEM((hw, sw), F32)],
        compiler_params=_params(("arbitrary", "arbitrary"), 48),
    )(u3, b_re, b_im, a_row_re, a_row_im)


def _ssm_scan_kernel(sre_ref, sim_ref, pre_ref, pim_ref, hre_ref, him_ref, *, nchunk, nsteps):
    hr, hi = sre_ref[...], sim_ref[...]
    row = lax.rem(lax.broadcasted_iota(jnp.int32, hr.shape, 0), nchunk)
    for k in range(nsteps):
        d = 1 << k
        ar, ai = pre_ref[0, k:k + 1, :], pim_ref[0, k:k + 1, :]
        keep = row >= d
        pr = jnp.where(keep, pltpu.roll(hr, d, 0), 0.0)
        pi = jnp.where(keep, pltpu.roll(hi, d, 0), 0.0)
        hr, hi = hr + ar * pr - ai * pi, hi + ar * pi + ai * pr
    keep = row >= 1
    hre_ref[...] = jnp.where(keep, pltpu.roll(hr, 1, 0), 0.0).astype(BF16)
    him_ref[...] = jnp.where(keep, pltpu.roll(hi, 1, 0), 0.0).astype(BF16)


def _ssm_scan(s_re, s_im, p_re, p_im, nchunk, cols=512):
    nrow, width = s_re.shape
    nsteps = p_re.shape[1]
    per_half = (width // 2) // cols
    kern = functools.partial(_ssm_scan_kernel, nchunk=nchunk, nsteps=nsteps)
    blk = pl.BlockSpec((nrow, cols), lambda j: (0, j))
    pblk = pl.BlockSpec((1, nsteps, cols), lambda j: (j // per_half, 0, j % per_half))
    return pl.pallas_call(
        kern, name="ssm_scan",
        grid=(width // cols,),
        in_specs=[blk, blk, pblk, pblk],
        out_specs=[blk, blk],
        out_shape=[jax.ShapeDtypeStruct((nrow, width), BF16)] * 2,
        compiler_params=_params(("arbitrary",), 48),
    )(s_re, s_im, p_re, p_im)


def _ssm_out_kernel(u_ref, kt_ref, hre_ref, him_ref, cre_ref, cim_ref, are_ref, aim_ref, y_ref, wre, wim, toep):
    t = pl.program_id(1)
    ar, ai = are_ref[0], aim_ref[0]

    @pl.when(t == 0)
    def _():
        cr, ci = cre_ref[0], cim_ref[0]
        wre[...] = cr * ar - ci * ai
        wim[...] = cr * ai + ci * ar

    @pl.when(t > 0)
    def _():
        wr, wi = wre[...], wim[...]
        wre[...] = wr * ar - wi * ai
        wim[...] = wr * ai + wi * ar

    hw = kt_ref.shape[-1]
    for s in range(CHUNK):
        tile = kt_ref[jnp.maximum(t - s, 0), 0]
        toep[s * hw:(s + 1) * hw, :] = jnp.where(s <= t, tile, jnp.zeros_like(tile))

    span = TOEP_SPAN * hw
    y_ref[0] = (jnp.dot(u_ref[:, :span], toep[:span, :], preferred_element_type=F32)
                + jnp.dot(hre_ref[...], wre[...].astype(BF16), preferred_element_type=F32)
                - jnp.dot(him_ref[...], wim[...].astype(BF16), preferred_element_type=F32))
    for piece in range(1, CHUNK // TOEP_SPAN):
        @pl.when(t >= piece * TOEP_SPAN)
        def _():
            lo = piece * span
            y_ref[0] += jnp.dot(u_ref[:, lo:lo + span], toep[lo:lo + span, :], preferred_element_type=F32)


def _ssm_out(u3, kt, h_re, h_im, c_re, c_im, a_col_re, a_col_im):
    nch = u3.shape[0]
    hw = HALF_GROUPS * SSM_GROUP
    sw = HALF_GROUPS * SSM_STATE
    return pl.pallas_call(
        _ssm_out_kernel, name="ssm_out",
        grid=(2, CHUNK),
        in_specs=[pl.BlockSpec((nch, CHUNK * hw), lambda hf, t: (0, hf)),
                  pl.BlockSpec((CHUNK, 1, hw, hw), lambda hf, t: (0, hf, 0, 0)),
                  pl.BlockSpec((nch, sw), lambda hf, t: (0, hf)),
                  pl.BlockSpec((nch, sw), lambda hf, t: (0, hf)),
                  pl.BlockSpec((1, sw, hw), lambda hf, t: (hf, 0, 0)),
                  pl.BlockSpec((1, sw, hw), lambda hf, t: (hf, 0, 0)),
                  pl.BlockSpec((1, sw, 1), lambda hf, t: (hf, 0, 0)),
                  pl.BlockSpec((1, sw, 1), lambda hf, t: (hf, 0, 0))],
        out_specs=pl.BlockSpec((1, nch, hw), lambda hf, t: (t, 0, hf)),
        out_shape=jax.ShapeDtypeStruct((CHUNK, nch, SSM_WIDTH), F32),
        scratch_shapes=[pltpu.VMEM((sw, hw), F32), pltpu.VMEM((sw, hw), F32), pltpu.VMEM((CHUNK * hw, hw), BF16)],
        compiler_params=_params(("arbitrary", "arbitrary"), 48),
    )(u3, kt, h_re, h_im, c_re, c_im, a_col_re, a_col_im)


def _ssm_constants(lam_re, lam_im, log_dt, b_re, b_im, c_re, c_im, d_skip, nchunk):
    lr, li = lam_re.astype(F32), lam_im.astype(F32)
    dt = jnp.exp(log_dt.astype(F32))[:, None]

    def lam_bar_pow(k):
        mag = jnp.exp(k * lr * dt)
        return mag * jnp.cos(k * li * dt), mag * jnp.sin(k * li * dt)

    a_re, a_im = lam_bar_pow(1.0)
    den = lr * lr + li * li
    coef_re = ((a_re - 1.0) * lr + a_im * li) / den
    coef_im = (a_im * lr - (a_re - 1.0) * li) / den
    bb_re = coef_re[..., None] * b_re.astype(F32) - coef_im[..., None] * b_im.astype(F32)
    bb_im = coef_re[..., None] * b_im.astype(F32) + coef_im[..., None] * b_re.astype(F32)
    cc_re, cc_im = c_re.astype(F32), c_im.astype(F32)
    lags = jnp.arange(CHUNK, dtype=F32)[:, None, None]
    pw_re, pw_im = lam_bar_pow(lags)
    pb_re = pw_re[..., None] * bb_re[None] - pw_im[..., None] * bb_im[None]
    pb_im = pw_re[..., None] * bb_im[None] + pw_im[..., None] * bb_re[None]
    kt = jnp.einsum('gcn,jgnd->jgdc', cc_re, pb_re) - jnp.einsum('gcn,jgnd->jgdc', cc_im, pb_im)
    skip = jnp.einsum('gc,dc->gdc', d_skip.astype(F32), jnp.eye(SSM_GROUP, dtype=F32))
    kt = jnp.concatenate([kt[:1] + skip[None], kt[1:]], axis=0)
    hw, sw = HALF_GROUPS * SSM_GROUP, HALF_GROUPS * SSM_STATE

    def block_diag(rows, row_group, col_group):
        wide = jnp.tile(rows, (1,) * (rows.ndim - 1) + (HALF_GROUPS,))
        r = lax.broadcasted_iota(jnp.int32, wide.shape, wide.ndim - 2) // row_group
        c = lax.broadcasted_iota(jnp.int32, wide.shape, wide.ndim - 1) // col_group
        return jnp.where(r == c, wide, 0.0)

    kt_t = block_diag(kt.reshape(CHUNK, 2, hw, SSM_GROUP), SSM_GROUP, SSM_GROUP).astype(BF16)

    def b_tiles(part):
        p = part.reshape(2, HALF_GROUPS, SSM_STATE, SSM_GROUP).transpose(0, 1, 3, 2)
        return block_diag(p.reshape(2, hw, SSM_STATE), SSM_GROUP, SSM_STATE)

    def c_tiles(part):
        p = part.reshape(2, HALF_GROUPS, SSM_GROUP, SSM_STATE).transpose(0, 1, 3, 2)
        return block_diag(p.reshape(2, sw, SSM_GROUP), SSM_STATE, SSM_GROUP)

    nsteps = max(int(math.log2(nchunk)), 1)
    steps = (CHUNK * 2.0 ** jnp.arange(nsteps, dtype=F32))[:, None, None]
    st_re, st_im = lam_bar_pow(steps)
    by_half = lambda p: p.reshape(nsteps, 2, sw).transpose(1, 0, 2)
    ar_h, ai_h = a_re.reshape(2, sw), a_im.reshape(2, sw)
    return dict(kt=kt_t, b_re=b_tiles(bb_re), b_im=b_tiles(bb_im),
                c_re=c_tiles(cc_re), c_im=c_tiles(cc_im),
                a_row_re=ar_h[:, None, :], a_row_im=ai_h[:, None, :],
                a_col_re=ar_h[:, :, None], a_col_im=ai_h[:, :, None],
                p_re=by_half(st_re), p_im=by_half(st_im))


def _mix_kernel(x_ref, attn_ref, y3_ref, wglu_ref, bglu_ref, gssm_ref, woa_ref, wos_ref, gmoe_ref,
                wr_ref, br_ref, x1_ref, h2_ref, info_ref, rt_ref, cnt_ref, ybuf, carry, *, tile):
    i = pl.program_id(0)

    @pl.when(i == 0)
    def _():
        carry[...] = jnp.zeros(carry.shape, F32)

    for s in range(CHUNK):
        for cb in range(SSM_WIDTH // LANES):
            ybuf[cb, pl.ds(s, tile // CHUNK, stride=CHUNK), :] = y3_ref[s, :, cb * LANES:(cb + 1) * LANES]
    y = jax.nn.gelu(jnp.concatenate([ybuf[cb] for cb in range(SSM_WIDTH // LANES)], axis=-1))
    z = jnp.dot(y.astype(BF16), wglu_ref[...], preferred_element_type=F32) + bglu_ref[...]
    y = y * jax.nn.sigmoid(z)
    ssm = _rms(y, gssm_ref[...])
    x1 = (x_ref[...] + jnp.dot(attn_ref[...], woa_ref[...], preferred_element_type=F32)
          + jnp.dot(ssm.astype(BF16), wos_ref[...], preferred_element_type=F32))
    x1_ref[...] = x1
    h2 = _rms(x1, gmoe_ref[...])
    _store_tile_rows(h2_ref, h2)

    h_hi = h2.astype(BF16)
    h_lo = (h2 - h_hi.astype(F32)).astype(BF16)
    wr = wr_ref[...]
    both = jnp.dot(h_hi, wr, preferred_element_type=F32)
    logits = (both[:, :LANES] + both[:, LANES:]
              + jnp.dot(h_lo, wr[:, :LANES], preferred_element_type=F32) + br_ref[...])
    lane = lax.broadcasted_iota(jnp.int32, logits.shape, 1)
    neg = -jnp.inf
    gl = jnp.where(lane < N_EXPERT_GROUPS, logits, neg)
    gmax = jnp.max(gl, axis=-1, keepdims=True)
    gsel = jnp.min(jnp.where(gl == gmax, lane, LANES), axis=-1, keepdims=True)
    p_group = 1.0 / jnp.sum(jnp.exp(gl - gmax), axis=-1, keepdims=True)
    elane = lane - N_EXPERT_GROUPS
    in_grp = (elane >= 0) & (elane < N_EXPERTS) & ((elane >> 3) == gsel)
    el = jnp.where(in_grp, logits, neg)
    m1 = jnp.max(el, axis=-1, keepdims=True)
    i1 = jnp.min(jnp.where(el == m1, lane, LANES), axis=-1, keepdims=True)
    den = jnp.sum(jnp.exp(el - m1), axis=-1, keepdims=True)
    el2 = jnp.where(lane == i1, neg, el)
    m2 = jnp.max(el2, axis=-1, keepdims=True)
    i2 = jnp.min(jnp.where(el2 == m2, lane, LANES), axis=-1, keepdims=True)
    g0 = p_group / den
    g1 = p_group * jnp.exp(m2 - m1) / den
    e0 = i1 - N_EXPERT_GROUPS
    e1 = i2 - N_EXPERT_GROUPS

    hit0 = lane == e0
    hit1 = lane == e1
    onehot = jnp.where(hit0 | hit1, 1.0, 0.0)
    r = lax.broadcasted_iota(jnp.int32, (tile, tile), 0)
    c = lax.broadcasted_iota(jnp.int32, (tile, tile), 1)
    tril = jnp.where(c < r, 1.0, 0.0).astype(BF16)
    before = jnp.dot(tril, onehot.astype(BF16), preferred_element_type=F32) + carry[...]
    rank0 = jnp.sum(jnp.where(hit0, before, 0.0), axis=-1, keepdims=True)
    rank1 = jnp.sum(jnp.where(hit1, before, 0.0), axis=-1, keepdims=True)
    carry[...] += jnp.sum(onehot, axis=0, keepdims=True)
    cnt_ref[...] = carry[...]
    info = jnp.where(lane == 0, e0.astype(F32),
                     jnp.where(lane == 1, e1.astype(F32),
                               jnp.where(lane == 2, g0,
                                         jnp.where(lane == 3, g1,
                                                   jnp.where(lane == 4, rank0, jnp.where(lane == 5, rank1, 0.0))))))
    info_ref[...] = info
    rt_ref[...] = info.T[:8]


def _mix(x2, attn, y3, wglu, bglu, gssm, wo_a, wo_s, gmoe, w_router, b_router, tile=512):
    t = x2.shape[0]
    kern = functools.partial(_mix_kernel, tile=tile)
    full = lambda shape: pl.BlockSpec(shape, lambda i: tuple(0 for _ in shape))
    return pl.pallas_call(
        kern, name="mix",
        grid=(t // tile,),
        in_specs=[pl.BlockSpec((tile, D_MODEL), lambda i: (i, 0)),
                  pl.BlockSpec((tile, ATTN_WIDTH), lambda i: (i, 0)),
                  pl.BlockSpec((CHUNK, tile // CHUNK, SSM_WIDTH), lambda i: (0, i, 0)),
                  full((SSM_WIDTH, SSM_WIDTH)), full((1, SSM_WIDTH)), full((1, SSM_WIDTH)),
                  full((ATTN_WIDTH, D_MODEL)), full((SSM_WIDTH, D_MODEL)), full((1, D_MODEL)),
                  full((D_MODEL, 2 * LANES)), full((1, LANES))],
        out_specs=[pl.BlockSpec((tile, D_MODEL), lambda i: (i, 0)),
                   pl.BlockSpec((tile * ROW_TILE, LANES), lambda i: (i, 0)),
                   pl.BlockSpec((tile, LANES), lambda i: (i, 0)),
                   pl.BlockSpec((8, tile), lambda i: (0, i)),
                   pl.BlockSpec((1, LANES), lambda i: (0, 0))],
        out_shape=[jax.ShapeDtypeStruct((t, D_MODEL), F32),
                   jax.ShapeDtypeStruct((t * ROW_TILE, LANES), F32),
                   jax.ShapeDtypeStruct((t, LANES), F32),
                   jax.ShapeDtypeStruct((8, t), F32),
                   jax.ShapeDtypeStruct((1, LANES), F32)],
        scratch_shapes=[pltpu.VMEM((SSM_WIDTH // LANES, tile, LANES), F32), pltpu.VMEM((1, LANES), F32)],
        compiler_params=_params(("arbitrary",), 48),
    )(x2, attn, y3, wglu, bglu, gssm, wo_a, wo_s, gmoe, w_router, b_router)


def _rows_at(ref, row, n_rows=1):
    return ref.at[pl.ds(pl.multiple_of(row * ROW_TILE, ROW_TILE), n_rows * ROW_TILE), :]


def _row_copy(src, s, dst, d, sem):
    return pltpu.make_async_copy(_rows_at(src, s), _rows_at(dst, d), sem)


UNROLL = 4
COMBINE_GROUPS = 8
PAD_SIZES = tuple(1 << b for b in reversed(range(ROW_BLOCK.bit_length() - 1)))


def _dispatch_kernel(dest_ref, pad_start_ref, pad_len_ref, h_ref, rows_out, zbuf, sem, zsem, *, tile, n_tok,
                     n_blocks):
    i = pl.program_id(0)
    base = i * tile

    def zero_fill(e, start):
        off, rem = pad_start_ref[e], pad_len_ref[e]
        for size in PAD_SIZES:
            @pl.when((rem & size) != 0)
            def _():
                cp = pltpu.make_async_copy(_rows_at(zbuf, 0, size), _rows_at(rows_out, off, size), zsem)
                cp.start() if start else cp.wait()
            off = off + (rem & size)

    def zero_tail(start):
        used = (pad_start_ref[N_EXPERTS - 1] + pad_len_ref[N_EXPERTS - 1]) // ROW_BLOCK

        def blk(b, carry):
            for half in range(ROW_BLOCK // PAD_SIZES[0]):
                cp = pltpu.make_async_copy(zbuf, _rows_at(rows_out, b * ROW_BLOCK + half * PAD_SIZES[0],
                                                          PAD_SIZES[0]), zsem)
                cp.start() if start else cp.wait()
            return carry

        lax.fori_loop(used, n_blocks, blk, 0)

    @pl.when(i == 0)
    def _():
        zbuf[...] = jnp.zeros(zbuf.shape, F32)
        lax.fori_loop(0, N_EXPERTS, lambda e, c: (zero_fill(e, True), c)[1], 0)
        zero_tail(True)

    def issue(tb, carry):
        for k in range(UNROLL):
            t = tb * UNROLL + k
            for j in range(2):
                _row_copy(h_ref, t, rows_out, dest_ref[j * n_tok + base + t], sem).start(priority=j)
        return carry

    def drain(tb, carry):
        for _ in range(2 * UNROLL):
            _row_copy(h_ref, 0, rows_out, 0, sem).wait()
        return carry

    lax.fori_loop(0, tile // UNROLL, issue, 0)
    lax.fori_loop(0, tile // UNROLL, drain, 0)

    @pl.when(i == 0)
    def _():
        lax.fori_loop(0, N_EXPERTS, lambda e, c: (zero_fill(e, False), c)[1], 0)
        zero_tail(False)


def _dispatch(dest_flat, pad_start, pad_len, h2, n_rows, tile=512):
    t = h2.shape[0] // ROW_TILE
    kern = functools.partial(_dispatch_kernel, tile=tile, n_tok=t, n_blocks=n_rows // ROW_BLOCK)
    grid_spec = pltpu.PrefetchScalarGridSpec(
        num_scalar_prefetch=3, grid=(t // tile,),
        in_specs=[pl.BlockSpec((tile * ROW_TILE, LANES), lambda i, *_: (i, 0))],
        out_specs=pl.BlockSpec(memory_space=pl.ANY),
        scratch_shapes=[pltpu.VMEM((PAD_SIZES[0] * ROW_TILE, LANES), F32),
                        pltpu.SemaphoreType.DMA(()), pltpu.SemaphoreType.DMA(())])
    return pl.pallas_call(
        kern, name="dispatch", grid_spec=grid_spec,
        out_shape=jax.ShapeDtypeStruct((n_rows * ROW_TILE, LANES), F32),
        compiler_params=_params(("arbitrary",), 32),
    )(dest_flat, pad_start, pad_len, h2)


def _expert_kernel(be_ref, nu_ref, rows_ref, wg_ref, wu_ref, wd_ref, out_ref, wg_b, wu_b, wd_b):
    i = pl.program_id(0)
    changed = (i == 0) | (be_ref[i] != be_ref[jnp.maximum(i - 1, 0)])

    @pl.when(changed)
    def _():
        wg_b[...] = wg_ref[0].astype(BF16)
        wu_b[...] = wu_ref[0].astype(BF16)
        wd_b[...] = wd_ref[0].astype(BF16)

    @pl.when(i < nu_ref[0])
    def _():
        xb = _load_tile_rows(rows_ref).astype(BF16)
        gate = jnp.dot(xb, wg_b[...], preferred_element_type=F32)
        up = jnp.dot(xb, wu_b[...], preferred_element_type=F32)
        hid = (gate * jax.nn.sigmoid(gate) * up).astype(BF16)
        _store_tile_rows(out_ref, jnp.dot(hid, wd_b[...], preferred_element_type=F32))

    @pl.when(i >= nu_ref[0])
    def _():
        out_ref[...] = jnp.zeros(out_ref.shape, F32)


def _experts(block_e, n_used, rows, w_gate, w_up, w_down):
    nb = rows.shape[0] // (ROW_BLOCK * ROW_TILE)
    last = lambda i, nu: jnp.maximum(jnp.minimum(i, nu[0] - 1), 0)
    grid_spec = pltpu.PrefetchScalarGridSpec(
        num_scalar_prefetch=2, grid=(nb,),
        in_specs=[pl.BlockSpec((ROW_BLOCK * ROW_TILE, LANES), lambda i, be, nu: (last(i, nu), 0)),
                  pl.BlockSpec((1, D_MODEL, EXPERT_FF), lambda i, be, nu: (be[i], 0, 0)),
                  pl.BlockSpec((1, D_MODEL, EXPERT_FF), lambda i, be, nu: (be[i], 0, 0)),
                  pl.BlockSpec((1, EXPERT_FF, D_MODEL), lambda i, be, nu: (be[i], 0, 0))],
        out_specs=pl.BlockSpec((ROW_BLOCK * ROW_TILE, LANES), lambda i, be, nu: (i, 0)),
        scratch_shapes=[pltpu.VMEM((D_MODEL, EXPERT_FF), BF16), pltpu.VMEM((D_MODEL, EXPERT_FF), BF16),
                        pltpu.VMEM((EXPERT_FF, D_MODEL), BF16)])
    return pl.pallas_call(
        _expert_kernel, name="experts", grid_spec=grid_spec,
        out_shape=jax.ShapeDtypeStruct(rows.shape, F32),
        compiler_params=_params(("arbitrary",), 48),
    )(block_e, n_used, rows, w_gate, w_up, w_down)


def _combine_kernel(dest_ref, x1_ref, info_ref, gfin_ref, rows_ref, o_ref, buf0, buf1, sems, *, tile, n_tok):
    i = pl.program_id(0)
    slot = i % 2

    def gather(step, to_slot):
        base = step * tile
        b0, b1, sem = buf0.at[to_slot], buf1.at[to_slot], sems.at[to_slot]

        def issue(tb, carry):
            for k in range(UNROLL):
                t = tb * UNROLL + k
                _row_copy(rows_ref, dest_ref[base + t], b0, t, sem).start(priority=0)
                _row_copy(rows_ref, dest_ref[n_tok + base + t], b1, t, sem).start(priority=1)
            return carry

        lax.fori_loop(0, tile // UNROLL, issue, 0)

    def drain(of_slot):
        b0, b1, sem = buf0.at[of_slot], buf1.at[of_slot], sems.at[of_slot]

        def wait(tb, carry):
            for _ in range(UNROLL):
                _row_copy(rows_ref, 0, b0, 0, sem).wait()
                _row_copy(rows_ref, 0, b1, 0, sem).wait()
            return carry

        lax.fori_loop(0, tile // UNROLL, wait, 0)

    @pl.when(i == 0)
    def _():
        gather(0, 0)

    drain(slot)

    last = pl.num_programs(0) - 1
    nxt_base = jnp.minimum(i + 1, last) * tile
    b0, b1 = buf0.at[slot], buf1.at[slot]
    n0, n1, nsem = buf0.at[1 - slot], buf1.at[1 - slot], sems.at[1 - slot]
    group = tile // COMBINE_GROUPS
    for g in range(COMBINE_GROUPS):
        lo = g * group
        for t in range(lo, lo + group):
            _row_copy(rows_ref, dest_ref[nxt_base + t], n0, t, nsem).start(priority=0)
            _row_copy(rows_ref, dest_ref[n_tok + nxt_base + t], n1, t, nsem).start(priority=1)
        info = info_ref[lo:lo + group, :]
        x2 = (x1_ref[lo:lo + group, :] + info[:, 2:3] * _load_tile_rows(b0, lo, group)
              + info[:, 3:4] * _load_tile_rows(b1, lo, group))
        o_ref[lo:lo + group, :] = _rms(x2, gfin_ref[...])

    @pl.when(i == last)
    def _():
        drain(1 - slot)


def _combine(dest_flat, x1, info, gfin, out_rows, tile=256):
    t = x1.shape[0]
    kern = functools.partial(_combine_kernel, tile=tile, n_tok=t)
    grid_spec = pltpu.PrefetchScalarGridSpec(
        num_scalar_prefetch=1, grid=(t // tile,),
        in_specs=[pl.BlockSpec((tile, D_MODEL), lambda i, d: (i, 0)),
                  pl.BlockSpec((tile, LANES), lambda i, d: (i, 0)),
                  pl.BlockSpec((1, D_MODEL), lambda i, d: (0, 0)),
                  pl.BlockSpec(memory_space=pl.ANY)],
        out_specs=pl.BlockSpec((tile, D_MODEL), lambda i, d: (i, 0)),
        scratch_shapes=[pltpu.VMEM((2, tile * ROW_TILE, LANES), F32), pltpu.VMEM((2, tile * ROW_TILE, LANES), F32),
                        pltpu.SemaphoreType.DMA((2,))])
    return pl.pallas_call(
        kern, name="combine", grid_spec=grid_spec,
        out_shape=jax.ShapeDtypeStruct((t, D_MODEL), F32),
        compiler_params=_params(("arbitrary",), 32),
    )(dest_flat, x1, info, gfin, out_rows)


def _split_w_in(w_in):
    def pad_blocks(w):
        w = w.reshape(D_MODEL, N_HEADS * 2, HEAD_DIM)
        return jnp.pad(w, ((0, 0), (0, 0), (0, AUG - HEAD_DIM))).reshape(D_MODEL, N_HEADS * 2 * AUG)
    wq_t = w_in[:, :ATTN_WIDTH].T.astype(BF16)
    wk = pad_blocks(w_in[:, ATTN_WIDTH:2 * ATTN_WIDTH]).astype(BF16)
    wv_t = w_in[:, 2 * ATTN_WIDTH:3 * ATTN_WIDTH].T.astype(BF16)
    wu = w_in[:, 3 * ATTN_WIDTH:].astype(BF16)
    return wq_t, wk, wv_t, wu


def kernel(x, norm_attn, w_in, lambda_q1, lambda_k1, lambda_q2, lambda_k2, attn_subln, ssm_lam_re, ssm_lam_im, ssm_log_dt, ssm_b_re, ssm_b_im, ssm_c_re, ssm_c_im, ssm_d, w_glu, b_glu, ssm_norm, w_out, norm_moe, w_router_group, b_router_group, w_router_expert, b_router_expert, w_gate, w_up, w_down, norm_final):
    batch, seq, d = x.shape
    t = batch * seq
    nchunk = seq // CHUNK
    x2 = x.reshape(t, d)
    l = 0

    q_t, k_aug, v_t, u3 = _in_proj(x2, norm_attn[l][None], *_split_w_in(w_in[l]), seq)
    attn = _attention(q_t, k_aug, v_t, lambda_q1[l][None], lambda_k1[l][None], lambda_q2[l][None],
                      lambda_k2[l][None], attn_subln[l][:, None], batch, seq)

    sc = _ssm_constants(ssm_lam_re[l], ssm_lam_im[l], ssm_log_dt[l], ssm_b_re[l], ssm_b_im[l],
                        ssm_c_re[l], ssm_c_im[l], ssm_d[l], nchunk)
    s_re, s_im = _ssm_state(u3, sc["b_re"], sc["b_im"], sc["a_row_re"], sc["a_row_im"])
    h_re, h_im = _ssm_scan(s_re, s_im, sc["p_re"], sc["p_im"], nchunk)
    y3 = _ssm_out(u3, sc["kt"], h_re, h_im, sc["c_re"], sc["c_im"], sc["a_col_re"], sc["a_col_im"])

    w_router = jnp.concatenate([w_router_group[l], w_router_expert[l]], axis=1).astype(F32)
    w_router = jnp.pad(w_router, ((0, 0), (0, LANES - w_router.shape[1])))
    w_router_hi = w_router.astype(BF16)
    w_router = jnp.concatenate([w_router_hi, (w_router - w_router_hi.astype(F32)).astype(BF16)], axis=1)
    b_router = jnp.concatenate([b_router_group[l], b_router_expert[l]]).astype(F32)
    b_router = jnp.pad(b_router, (0, LANES - b_router.shape[0]))[None]
    x1, h2, info, route_t, cnt = _mix(x2, attn, y3, w_glu[l].astype(BF16), b_glu[l][None], ssm_norm[l][None],
                                      w_out[l][:ATTN_WIDTH].astype(BF16), w_out[l][ATTN_WIDTH:].astype(BF16),
                                      norm_moe[l][None], w_router, b_router)

    experts = route_t[0:2].astype(jnp.int32)
    ranks = route_t[4:6].astype(jnp.int32)
    counts = cnt[0, :N_EXPERTS].astype(jnp.int32)
    padded = ((counts + ROW_BLOCK - 1) // ROW_BLOCK) * ROW_BLOCK
    ids = jnp.arange(N_EXPERTS, dtype=jnp.int32)
    pend = jnp.sum(jnp.where(ids[None, :] <= ids[:, None], padded[None, :], 0), axis=1)
    pstart = pend - padded
    dest = ranks
    for e in range(N_EXPERTS):
        dest = dest + jnp.where(experts == e, pstart[e], 0)
    dest = dest.reshape(-1)
    n_rows = ((2 * t + N_EXPERTS * (ROW_BLOCK - 1) + ROW_BLOCK - 1) // ROW_BLOCK) * ROW_BLOCK
    nb = n_rows // ROW_BLOCK
    n_used = (pend[-1] // ROW_BLOCK).astype(jnp.int32)
    blk = jnp.minimum(jnp.arange(nb, dtype=jnp.int32), n_used - 1) * ROW_BLOCK
    block_e = jnp.minimum(jnp.sum((pend[None, :] <= blk[:, None]).astype(jnp.int32), axis=1), N_EXPERTS - 1)

    rows = _dispatch(dest, pstart + counts, padded - counts, h2, n_rows)
    out_rows = _experts(block_e, n_used[None], rows, w_gate[l], w_up[l], w_down[l])
    out = _combine(dest, x1, info, norm_final[None], out_rows)
    return out.reshape(batch, seq, d)
```

```python
import functools
import math

import jax
import jax.numpy as jnp
import numpy as np
from jax import lax
from jax.experimental import pallas as pl
from jax.experimental.pallas import tpu as pltpu

F32 = jnp.float32
BF16 = jnp.bfloat16

D_MODEL = 1024
N_HEADS = 4
HEAD_DIM = 64
VALUE_DIM = 128
ATTN_WIDTH = 512
SSM_WIDTH = 512
SSM_GROUP = 16
N_GROUPS = 32
SSM_STATE = 64
N_EXPERT_GROUPS = 4
EXPERTS_PER_GROUP = 8
N_EXPERTS = 32
EXPERT_FF = 512
RMS_EPS = 1e-6
LAMBDA_INIT = 0.8 - 0.6 * math.exp(-0.3 * 0)
LOG2E = math.log2(math.e)

CHUNK = 16
HALF_GROUPS = 16
TOEP_SPAN = 4
LANES = 128
AUG = 128
NORM_LANE = 70
QNORM_ROW = 71
V_ROWS = VALUE_DIM + 16
UNDERFLOW_LOG2 = 152.0
ROW_BLOCK = 256
ROW_TILE = D_MODEL // 128
VMEM_LIMIT_CAP = 56 * 1024 * 1024


def _params(dims, vmem_mb):
    return pltpu.CompilerParams(dimension_semantics=dims,
                                vmem_limit_bytes=min(vmem_mb * 1024 * 1024, VMEM_LIMIT_CAP))


def _rms(x, gain):
    return x * lax.rsqrt(jnp.mean(x * x, axis=-1, keepdims=True) + RMS_EPS) * gain


def _store_tile_rows(ref, val):
    n = val.shape[0]
    for c in range(ROW_TILE):
        ref[pl.ds(c, n, stride=ROW_TILE), :] = val[:, c * LANES:(c + 1) * LANES]


def _load_tile_rows(ref, first=0, n=None):
    n = ref.shape[0] // ROW_TILE if n is None else n
    return jnp.concatenate([ref[pl.ds(first * ROW_TILE + c, n, stride=ROW_TILE), :] for c in range(ROW_TILE)],
                           axis=-1)


def _split3(val):
    hi = val.astype(BF16).astype(F32)
    r1 = val - hi
    mid = r1.astype(BF16).astype(F32)
    lo = r1 - mid
    return hi, mid, lo


def _inproj_kernel(x_ref, g_ref, wq_ref, wk_ref, wv_ref, wu_ref, q_ref, k_ref, v_ref, u_ref, ubuf, *, tile, seq):
    i = pl.program_id(0)
    h = _rms(x_ref[...], g_ref[...]).astype(BF16)
    nt = (((1,), (1,)), ((), ()))
    qt = lax.dot_general(wq_ref[...], h, nt, preferred_element_type=F32)
    kp = jnp.dot(h, wk_ref[...], preferred_element_type=F32)
    vt = lax.dot_general(wv_ref[...], h, nt, preferred_element_type=F32)
    ones_row = jnp.where(lax.broadcasted_iota(jnp.int32, (V_ROWS - VALUE_DIM, tile), 0) == 0, 1.0, 0.0)
    for hd in range(N_HEADS):
        v_ref[hd * V_ROWS:hd * V_ROWS + VALUE_DIM, :] = vt[hd * VALUE_DIM:(hd + 1) * VALUE_DIM, :].astype(BF16)
        v_ref[hd * V_ROWS + VALUE_DIM:(hd + 1) * V_ROWS, :] = ones_row.astype(BF16)
    up = jnp.dot(h, wu_ref[...], preferred_element_type=F32)
    pos0 = lax.rem(i * tile, seq)
    pos_k = (pos0 + lax.broadcasted_iota(jnp.int32, (tile, AUG), 0)).astype(F32)
    pos_q = (pos0 + lax.broadcasted_iota(jnp.int32, (1, tile), 1)).astype(F32)
    lane = lax.broadcasted_iota(jnp.int32, (tile, AUG), 1)
    srow = lax.broadcasted_iota(jnp.int32, (AUG - HEAD_DIM, tile), 0)
    qscale = HEAD_DIM ** -0.5 * LOG2E
    for hd in range(N_HEADS):
        slope = 2.0 ** (-8.0 * (hd + 1) / N_HEADS) * LOG2E
        hi, mid, lo = _split3(pos_k * slope)
        k_add = jnp.where(lane == 64, hi,
                          jnp.where(lane == 65, mid,
                                    jnp.where(lane == 66, lo, jnp.where((lane >= 67) & (lane < 70), 1.0, 0.0))))
        hi, mid, lo = _split3(pos_q * slope)
        q_add = jnp.where(srow < 3, 1.0,
                          jnp.where(srow == 3, -hi, jnp.where(srow == 4, -mid, jnp.where(srow == 5, -lo, 0.0))))
        for m in range(2):
            c0 = (hd * 2 + m) * AUG
            qb = (qt[c0 // 2:c0 // 2 + HEAD_DIM, :] * qscale).astype(BF16)
            q_ref[c0:c0 + HEAD_DIM, :] = qb
            qr = qb.astype(F32)
            qnorm2 = jnp.sum(qr * qr, axis=0, keepdims=True) * (1.0 + 2.0 ** -6)
            q_ref[c0 + HEAD_DIM:c0 + AUG, :] = (
                q_add + jnp.where(srow == QNORM_ROW - HEAD_DIM, qnorm2, 0.0)).astype(BF16)
            kb = kp[:, c0:c0 + AUG]
            kr = kb.astype(BF16).astype(F32)
            norm2 = jnp.sum(kr * kr, axis=-1, keepdims=True) * (1.0 + 2.0 ** -6)
            k_ref[:, c0:c0 + AUG] = (kb + k_add + jnp.where(lane == NORM_LANE, norm2, 0.0)).astype(BF16)
    for cb in range(SSM_WIDTH // LANES):
        ubuf[cb] = up[:, cb * LANES:(cb + 1) * LANES]
    per_half = HALF_GROUPS * SSM_GROUP // LANES
    for s in range(CHUNK):
        for cb in range(SSM_WIDTH // LANES):
            c0 = ((cb // per_half) * CHUNK + s) * HALF_GROUPS * SSM_GROUP + (cb % per_half) * LANES
            u_ref[:, c0:c0 + LANES] = ubuf[cb, pl.ds(s, tile // CHUNK, stride=CHUNK), :].astype(BF16)


def _in_proj(x2, gain, wq_t, wk, wv_t, wu, seq, tile=512):
    t = x2.shape[0]
    qk = 2 * N_HEADS * AUG
    kern = functools.partial(_inproj_kernel, tile=tile, seq=seq)
    full = lambda shape: pl.BlockSpec(shape, lambda i: (0, 0))
    return pl.pallas_call(
        kern, name="in_proj",
        grid=(t // tile,),
        in_specs=[pl.BlockSpec((tile, D_MODEL), lambda i: (i, 0)),
                  full((1, D_MODEL)), full((ATTN_WIDTH, D_MODEL)), full((D_MODEL, qk)),
                  full((ATTN_WIDTH, D_MODEL)), full((D_MODEL, SSM_WIDTH))],
        out_specs=[pl.BlockSpec((qk, tile), lambda i: (0, i)),
                   pl.BlockSpec((tile, qk), lambda i: (i, 0)),
                   pl.BlockSpec((N_HEADS * V_ROWS, tile), lambda i: (0, i)),
                   pl.BlockSpec((tile // CHUNK, CHUNK * SSM_WIDTH), lambda i: (i, 0))],
        out_shape=[jax.ShapeDtypeStruct((qk, t), BF16),
                   jax.ShapeDtypeStruct((t, qk), BF16),
                   jax.ShapeDtypeStruct((N_HEADS * V_ROWS, t), BF16),
                   jax.ShapeDtypeStruct((t // CHUNK, CHUNK * SSM_WIDTH), BF16)],
        scratch_shapes=[pltpu.VMEM((SSM_WIDTH // LANES, tile, LANES), F32)],
        compiler_params=_params(("arbitrary",), 48),
    )(x2, gain, wq_t, wk, wv_t, wu)


def _attn_kernel(lq1, lk1, lq2, lk2, sub_ref, q_ref, k_ref, v_ref, o_ref, m_sc, acc_sc, kn_sc, *, tq, kb, strip):
    hd = pl.program_id(1)
    qi = pl.program_id(2)
    lam = (jnp.exp(jnp.sum(lq1[...] * lk1[...], axis=-1, keepdims=True))
           - jnp.exp(jnp.sum(lq2[...] * lk2[...], axis=-1, keepdims=True)) + LAMBDA_INIT)

    @pl.when(qi == 0)
    def _():
        for m in range(2):
            kn_sc[m] = jnp.max(k_ref[:, m * AUG:(m + 1) * AUG].astype(F32), axis=0, keepdims=True)

    def step(start, nkeys, align, q_lo=0, nq=tq, key_lo=None, first=False):
        r0 = pl.multiple_of(start, align)
        kblk = k_ref[pl.ds(r0, nkeys), :]
        vblk = v_ref[:, pl.ds(r0, nkeys)]
        scores = [jnp.dot(kblk[:, m * AUG:(m + 1) * AUG], q_ref[m * AUG:(m + 1) * AUG, q_lo:q_lo + nq],
                          preferred_element_type=F32) for m in range(2)]
        for m in range(2):
            s = scores[m]
            if key_lo is not None:
                key = lax.broadcasted_iota(jnp.int32, (nkeys, nq), 0) + key_lo
                qry = lax.broadcasted_iota(jnp.int32, (nkeys, nq), 1) + q_lo
                s = jnp.where(key <= qry, s, -jnp.inf)
            if first:
                m_new = jnp.max(s, axis=0, keepdims=True)
                pv = jnp.dot(vblk, jnp.exp2(s - m_new).astype(BF16), preferred_element_type=F32)
            else:
                m_prev = m_sc[m, :, q_lo:q_lo + nq]
                m_new = jnp.maximum(m_prev, jnp.max(s, axis=0, keepdims=True))
                pv = (jnp.exp2(m_prev - m_new) * acc_sc[m, :, q_lo:q_lo + nq]
                      + jnp.dot(vblk, jnp.exp2(s - m_new).astype(BF16), preferred_element_type=F32))
            acc_sc[m, :, q_lo:q_lo + nq] = pv
            m_sc[m, :, q_lo:q_lo + nq] = m_new

    for lo in range(0, tq, strip):
        step(qi * tq + lo, strip, strip, q_lo=lo, nq=tq - lo, key_lo=lo, first=lo == 0)

    lane = lax.broadcasted_iota(jnp.int32, (1, AUG), 1)
    sub = lax.broadcasted_iota(jnp.int32, (16, 1), 0)
    slope = LOG2E * jnp.exp2(-2.0 * (jnp.full((1, 1), hd, jnp.int32) + 1).astype(F32))
    reach = jnp.zeros((1, 1), F32)
    for m in range(2):
        qrows = jnp.max(q_ref[m * AUG + HEAD_DIM:m * AUG + HEAD_DIM + 16, :].astype(F32), axis=-1, keepdims=True)
        q2 = jnp.max(jnp.where(sub == QNORM_ROW - HEAD_DIM, qrows, 0.0), axis=0, keepdims=True)
        k2 = jnp.max(jnp.where(lane == NORM_LANE, kn_sc[m], 0.0), axis=-1, keepdims=True)
        m_min = jnp.min(m_sc[m], axis=-1, keepdims=True)
        reach = jnp.maximum(reach, (jnp.sqrt(q2 * k2) * 1.001 + (UNDERFLOW_LOG2 + 0.5) - m_min) / slope)
    need = jnp.ceil((reach + (kb - 1)) / kb).astype(jnp.int32) - 1
    n_below = jnp.minimum(jnp.max(jnp.maximum(need, 0)), qi * (tq // kb))
    top = qi * tq

    def quad(i, carry):
        step(top - (2 + 4 * i) * kb, 2 * kb, kb)
        step(top - (4 + 4 * i) * kb, 2 * kb, kb)
        return carry

    lax.fori_loop(0, n_below // 4, quad, 0)
    rest = n_below % 4
    done = n_below - rest

    @pl.when(rest >= 2)
    def _():
        step(top - (done + 2) * kb, 2 * kb, kb)

    @pl.when(rest % 2 == 1)
    def _():
        step(top - n_below * kb, kb, kb)

    l0 = acc_sc[0, VALUE_DIM:VALUE_DIM + 1, :]
    l1 = acc_sc[1, VALUE_DIM:VALUE_DIM + 1, :]
    o = acc_sc[0, :VALUE_DIM, :] / l0 - lam * (acc_sc[1, :VALUE_DIM, :] / l1)
    o = o * lax.rsqrt(jnp.mean(o * o, axis=0, keepdims=True) + RMS_EPS) * sub_ref[...] * (1.0 - LAMBDA_INIT)
    o_ref[...] = o.T.astype(BF16)


def _attention(q_t, k_aug, v_t, lq1, lk1, lq2, lk2, subln_col, batch, seq, tq=1024, kb=512, strip=256):
    tq = min(tq, seq)
    nq = seq // tq
    t = batch * seq
    small = pl.BlockSpec((1, HEAD_DIM), lambda b, h, i: (0, 0))
    kern = functools.partial(_attn_kernel, tq=tq, kb=kb, strip=strip)
    return pl.pallas_call(
        kern, name="attention",
        grid=(batch, N_HEADS, nq),
        in_specs=[small, small, small, small,
                  pl.BlockSpec((VALUE_DIM, 1), lambda b, h, i: (0, 0)),
                  pl.BlockSpec((2 * AUG, tq), lambda b, h, i: (h, b * nq + i)),
                  pl.BlockSpec((seq, 2 * AUG), lambda b, h, i: (b, h)),
                  pl.BlockSpec((V_ROWS, seq), lambda b, h, i: (h, b))],
        out_specs=pl.BlockSpec((tq, VALUE_DIM), lambda b, h, i: (b * nq + i, h)),
        out_shape=jax.ShapeDtypeStruct((t, ATTN_WIDTH), BF16),
        scratch_shapes=[pltpu.VMEM((2, 1, tq), F32), pltpu.VMEM((2, V_ROWS, tq), F32),
                        pltpu.VMEM((2, 1, AUG), F32)],
        compiler_params=_params(("arbitrary", "arbitrary", "arbitrary"), 48),
    )(lq1, lk1, lq2, lk2, subln_col, q_t, k_aug, v_t)


def _ssm_state_kernel(u_ref, bre_ref, bim_ref, are_ref, aim_ref, sre_ref, sim_ref, wre, wim):
    i = pl.program_id(1)

    @pl.when(i == 0)
    def _():
        wre[...] = bre_ref[0]
        wim[...] = bim_ref[0]
        sre_ref[...] = jnp.zeros(sre_ref.shape, F32)
        sim_ref[...] = jnp.zeros(sim_ref.shape, F32)

    @pl.when(i > 0)
    def _():
        ar, ai = are_ref[0], aim_ref[0]
        wr, wi = wre[...], wim[...]
        wre[...] = wr * ar - wi * ai
        wim[...] = wr * ai + wi * ar

    u = u_ref[...]
    sre_ref[...] += jnp.dot(u, wre[...].astype(BF16), preferred_element_type=F32)
    sim_ref[...] += jnp.dot(u, wim[...].astype(BF16), preferred_element_type=F32)


def _ssm_state(u3, b_re, b_im, a_row_re, a_row_im):
    nch = u3.shape[0]
    hw = HALF_GROUPS * SSM_GROUP
    sw = HALF_GROUPS * SSM_STATE
    return pl.pallas_call(
        _ssm_state_kernel, name="ssm_state",
        grid=(2, CHUNK),
        in_specs=[pl.BlockSpec((nch, hw), lambda hf, i: (0, hf * CHUNK + CHUNK - 1 - i)),
                  pl.BlockSpec((1, hw, sw), lambda hf, i: (hf, 0, 0)),
                  pl.BlockSpec((1, hw, sw), lambda hf, i: (hf, 0, 0)),
                  pl.BlockSpec((1, 1, sw), lambda hf, i: (hf, 0, 0)),
                  pl.BlockSpec((1, 1, sw), lambda hf, i: (hf, 0, 0))],
        out_specs=[pl.BlockSpec((nch, sw), lambda hf, i: (0, hf)),
                   pl.BlockSpec((nch, sw), lambda hf, i: (0, hf))],
        out_shape=[jax.ShapeDtypeStruct((nch, 2 * sw), F32)] * 2,
        scratch_shapes=[pltpu.VMEM((hw, sw), F32), pltpu.VMEM((hw, sw), F32)],
        compiler_params=_params(("arbitrary", "arbitrary"), 48),
    )(u3, b_re, b_im, a_row_re, a_row_im)


def _ssm_scan_kernel(sre_ref, sim_ref, pre_ref, pim_ref, hre_ref, him_ref, *, nchunk, nsteps):
    hr, hi = sre_ref[...], sim_ref[...]
    row = lax.rem(lax.broadcasted_iota(jnp.int32, hr.shape, 0), nchunk)
    for k in range(nsteps):
        d = 1 << k
        ar, ai = pre_ref[0, k:k + 1, :], pim_ref[0, k:k + 1, :]
        keep = row >= d
        pr = jnp.where(keep, pltpu.roll(hr, d, 0), 0.0)
        pi = jnp.where(keep, pltpu.roll(hi, d, 0), 0.0)
        hr, hi = hr + ar * pr - ai * pi, hi + ar * pi + ai * pr
    keep = row >= 1
    hre_ref[...] = jnp.where(keep, pltpu.roll(hr, 1, 0), 0.0).astype(BF16)
    him_ref[...] = jnp.where(keep, pltpu.roll(hi, 1, 0), 0.0).astype(BF16)


def _ssm_scan(s_re, s_im, p_re, p_im, nchunk, cols=512):
    nrow, width = s_re.shape
    nsteps = p_re.shape[1]
    per_half = (width // 2) // cols
    kern = functools.partial(_ssm_scan_kernel, nchunk=nchunk, nsteps=nsteps)
    blk = pl.BlockSpec((nrow, cols), lambda j: (0, j))
    pblk = pl.BlockSpec((1, nsteps, cols), lambda j: (j // per_half, 0, j % per_half))
    return pl.pallas_call(
        kern, name="ssm_scan",
        grid=(width // cols,),
        in_specs=[blk, blk, pblk, pblk],
        out_specs=[blk, blk],
        out_shape=[jax.ShapeDtypeStruct((nrow, width), BF16)] * 2,
        compiler_params=_params(("arbitrary",), 48),
    )(s_re, s_im, p_re, p_im)


def _ssm_out_kernel(u_ref, kt_ref, hre_ref, him_ref, cre_ref, cim_ref, are_ref, aim_ref, y_ref, wre, wim, toep):
    t = pl.program_id(1)
    ar, ai = are_ref[0], aim_ref[0]

    @pl.when(t == 0)
    def _():
        cr, ci = cre_ref[0], cim_ref[0]
        wre[...] = cr * ar - ci * ai
        wim[...] = cr * ai + ci * ar

    @pl.when(t > 0)
    def _():
        wr, wi = wre[...], wim[...]
        wre[...] = wr * ar - wi * ai
        wim[...] = wr * ai + wi * ar

    hw = kt_ref.shape[-1]
    for s in range(CHUNK):
        tile = kt_ref[jnp.maximum(t - s, 0), 0]
        toep[s * hw:(s + 1) * hw, :] = jnp.where(s <= t, tile, jnp.zeros_like(tile))

    span = TOEP_SPAN * hw
    y_ref[0] = (jnp.dot(u_ref[:, :span], toep[:span, :], preferred_element_type=F32)
                + jnp.dot(hre_ref[...], wre[...].astype(BF16), preferred_element_type=F32)
                - jnp.dot(him_ref[...], wim[...].astype(BF16), preferred_element_type=F32))
    for piece in range(1, CHUNK // TOEP_SPAN):
        @pl.when(t >= piece * TOEP_SPAN)
        def _():
            lo = piece * span
            y_ref[0] += jnp.dot(u_ref[:, lo:lo + span], toep[lo:lo + span, :], preferred_element_type=F32)


def _ssm_out(u3, kt, h_re, h_im, c_re, c_im, a_col_re, a_col_im):
    nch = u3.shape[0]
    hw = HALF_GROUPS * SSM_GROUP
    sw = HALF_GROUPS * SSM_STATE
    return pl.pallas_call(
        _ssm_out_kernel, name="ssm_out",
        grid=(2, CHUNK),
        in_specs=[pl.BlockSpec((nch, CHUNK * hw), lambda hf, t: (0, hf)),
                  pl.BlockSpec((CHUNK, 1, hw, hw), lambda hf, t: (0, hf, 0, 0)),
                  pl.BlockSpec((nch, sw), lambda hf, t: (0, hf)),
                  pl.BlockSpec((nch, sw), lambda hf, t: (0, hf)),
                  pl.BlockSpec((1, sw, hw), lambda hf, t: (hf, 0, 0)),
                  pl.BlockSpec((1, sw, hw), lambda hf, t: (hf, 0, 0)),
                  pl.BlockSpec((1, sw, 1), lambda hf, t: (hf, 0, 0)),
                  pl.BlockSpec((1, sw, 1), lambda hf, t: (hf, 0, 0))],
        out_specs=pl.BlockSpec((1, nch, hw), lambda hf, t: (t, 0, hf)),
        out_shape=jax.ShapeDtypeStruct((CHUNK, nch, SSM_WIDTH), F32),
        scratch_shapes=[pltpu.VMEM((sw, hw), F32), pltpu.VMEM((sw, hw), F32), pltpu.VMEM((CHUNK * hw, hw), BF16)],
        compiler_params=_params(("arbitrary", "arbitrary"), 48),
    )(u3, kt, h_re, h_im, c_re, c_im, a_col_re, a_col_im)


def _ssm_constants(lam_re, lam_im, log_dt, b_re, b_im, c_re, c_im, d_skip, nchunk):
    lr, li = lam_re.astype(F32), lam_im.astype(F32)
    dt = jnp.exp(log_dt.astype(F32))[:, None]

    def lam_bar_pow(k):
        mag = jnp.exp(k * lr * dt)
        return mag * jnp.cos(k * li * dt), mag * jnp.sin(k * li * dt)

    a_re, a_im = lam_bar_pow(1.0)
    den = lr * lr + li * li
    coef_re = ((a_re - 1.0) * lr + a_im * li) / den
    coef_im = (a_im * lr - (a_re - 1.0) * li) / den
    bb_re = coef_re[..., None] * b_re.astype(F32) - coef_im[..., None] * b_im.astype(F32)
    bb_im = coef_re[..., None] * b_im.astype(F32) + coef_im[..., None] * b_re.astype(F32)
    cc_re, cc_im = c_re.astype(F32), c_im.astype(F32)
    lags = jnp.arange(CHUNK, dtype=F32)[:, None, None]
    pw_re, pw_im = lam_bar_pow(lags)
    pb_re = pw_re[..., None] * bb_re[None] - pw_im[..., None] * bb_im[None]
    pb_im = pw_re[..., None] * bb_im[None] + pw_im[..., None] * bb_re[None]
    kt = jnp.einsum('gcn,jgnd->jgdc', cc_re, pb_re) - jnp.einsum('gcn,jgnd->jgdc', cc_im, pb_im)
    skip = jnp.einsum('gc,dc->gdc', d_skip.astype(F32), jnp.eye(SSM_GROUP, dtype=F32))
    kt = jnp.concatenate([kt[:1] + skip[None], kt[1:]], axis=0)
    hw, sw = HALF_GROUPS * SSM_GROUP, HALF_GROUPS * SSM_STATE

    def block_diag(rows, row_group, col_group):
        wide = jnp.tile(rows, (1,) * (rows.ndim - 1) + (HALF_GROUPS,))
        r = lax.broadcasted_iota(jnp.int32, wide.shape, wide.ndim - 2) // row_group
        c = lax.broadcasted_iota(jnp.int32, wide.shape, wide.ndim - 1) // col_group
        return jnp.where(r == c, wide, 0.0)

    kt_t = block_diag(kt.reshape(CHUNK, 2, hw, SSM_GROUP), SSM_GROUP, SSM_GROUP).astype(BF16)

    def b_tiles(part):
        p = part.reshape(2, HALF_GROUPS, SSM_STATE, SSM_GROUP).transpose(0, 1, 3, 2)
        return block_diag(p.reshape(2, hw, SSM_STATE), SSM_GROUP, SSM_STATE)

    def c_tiles(part):
        p = part.reshape(2, HALF_GROUPS, SSM_GROUP, SSM_STATE).transpose(0, 1, 3, 2)
        return block_diag(p.reshape(2, sw, SSM_GROUP), SSM_STATE, SSM_GROUP)

    nsteps = max(int(math.log2(nchunk)), 1)
    steps = (CHUNK * 2.0 ** jnp.arange(nsteps, dtype=F32))[:, None, None]
    st_re, st_im = lam_bar_pow(steps)
    by_half = lambda p: p.reshape(nsteps, 2, sw).transpose(1, 0, 2)
    ar_h, ai_h = a_re.reshape(2, sw), a_im.reshape(2, sw)
    return dict(kt=kt_t, b_re=b_tiles(bb_re), b_im=b_tiles(bb_im),
                c_re=c_tiles(cc_re), c_im=c_tiles(cc_im),
                a_row_re=ar_h[:, None, :], a_row_im=ai_h[:, None, :],
                a_col_re=ar_h[:, :, None], a_col_im=ai_h[:, :, None],
                p_re=by_half(st_re), p_im=by_half(st_im))


def _mix_kernel(x_ref, attn_ref, y3_ref, wglu_ref, bglu_ref, gssm_ref, woa_ref, wos_ref, gmoe_ref,
                wr_ref, br_ref, x1_ref, h2_ref, info_ref, rt_ref, cnt_ref, ybuf, carry, *, tile):
    i = pl.program_id(0)

    @pl.when(i == 0)
    def _():
        carry[...] = jnp.zeros(carry.shape, F32)

    for s in range(CHUNK):
        for cb in range(SSM_WIDTH // LANES):
            ybuf[cb, pl.ds(s, tile // CHUNK, stride=CHUNK), :] = y3_ref[s, :, cb * LANES:(cb + 1) * LANES]
    y = jax.nn.gelu(jnp.concatenate([ybuf[cb] for cb in range(SSM_WIDTH // LANES)], axis=-1))
    z = jnp.dot(y.astype(BF16), wglu_ref[...], preferred_element_type=F32) + bglu_ref[...]
    y = y * jax.nn.sigmoid(z)
    ssm = _rms(y, gssm_ref[...])
    x1 = (x_ref[...] + jnp.dot(attn_ref[...], woa_ref[...], preferred_element_type=F32)
          + jnp.dot(ssm.astype(BF16), wos_ref[...], preferred_element_type=F32))
    x1_ref[...] = x1
    h2 = _rms(x1, gmoe_ref[...])
    _store_tile_rows(h2_ref, h2)

    h_hi = h2.astype(BF16)
    h_lo = (h2 - h_hi.astype(F32)).astype(BF16)
    wr = wr_ref[...]
    both = jnp.dot(h_hi, wr, preferred_element_type=F32)
    logits = (both[:, :LANES] + both[:, LANES:]
              + jnp.dot(h_lo, wr[:, :LANES], preferred_element_type=F32) + br_ref[...])
    lane = lax.broadcasted_iota(jnp.int32, logits.shape, 1)
    neg = -jnp.inf
    gl = jnp.where(lane < N_EXPERT_GROUPS, logits, neg)
    gmax = jnp.max(gl, axis=-1, keepdims=True)
    gsel = jnp.min(jnp.where(gl == gmax, lane, LANES), axis=-1, keepdims=True)
    p_group = 1.0 / jnp.sum(jnp.exp(gl - gmax), axis=-1, keepdims=True)
    elane = lane - N_EXPERT_GROUPS
    in_grp = (elane >= 0) & (elane < N_EXPERTS) & ((elane >> 3) == gsel)
    el = jnp.where(in_grp, logits, neg)
    m1 = jnp.max(el, axis=-1, keepdims=True)
    i1 = jnp.min(jnp.where(el == m1, lane, LANES), axis=-1, keepdims=True)
    den = jnp.sum(jnp.exp(el - m1), axis=-1, keepdims=True)
    el2 = jnp.where(lane == i1, neg, el)
    m2 = jnp.max(el2, axis=-1, keepdims=True)
    i2 = jnp.min(jnp.where(el2 == m2, lane, LANES), axis=-1, keepdims=True)
    g0 = p_group / den
    g1 = p_group * jnp.exp(m2 - m1) / den
    e0 = i1 - N_EXPERT_GROUPS
    e1 = i2 - N_EXPERT_GROUPS

    hit0 = lane == e0
    hit1 = lane == e1
    onehot = jnp.where(hit0 | hit1, 1.0, 0.0)
    r = lax.broadcasted_iota(jnp.int32, (tile, tile), 0)
    c = lax.broadcasted_iota(jnp.int32, (tile, tile), 1)
    tril = jnp.where(c < r, 1.0, 0.0).astype(BF16)
    before = jnp.dot(tril, onehot.astype(BF16), preferred_element_type=F32) + carry[...]
    rank0 = jnp.sum(jnp.where(hit0, before, 0.0), axis=-1, keepdims=True)
    rank1 = jnp.sum(jnp.where(hit1, before, 0.0), axis=-1, keepdims=True)
    carry[...] += jnp.sum(onehot, axis=0, keepdims=True)
    cnt_ref[...] = carry[...]
    info = jnp.where(lane == 0, e0.astype(F32),
                     jnp.where(lane == 1, e1.astype(F32),
                               jnp.where(lane == 2, g0,
                                         jnp.where(lane == 3, g1,
                                                   jnp.where(lane == 4, rank0, jnp.where(lane == 5, rank1, 0.0))))))
    info_ref[...] = info
    rt_ref[...] = info.T[:8]


def _mix(x2, attn, y3, wglu, bglu, gssm, wo_a, wo_s, gmoe, w_router, b_router, tile=512):
    t = x2.shape[0]
    kern = functools.partial(_mix_kernel, tile=tile)
    full = lambda shape: pl.BlockSpec(shape, lambda i: tuple(0 for _ in shape))
    return pl.pallas_call(
        kern, name="mix",
        grid=(t // tile,),
        in_specs=[pl.BlockSpec((tile, D_MODEL), lambda i: (i, 0)),
                  pl.BlockSpec((tile, ATTN_WIDTH), lambda i: (i, 0)),
                  pl.BlockSpec((CHUNK, tile // CHUNK, SSM_WIDTH), lambda i: (0, i, 0)),
                  full((SSM_WIDTH, SSM_WIDTH)), full((1, SSM_WIDTH)), full((1, SSM_WIDTH)),
                  full((ATTN_WIDTH, D_MODEL)), full((SSM_WIDTH, D_MODEL)), full((1, D_MODEL)),
                  full((D_MODEL, 2 * LANES)), full((1, LANES))],
        out_specs=[pl.BlockSpec((tile, D_MODEL), lambda i: (i, 0)),
                   pl.BlockSpec((tile * ROW_TILE, LANES), lambda i: (i, 0)),
                   pl.BlockSpec((tile, LANES), lambda i: (i, 0)),
                   pl.BlockSpec((8, tile), lambda i: (0, i)),
                   pl.BlockSpec((1, LANES), lambda i: (0, 0))],
        out_shape=[jax.ShapeDtypeStruct((t, D_MODEL), F32),
                   jax.ShapeDtypeStruct((t * ROW_TILE, LANES), F32),
                   jax.ShapeDtypeStruct((t, LANES), F32),
                   jax.ShapeDtypeStruct((8, t), F32),
                   jax.ShapeDtypeStruct((1, LANES), F32)],
        scratch_shapes=[pltpu.VMEM((SSM_WIDTH // LANES, tile, LANES), F32), pltpu.VMEM((1, LANES), F32)],
        compiler_params=_params(("arbitrary",), 48),
    )(x2, attn, y3, wglu, bglu, gssm, wo_a, wo_s, gmoe, w_router, b_router)


def _rows_at(ref, row, n_rows=1):
    return ref.at[pl.ds(pl.multiple_of(row * ROW_TILE, ROW_TILE), n_rows * ROW_TILE), :]


def _row_copy(src, s, dst, d, sem):
    return pltpu.make_async_copy(_rows_at(src, s), _rows_at(dst, d), sem)


UNROLL = 4
COMBINE_GROUPS = 8
PAD_SIZES = tuple(1 << b for b in reversed(range(ROW_BLOCK.bit_length() - 1)))


def _dispatch_kernel(dest_ref, pad_start_ref, pad_len_ref, h_ref, rows_out, zbuf, sem, zsem, *, tile, n_tok,
                     n_blocks):
    i = pl.program_id(0)
    base = i * tile

    def zero_fill(e, start):
        off, rem = pad_start_ref[e], pad_len_ref[e]
        for size in PAD_SIZES:
            @pl.when((rem & size) != 0)
            def _():
                cp = pltpu.make_async_copy(_rows_at(zbuf, 0, size), _rows_at(rows_out, off, size), zsem)
                cp.start() if start else cp.wait()
            off = off + (rem & size)

    def zero_tail(start):
        used = (pad_start_ref[N_EXPERTS - 1] + pad_len_ref[N_EXPERTS - 1]) // ROW_BLOCK

        def blk(b, carry):
            for half in range(ROW_BLOCK // PAD_SIZES[0]):
                cp = pltpu.make_async_copy(zbuf, _rows_at(rows_out, b * ROW_BLOCK + half * PAD_SIZES[0],
                                                          PAD_SIZES[0]), zsem)
                cp.start() if start else cp.wait()
            return carry

        lax.fori_loop(used, n_blocks, blk, 0)

    @pl.when(i == 0)
    def _():
        zbuf[...] = jnp.zeros(zbuf.shape, F32)
        lax.fori_loop(0, N_EXPERTS, lambda e, c: (zero_fill(e, True), c)[1], 0)
        zero_tail(True)

    def issue(tb, carry):
        for k in range(UNROLL):
            t = tb * UNROLL + k
            for j in range(2):
                _row_copy(h_ref, t, rows_out, dest_ref[j * n_tok + base + t], sem).start(priority=j)
        return carry

    def drain(tb, carry):
        for _ in range(2 * UNROLL):
            _row_copy(h_ref, 0, rows_out, 0, sem).wait()
        return carry

    lax.fori_loop(0, tile // UNROLL, issue, 0)
    lax.fori_loop(0, tile // UNROLL, drain, 0)

    @pl.when(i == 0)
    def _():
        lax.fori_loop(0, N_EXPERTS, lambda e, c: (zero_fill(e, False), c)[1], 0)
        zero_tail(False)


def _dispatch(dest_flat, pad_start, pad_len, h2, n_rows, tile=512):
    t = h2.shape[0] // ROW_TILE
    kern = functools.partial(_dispatch_kernel, tile=tile, n_tok=t, n_blocks=n_rows // ROW_BLOCK)
    grid_spec = pltpu.PrefetchScalarGridSpec(
        num_scalar_prefetch=3, grid=(t // tile,),
        in_specs=[pl.BlockSpec((tile * ROW_TILE, LANES), lambda i, *_: (i, 0))],
        out_specs=pl.BlockSpec(memory_space=pl.ANY),
        scratch_shapes=[pltpu.VMEM((PAD_SIZES[0] * ROW_TILE, LANES), F32),
                        pltpu.SemaphoreType.DMA(()), pltpu.SemaphoreType.DMA(())])
    return pl.pallas_call(
        kern, name="dispatch", grid_spec=grid_spec,
        out_shape=jax.ShapeDtypeStruct((n_rows * ROW_TILE, LANES), F32),
        compiler_params=_params(("arbitrary",), 32),
    )(dest_flat, pad_start, pad_len, h2)


def _expert_kernel(be_ref, nu_ref, rows_ref, wg_ref, wu_ref, wd_ref, out_ref, wg_b, wu_b, wd_b):
    i = pl.program_id(0)
    changed = (i == 0) | (be_ref[i] != be_ref[jnp.maximum(i - 1, 0)])

    @pl.when(changed)
    def _():
        wg_b[...] = wg_ref[0].astype(BF16)
        wu_b[...] = wu_ref[0].astype(BF16)
        wd_b[...] = wd_ref[0].astype(BF16)

    @pl.when(i < nu_ref[0])
    def _():
        xb = _load_tile_rows(rows_ref).astype(BF16)
        gate = jnp.dot(xb, wg_b[...], preferred_element_type=F32)
        up = jnp.dot(xb, wu_b[...], preferred_element_type=F32)
        hid = (gate * jax.nn.sigmoid(gate) * up).astype(BF16)
        _store_tile_rows(out_ref, jnp.dot(hid, wd_b[...], preferred_element_type=F32))

    @pl.when(i >= nu_ref[0])
    def _():
        out_ref[...] = jnp.zeros(out_ref.shape, F32)


def _experts(block_e, n_used, rows, w_gate, w_up, w_down):
    nb = rows.shape[0] // (ROW_BLOCK * ROW_TILE)
    last = lambda i, nu: jnp.maximum(jnp.minimum(i, nu[0] - 1), 0)
    grid_spec = pltpu.PrefetchScalarGridSpec(
        num_scalar_prefetch=2, grid=(nb,),
        in_specs=[pl.BlockSpec((ROW_BLOCK * ROW_TILE, LANES), lambda i, be, nu: (last(i, nu), 0)),
                  pl.BlockSpec((1, D_MODEL, EXPERT_FF), lambda i, be, nu: (be[i], 0, 0)),
                  pl.BlockSpec((1, D_MODEL, EXPERT_FF), lambda i, be, nu: (be[i], 0, 0)),
                  pl.BlockSpec((1, EXPERT_FF, D_MODEL), lambda i, be, nu: (be[i], 0, 0))],
        out_specs=pl.BlockSpec((ROW_BLOCK * ROW_TILE, LANES), lambda i, be, nu: (i, 0)),
        scratch_shapes=[pltpu.VMEM((D_MODEL, EXPERT_FF), BF16), pltpu.VMEM((D_MODEL, EXPERT_FF), BF16),
                        pltpu.VMEM((EXPERT_FF, D_MODEL), BF16)])
    return pl.pallas_call(
        _expert_kernel, name="experts", grid_spec=grid_spec,
        out_shape=jax.ShapeDtypeStruct(rows.shape, F32),
        compiler_params=_params(("arbitrary",), 48),
    )(block_e, n_used, rows, w_gate, w_up, w_down)


def _combine_kernel(dest_ref, x1_ref, info_ref, gfin_ref, rows_ref, o_ref, buf0, buf1, sems, *, tile, n_tok):
    i = pl.program_id(0)
    slot = i % 2

    def gather(step, to_slot):
        base = step * tile
        b0, b1, sem = buf0.at[to_slot], buf1.at[to_slot], sems.at[to_slot]

        def issue(tb, carry):
            for k in range(UNROLL):
                t = tb * UNROLL + k
                _row_copy(rows_ref, dest_ref[base + t], b0, t, sem).start(priority=0)
                _row_copy(rows_ref, dest_ref[n_tok + base + t], b1, t, sem).start(priority=1)
            return carry

        lax.fori_loop(0, tile // UNROLL, issue, 0)

    def drain(of_slot):
        b0, b1, sem = buf0.at[of_slot], buf1.at[of_slot], sems.at[of_slot]

        def wait(tb, carry):
            for _ in range(UNROLL):
                _row_copy(rows_ref, 0, b0, 0, sem).wait()
                _row_copy(rows_ref, 0, b1, 0, sem).wait()
            return carry

        lax.fori_loop(0, tile // UNROLL, wait, 0)

    @pl.when(i == 0)
    def _():
        gather(0, 0)

    drain(slot)

    last = pl.num_programs(0) - 1
    nxt_base = jnp.minimum(i + 1, last) * tile
    b0, b1 = buf0.at[slot], buf1.at[slot]
    n0, n1, nsem = buf0.at[1 - slot], buf1.at[1 - slot], sems.at[1 - slot]
    group = tile // COMBINE_GROUPS
    for g in range(COMBINE_GROUPS):
        lo = g * group
        for t in range(lo, lo + group):
            _row_copy(rows_ref, dest_ref[nxt_base + t], n0, t, nsem).start(priority=0)
            _row_copy(rows_ref, dest_ref[n_tok + nxt_base + t], n1, t, nsem).start(priority=1)
        info = info_ref[lo:lo + group, :]
        x2 = (x1_ref[lo:lo + group, :] + info[:, 2:3] * _load_tile_rows(b0, lo, group)
              + info[:, 3:4] * _load_tile_rows(b1, lo, group))
        o_ref[lo:lo + group, :] = _rms(x2, gfin_ref[...])

    @pl.when(i == last)
    def _():
        drain(1 - slot)


def _combine(dest_flat, x1, info, gfin, out_rows, tile=256):
    t = x1.shape[0]
    kern = functools.partial(_combine_kernel, tile=tile, n_tok=t)
    grid_spec = pltpu.PrefetchScalarGridSpec(
        num_scalar_prefetch=1, grid=(t // tile,),
        in_specs=[pl.BlockSpec((tile, D_MODEL), lambda i, d: (i, 0)),
                  pl.BlockSpec((tile, LANES), lambda i, d: (i, 0)),
                  pl.BlockSpec((1, D_MODEL), lambda i, d: (0, 0)),
                  pl.BlockSpec(memory_space=pl.ANY)],
        out_specs=pl.BlockSpec((tile, D_MODEL), lambda i, d: (i, 0)),
        scratch_shapes=[pltpu.VMEM((2, tile * ROW_TILE, LANES), F32), pltpu.VMEM((2, tile * ROW_TILE, LANES), F32),
                        pltpu.SemaphoreType.DMA((2,))])
    return pl.pallas_call(
        kern, name="combine", grid_spec=grid_spec,
        out_shape=jax.ShapeDtypeStruct((t, D_MODEL), F32),
        compiler_params=_params(("arbitrary",), 32),
    )(dest_flat, x1, info, gfin, out_rows)


def _split_w_in(w_in):
    def pad_blocks(w):
        w = w.reshape(D_MODEL, N_HEADS * 2, HEAD_DIM)
        return jnp.pad(w, ((0, 0), (0, 0), (0, AUG - HEAD_DIM))).reshape(D_MODEL, N_HEADS * 2 * AUG)
    wq_t = w_in[:, :ATTN_WIDTH].T.astype(BF16)
    wk = pad_blocks(w_in[:, ATTN_WIDTH:2 * ATTN_WIDTH]).astype(BF16)
    wv_t = w_in[:, 2 * ATTN_WIDTH:3 * ATTN_WIDTH].T.astype(BF16)
    wu = w_in[:, 3 * ATTN_WIDTH:].astype(BF16)
    return wq_t, wk, wv_t, wu


def kernel(x, norm_attn, w_in, lambda_q1, lambda_k1, lambda_q2, lambda_k2, attn_subln, ssm_lam_re, ssm_lam_im, ssm_log_dt, ssm_b_re, ssm_b_im, ssm_c_re, ssm_c_im, ssm_d, w_glu, b_glu, ssm_norm, w_out, norm_moe, w_router_group, b_router_group, w_router_expert, b_router_expert, w_gate, w_up, w_down, norm_final):
    batch, seq, d = x.shape
    t = batch * seq
    nchunk = seq // CHUNK
    x2 = x.reshape(t, d)
    l = 0

    q_t, k_aug, v_t, u3 = _in_proj(x2, norm_attn[l][None], *_split_w_in(w_in[l]), seq)
    attn = _attention(q_t, k_aug, v_t, lambda_q1[l][None], lambda_k1[l][None], lambda_q2[l][None],
                      lambda_k2[l][None], attn_subln[l][:, None], batch, seq)

    sc = _ssm_constants(ssm_lam_re[l], ssm_lam_im[l], ssm_log_dt[l], ssm_b_re[l], ssm_b_im[l],
                        ssm_c_re[l], ssm_c_im[l], ssm_d[l], nchunk)
    s_re, s_im = _ssm_state(u3, sc["b_re"], sc["b_im"], sc["a_row_re"], sc["a_row_im"])
    h_re, h_im = _ssm_scan(s_re, s_im, sc["p_re"], sc["p_im"], nchunk)
    y3 = _ssm_out(u3, sc["kt"], h_re, h_im, sc["c_re"], sc["c_im"], sc["a_col_re"], sc["a_col_im"])

    w_router = jnp.concatenate([w_router_group[l], w_router_expert[l]], axis=1).astype(F32)
    w_router = jnp.pad(w_router, ((0, 0), (0, LANES - w_router.shape[1])))
    w_router_hi = w_router.astype(BF16)
    w_router = jnp.concatenate([w_router_hi, (w_router - w_router_hi.astype(F32)).astype(BF16)], axis=1)
    b_router = jnp.concatenate([b_router_group[l], b_router_expert[l]]).astype(F32)
    b_router = jnp.pad(b_router, (0, LANES - b_router.shape[0]))[None]
    x1, h2, info, route_t, cnt = _mix(x2, attn, y3, w_glu[l].astype(BF16), b_glu[l][None], ssm_norm[l][None],
                                      w_out[l][:ATTN_WIDTH].astype(BF16), w_out[l][ATTN_WIDTH:].astype(BF16),
                                      norm_moe[l][None], w_router, b_router)

    experts = route_t[0:2].astype(jnp.int32)
    ranks = route_t[4:6].astype(jnp.int32)
    counts = cnt[0, :N_EXPERTS].astype(jnp.int32)
    padded = ((counts + ROW_BLOCK - 1) // ROW_BLOCK) * ROW_BLOCK
    ids = jnp.arange(N_EXPERTS, dtype=jnp.int32)
    pend = jnp.sum(jnp.where(ids[None, :] <= ids[:, None], padded[None, :], 0), axis=1)
    pstart = pend - padded
    dest = ranks
    for e in range(N_EXPERTS):
        dest = dest + jnp.where(experts == e, pstart[e], 0)
    dest = dest.reshape(-1)
    n_rows = ((2 * t + N_EXPERTS * (ROW_BLOCK - 1) + ROW_BLOCK - 1) // ROW_BLOCK) * ROW_BLOCK
    nb = n_rows // ROW_BLOCK
    n_used = (pend[-1] // ROW_BLOCK).astype(jnp.int32)
    blk = jnp.minimum(jnp.arange(nb, dtype=jnp.int32), n_used - 1) * ROW_BLOCK
    block_e = jnp.minimum(jnp.sum((pend[None, :] <= blk[:, None]).astype(jnp.int32), axis=1), N_EXPERTS - 1)

    rows = _dispatch(dest, pstart + counts, padded - counts, h2, n_rows)
    out_rows = _experts(block_e, n_used[None], rows, w_gate[l], w_up[l], w_down[l])
    out = _combine(dest, x1, info, norm_final[None], out_rows)
    return out.reshape(batch, seq, d)
```

```python
import functools
import math

import jax
import jax.numpy as jnp
import numpy as np
from jax import lax
from jax.experimental import pallas as pl
from jax.experimental.pallas import tpu as pltpu

F32 = jnp.float32
BF16 = jnp.bfloat16

D_MODEL = 1024
N_HEADS = 4
HEAD_DIM = 64
VALUE_DIM = 128
ATTN_WIDTH = 512
SSM_WIDTH = 512
SSM_GROUP = 16
N_GROUPS = 32
SSM_STATE = 64
N_EXPERT_GROUPS = 4
EXPERTS_PER_GROUP = 8
N_EXPERTS = 32
EXPERT_FF = 512
RMS_EPS = 1e-6
LAMBDA_INIT = 0.8 - 0.6 * math.exp(-0.3 * 0)
LOG2E = math.log2(math.e)

CHUNK = 16
HALF_GROUPS = 16
TOEP_SPAN = 4
LANES = 128
AUG = 128
NORM_LANE = 70
QNORM_ROW = 71
V_ROWS = VALUE_DIM + 16
UNDERFLOW_LOG2 = 152.0
ROW_BLOCK = 256
ROW_TILE = D_MODEL // 128
VMEM_LIMIT_CAP = 56 * 1024 * 1024


def _params(dims, vmem_mb):
    return pltpu.CompilerParams(dimension_semantics=dims,
                                vmem_limit_bytes=min(vmem_mb * 1024 * 1024, VMEM_LIMIT_CAP))


def _rms(x, gain):
    return x * lax.rsqrt(jnp.mean(x * x, axis=-1, keepdims=True) + RMS_EPS) * gain


def _store_tile_rows(ref, val):
    n = val.shape[0]
    for c in range(ROW_TILE):
        ref[pl.ds(c, n, stride=ROW_TILE), :] = val[:, c * LANES:(c + 1) * LANES]


def _load_tile_rows(ref, first=0, n=None):
    n = ref.shape[0] // ROW_TILE if n is None else n
    return jnp.concatenate([ref[pl.ds(first * ROW_TILE + c, n, stride=ROW_TILE), :] for c in range(ROW_TILE)],
                           axis=-1)


def _split3(val):
    hi = val.astype(BF16).astype(F32)
    r1 = val - hi
    mid = r1.astype(BF16).astype(F32)
    lo = r1 - mid
    return hi, mid, lo


def _inproj_kernel(x_ref, g_ref, wq_ref, wk_ref, wv_ref, wu_ref, q_ref, k_ref, v_ref, u_ref, ubuf, *, tile, seq):
    i = pl.program_id(0)
    h = _rms(x_ref[...], g_ref[...]).astype(BF16)
    nt = (((1,), (1,)), ((), ()))
    qt = lax.dot_general(wq_ref[...], h, nt, preferred_element_type=F32)
    kp = jnp.dot(h, wk_ref[...], preferred_element_type=F32)
    vt = lax.dot_general(wv_ref[...], h, nt, preferred_element_type=F32)
    ones_row = jnp.where(lax.broadcasted_iota(jnp.int32, (V_ROWS - VALUE_DIM, tile), 0) == 0, 1.0, 0.0)
    for hd in range(N_HEADS):
        v_ref[hd * V_ROWS:hd * V_ROWS + VALUE_DIM, :] = vt[hd * VALUE_DIM:(hd + 1) * VALUE_DIM, :].astype(BF16)
        v_ref[hd * V_ROWS + VALUE_DIM:(hd + 1) * V_ROWS, :] = ones_row.astype(BF16)
    up = jnp.dot(h, wu_ref[...], preferred_element_type=F32)
    pos0 = lax.rem(i * tile, seq)
    pos_k = (pos0 + lax.broadcasted_iota(jnp.int32, (tile, AUG), 0)).astype(F32)
    pos_q = (pos0 + lax.broadcasted_iota(jnp.int32, (1, tile), 1)).astype(F32)
    lane = lax.broadcasted_iota(jnp.int32, (tile, AUG), 1)
    srow = lax.broadcasted_iota(jnp.int32, (AUG - HEAD_DIM, tile), 0)
    qscale = HEAD_DIM ** -0.5 * LOG2E
    for hd in range(N_HEADS):
        slope = 2.0 ** (-8.0 * (hd + 1) / N_HEADS) * LOG2E
        hi, mid, lo = _split3(pos_k * slope)
        k_add = jnp.where(lane == 64, hi,
                          jnp.where(lane == 65, mid,
                                    jnp.where(lane == 66, lo, jnp.where((lane >= 67) & (lane < 70), 1.0, 0.0))))
        hi, mid, lo = _split3(pos_q * slope)
        q_add = jnp.where(srow < 3, 1.0,
                          jnp.where(srow == 3, -hi, jnp.where(srow == 4, -mid, jnp.where(srow == 5, -lo, 0.0))))
        for m in range(2):
            c0 = (hd * 2 + m) * AUG
            qb = (qt[c0 // 2:c0 // 2 + HEAD_DIM, :] * qscale).astype(BF16)
            q_ref[c0:c0 + HEAD_DIM, :] = qb
            qr = qb.astype(F32)
            qnorm2 = jnp.sum(qr * qr, axis=0, keepdims=True) * (1.0 + 2.0 ** -6)
            q_ref[c0 + HEAD_DIM:c0 + AUG, :] = (
                q_add + jnp.where(srow == QNORM_ROW - HEAD_DIM, qnorm2, 0.0)).astype(BF16)
            kb = kp[:, c0:c0 + AUG]
            kr = kb.astype(BF16).astype(F32)
            norm2 = jnp.sum(kr * kr, axis=-1, keepdims=True) * (1.0 + 2.0 ** -6)
            k_ref[:, c0:c0 + AUG] = (kb + k_add + jnp.where(lane == NORM_LANE, norm2, 0.0)).astype(BF16)
    for cb in range(SSM_WIDTH // LANES):
        ubuf[cb] = up[:, cb * LANES:(cb + 1) * LANES]
    per_half = HALF_GROUPS * SSM_GROUP // LANES
    for s in range(CHUNK):
        for cb in range(SSM_WIDTH // LANES):
            c0 = ((cb // per_half) * CHUNK + s) * HALF_GROUPS * SSM_GROUP + (cb % per_half) * LANES
            u_ref[:, c0:c0 + LANES] = ubuf[cb, pl.ds(s, tile // CHUNK, stride=CHUNK), :].astype(BF16)


def _in_proj(x2, gain, wq_t, wk, wv_t, wu, seq, tile=512):
    t = x2.shape[0]
    qk = 2 * N_HEADS * AUG
    kern = functools.partial(_inproj_kernel, tile=tile, seq=seq)
    full = lambda shape: pl.BlockSpec(shape, lambda i: (0, 0))
    return pl.pallas_call(
        kern, name="in_proj",
        grid=(t // tile,),
        in_specs=[pl.BlockSpec((tile, D_MODEL), lambda i: (i, 0)),
                  full((1, D_MODEL)), full((ATTN_WIDTH, D_MODEL)), full((D_MODEL, qk)),
                  full((ATTN_WIDTH, D_MODEL)), full((D_MODEL, SSM_WIDTH))],
        out_specs=[pl.BlockSpec((qk, tile), lambda i: (0, i)),
                   pl.BlockSpec((tile, qk), lambda i: (i, 0)),
                   pl.BlockSpec((N_HEADS * V_ROWS, tile), lambda i: (0, i)),
                   pl.BlockSpec((tile // CHUNK, CHUNK * SSM_WIDTH), lambda i: (i, 0))],
        out_shape=[jax.ShapeDtypeStruct((qk, t), BF16),
                   jax.ShapeDtypeStruct((t, qk), BF16),
                   jax.ShapeDtypeStruct((N_HEADS * V_ROWS, t), BF16),
                   jax.ShapeDtypeStruct((t // CHUNK, CHUNK * SSM_WIDTH), BF16)],
        scratch_shapes=[pltpu.VMEM((SSM_WIDTH // LANES, tile, LANES), F32)],
        compiler_params=_params(("arbitrary",), 48),
    )(x2, gain, wq_t, wk, wv_t, wu)


def _attn_kernel(lq1, lk1, lq2, lk2, sub_ref, q_ref, k_ref, v_ref, o_ref, m_sc, acc_sc, kn_sc, *, tq, kb, strip):
    hd = pl.program_id(1)
    qi = pl.program_id(2)
    lam = (jnp.exp(jnp.sum(lq1[...] * lk1[...], axis=-1, keepdims=True))
           - jnp.exp(jnp.sum(lq2[...] * lk2[...], axis=-1, keepdims=True)) + LAMBDA_INIT)

    @pl.when(qi == 0)
    def _():
        for m in range(2):
            kn_sc[m] = jnp.max(k_ref[:, m * AUG:(m + 1) * AUG].astype(F32), axis=0, keepdims=True)

    def step(start, nkeys, align, q_lo=0, nq=tq, key_lo=None, first=False):
        r0 = pl.multiple_of(start, align)
        kblk = k_ref[pl.ds(r0, nkeys), :]
        vblk = v_ref[:, pl.ds(r0, nkeys)]
        scores = [jnp.dot(kblk[:, m * AUG:(m + 1) * AUG], q_ref[m * AUG:(m + 1) * AUG, q_lo:q_lo + nq],
                          preferred_element_type=F32) for m in range(2)]
        for m in range(2):
            s = scores[m]
            if key_lo is not None:
                key = lax.broadcasted_iota(jnp.int32, (nkeys, nq), 0) + key_lo
                qry = lax.broadcasted_iota(jnp.int32, (nkeys, nq), 1) + q_lo
                s = jnp.where(key <= qry, s, -jnp.inf)
            if first:
                m_new = jnp.max(s, axis=0, keepdims=True)
                pv = jnp.dot(vblk, jnp.exp2(s - m_new).astype(BF16), preferred_element_type=F32)
            else:
                m_prev = m_sc[m, :, q_lo:q_lo + nq]
                m_new = jnp.maximum(m_prev, jnp.max(s, axis=0, keepdims=True))
                pv = (jnp.exp2(m_prev - m_new) * acc_sc[m, :, q_lo:q_lo + nq]
                      + jnp.dot(vblk, jnp.exp2(s - m_new).astype(BF16), preferred_element_type=F32))
            acc_sc[m, :, q_lo:q_lo + nq] = pv
            m_sc[m, :, q_lo:q_lo + nq] = m_new

    for lo in range(0, tq, strip):
        step(qi * tq + lo, strip, strip, q_lo=lo, nq=tq - lo, key_lo=lo, first=lo == 0)

    lane = lax.broadcasted_iota(jnp.int32, (1, AUG), 1)
    sub = lax.broadcasted_iota(jnp.int32, (16, 1), 0)
    slope = LOG2E * jnp.exp2(-2.0 * (jnp.full((1, 1), hd, jnp.int32) + 1).astype(F32))
    reach = jnp.zeros((1, 1), F32)
    for m in range(2):
        qrows = jnp.max(q_ref[m * AUG + HEAD_DIM:m * AUG + HEAD_DIM + 16, :].astype(F32), axis=-1, keepdims=True)
        q2 = jnp.max(jnp.where(sub == QNORM_ROW - HEAD_DIM, qrows, 0.0), axis=0, keepdims=True)
        k2 = jnp.max(jnp.where(lane == NORM_LANE, kn_sc[m], 0.0), axis=-1, keepdims=True)
        m_min = jnp.min(m_sc[m], axis=-1, keepdims=True)
        reach = jnp.maximum(reach, (jnp.sqrt(q2 * k2) * 1.001 + (UNDERFLOW_LOG2 + 0.5) - m_min) / slope)
    need = jnp.ceil((reach + (kb - 1)) / kb).astype(jnp.int32) - 1
    n_below = jnp.minimum(jnp.max(jnp.maximum(need, 0)), qi * (tq // kb))
    top = qi * tq

    def quad(i, carry):
        step(top - (2 + 4 * i) * kb, 2 * kb, kb)
        step(top - (4 + 4 * i) * kb, 2 * kb, kb)
        return carry

    lax.fori_loop(0, n_below // 4, quad, 0)
    rest = n_below % 4
    done = n_below - rest

    @pl.when(rest >= 2)
    def _():
        step(top - (done + 2) * kb, 2 * kb, kb)

    @pl.when(rest % 2 == 1)
    def _():
        step(top - n_below * kb, kb, kb)

    l0 = acc_sc[0, VALUE_DIM:VALUE_DIM + 1, :]
    l1 = acc_sc[1, VALUE_DIM:VALUE_DIM + 1, :]
    o = acc_sc[0, :VALUE_DIM, :] / l0 - lam * (acc_sc[1, :VALUE_DIM, :] / l1)
    o = o * lax.rsqrt(jnp.mean(o * o, axis=0, keepdims=True) + RMS_EPS) * sub_ref[...] * (1.0 - LAMBDA_INIT)
    o_ref[...] = o.T.astype(BF16)


def _attention(q_t, k_aug, v_t, lq1, lk1, lq2, lk2, subln_col, batch, seq, tq=1024, kb=512, strip=256):
    tq = min(tq, seq)
    nq = seq // tq
    t = batch * seq
    small = pl.BlockSpec((1, HEAD_DIM), lambda b, h, i: (0, 0))
    kern = functools.partial(_attn_kernel, tq=tq, kb=kb, strip=strip)
    return pl.pallas_call(
        kern, name="attention",
        grid=(batch, N_HEADS, nq),
        in_specs=[small, small, small, small,
                  pl.BlockSpec((VALUE_DIM, 1), lambda b, h, i: (0, 0)),
                  pl.BlockSpec((2 * AUG, tq), lambda b, h, i: (h, b * nq + i)),
                  pl.BlockSpec((seq, 2 * AUG), lambda b, h, i: (b, h)),
                  pl.BlockSpec((V_ROWS, seq), lambda b, h, i: (h, b))],
        out_specs=pl.BlockSpec((tq, VALUE_DIM), lambda b, h, i: (b * nq + i, h)),
        out_shape=jax.ShapeDtypeStruct((t, ATTN_WIDTH), BF16),
        scratch_shapes=[pltpu.VMEM((2, 1, tq), F32), pltpu.VMEM((2, V_ROWS, tq), F32),
                        pltpu.VMEM((2, 1, AUG), F32)],
        compiler_params=_params(("arbitrary", "arbitrary", "arbitrary"), 48),
    )(lq1, lk1, lq2, lk2, subln_col, q_t, k_aug, v_t)


def _ssm_state_kernel(u_ref, bre_ref, bim_ref, are_ref, aim_ref, sre_ref, sim_ref, wre, wim):
    i = pl.program_id(1)

    @pl.when(i == 0)
    def _():
        wre[...] = bre_ref[0]
        wim[...] = bim_ref[0]
        sre_ref[...] = jnp.zeros(sre_ref.shape, F32)
        sim_ref[...] = jnp.zeros(sim_ref.shape, F32)

    @pl.when(i > 0)
    def _():
        ar, ai = are_ref[0], aim_ref[0]
        wr, wi = wre[...], wim[...]
        wre[...] = wr * ar - wi * ai
        wim[...] = wr * ai + wi * ar

    u = u_ref[...]
    sre_ref[...] += jnp.dot(u, wre[...].astype(BF16), preferred_element_type=F32)
    sim_ref[...] += jnp.dot(u, wim[...].astype(BF16), preferred_element_type=F32)


def _ssm_state(u3, b_re, b_im, a_row_re, a_row_im):
    nch = u3.shape[0]
    hw = HALF_GROUPS * SSM_GROUP
    sw = HALF_GROUPS * SSM_STATE
    return pl.pallas_call(
        _ssm_state_kernel, name="ssm_state",
        grid=(2, CHUNK),
        in_specs=[pl.BlockSpec((nch, hw), lambda hf, i: (0, hf * CHUNK + CHUNK - 1 - i)),
                  pl.BlockSpec((1, hw, sw), lambda hf, i: (hf, 0, 0)),
                  pl.BlockSpec((1, hw, sw), lambda hf, i: (hf, 0, 0)),
                  pl.BlockSpec((1, 1, sw), lambda hf, i: (hf, 0, 0)),
                  pl.BlockSpec((1, 1, sw), lambda hf, i: (hf, 0, 0))],
        out_specs=[pl.BlockSpec((nch, sw), lambda hf, i: (0, hf)),
                   pl.BlockSpec((nch, sw), lambda hf, i: (0, hf))],
        out_shape=[jax.ShapeDtypeStruct((nch, 2 * sw), F32)] * 2,
        scratch_shapes=[pltpu.VMEM((hw, sw), F32), pltpu.VMEM((hw, sw), F32)],
        compiler_params=_params(("arbitrary", "arbitrary"), 48),
    )(u3, b_re, b_im, a_row_re, a_row_im)


def _ssm_scan_kernel(sre_ref, sim_ref, pre_ref, pim_ref, hre_ref, him_ref, *, nchunk, nsteps):
    hr, hi = sre_ref[...], sim_ref[...]
    row = lax.rem(lax.broadcasted_iota(jnp.int32, hr.shape, 0), nchunk)
    for k in range(nsteps):
        d = 1 << k
        ar, ai = pre_ref[0, k:k + 1, :], pim_ref[0, k:k + 1, :]
        keep = row >= d
        pr = jnp.where(keep, pltpu.roll(hr, d, 0), 0.0)
        pi = jnp.where(keep, pltpu.roll(hi, d, 0), 0.0)
        hr, hi = hr + ar * pr - ai * pi, hi + ar * pi + ai * pr
    keep = row >= 1
    hre_ref[...] = jnp.where(keep, pltpu.roll(hr, 1, 0), 0.0).astype(BF16)
    him_ref[...] = jnp.where(keep, pltpu.roll(hi, 1, 0), 0.0).astype(BF16)


def _ssm_scan(s_re, s_im, p_re, p_im, nchunk, cols=512):
    nrow, width = s_re.shape
    nsteps = p_re.shape[1]
    per_half = (width // 2) // cols
    kern = functools.partial(_ssm_scan_kernel, nchunk=nchunk, nsteps=nsteps)
    blk = pl.BlockSpec((nrow, cols), lambda j: (0, j))
    pblk = pl.BlockSpec((1, nsteps, cols), lambda j: (j // per_half, 0, j % per_half))
    return pl.pallas_call(
        kern, name="ssm_scan",
        grid=(width // cols,),
        in_specs=[blk, blk, pblk, pblk],
        out_specs=[blk, blk],
        out_shape=[jax.ShapeDtypeStruct((nrow, width), BF16)] * 2,
        compiler_params=_params(("arbitrary",), 48),
    )(s_re, s_im, p_re, p_im)


def _ssm_out_kernel(u_ref, kt_ref, hre_ref, him_ref, cre_ref, cim_ref, are_ref, aim_ref, y_ref, wre, wim, toep):
    t = pl.program_id(1)
    ar, ai = are_ref[0], aim_ref[0]

    @pl.when(t == 0)
    def _():
        cr, ci = cre_ref[0], cim_ref[0]
        wre[...] = cr * ar - ci * ai
        wim[...] = cr * ai + ci * ar

    @pl.when(t > 0)
    def _():
        wr, wi = wre[...], wim[...]
        wre[...] = wr * ar - wi * ai
        wim[...] = wr * ai + wi * ar

    hw = kt_ref.shape[-1]
    for s in range(CHUNK):
        tile = kt_ref[jnp.maximum(t - s, 0), 0]
        toep[s * hw:(s + 1) * hw, :] = jnp.where(s <= t, tile, jnp.zeros_like(tile))

    span = TOEP_SPAN * hw
    y_ref[0] = (jnp.dot(u_ref[:, :span], toep[:span, :], preferred_element_type=F32)
                + jnp.dot(hre_ref[...], wre[...].astype(BF16), preferred_element_type=F32)
                - jnp.dot(him_ref[...], wim[...].astype(BF16), preferred_element_type=F32))
    for piece in range(1, CHUNK // TOEP_SPAN):
        @pl.when(t >= piece * TOEP_SPAN)
        def _():
            lo = piece * span
            y_ref[0] += jnp.dot(u_ref[:, lo:lo + span], toep[lo:lo + span, :], preferred_element_type=F32)


def _ssm_out(u3, kt, h_re, h_im, c_re, c_im, a_col_re, a_col_im):
    nch = u3.shape[0]
    hw = HALF_GROUPS * SSM_GROUP
    sw = HALF_GROUPS * SSM_STATE
    return pl.pallas_call(
        _ssm_out_kernel, name="ssm_out",
        grid=(2, CHUNK),
        in_specs=[pl.BlockSpec((nch, CHUNK * hw), lambda hf, t: (0, hf)),
                  pl.BlockSpec((CHUNK, 1, hw, hw), lambda hf, t: (0, hf, 0, 0)),
                  pl.BlockSpec((nch, sw), lambda hf, t: (0, hf)),
                  pl.BlockSpec((nch, sw), lambda hf, t: (0, hf)),
                  pl.BlockSpec((1, sw, hw), lambda hf, t: (hf, 0, 0)),
                  pl.BlockSpec((1, sw, hw), lambda hf, t: (hf, 0, 0)),
                  pl.BlockSpec((1, sw, 1), lambda hf, t: (hf, 0, 0)),
                  pl.BlockSpec((1, sw, 1), lambda hf, t: (hf, 0, 0))],
        out_specs=pl.BlockSpec((1, nch, hw), lambda hf, t: (t, 0, hf)),
        out_shape=jax.ShapeDtypeStruct((CHUNK, nch, SSM_WIDTH), F32),
        scratch_shapes=[pltpu.VMEM((sw, hw), F32), pltpu.VMEM((sw, hw), F32), pltpu.VMEM((CHUNK * hw, hw), BF16)],
        compiler_params=_params(("arbitrary", "arbitrary"), 48),
    )(u3, kt, h_re, h_im, c_re, c_im, a_col_re, a_col_im)


def _ssm_constants(lam_re, lam_im, log_dt, b_re, b_im, c_re, c_im, d_skip, nchunk):
    lr, li = lam_re.astype(F32), lam_im.astype(F32)
    dt = jnp.exp(log_dt.astype(F32))[:, None]

    def lam_bar_pow(k):
        mag = jnp.exp(k * lr * dt)
        return mag * jnp.cos(k * li * dt), mag * jnp.sin(k * li * dt)

    a_re, a_im = lam_bar_pow(1.0)
    den = lr * lr + li * li
    coef_re = ((a_re - 1.0) * lr + a_im * li) / den
    coef_im = (a_im * lr - (a_re - 1.0) * li) / den
    bb_re = coef_re[..., None] * b_re.astype(F32) - coef_im[..., None] * b_im.astype(F32)
    bb_im = coef_re[..., None] * b_im.astype(F32) + coef_im[..., None] * b_re.astype(F32)
    cc_re, cc_im = c_re.astype(F32), c_im.astype(F32)
    lags = jnp.arange(CHUNK, dtype=F32)[:, None, None]
    pw_re, pw_im = lam_bar_pow(lags)
    pb_re = pw_re[..., None] * bb_re[None] - pw_im[..., None] * bb_im[None]
    pb_im = pw_re[..., None] * bb_im[None] + pw_im[..., None] * bb_re[None]
    kt = jnp.einsum('gcn,jgnd->jgdc', cc_re, pb_re) - jnp.einsum('gcn,jgnd->jgdc', cc_im, pb_im)
    skip = jnp.einsum('gc,dc->gdc', d_skip.astype(F32), jnp.eye(SSM_GROUP, dtype=F32))
    kt = jnp.concatenate([kt[:1] + skip[None], kt[1:]], axis=0)
    hw, sw = HALF_GROUPS * SSM_GROUP, HALF_GROUPS * SSM_STATE

    def block_diag(rows, row_group, col_group):
        wide = jnp.tile(rows, (1,) * (rows.ndim - 1) + (HALF_GROUPS,))
        r = lax.broadcasted_iota(jnp.int32, wide.shape, wide.ndim - 2) // row_group
        c = lax.broadcasted_iota(jnp.int32, wide.shape, wide.ndim - 1) // col_group
        return jnp.where(r == c, wide, 0.0)

    kt_t = block_diag(kt.reshape(CHUNK, 2, hw, SSM_GROUP), SSM_GROUP, SSM_GROUP).astype(BF16)

    def b_tiles(part):
        p = part.reshape(2, HALF_GROUPS, SSM_STATE, SSM_GROUP).transpose(0, 1, 3, 2)
        return block_diag(p.reshape(2, hw, SSM_STATE), SSM_GROUP, SSM_STATE)

    def c_tiles(part):
        p = part.reshape(2, HALF_GROUPS, SSM_GROUP, SSM_STATE).transpose(0, 1, 3, 2)
        return block_diag(p.reshape(2, sw, SSM_GROUP), SSM_STATE, SSM_GROUP)

    nsteps = max(int(math.log2(nchunk)), 1)
    steps = (CHUNK * 2.0 ** jnp.arange(nsteps, dtype=F32))[:, None, None]
    st_re, st_im = lam_bar_pow(steps)
    by_half = lambda p: p.reshape(nsteps, 2, sw).transpose(1, 0, 2)
    ar_h, ai_h = a_re.reshape(2, sw), a_im.reshape(2, sw)
    return dict(kt=kt_t, b_re=b_tiles(bb_re), b_im=b_tiles(bb_im),
                c_re=c_tiles(cc_re), c_im=c_tiles(cc_im),
                a_row_re=ar_h[:, None, :], a_row_im=ai_h[:, None, :],
                a_col_re=ar_h[:, :, None], a_col_im=ai_h[:, :, None],
                p_re=by_half(st_re), p_im=by_half(st_im))


def _mix_kernel(x_ref, attn_ref, y3_ref, wglu_ref, bglu_ref, gssm_ref, woa_ref, wos_ref, gmoe_ref,
                wr_ref, br_ref, x1_ref, h2_ref, info_ref, rt_ref, cnt_ref, ybuf, carry, *, tile):
    i = pl.program_id(0)

    @pl.when(i == 0)
    def _():
        carry[...] = jnp.zeros(carry.shape, F32)

    for s in range(CHUNK):
        for cb in range(SSM_WIDTH // LANES):
            ybuf[cb, pl.ds(s, tile // CHUNK, stride=CHUNK), :] = y3_ref[s, :, cb * LANES:(cb + 1) * LANES]
    y = jax.nn.gelu(jnp.concatenate([ybuf[cb] for cb in range(SSM_WIDTH // LANES)], axis=-1))
    z = jnp.dot(y.astype(BF16), wglu_ref[...], preferred_element_type=F32) + bglu_ref[...]
    y = y * jax.nn.sigmoid(z)
    ssm = _rms(y, gssm_ref[...])
    x1 = (x_ref[...] + jnp.dot(attn_ref[...], woa_ref[...], preferred_element_type=F32)
          + jnp.dot(ssm.astype(BF16), wos_ref[...], preferred_element_type=F32))
    x1_ref[...] = x1
    h2 = _rms(x1, gmoe_ref[...])
    _store_tile_rows(h2_ref, h2)

    h_hi = h2.astype(BF16)
    h_lo = (h2 - h_hi.astype(F32)).astype(BF16)
    wr = wr_ref[...]
    both = jnp.dot(h_hi, wr, preferred_element_type=F32)
    logits = (both[:, :LANES] + both[:, LANES:]
              + jnp.dot(h_lo, wr[:, :LANES], preferred_element_type=F32) + br_ref[...])
    lane = lax.broadcasted_iota(jnp.int32, logits.shape, 1)
    neg = -jnp.inf
    gl = jnp.where(lane < N_EXPERT_GROUPS, logits, neg)
    gmax = jnp.max(gl, axis=-1, keepdims=True)
    gsel = jnp.min(jnp.where(gl == gmax, lane, LANES), axis=-1, keepdims=True)
    p_group = 1.0 / jnp.sum(jnp.exp(gl - gmax), axis=-1, keepdims=True)
    elane = lane - N_EXPERT_GROUPS
    in_grp = (elane >= 0) & (elane < N_EXPERTS) & ((elane >> 3) == gsel)
    el = jnp.where(in_grp, logits, neg)
    m1 = jnp.max(el, axis=-1, keepdims=True)
    i1 = jnp.min(jnp.where(el == m1, lane, LANES), axis=-1, keepdims=True)
    den = jnp.sum(jnp.exp(el - m1), axis=-1, keepdims=True)
    el2 = jnp.where(lane == i1, neg, el)
    m2 = jnp.max(el2, axis=-1, keepdims=True)
    i2 = jnp.min(jnp.where(el2 == m2, lane, LANES), axis=-1, keepdims=True)
    g0 = p_group / den
    g1 = p_group * jnp.exp(m2 - m1) / den
    e0 = i1 - N_EXPERT_GROUPS
    e1 = i2 - N_EXPERT_GROUPS

    hit0 = lane == e0
    hit1 = lane == e1
    onehot = jnp.where(hit0 | hit1, 1.0, 0.0)
    r = lax.broadcasted_iota(jnp.int32, (tile, tile), 0)
    c = lax.broadcasted_iota(jnp.int32, (tile, tile), 1)
    tril = jnp.where(c < r, 1.0, 0.0).astype(BF16)
    before = jnp.dot(tril, onehot.astype(BF16), preferred_element_type=F32) + carry[...]
    rank0 = jnp.sum(jnp.where(hit0, before, 0.0), axis=-1, keepdims=True)
    rank1 = jnp.sum(jnp.where(hit1, before, 0.0), axis=-1, keepdims=True)
    carry[...] += jnp.sum(onehot, axis=0, keepdims=True)
    cnt_ref[...] = carry[...]
    info = jnp.where(lane == 0, e0.astype(F32),
                     jnp.where(lane == 1, e1.astype(F32),
                               jnp.where(lane == 2, g0,
                                         jnp.where(lane == 3, g1,
                                                   jnp.where(lane == 4, rank0, jnp.where(lane == 5, rank1, 0.0))))))
    info_ref[...] = info
    rt_ref[...] = info.T[:8]


def _mix(x2, attn, y3, wglu, bglu, gssm, wo_a, wo_s, gmoe, w_router, b_router, tile=512):
    t = x2.shape[0]
    kern = functools.partial(_mix_kernel, tile=tile)
    full = lambda shape: pl.BlockSpec(shape, lambda i: tuple(0 for _ in shape))
    return pl.pallas_call(
        kern, name="mix",
        grid=(t // tile,),
        in_specs=[pl.BlockSpec((tile, D_MODEL), lambda i: (i, 0)),
                  pl.BlockSpec((tile, ATTN_WIDTH), lambda i: (i, 0)),
                  pl.BlockSpec((CHUNK, tile // CHUNK, SSM_WIDTH), lambda i: (0, i, 0)),
                  full((SSM_WIDTH, SSM_WIDTH)), full((1, SSM_WIDTH)), full((1, SSM_WIDTH)),
                  full((ATTN_WIDTH, D_MODEL)), full((SSM_WIDTH, D_MODEL)), full((1, D_MODEL)),
                  full((D_MODEL, 2 * LANES)), full((1, LANES))],
        out_specs=[pl.BlockSpec((tile, D_MODEL), lambda i: (i, 0)),
                   pl.BlockSpec((tile * ROW_TILE, LANES), lambda i: (i, 0)),
                   pl.BlockSpec((tile, LANES), lambda i: (i, 0)),
                   pl.BlockSpec((8, tile), lambda i: (0, i)),
                   pl.BlockSpec((1, LANES), lambda i: (0, 0))],
        out_shape=[jax.ShapeDtypeStruct((t, D_MODEL), F32),
                   jax.ShapeDtypeStruct((t * ROW_TILE, LANES), F32),
                   jax.ShapeDtypeStruct((t, LANES), F32),
                   jax.ShapeDtypeStruct((8, t), F32),
                   jax.ShapeDtypeStruct((1, LANES), F32)],
        scratch_shapes=[pltpu.VMEM((SSM_WIDTH // LANES, tile, LANES), F32), pltpu.VMEM((1, LANES), F32)],
        compiler_params=_params(("arbitrary",), 48),
    )(x2, attn, y3, wglu, bglu, gssm, wo_a, wo_s, gmoe, w_router, b_router)


def _rows_at(ref, row, n_rows=1):
    return ref.at[pl.ds(pl.multiple_of(row * ROW_TILE, ROW_TILE), n_rows * ROW_TILE), :]


def _row_copy(src, s, dst, d, sem):
    return pltpu.make_async_copy(_rows_at(src, s), _rows_at(dst, d), sem)


UNROLL = 4
COMBINE_GROUPS = 8
PAD_SIZES = tuple(1 << b for b in reversed(range(ROW_BLOCK.bit_length() - 1)))


def _dispatch_kernel(dest_ref, pad_start_ref, pad_len_ref, h_ref, rows_out, zbuf, sem, zsem, *, tile, n_tok,
                     n_blocks):
    i = pl.program_id(0)
    base = i * tile

    def zero_fill(e, start):
        off, rem = pad_start_ref[e], pad_len_ref[e]
        for size in PAD_SIZES:
            @pl.when((rem & size) != 0)
            def _():
                cp = pltpu.make_async_copy(_rows_at(zbuf, 0, size), _rows_at(rows_out, off, size), zsem)
                cp.start() if start else cp.wait()
            off = off + (rem & size)

    def zero_tail(start):
        used = (pad_start_ref[N_EXPERTS - 1] + pad_len_ref[N_EXPERTS - 1]) // ROW_BLOCK

        def blk(b, carry):
            for half in range(ROW_BLOCK // PAD_SIZES[0]):
                cp = pltpu.make_async_copy(zbuf, _rows_at(rows_out, b * ROW_BLOCK + half * PAD_SIZES[0],
                                                          PAD_SIZES[0]), zsem)
                cp.start() if start else cp.wait()
            return carry

        lax.fori_loop(used, n_blocks, blk, 0)

    @pl.when(i == 0)
    def _():
        zbuf[...] = jnp.zeros(zbuf.shape, F32)
        lax.fori_loop(0, N_EXPERTS, lambda e, c: (zero_fill(e, True), c)[1], 0)
        zero_tail(True)

    def issue(tb, carry):
        for k in range(UNROLL):
            t = tb * UNROLL + k
            for j in range(2):
                _row_copy(h_ref, t, rows_out, dest_ref[j * n_tok + base + t], sem).start(priority=j)
        return carry

    def drain(tb, carry):
        for _ in range(2 * UNROLL):
            _row_copy(h_ref, 0, rows_out, 0, sem).wait()
        return carry

    lax.fori_loop(0, tile // UNROLL, issue, 0)
    lax.fori_loop(0, tile // UNROLL, drain, 0)

    @pl.when(i == 0)
    def _():
        lax.fori_loop(0, N_EXPERTS, lambda e, c: (zero_fill(e, False), c)[1], 0)
        zero_tail(False)


def _dispatch(dest_flat, pad_start, pad_len, h2, n_rows, tile=512):
    t = h2.shape[0] // ROW_TILE
    kern = functools.partial(_dispatch_kernel, tile=tile, n_tok=t, n_blocks=n_rows // ROW_BLOCK)
    grid_spec = pltpu.PrefetchScalarGridSpec(
        num_scalar_prefetch=3, grid=(t // tile,),
        in_specs=[pl.BlockSpec((tile * ROW_TILE, LANES), lambda i, *_: (i, 0))],
        out_specs=pl.BlockSpec(memory_space=pl.ANY),
        scratch_shapes=[pltpu.VMEM((PAD_SIZES[0] * ROW_TILE, LANES), F32),
                        pltpu.SemaphoreType.DMA(()), pltpu.SemaphoreType.DMA(())])
    return pl.pallas_call(
        kern, name="dispatch", grid_spec=grid_spec,
        out_shape=jax.ShapeDtypeStruct((n_rows * ROW_TILE, LANES), F32),
        compiler_params=_params(("arbitrary",), 32),
    )(dest_flat, pad_start, pad_len, h2)


def _expert_kernel(be_ref, nu_ref, first_ref, next_ref, slot_ref, rows_ref, wg_hbm, wu_hbm, wd_hbm, out_ref,
                   wg_f, wu_f, wd_f, wg_b, wu_b, wd_b, sems):
    i = pl.program_id(0)

    def fetch(e, slot):
        return (pltpu.make_async_copy(wg_hbm.at[e], wg_f.at[slot], sems.at[slot, 0]),
                pltpu.make_async_copy(wu_hbm.at[e], wu_f.at[slot], sems.at[slot, 1]),
                pltpu.make_async_copy(wd_hbm.at[e], wd_f.at[slot], sems.at[slot, 2]))

    @pl.when((i == 0) & (nu_ref[0] > 0))
    def _():
        for cp in fetch(be_ref[0], 0):
            cp.start()

    @pl.when((first_ref[i] == 1) & (i < nu_ref[0]))
    def _():
        slot = slot_ref[i]

        @pl.when(next_ref[i] >= 0)
        def _():
            for cp in fetch(next_ref[i], 1 - slot):
                cp.start()

        for cp in fetch(be_ref[i], slot):
            cp.wait()
        wg_b[...] = wg_f[slot].astype(BF16)
        wu_b[...] = wu_f[slot].astype(BF16)
        wd_b[...] = wd_f[slot].astype(BF16)

    @pl.when(i < nu_ref[0])
    def _():
        xb = _load_tile_rows(rows_ref).astype(BF16)
        gate = jnp.dot(xb, wg_b[...], preferred_element_type=F32)
        up = jnp.dot(xb, wu_b[...], preferred_element_type=F32)
        hid = (gate * jax.nn.sigmoid(gate) * up).astype(BF16)
        _store_tile_rows(out_ref, jnp.dot(hid, wd_b[...], preferred_element_type=F32))

    @pl.when(i >= nu_ref[0])
    def _():
        out_ref[...] = jnp.zeros(out_ref.shape, F32)


def _experts(block_e, n_used, seg_first, seg_next, seg_slot, rows, w_gate, w_up, w_down):
    nb = rows.shape[0] // (ROW_BLOCK * ROW_TILE)
    last = lambda i, be, nu, *_: jnp.maximum(jnp.minimum(i, nu[0] - 1), 0)
    hbm = pl.BlockSpec(memory_space=pl.ANY)
    grid_spec = pltpu.PrefetchScalarGridSpec(
        num_scalar_prefetch=5, grid=(nb,),
        in_specs=[pl.BlockSpec((ROW_BLOCK * ROW_TILE, LANES), lambda i, *s: (last(i, *s), 0)), hbm, hbm, hbm],
        out_specs=pl.BlockSpec((ROW_BLOCK * ROW_TILE, LANES), lambda i, *s: (i, 0)),
        scratch_shapes=[pltpu.VMEM((2, D_MODEL, EXPERT_FF), F32), pltpu.VMEM((2, D_MODEL, EXPERT_FF), F32),
                        pltpu.VMEM((2, EXPERT_FF, D_MODEL), F32),
                        pltpu.VMEM((D_MODEL, EXPERT_FF), BF16), pltpu.VMEM((D_MODEL, EXPERT_FF), BF16),
                        pltpu.VMEM((EXPERT_FF, D_MODEL), BF16), pltpu.SemaphoreType.DMA((2, 3))])
    return pl.pallas_call(
        _expert_kernel, name="experts", grid_spec=grid_spec,
        out_shape=jax.ShapeDtypeStruct(rows.shape, F32),
        compiler_params=_params(("arbitrary",), 48),
    )(block_e, n_used, seg_first, seg_next, seg_slot, rows, w_gate, w_up, w_down)


def _combine_kernel(dest_ref, x1_ref, info_ref, gfin_ref, rows_ref, o_ref, buf0, buf1, sems, *, tile, n_tok):
    i = pl.program_id(0)
    slot = i % 2

    def gather(step, to_slot):
        base = step * tile
        b0, b1, sem = buf0.at[to_slot], buf1.at[to_slot], sems.at[to_slot]

        def issue(tb, carry):
            for k in range(UNROLL):
                t = tb * UNROLL + k
                _row_copy(rows_ref, dest_ref[base + t], b0, t, sem).start(priority=0)
                _row_copy(rows_ref, dest_ref[n_tok + base + t], b1, t, sem).start(priority=1)
            return carry

        lax.fori_loop(0, tile // UNROLL, issue, 0)

    def drain(of_slot):
        b0, b1, sem = buf0.at[of_slot], buf1.at[of_slot], sems.at[of_slot]

        def wait(tb, carry):
            for _ in range(UNROLL):
                _row_copy(rows_ref, 0, b0, 0, sem).wait()
                _row_copy(rows_ref, 0, b1, 0, sem).wait()
            return carry

        lax.fori_loop(0, tile // UNROLL, wait, 0)

    @pl.when(i == 0)
    def _():
        gather(0, 0)

    drain(slot)

    last = pl.num_programs(0) - 1
    nxt_base = jnp.minimum(i + 1, last) * tile
    b0, b1 = buf0.at[slot], buf1.at[slot]
    n0, n1, nsem = buf0.at[1 - slot], buf1.at[1 - slot], sems.at[1 - slot]
    group = tile // COMBINE_GROUPS
    for g in range(COMBINE_GROUPS):
        lo = g * group
        for t in range(lo, lo + group):
            _row_copy(rows_ref, dest_ref[nxt_base + t], n0, t, nsem).start(priority=0)
            _row_copy(rows_ref, dest_ref[n_tok + nxt_base + t], n1, t, nsem).start(priority=1)
        info = info_ref[lo:lo + group, :]
        x2 = (x1_ref[lo:lo + group, :] + info[:, 2:3] * _load_tile_rows(b0, lo, group)
              + info[:, 3:4] * _load_tile_rows(b1, lo, group))
        o_ref[lo:lo + group, :] = _rms(x2, gfin_ref[...])

    @pl.when(i == last)
    def _():
        drain(1 - slot)


def _combine(dest_flat, x1, info, gfin, out_rows, tile=256):
    t = x1.shape[0]
    kern = functools.partial(_combine_kernel, tile=tile, n_tok=t)
    grid_spec = pltpu.PrefetchScalarGridSpec(
        num_scalar_prefetch=1, grid=(t // tile,),
        in_specs=[pl.BlockSpec((tile, D_MODEL), lambda i, d: (i, 0)),
                  pl.BlockSpec((tile, LANES), lambda i, d: (i, 0)),
                  pl.BlockSpec((1, D_MODEL), lambda i, d: (0, 0)),
                  pl.BlockSpec(memory_space=pl.ANY)],
        out_specs=pl.BlockSpec((tile, D_MODEL), lambda i, d: (i, 0)),
        scratch_shapes=[pltpu.VMEM((2, tile * ROW_TILE, LANES), F32), pltpu.VMEM((2, tile * ROW_TILE, LANES), F32),
                        pltpu.SemaphoreType.DMA((2,))])
    return pl.pallas_call(
        kern, name="combine", grid_spec=grid_spec,
        out_shape=jax.ShapeDtypeStruct((t, D_MODEL), F32),
        compiler_params=_params(("arbitrary",), 32),
    )(dest_flat, x1, info, gfin, out_rows)


def _split_w_in(w_in):
    def pad_blocks(w):
        w = w.reshape(D_MODEL, N_HEADS * 2, HEAD_DIM)
        return jnp.pad(w, ((0, 0), (0, 0), (0, AUG - HEAD_DIM))).reshape(D_MODEL, N_HEADS * 2 * AUG)
    wq_t = w_in[:, :ATTN_WIDTH].T.astype(BF16)
    wk = pad_blocks(w_in[:, ATTN_WIDTH:2 * ATTN_WIDTH]).astype(BF16)
    wv_t = w_in[:, 2 * ATTN_WIDTH:3 * ATTN_WIDTH].T.astype(BF16)
    wu = w_in[:, 3 * ATTN_WIDTH:].astype(BF16)
    return wq_t, wk, wv_t, wu


def kernel(x, norm_attn, w_in, lambda_q1, lambda_k1, lambda_q2, lambda_k2, attn_subln, ssm_lam_re, ssm_lam_im, ssm_log_dt, ssm_b_re, ssm_b_im, ssm_c_re, ssm_c_im, ssm_d, w_glu, b_glu, ssm_norm, w_out, norm_moe, w_router_group, b_router_group, w_router_expert, b_router_expert, w_gate, w_up, w_down, norm_final):
    batch, seq, d = x.shape
    t = batch * seq
    nchunk = seq // CHUNK
    x2 = x.reshape(t, d)
    l = 0

    q_t, k_aug, v_t, u3 = _in_proj(x2, norm_attn[l][None], *_split_w_in(w_in[l]), seq)
    attn = _attention(q_t, k_aug, v_t, lambda_q1[l][None], lambda_k1[l][None], lambda_q2[l][None],
                      lambda_k2[l][None], attn_subln[l][:, None], batch, seq)

    sc = _ssm_constants(ssm_lam_re[l], ssm_lam_im[l], ssm_log_dt[l], ssm_b_re[l], ssm_b_im[l],
                        ssm_c_re[l], ssm_c_im[l], ssm_d[l], nchunk)
    s_re, s_im = _ssm_state(u3, sc["b_re"], sc["b_im"], sc["a_row_re"], sc["a_row_im"])
    h_re, h_im = _ssm_scan(s_re, s_im, sc["p_re"], sc["p_im"], nchunk)
    y3 = _ssm_out(u3, sc["kt"], h_re, h_im, sc["c_re"], sc["c_im"], sc["a_col_re"], sc["a_col_im"])

    w_router = jnp.concatenate([w_router_group[l], w_router_expert[l]], axis=1).astype(F32)
    w_router = jnp.pad(w_router, ((0, 0), (0, LANES - w_router.shape[1])))
    w_router_hi = w_router.astype(BF16)
    w_router = jnp.concatenate([w_router_hi, (w_router - w_router_hi.astype(F32)).astype(BF16)], axis=1)
    b_router = jnp.concatenate([b_router_group[l], b_router_expert[l]]).astype(F32)
    b_router = jnp.pad(b_router, (0, LANES - b_router.shape[0]))[None]
    x1, h2, info, route_t, cnt = _mix(x2, attn, y3, w_glu[l].astype(BF16), b_glu[l][None], ssm_norm[l][None],
                                      w_out[l][:ATTN_WIDTH].astype(BF16), w_out[l][ATTN_WIDTH:].astype(BF16),
                                      norm_moe[l][None], w_router, b_router)

    experts = route_t[0:2].astype(jnp.int32)
    ranks = route_t[4:6].astype(jnp.int32)
    counts = cnt[0, :N_EXPERTS].astype(jnp.int32)
    padded = ((counts + ROW_BLOCK - 1) // ROW_BLOCK) * ROW_BLOCK
    ids = jnp.arange(N_EXPERTS, dtype=jnp.int32)
    pend = jnp.sum(jnp.where(ids[None, :] <= ids[:, None], padded[None, :], 0), axis=1)
    pstart = pend - padded
    dest = ranks
    for e in range(N_EXPERTS):
        dest = dest + jnp.where(experts == e, pstart[e], 0)
    dest = dest.reshape(-1)
    n_rows = ((2 * t + N_EXPERTS * (ROW_BLOCK - 1) + ROW_BLOCK - 1) // ROW_BLOCK) * ROW_BLOCK
    nb = n_rows // ROW_BLOCK
    n_used = (pend[-1] // ROW_BLOCK).astype(jnp.int32)
    blk = jnp.minimum(jnp.arange(nb, dtype=jnp.int32), n_used - 1) * ROW_BLOCK
    block_e = jnp.minimum(jnp.sum((pend[None, :] <= blk[:, None]).astype(jnp.int32), axis=1), N_EXPERTS - 1)

    bidx = jnp.arange(nb, dtype=jnp.int32)
    seg_first = ((bidx == 0) | (block_e != jnp.concatenate([block_e[:1], block_e[:-1]]))).astype(jnp.int32)
    seg_slot = (jnp.sum(jnp.where(bidx[None, :] <= bidx[:, None], seg_first[None, :], 0), axis=1) - 1) & 1
    seg_end = jnp.sum(jnp.where(block_e[:, None] == ids[None, :], pend[None, :], 0), axis=1) // ROW_BLOCK
    after = jnp.sum(jnp.where(bidx[None, :] == seg_end[:, None], block_e[None, :], 0), axis=1)
    seg_next = jnp.where(seg_end < n_used, after, -1).astype(jnp.int32)

    rows = _dispatch(dest, pstart + counts, padded - counts, h2, n_rows)
    out_rows = _experts(block_e, n_used[None], seg_first, seg_next, seg_slot.astype(jnp.int32), rows,
                        w_gate[l], w_up[l], w_down[l])
    out = _combine(dest, x1, info, norm_final[None], out_rows)
    return out.reshape(batch, seq, d)
```

```python
import functools
import math

import jax
import jax.numpy as jnp
import numpy as np
from jax import lax
from jax.experimental import pallas as pl
from jax.experimental.pallas import tpu as pltpu

F32 = jnp.float32
BF16 = jnp.bfloat16

D_MODEL = 1024
N_HEADS = 4
HEAD_DIM = 64
VALUE_DIM = 128
ATTN_WIDTH = 512
SSM_WIDTH = 512
SSM_GROUP = 16
N_GROUPS = 32
SSM_STATE = 64
N_EXPERT_GROUPS = 4
EXPERTS_PER_GROUP = 8
N_EXPERTS = 32
EXPERT_FF = 512
RMS_EPS = 1e-6
LAMBDA_INIT = 0.8 - 0.6 * math.exp(-0.3 * 0)
LOG2E = math.log2(math.e)

CHUNK = 16
HALF_GROUPS = 16
TOEP_SPAN = 4
LANES = 128
AUG = 128
NORM_LANE = 70
QNORM_ROW = 71
V_ROWS = VALUE_DIM + 16
UNDERFLOW_LOG2 = 152.0
ROW_BLOCK = 256
ROW_TILE = D_MODEL // 128
VMEM_LIMIT_CAP = 56 * 1024 * 1024


def _params(dims, vmem_mb):
    return pltpu.CompilerParams(dimension_semantics=dims,
                                vmem_limit_bytes=min(vmem_mb * 1024 * 1024, VMEM_LIMIT_CAP))


def _rms(x, gain):
    return x * lax.rsqrt(jnp.mean(x * x, axis=-1, keepdims=True) + RMS_EPS) * gain


def _store_tile_rows(ref, val):
    n = val.shape[0]
    for c in range(ROW_TILE):
        ref[pl.ds(c, n, stride=ROW_TILE), :] = val[:, c * LANES:(c + 1) * LANES]


def _load_tile_rows(ref, first=0, n=None):
    n = ref.shape[0] // ROW_TILE if n is None else n
    return jnp.concatenate([ref[pl.ds(first * ROW_TILE + c, n, stride=ROW_TILE), :] for c in range(ROW_TILE)],
                           axis=-1)


def _split3(val):
    hi = val.astype(BF16).astype(F32)
    r1 = val - hi
    mid = r1.astype(BF16).astype(F32)
    lo = r1 - mid
    return hi, mid, lo


def _inproj_kernel(x_ref, g_ref, wq_ref, wk_ref, wv_ref, wu_ref, q_ref, k_ref, v_ref, u_ref, ubuf, *, tile, seq):
    i = pl.program_id(0)
    h = _rms(x_ref[...], g_ref[...]).astype(BF16)
    nt = (((1,), (1,)), ((), ()))
    qt = lax.dot_general(wq_ref[...], h, nt, preferred_element_type=F32)
    kp = jnp.dot(h, wk_ref[...], preferred_element_type=F32)
    vt = lax.dot_general(wv_ref[...], h, nt, preferred_element_type=F32)
    ones_row = jnp.where(lax.broadcasted_iota(jnp.int32, (V_ROWS - VALUE_DIM, tile), 0) == 0, 1.0, 0.0)
    for hd in range(N_HEADS):
        v_ref[hd * V_ROWS:hd * V_ROWS + VALUE_DIM, :] = vt[hd * VALUE_DIM:(hd + 1) * VALUE_DIM, :].astype(BF16)
        v_ref[hd * V_ROWS + VALUE_DIM:(hd + 1) * V_ROWS, :] = ones_row.astype(BF16)
    up = jnp.dot(h, wu_ref[...], preferred_element_type=F32)
    pos0 = lax.rem(i * tile, seq)
    pos_k = (pos0 + lax.broadcasted_iota(jnp.int32, (tile, AUG), 0)).astype(F32)
    pos_q = (pos0 + lax.broadcasted_iota(jnp.int32, (1, tile), 1)).astype(F32)
    lane = lax.broadcasted_iota(jnp.int32, (tile, AUG), 1)
    srow = lax.broadcasted_iota(jnp.int32, (AUG - HEAD_DIM, tile), 0)
    qscale = HEAD_DIM ** -0.5 * LOG2E
    for hd in range(N_HEADS):
        slope = 2.0 ** (-8.0 * (hd + 1) / N_HEADS) * LOG2E
        hi, mid, lo = _split3(pos_k * slope)
        k_add = jnp.where(lane == 64, hi,
                          jnp.where(lane == 65, mid,
                                    jnp.where(lane == 66, lo, jnp.where((lane >= 67) & (lane < 70), 1.0, 0.0))))
        hi, mid, lo = _split3(pos_q * slope)
        q_add = jnp.where(srow < 3, 1.0,
                          jnp.where(srow == 3, -hi, jnp.where(srow == 4, -mid, jnp.where(srow == 5, -lo, 0.0))))
        for m in range(2):
            c0 = (hd * 2 + m) * AUG
            qb = (qt[c0 // 2:c0 // 2 + HEAD_DIM, :] * qscale).astype(BF16)
            q_ref[c0:c0 + HEAD_DIM, :] = qb
            qr = qb.astype(F32)
            qnorm2 = jnp.sum(qr * qr, axis=0, keepdims=True) * (1.0 + 2.0 ** -6)
            q_ref[c0 + HEAD_DIM:c0 + AUG, :] = (
                q_add + jnp.where(srow == QNORM_ROW - HEAD_DIM, qnorm2, 0.0)).astype(BF16)
            kb = kp[:, c0:c0 + AUG]
            kr = kb.astype(BF16).astype(F32)
            norm2 = jnp.sum(kr * kr, axis=-1, keepdims=True) * (1.0 + 2.0 ** -6)
            k_ref[:, c0:c0 + AUG] = (kb + k_add + jnp.where(lane == NORM_LANE, norm2, 0.0)).astype(BF16)
    for cb in range(SSM_WIDTH // LANES):
        ubuf[cb] = up[:, cb * LANES:(cb + 1) * LANES]
    per_half = HALF_GROUPS * SSM_GROUP // LANES
    for s in range(CHUNK):
        for cb in range(SSM_WIDTH // LANES):
            c0 = ((cb // per_half) * CHUNK + s) * HALF_GROUPS * SSM_GROUP + (cb % per_half) * LANES
            u_ref[:, c0:c0 + LANES] = ubuf[cb, pl.ds(s, tile // CHUNK, stride=CHUNK), :].astype(BF16)


def _in_proj(x2, gain, wq_t, wk, wv_t, wu, seq, tile=512):
    t = x2.shape[0]
    qk = 2 * N_HEADS * AUG
    kern = functools.partial(_inproj_kernel, tile=tile, seq=seq)
    full = lambda shape: pl.BlockSpec(shape, lambda i: (0, 0))
    return pl.pallas_call(
        kern, name="in_proj",
        grid=(t // tile,),
        in_specs=[pl.BlockSpec((tile, D_MODEL), lambda i: (i, 0)),
                  full((1, D_MODEL)), full((ATTN_WIDTH, D_MODEL)), full((D_MODEL, qk)),
                  full((ATTN_WIDTH, D_MODEL)), full((D_MODEL, SSM_WIDTH))],
        out_specs=[pl.BlockSpec((qk, tile), lambda i: (0, i)),
                   pl.BlockSpec((tile, qk), lambda i: (i, 0)),
                   pl.BlockSpec((N_HEADS * V_ROWS, tile), lambda i: (0, i)),
                   pl.BlockSpec((tile // CHUNK, CHUNK * SSM_WIDTH), lambda i: (i, 0))],
        out_shape=[jax.ShapeDtypeStruct((qk, t), BF16),
                   jax.ShapeDtypeStruct((t, qk), BF16),
                   jax.ShapeDtypeStruct((N_HEADS * V_ROWS, t), BF16),
                   jax.ShapeDtypeStruct((t // CHUNK, CHUNK * SSM_WIDTH), BF16)],
        scratch_shapes=[pltpu.VMEM((SSM_WIDTH // LANES, tile, LANES), F32)],
        compiler_params=_params(("arbitrary",), 48),
    )(x2, gain, wq_t, wk, wv_t, wu)


def _attn_kernel(lq1, lk1, lq2, lk2, sub_ref, q_ref, k_ref, v_ref, o_ref, m_sc, acc_sc, kn_sc, *, tq, kb, strip):
    hd = pl.program_id(1)
    qi = pl.program_id(2)
    lam = (jnp.exp(jnp.sum(lq1[...] * lk1[...], axis=-1, keepdims=True))
           - jnp.exp(jnp.sum(lq2[...] * lk2[...], axis=-1, keepdims=True)) + LAMBDA_INIT)

    @pl.when(qi == 0)
    def _():
        for m in range(2):
            kn_sc[m] = jnp.max(k_ref[:, m * AUG:(m + 1) * AUG].astype(F32), axis=0, keepdims=True)

    def score(start, nkeys, align, q_lo=0, nq=tq):
        r0 = pl.multiple_of(start, align)
        kblk = k_ref[pl.ds(r0, nkeys), :]
        return [jnp.dot(kblk[:, m * AUG:(m + 1) * AUG], q_ref[m * AUG:(m + 1) * AUG, q_lo:q_lo + nq],
                        preferred_element_type=F32) for m in range(2)]

    def step(start, nkeys, align, q_lo=0, nq=tq, key_lo=None, first=False, scores=None):
        if scores is None:
            scores = score(start, nkeys, align, q_lo, nq)
        vblk = v_ref[:, pl.ds(pl.multiple_of(start, align), nkeys)]
        for m in range(2):
            s = scores[m]
            if key_lo is not None:
                key = lax.broadcasted_iota(jnp.int32, (nkeys, nq), 0) + key_lo
                qry = lax.broadcasted_iota(jnp.int32, (nkeys, nq), 1) + q_lo
                s = jnp.where(key <= qry, s, -jnp.inf)
            if first:
                m_new = jnp.max(s, axis=0, keepdims=True)
                pv = jnp.dot(vblk, jnp.exp2(s - m_new).astype(BF16), preferred_element_type=F32)
            else:
                m_prev = m_sc[m, :, q_lo:q_lo + nq]
                m_new = jnp.maximum(m_prev, jnp.max(s, axis=0, keepdims=True))
                pv = (jnp.exp2(m_prev - m_new) * acc_sc[m, :, q_lo:q_lo + nq]
                      + jnp.dot(vblk, jnp.exp2(s - m_new).astype(BF16), preferred_element_type=F32))
            acc_sc[m, :, q_lo:q_lo + nq] = pv
            m_sc[m, :, q_lo:q_lo + nq] = m_new

    for lo in range(0, tq, strip):
        step(qi * tq + lo, strip, strip, q_lo=lo, nq=tq - lo, key_lo=lo, first=lo == 0)

    lane = lax.broadcasted_iota(jnp.int32, (1, AUG), 1)
    sub = lax.broadcasted_iota(jnp.int32, (16, 1), 0)
    slope = LOG2E * jnp.exp2(-2.0 * (jnp.full((1, 1), hd, jnp.int32) + 1).astype(F32))
    reach = jnp.zeros((1, 1), F32)
    for m in range(2):
        qrows = jnp.max(q_ref[m * AUG + HEAD_DIM:m * AUG + HEAD_DIM + 16, :].astype(F32), axis=-1, keepdims=True)
        q2 = jnp.max(jnp.where(sub == QNORM_ROW - HEAD_DIM, qrows, 0.0), axis=0, keepdims=True)
        k2 = jnp.max(jnp.where(lane == NORM_LANE, kn_sc[m], 0.0), axis=-1, keepdims=True)
        m_min = jnp.min(m_sc[m], axis=-1, keepdims=True)
        reach = jnp.maximum(reach, (jnp.sqrt(q2 * k2) * 1.001 + (UNDERFLOW_LOG2 + 0.5) - m_min) / slope)
    need = jnp.ceil((reach + (kb - 1)) / kb).astype(jnp.int32) - 1
    n_below = jnp.minimum(jnp.max(jnp.maximum(need, 0)), qi * (tq // kb))
    top = qi * tq

    def quad(i, carry):
        a, b = top - (2 + 4 * i) * kb, top - (4 + 4 * i) * kb
        sa, sb = score(a, 2 * kb, kb), score(b, 2 * kb, kb)
        step(a, 2 * kb, kb, scores=sa)
        step(b, 2 * kb, kb, scores=sb)
        return carry

    lax.fori_loop(0, n_below // 4, quad, 0)
    rest = n_below % 4
    done = n_below - rest

    @pl.when(rest >= 2)
    def _():
        step(top - (done + 2) * kb, 2 * kb, kb)

    @pl.when(rest % 2 == 1)
    def _():
        step(top - n_below * kb, kb, kb)

    l0 = acc_sc[0, VALUE_DIM:VALUE_DIM + 1, :]
    l1 = acc_sc[1, VALUE_DIM:VALUE_DIM + 1, :]
    o = acc_sc[0, :VALUE_DIM, :] / l0 - lam * (acc_sc[1, :VALUE_DIM, :] / l1)
    o = o * lax.rsqrt(jnp.mean(o * o, axis=0, keepdims=True) + RMS_EPS) * sub_ref[...] * (1.0 - LAMBDA_INIT)
    o_ref[...] = o.T.astype(BF16)


def _attention(q_t, k_aug, v_t, lq1, lk1, lq2, lk2, subln_col, batch, seq, tq=1024, kb=512, strip=256):
    tq = min(tq, seq)
    nq = seq // tq
    t = batch * seq
    small = pl.BlockSpec((1, HEAD_DIM), lambda b, h, i: (0, 0))
    kern = functools.partial(_attn_kernel, tq=tq, kb=kb, strip=strip)
    return pl.pallas_call(
        kern, name="attention",
        grid=(batch, N_HEADS, nq),
        in_specs=[small, small, small, small,
                  pl.BlockSpec((VALUE_DIM, 1), lambda b, h, i: (0, 0)),
                  pl.BlockSpec((2 * AUG, tq), lambda b, h, i: (h, b * nq + i)),
                  pl.BlockSpec((seq, 2 * AUG), lambda b, h, i: (b, h)),
                  pl.BlockSpec((V_ROWS, seq), lambda b, h, i: (h, b))],
        out_specs=pl.BlockSpec((tq, VALUE_DIM), lambda b, h, i: (b * nq + i, h)),
        out_shape=jax.ShapeDtypeStruct((t, ATTN_WIDTH), BF16),
        scratch_shapes=[pltpu.VMEM((2, 1, tq), F32), pltpu.VMEM((2, V_ROWS, tq), F32),
                        pltpu.VMEM((2, 1, AUG), F32)],
        compiler_params=_params(("arbitrary", "arbitrary", "arbitrary"), 48),
    )(lq1, lk1, lq2, lk2, subln_col, q_t, k_aug, v_t)


def _ssm_state_kernel(u_ref, bre_ref, bim_ref, are_ref, aim_ref, sre_ref, sim_ref, wre, wim):
    i = pl.program_id(1)

    @pl.when(i == 0)
    def _():
        wre[...] = bre_ref[0]
        wim[...] = bim_ref[0]
        sre_ref[...] = jnp.zeros(sre_ref.shape, F32)
        sim_ref[...] = jnp.zeros(sim_ref.shape, F32)

    @pl.when(i > 0)
    def _():
        ar, ai = are_ref[0], aim_ref[0]
        wr, wi = wre[...], wim[...]
        wre[...] = wr * ar - wi * ai
        wim[...] = wr * ai + wi * ar

    u = u_ref[...]
    sre_ref[...] += jnp.dot(u, wre[...].astype(BF16), preferred_element_type=F32)
    sim_ref[...] += jnp.dot(u, wim[...].astype(BF16), preferred_element_type=F32)


def _ssm_state(u3, b_re, b_im, a_row_re, a_row_im):
    nch = u3.shape[0]
    hw = HALF_GROUPS * SSM_GROUP
    sw = HALF_GROUPS * SSM_STATE
    return pl.pallas_call(
        _ssm_state_kernel, name="ssm_state",
        grid=(2, CHUNK),
        in_specs=[pl.BlockSpec((nch, hw), lambda hf, i: (0, hf * CHUNK + CHUNK - 1 - i)),
                  pl.BlockSpec((1, hw, sw), lambda hf, i: (hf, 0, 0)),
                  pl.BlockSpec((1, hw, sw), lambda hf, i: (hf, 0, 0)),
                  pl.BlockSpec((1, 1, sw), lambda hf, i: (hf, 0, 0)),
                  pl.BlockSpec((1, 1, sw), lambda hf, i: (hf, 0, 0))],
        out_specs=[pl.BlockSpec((nch, sw), lambda hf, i: (0, hf)),
                   pl.BlockSpec((nch, sw), lambda hf, i: (0, hf))],
        out_shape=[jax.ShapeDtypeStruct((nch, 2 * sw), F32)] * 2,
        scratch_shapes=[pltpu.VMEM((hw, sw), F32), pltpu.VMEM((hw, sw), F32)],
        compiler_params=_params(("arbitrary", "arbitrary"), 48),
    )(u3, b_re, b_im, a_row_re, a_row_im)


def _ssm_scan_kernel(sre_ref, sim_ref, pre_ref, pim_ref, hre_ref, him_ref, *, nchunk, nsteps):
    hr, hi = sre_ref[...], sim_ref[...]
    row = lax.rem(lax.broadcasted_iota(jnp.int32, hr.shape, 0), nchunk)
    for k in range(nsteps):
        d = 1 << k
        ar, ai = pre_ref[0, k:k + 1, :], pim_ref[0, k:k + 1, :]
        keep = row >= d
        pr = jnp.where(keep, pltpu.roll(hr, d, 0), 0.0)
        pi = jnp.where(keep, pltpu.roll(hi, d, 0), 0.0)
        hr, hi = hr + ar * pr - ai * pi, hi + ar * pi + ai * pr
    keep = row >= 1
    hre_ref[...] = jnp.where(keep, pltpu.roll(hr, 1, 0), 0.0).astype(BF16)
    him_ref[...] = jnp.where(keep, pltpu.roll(hi, 1, 0), 0.0).astype(BF16)


def _ssm_scan(s_re, s_im, p_re, p_im, nchunk, cols=512):
    nrow, width = s_re.shape
    nsteps = p_re.shape[1]
    per_half = (width // 2) // cols
    kern = functools.partial(_ssm_scan_kernel, nchunk=nchunk, nsteps=nsteps)
    blk = pl.BlockSpec((nrow, cols), lambda j: (0, j))
    pblk = pl.BlockSpec((1, nsteps, cols), lambda j: (j // per_half, 0, j % per_half))
    return pl.pallas_call(
        kern, name="ssm_scan",
        grid=(width // cols,),
        in_specs=[blk, blk, pblk, pblk],
        out_specs=[blk, blk],
        out_shape=[jax.ShapeDtypeStruct((nrow, width), BF16)] * 2,
        compiler_params=_params(("arbitrary",), 48),
    )(s_re, s_im, p_re, p_im)


def _ssm_out_kernel(u_ref, kt_ref, hre_ref, him_ref, cre_ref, cim_ref, are_ref, aim_ref, y_ref, wre, wim, toep):
    t = pl.program_id(1)
    ar, ai = are_ref[0], aim_ref[0]

    @pl.when(t == 0)
    def _():
        cr, ci = cre_ref[0], cim_ref[0]
        wre[...] = cr * ar - ci * ai
        wim[...] = cr * ai + ci * ar

    @pl.when(t > 0)
    def _():
        wr, wi = wre[...], wim[...]
        wre[...] = wr * ar - wi * ai
        wim[...] = wr * ai + wi * ar

    hw = kt_ref.shape[-1]
    for s in range(CHUNK):
        tile = kt_ref[jnp.maximum(t - s, 0), 0]
        toep[s * hw:(s + 1) * hw, :] = jnp.where(s <= t, tile, jnp.zeros_like(tile))

    span = TOEP_SPAN * hw
    y_ref[0] = (jnp.dot(u_ref[:, :span], toep[:span, :], preferred_element_type=F32)
                + jnp.dot(hre_ref[...], wre[...].astype(BF16), preferred_element_type=F32)
                - jnp.dot(him_ref[...], wim[...].astype(BF16), preferred_element_type=F32))
    for piece in range(1, CHUNK // TOEP_SPAN):
        @pl.when(t >= piece * TOEP_SPAN)
        def _():
            lo = piece * span
            y_ref[0] += jnp.dot(u_ref[:, lo:lo + span], toep[lo:lo + span, :], preferred_element_type=F32)


def _ssm_out(u3, kt, h_re, h_im, c_re, c_im, a_col_re, a_col_im):
    nch = u3.shape[0]
    hw = HALF_GROUPS * SSM_GROUP
    sw = HALF_GROUPS * SSM_STATE
    return pl.pallas_call(
        _ssm_out_kernel, name="ssm_out",
        grid=(2, CHUNK),
        in_specs=[pl.BlockSpec((nch, CHUNK * hw), lambda hf, t: (0, hf)),
                  pl.BlockSpec((CHUNK, 1, hw, hw), lambda hf, t: (0, hf, 0, 0)),
                  pl.BlockSpec((nch, sw), lambda hf, t: (0, hf)),
                  pl.BlockSpec((nch, sw), lambda hf, t: (0, hf)),
                  pl.BlockSpec((1, sw, hw), lambda hf, t: (hf, 0, 0)),
                  pl.BlockSpec((1, sw, hw), lambda hf, t: (hf, 0, 0)),
                  pl.BlockSpec((1, sw, 1), lambda hf, t: (hf, 0, 0)),
                  pl.BlockSpec((1, sw, 1), lambda hf, t: (hf, 0, 0))],
        out_specs=pl.BlockSpec((1, nch, hw), lambda hf, t: (t, 0, hf)),
        out_shape=jax.ShapeDtypeStruct((CHUNK, nch, SSM_WIDTH), F32),
        scratch_shapes=[pltpu.VMEM((sw, hw), F32), pltpu.VMEM((sw, hw), F32), pltpu.VMEM((CHUNK * hw, hw), BF16)],
        compiler_params=_params(("arbitrary", "arbitrary"), 48),
    )(u3, kt, h_re, h_im, c_re, c_im, a_col_re, a_col_im)


def _ssm_constants(lam_re, lam_im, log_dt, b_re, b_im, c_re, c_im, d_skip, nchunk):
    lr, li = lam_re.astype(F32), lam_im.astype(F32)
    dt = jnp.exp(log_dt.astype(F32))[:, None]

    def lam_bar_pow(k):
        mag = jnp.exp(k * lr * dt)
        return mag * jnp.cos(k * li * dt), mag * jnp.sin(k * li * dt)

    a_re, a_im = lam_bar_pow(1.0)
    den = lr * lr + li * li
    coef_re = ((a_re - 1.0) * lr + a_im * li) / den
    coef_im = (a_im * lr - (a_re - 1.0) * li) / den
    bb_re = coef_re[..., None] * b_re.astype(F32) - coef_im[..., None] * b_im.astype(F32)
    bb_im = coef_re[..., None] * b_im.astype(F32) + coef_im[..., None] * b_re.astype(F32)
    cc_re, cc_im = c_re.astype(F32), c_im.astype(F32)
    lags = jnp.arange(CHUNK, dtype=F32)[:, None, None]
    pw_re, pw_im = lam_bar_pow(lags)
    pb_re = pw_re[..., None] * bb_re[None] - pw_im[..., None] * bb_im[None]
    pb_im = pw_re[..., None] * bb_im[None] + pw_im[..., None] * bb_re[None]
    kt = jnp.einsum('gcn,jgnd->jgdc', cc_re, pb_re) - jnp.einsum('gcn,jgnd->jgdc', cc_im, pb_im)
    skip = jnp.einsum('gc,dc->gdc', d_skip.astype(F32), jnp.eye(SSM_GROUP, dtype=F32))
    kt = jnp.concatenate([kt[:1] + skip[None], kt[1:]], axis=0)
    hw, sw = HALF_GROUPS * SSM_GROUP, HALF_GROUPS * SSM_STATE

    def block_diag(rows, row_group, col_group):
        wide = jnp.tile(rows, (1,) * (rows.ndim - 1) + (HALF_GROUPS,))
        r = lax.broadcasted_iota(jnp.int32, wide.shape, wide.ndim - 2) // row_group
        c = lax.broadcasted_iota(jnp.int32, wide.shape, wide.ndim - 1) // col_group
        return jnp.where(r == c, wide, 0.0)

    kt_t = block_diag(kt.reshape(CHUNK, 2, hw, SSM_GROUP), SSM_GROUP, SSM_GROUP).astype(BF16)

    def b_tiles(part):
        p = part.reshape(2, HALF_GROUPS, SSM_STATE, SSM_GROUP).transpose(0, 1, 3, 2)
        return block_diag(p.reshape(2, hw, SSM_STATE), SSM_GROUP, SSM_STATE)

    def c_tiles(part):
        p = part.reshape(2, HALF_GROUPS, SSM_GROUP, SSM_STATE).transpose(0, 1, 3, 2)
        return block_diag(p.reshape(2, sw, SSM_GROUP), SSM_STATE, SSM_GROUP)

    nsteps = max(int(math.log2(nchunk)), 1)
    steps = (CHUNK * 2.0 ** jnp.arange(nsteps, dtype=F32))[:, None, None]
    st_re, st_im = lam_bar_pow(steps)
    by_half = lambda p: p.reshape(nsteps, 2, sw).transpose(1, 0, 2)
    ar_h, ai_h = a_re.reshape(2, sw), a_im.reshape(2, sw)
    return dict(kt=kt_t, b_re=b_tiles(bb_re), b_im=b_tiles(bb_im),
                c_re=c_tiles(cc_re), c_im=c_tiles(cc_im),
                a_row_re=ar_h[:, None, :], a_row_im=ai_h[:, None, :],
                a_col_re=ar_h[:, :, None], a_col_im=ai_h[:, :, None],
                p_re=by_half(st_re), p_im=by_half(st_im))


def _mix_kernel(x_ref, attn_ref, y3_ref, wglu_ref, bglu_ref, gssm_ref, woa_ref, wos_ref, gmoe_ref,
                wr_ref, br_ref, x1_ref, h2_ref, info_ref, rt_ref, cnt_ref, ybuf, carry, *, tile):
    i = pl.program_id(0)

    @pl.when(i == 0)
    def _():
        carry[...] = jnp.zeros(carry.shape, F32)

    for s in range(CHUNK):
        for cb in range(SSM_WIDTH // LANES):
            ybuf[cb, pl.ds(s, tile // CHUNK, stride=CHUNK), :] = y3_ref[s, :, cb * LANES:(cb + 1) * LANES]
    y = jax.nn.gelu(jnp.concatenate([ybuf[cb] for cb in range(SSM_WIDTH // LANES)], axis=-1))
    z = jnp.dot(y.astype(BF16), wglu_ref[...], preferred_element_type=F32) + bglu_ref[...]
    y = y * jax.nn.sigmoid(z)
    ssm = _rms(y, gssm_ref[...])
    x1 = (x_ref[...] + jnp.dot(attn_ref[...], woa_ref[...], preferred_element_type=F32)
          + jnp.dot(ssm.astype(BF16), wos_ref[...], preferred_element_type=F32))
    x1_ref[...] = x1
    h2 = _rms(x1, gmoe_ref[...])
    _store_tile_rows(h2_ref, h2)

    h_hi = h2.astype(BF16)
    h_lo = (h2 - h_hi.astype(F32)).astype(BF16)
    wr = wr_ref[...]
    both = jnp.dot(h_hi, wr, preferred_element_type=F32)
    logits = (both[:, :LANES] + both[:, LANES:]
              + jnp.dot(h_lo, wr[:, :LANES], preferred_element_type=F32) + br_ref[...])
    lane = lax.broadcasted_iota(jnp.int32, logits.shape, 1)
    neg = -jnp.inf
    gl = jnp.where(lane < N_EXPERT_GROUPS, logits, neg)
    gmax = jnp.max(gl, axis=-1, keepdims=True)
    gsel = jnp.min(jnp.where(gl == gmax, lane, LANES), axis=-1, keepdims=True)
    p_group = 1.0 / jnp.sum(jnp.exp(gl - gmax), axis=-1, keepdims=True)
    elane = lane - N_EXPERT_GROUPS
    in_grp = (elane >= 0) & (elane < N_EXPERTS) & ((elane >> 3) == gsel)
    el = jnp.where(in_grp, logits, neg)
    m1 = jnp.max(el, axis=-1, keepdims=True)
    i1 = jnp.min(jnp.where(el == m1, lane, LANES), axis=-1, keepdims=True)
    den = jnp.sum(jnp.exp(el - m1), axis=-1, keepdims=True)
    el2 = jnp.where(lane == i1, neg, el)
    m2 = jnp.max(el2, axis=-1, keepdims=True)
    i2 = jnp.min(jnp.where(el2 == m2, lane, LANES), axis=-1, keepdims=True)
    g0 = p_group / den
    g1 = p_group * jnp.exp(m2 - m1) / den
    e0 = i1 - N_EXPERT_GROUPS
    e1 = i2 - N_EXPERT_GROUPS

    hit0 = lane == e0
    hit1 = lane == e1
    onehot = jnp.where(hit0 | hit1, 1.0, 0.0)
    r = lax.broadcasted_iota(jnp.int32, (tile, tile), 0)
    c = lax.broadcasted_iota(jnp.int32, (tile, tile), 1)
    tril = jnp.where(c < r, 1.0, 0.0).astype(BF16)
    before = jnp.dot(tril, onehot.astype(BF16), preferred_element_type=F32) + carry[...]
    rank0 = jnp.sum(jnp.where(hit0, before, 0.0), axis=-1, keepdims=True)
    rank1 = jnp.sum(jnp.where(hit1, before, 0.0), axis=-1, keepdims=True)
    carry[...] += jnp.sum(onehot, axis=0, keepdims=True)
    cnt_ref[...] = carry[...]
    info = jnp.where(lane == 0, e0.astype(F32),
                     jnp.where(lane == 1, e1.astype(F32),
                               jnp.where(lane == 2, g0,
                                         jnp.where(lane == 3, g1,
                                                   jnp.where(lane == 4, rank0, jnp.where(lane == 5, rank1, 0.0))))))
    info_ref[...] = info
    rt_ref[...] = info.T[:8]


def _mix(x2, attn, y3, wglu, bglu, gssm, wo_a, wo_s, gmoe, w_router, b_router, tile=512):
    t = x2.shape[0]
    kern = functools.partial(_mix_kernel, tile=tile)
    full = lambda shape: pl.BlockSpec(shape, lambda i: tuple(0 for _ in shape))
    return pl.pallas_call(
        kern, name="mix",
        grid=(t // tile,),
        in_specs=[pl.BlockSpec((tile, D_MODEL), lambda i: (i, 0)),
                  pl.BlockSpec((tile, ATTN_WIDTH), lambda i: (i, 0)),
                  pl.BlockSpec((CHUNK, tile // CHUNK, SSM_WIDTH), lambda i: (0, i, 0)),
                  full((SSM_WIDTH, SSM_WIDTH)), full((1, SSM_WIDTH)), full((1, SSM_WIDTH)),
                  full((ATTN_WIDTH, D_MODEL)), full((SSM_WIDTH, D_MODEL)), full((1, D_MODEL)),
                  full((D_MODEL, 2 * LANES)), full((1, LANES))],
        out_specs=[pl.BlockSpec((tile, D_MODEL), lambda i: (i, 0)),
                   pl.BlockSpec((tile * ROW_TILE, LANES), lambda i: (i, 0)),
                   pl.BlockSpec((tile, LANES), lambda i: (i, 0)),
                   pl.BlockSpec((8, tile), lambda i: (0, i)),
                   pl.BlockSpec((1, LANES), lambda i: (0, 0))],
        out_shape=[jax.ShapeDtypeStruct((t, D_MODEL), F32),
                   jax.ShapeDtypeStruct((t * ROW_TILE, LANES), F32),
                   jax.ShapeDtypeStruct((t, LANES), F32),
                   jax.ShapeDtypeStruct((8, t), F32),
                   jax.ShapeDtypeStruct((1, LANES), F32)],
        scratch_shapes=[pltpu.VMEM((SSM_WIDTH // LANES, tile, LANES), F32), pltpu.VMEM((1, LANES), F32)],
        compiler_params=_params(("arbitrary",), 48),
    )(x2, attn, y3, wglu, bglu, gssm, wo_a, wo_s, gmoe, w_router, b_router)


def _rows_at(ref, row, n_rows=1):
    return ref.at[pl.ds(pl.multiple_of(row * ROW_TILE, ROW_TILE), n_rows * ROW_TILE), :]


def _row_copy(src, s, dst, d, sem):
    return pltpu.make_async_copy(_rows_at(src, s), _rows_at(dst, d), sem)


UNROLL = 4
COMBINE_GROUPS = 8
PAD_SIZES = tuple(1 << b for b in reversed(range(ROW_BLOCK.bit_length() - 1)))


def _dispatch_kernel(dest_ref, pad_start_ref, pad_len_ref, h_ref, rows_out, zbuf, sem, zsem, *, tile, n_tok,
                     n_blocks):
    i = pl.program_id(0)
    base = i * tile

    def zero_fill(e, start):
        off, rem = pad_start_ref[e], pad_len_ref[e]
        for size in PAD_SIZES:
            @pl.when((rem & size) != 0)
            def _():
                cp = pltpu.make_async_copy(_rows_at(zbuf, 0, size), _rows_at(rows_out, off, size), zsem)
                cp.start() if start else cp.wait()
            off = off + (rem & size)

    def zero_tail(start):
        used = (pad_start_ref[N_EXPERTS - 1] + pad_len_ref[N_EXPERTS - 1]) // ROW_BLOCK

        def blk(b, carry):
            for half in range(ROW_BLOCK // PAD_SIZES[0]):
                cp = pltpu.make_async_copy(zbuf, _rows_at(rows_out, b * ROW_BLOCK + half * PAD_SIZES[0],
                                                          PAD_SIZES[0]), zsem)
                cp.start() if start else cp.wait()
            return carry

        lax.fori_loop(used, n_blocks, blk, 0)

    @pl.when(i == 0)
    def _():
        zbuf[...] = jnp.zeros(zbuf.shape, F32)
        lax.fori_loop(0, N_EXPERTS, lambda e, c: (zero_fill(e, True), c)[1], 0)
        zero_tail(True)

    def issue(tb, carry):
        for k in range(UNROLL):
            t = tb * UNROLL + k
            for j in range(2):
                _row_copy(h_ref, t, rows_out, dest_ref[j * n_tok + base + t], sem).start(priority=j)
        return carry

    def drain(tb, carry):
        for _ in range(2 * UNROLL):
            _row_copy(h_ref, 0, rows_out, 0, sem).wait()
        return carry

    lax.fori_loop(0, tile // UNROLL, issue, 0)
    lax.fori_loop(0, tile // UNROLL, drain, 0)

    @pl.when(i == 0)
    def _():
        lax.fori_loop(0, N_EXPERTS, lambda e, c: (zero_fill(e, False), c)[1], 0)
        zero_tail(False)


def _dispatch(dest_flat, pad_start, pad_len, h2, n_rows, tile=512):
    t = h2.shape[0] // ROW_TILE
    kern = functools.partial(_dispatch_kernel, tile=tile, n_tok=t, n_blocks=n_rows // ROW_BLOCK)
    grid_spec = pltpu.PrefetchScalarGridSpec(
        num_scalar_prefetch=3, grid=(t // tile,),
        in_specs=[pl.BlockSpec((tile * ROW_TILE, LANES), lambda i, *_: (i, 0))],
        out_specs=pl.BlockSpec(memory_space=pl.ANY),
        scratch_shapes=[pltpu.VMEM((PAD_SIZES[0] * ROW_TILE, LANES), F32),
                        pltpu.SemaphoreType.DMA(()), pltpu.SemaphoreType.DMA(())])
    return pl.pallas_call(
        kern, name="dispatch", grid_spec=grid_spec,
        out_shape=jax.ShapeDtypeStruct((n_rows * ROW_TILE, LANES), F32),
        compiler_params=_params(("arbitrary",), 32),
    )(dest_flat, pad_start, pad_len, h2)


def _expert_kernel(be_ref, nu_ref, first_ref, next_ref, slot_ref, rows_ref, wg_hbm, wu_hbm, wd_hbm, out_ref,
                   wg_f, wu_f, wd_f, wg_b, wu_b, wd_b, sems):
    i = pl.program_id(0)

    def fetch(e, slot):
        return (pltpu.make_async_copy(wg_hbm.at[e], wg_f.at[slot], sems.at[slot, 0]),
                pltpu.make_async_copy(wu_hbm.at[e], wu_f.at[slot], sems.at[slot, 1]),
                pltpu.make_async_copy(wd_hbm.at[e], wd_f.at[slot], sems.at[slot, 2]))

    @pl.when((i == 0) & (nu_ref[0] > 0))
    def _():
        for cp in fetch(be_ref[0], 0):
            cp.start()

    @pl.when((first_ref[i] == 1) & (i < nu_ref[0]))
    def _():
        slot = slot_ref[i]

        @pl.when(next_ref[i] >= 0)
        def _():
            for cp in fetch(next_ref[i], 1 - slot):
                cp.start()

        for cp in fetch(be_ref[i], slot):
            cp.wait()
        wg_b[...] = wg_f[slot].astype(BF16)
        wu_b[...] = wu_f[slot].astype(BF16)
        wd_b[...] = wd_f[slot].astype(BF16)

    @pl.when(i < nu_ref[0])
    def _():
        xb = _load_tile_rows(rows_ref).astype(BF16)
        gate = jnp.dot(xb, wg_b[...], preferred_element_type=F32)
        up = jnp.dot(xb, wu_b[...], preferred_element_type=F32)
        hid = (gate * jax.nn.sigmoid(gate) * up).astype(BF16)
        _store_tile_rows(out_ref, jnp.dot(hid, wd_b[...], preferred_element_type=F32))

    @pl.when(i >= nu_ref[0])
    def _():
        out_ref[...] = jnp.zeros(out_ref.shape, F32)


def _experts(block_e, n_used, seg_first, seg_next, seg_slot, rows, w_gate, w_up, w_down):
    nb = rows.shape[0] // (ROW_BLOCK * ROW_TILE)
    last = lambda i, be, nu, *_: jnp.maximum(jnp.minimum(i, nu[0] - 1), 0)
    hbm = pl.BlockSpec(memory_space=pl.ANY)
    grid_spec = pltpu.PrefetchScalarGridSpec(
        num_scalar_prefetch=5, grid=(nb,),
        in_specs=[pl.BlockSpec((ROW_BLOCK * ROW_TILE, LANES), lambda i, *s: (last(i, *s), 0)), hbm, hbm, hbm],
        out_specs=pl.BlockSpec((ROW_BLOCK * ROW_TILE, LANES), lambda i, *s: (i, 0)),
        scratch_shapes=[pltpu.VMEM((2, D_MODEL, EXPERT_FF), F32), pltpu.VMEM((2, D_MODEL, EXPERT_FF), F32),
                        pltpu.VMEM((2, EXPERT_FF, D_MODEL), F32),
                        pltpu.VMEM((D_MODEL, EXPERT_FF), BF16), pltpu.VMEM((D_MODEL, EXPERT_FF), BF16),
                        pltpu.VMEM((EXPERT_FF, D_MODEL), BF16), pltpu.SemaphoreType.DMA((2, 3))])
    return pl.pallas_call(
        _expert_kernel, name="experts", grid_spec=grid_spec,
        out_shape=jax.ShapeDtypeStruct(rows.shape, F32),
        compiler_params=_params(("arbitrary",), 48),
    )(block_e, n_used, seg_first, seg_next, seg_slot, rows, w_gate, w_up, w_down)


def _combine_kernel(dest_ref, x1_ref, info_ref, gfin_ref, rows_ref, o_ref, buf0, buf1, sems, *, tile, n_tok):
    i = pl.program_id(0)
    slot = i % 2

    def gather(step, to_slot):
        base = step * tile
        b0, b1, sem = buf0.at[to_slot], buf1.at[to_slot], sems.at[to_slot]

        def issue(tb, carry):
            for k in range(UNROLL):
                t = tb * UNROLL + k
                _row_copy(rows_ref, dest_ref[base + t], b0, t, sem).start(priority=0)
                _row_copy(rows_ref, dest_ref[n_tok + base + t], b1, t, sem).start(priority=1)
            return carry

        lax.fori_loop(0, tile // UNROLL, issue, 0)

    def drain(of_slot):
        b0, b1, sem = buf0.at[of_slot], buf1.at[of_slot], sems.at[of_slot]

        def wait(tb, carry):
            for _ in range(UNROLL):
                _row_copy(rows_ref, 0, b0, 0, sem).wait()
                _row_copy(rows_ref, 0, b1, 0, sem).wait()
            return carry

        lax.fori_loop(0, tile // UNROLL, wait, 0)

    @pl.when(i == 0)
    def _():
        gather(0, 0)

    drain(slot)

    last = pl.num_programs(0) - 1
    nxt_base = jnp.minimum(i + 1, last) * tile
    b0, b1 = buf0.at[slot], buf1.at[slot]
    n0, n1, nsem = buf0.at[1 - slot], buf1.at[1 - slot], sems.at[1 - slot]
    group = tile // COMBINE_GROUPS
    for g in range(COMBINE_GROUPS):
        lo = g * group
        for t in range(lo, lo + group):
            _row_copy(rows_ref, dest_ref[nxt_base + t], n0, t, nsem).start(priority=0)
            _row_copy(rows_ref, dest_ref[n_tok + nxt_base + t], n1, t, nsem).start(priority=1)
        info = info_ref[lo:lo + group, :]
        x2 = (x1_ref[lo:lo + group, :] + info[:, 2:3] * _load_tile_rows(b0, lo, group)
              + info[:, 3:4] * _load_tile_rows(b1, lo, group))
        o_ref[lo:lo + group, :] = _rms(x2, gfin_ref[...])

    @pl.when(i == last)
    def _():
        drain(1 - slot)


def _combine(dest_flat, x1, info, gfin, out_rows, tile=256):
    t = x1.shape[0]
    kern = functools.partial(_combine_kernel, tile=tile, n_tok=t)
    grid_spec = pltpu.PrefetchScalarGridSpec(
        num_scalar_prefetch=1, grid=(t // tile,),
        in_specs=[pl.BlockSpec((tile, D_MODEL), lambda i, d: (i, 0)),
                  pl.BlockSpec((tile, LANES), lambda i, d: (i, 0)),
                  pl.BlockSpec((1, D_MODEL), lambda i, d: (0, 0)),
                  pl.BlockSpec(memory_space=pl.ANY)],
        out_specs=pl.BlockSpec((tile, D_MODEL), lambda i, d: (i, 0)),
        scratch_shapes=[pltpu.VMEM((2, tile * ROW_TILE, LANES), F32), pltpu.VMEM((2, tile * ROW_TILE, LANES), F32),
                        pltpu.SemaphoreType.DMA((2,))])
    return pl.pallas_call(
        kern, name="combine", grid_spec=grid_spec,
        out_shape=jax.ShapeDtypeStruct((t, D_MODEL), F32),
        compiler_params=_params(("arbitrary",), 32),
    )(dest_flat, x1, info, gfin, out_rows)


def _split_w_in(w_in):
    def pad_blocks(w):
        w = w.reshape(D_MODEL, N_HEADS * 2, HEAD_DIM)
        return jnp.pad(w, ((0, 0), (0, 0), (0, AUG - HEAD_DIM))).reshape(D_MODEL, N_HEADS * 2 * AUG)
    wq_t = w_in[:, :ATTN_WIDTH].T.astype(BF16)
    wk = pad_blocks(w_in[:, ATTN_WIDTH:2 * ATTN_WIDTH]).astype(BF16)
    wv_t = w_in[:, 2 * ATTN_WIDTH:3 * ATTN_WIDTH].T.astype(BF16)
    wu = w_in[:, 3 * ATTN_WIDTH:].astype(BF16)
    return wq_t, wk, wv_t, wu


def kernel(x, norm_attn, w_in, lambda_q1, lambda_k1, lambda_q2, lambda_k2, attn_subln, ssm_lam_re, ssm_lam_im, ssm_log_dt, ssm_b_re, ssm_b_im, ssm_c_re, ssm_c_im, ssm_d, w_glu, b_glu, ssm_norm, w_out, norm_moe, w_router_group, b_router_group, w_router_expert, b_router_expert, w_gate, w_up, w_down, norm_final):
    batch, seq, d = x.shape
    t = batch * seq
    nchunk = seq // CHUNK
    x2 = x.reshape(t, d)
    l = 0

    q_t, k_aug, v_t, u3 = _in_proj(x2, norm_attn[l][None], *_split_w_in(w_in[l]), seq)
    attn = _attention(q_t, k_aug, v_t, lambda_q1[l][None], lambda_k1[l][None], lambda_q2[l][None],
                      lambda_k2[l][None], attn_subln[l][:, None], batch, seq)

    sc = _ssm_constants(ssm_lam_re[l], ssm_lam_im[l], ssm_log_dt[l], ssm_b_re[l], ssm_b_im[l],
                        ssm_c_re[l], ssm_c_im[l], ssm_d[l], nchunk)
    s_re, s_im = _ssm_state(u3, sc["b_re"], sc["b_im"], sc["a_row_re"], sc["a_row_im"])
    h_re, h_im = _ssm_scan(s_re, s_im, sc["p_re"], sc["p_im"], nchunk)
    y3 = _ssm_out(u3, sc["kt"], h_re, h_im, sc["c_re"], sc["c_im"], sc["a_col_re"], sc["a_col_im"])

    w_router = jnp.concatenate([w_router_group[l], w_router_expert[l]], axis=1).astype(F32)
    w_router = jnp.pad(w_router, ((0, 0), (0, LANES - w_router.shape[1])))
    w_router_hi = w_router.astype(BF16)
    w_router = jnp.concatenate([w_router_hi, (w_router - w_router_hi.astype(F32)).astype(BF16)], axis=1)
    b_router = jnp.concatenate([b_router_group[l], b_router_expert[l]]).astype(F32)
    b_router = jnp.pad(b_router, (0, LANES - b_router.shape[0]))[None]
    x1, h2, info, route_t, cnt = _mix(x2, attn, y3, w_glu[l].astype(BF16), b_glu[l][None], ssm_norm[l][None],
                                      w_out[l][:ATTN_WIDTH].astype(BF16), w_out[l][ATTN_WIDTH:].astype(BF16),
                                      norm_moe[l][None], w_router, b_router)

    experts = route_t[0:2].astype(jnp.int32)
    ranks = route_t[4:6].astype(jnp.int32)
    counts = cnt[0, :N_EXPERTS].astype(jnp.int32)
    padded = ((counts + ROW_BLOCK - 1) // ROW_BLOCK) * ROW_BLOCK
    ids = jnp.arange(N_EXPERTS, dtype=jnp.int32)
    pend = jnp.sum(jnp.where(ids[None, :] <= ids[:, None], padded[None, :], 0), axis=1)
    pstart = pend - padded
    dest = ranks + jnp.sum(jnp.where(experts[:, None, :] == ids[None, :, None], pstart[None, :, None], 0), axis=1)
    dest = dest.reshape(-1)
    n_rows = ((2 * t + N_EXPERTS * (ROW_BLOCK - 1) + ROW_BLOCK - 1) // ROW_BLOCK) * ROW_BLOCK
    nb = n_rows // ROW_BLOCK
    n_used = (pend[-1] // ROW_BLOCK).astype(jnp.int32)
    blk = jnp.minimum(jnp.arange(nb, dtype=jnp.int32), n_used - 1) * ROW_BLOCK
    block_e = jnp.minimum(jnp.sum((pend[None, :] <= blk[:, None]).astype(jnp.int32), axis=1), N_EXPERTS - 1)

    bidx = jnp.arange(nb, dtype=jnp.int32)
    seg_first = ((bidx == 0) | (block_e != jnp.concatenate([block_e[:1], block_e[:-1]]))).astype(jnp.int32)
    seg_slot = (jnp.sum(jnp.where(bidx[None, :] <= bidx[:, None], seg_first[None, :], 0), axis=1) - 1) & 1
    seg_end = jnp.sum(jnp.where(block_e[:, None] == ids[None, :], pend[None, :], 0), axis=1) // ROW_BLOCK
    after = jnp.sum(jnp.where(bidx[None, :] == seg_end[:, None], block_e[None, :], 0), axis=1)
    seg_next = jnp.where(seg_end < n_used, after, -1).astype(jnp.int32)

    rows = _dispatch(dest, pstart + counts, padded - counts, h2, n_rows)
    out_rows = _experts(block_e, n_used[None], seg_first, seg_next, seg_slot.astype(jnp.int32), rows,
                        w_gate[l], w_up[l], w_down[l])
    out = _combine(dest, x1, info, norm_final[None], out_rows)
    return out.reshape(batch, seq, d)
```

```python
import functools
import math

import jax
import jax.numpy as jnp
import numpy as np
from jax import lax
from jax.experimental import pallas as pl
from jax.experimental.pallas import tpu as pltpu

F32 = jnp.float32
BF16 = jnp.bfloat16

D_MODEL = 1024
N_HEADS = 4
HEAD_DIM = 64
VALUE_DIM = 128
ATTN_WIDTH = 512
SSM_WIDTH = 512
SSM_GROUP = 16
N_GROUPS = 32
SSM_STATE = 64
N_EXPERT_GROUPS = 4
EXPERTS_PER_GROUP = 8
N_EXPERTS = 32
EXPERT_FF = 512
RMS_EPS = 1e-6
LAMBDA_INIT = 0.8 - 0.6 * math.exp(-0.3 * 0)
LOG2E = math.log2(math.e)

CHUNK = 16
HALF_GROUPS = 16
TOEP_SPAN = 4
LANES = 128
AUG = 128
NORM_LANE = 70
QNORM_ROW = 71
V_ROWS = VALUE_DIM + 16
UNDERFLOW_LOG2 = 152.0
ROW_BLOCK = 256
ROW_TILE = D_MODEL // 128
VMEM_LIMIT_CAP = 56 * 1024 * 1024


def _params(dims, vmem_mb):
    return pltpu.CompilerParams(dimension_semantics=dims,
                                vmem_limit_bytes=min(vmem_mb * 1024 * 1024, VMEM_LIMIT_CAP))


def _rms(x, gain):
    return x * lax.rsqrt(jnp.mean(x * x, axis=-1, keepdims=True) + RMS_EPS) * gain


def _store_tile_rows(ref, val):
    n = val.shape[0]
    for c in range(ROW_TILE):
        ref[pl.ds(c, n, stride=ROW_TILE), :] = val[:, c * LANES:(c + 1) * LANES]


def _load_tile_rows(ref, first=0, n=None):
    n = ref.shape[0] // ROW_TILE if n is None else n
    return jnp.concatenate([ref[pl.ds(first * ROW_TILE + c, n, stride=ROW_TILE), :] for c in range(ROW_TILE)],
                           axis=-1)


def _split3(val):
    hi = val.astype(BF16).astype(F32)
    r1 = val - hi
    mid = r1.astype(BF16).astype(F32)
    lo = r1 - mid
    return hi, mid, lo


def _inproj_kernel(x_ref, g_ref, wqv_ref, wku_ref, q_ref, k_ref, v_ref, u_ref, ubuf, *, tile, seq):
    i = pl.program_id(0)
    h = _rms(x_ref[...], g_ref[...]).astype(BF16)
    nt = (((1,), (1,)), ((), ()))
    kcols = 2 * N_HEADS * AUG
    qt = lax.dot_general(wqv_ref[:ATTN_WIDTH, :], h, nt, preferred_element_type=F32)
    kp = jnp.dot(h, wku_ref[:, :kcols], preferred_element_type=F32)
    vt = lax.dot_general(wqv_ref[ATTN_WIDTH:, :], h, nt, preferred_element_type=F32)
    up = jnp.dot(h, wku_ref[:, kcols:], preferred_element_type=F32)
    ones_row = jnp.where(lax.broadcasted_iota(jnp.int32, (V_ROWS - VALUE_DIM, tile), 0) == 0, 1.0, 0.0)
    for hd in range(N_HEADS):
        v_ref[hd * V_ROWS:hd * V_ROWS + VALUE_DIM, :] = vt[hd * VALUE_DIM:(hd + 1) * VALUE_DIM, :].astype(BF16)
        v_ref[hd * V_ROWS + VALUE_DIM:(hd + 1) * V_ROWS, :] = ones_row.astype(BF16)
    pos0 = lax.rem(i * tile, seq)
    pos_k = (pos0 + lax.broadcasted_iota(jnp.int32, (tile, AUG), 0)).astype(F32)
    pos_q = (pos0 + lax.broadcasted_iota(jnp.int32, (1, tile), 1)).astype(F32)
    lane = lax.broadcasted_iota(jnp.int32, (tile, AUG), 1)
    srow = lax.broadcasted_iota(jnp.int32, (AUG - HEAD_DIM, tile), 0)
    qscale = HEAD_DIM ** -0.5 * LOG2E
    for hd in range(N_HEADS):
        slope = 2.0 ** (-8.0 * (hd + 1) / N_HEADS) * LOG2E
        hi, mid, lo = _split3(pos_k * slope)
        k_add = jnp.where(lane == 64, hi,
                          jnp.where(lane == 65, mid,
                                    jnp.where(lane == 66, lo, jnp.where((lane >= 67) & (lane < 70), 1.0, 0.0))))
        hi, mid, lo = _split3(pos_q * slope)
        q_add = jnp.where(srow < 3, 1.0,
                          jnp.where(srow == 3, -hi, jnp.where(srow == 4, -mid, jnp.where(srow == 5, -lo, 0.0))))
        for m in range(2):
            c0 = (hd * 2 + m) * AUG
            qb = (qt[c0 // 2:c0 // 2 + HEAD_DIM, :] * qscale).astype(BF16)
            q_ref[c0:c0 + HEAD_DIM, :] = qb
            qr = qb.astype(F32)
            qnorm2 = jnp.sum(qr * qr, axis=0, keepdims=True) * (1.0 + 2.0 ** -6)
            q_ref[c0 + HEAD_DIM:c0 + AUG, :] = (
                q_add + jnp.where(srow == QNORM_ROW - HEAD_DIM, qnorm2, 0.0)).astype(BF16)
            kb = kp[:, c0:c0 + AUG]
            kr = kb.astype(BF16).astype(F32)
            norm2 = jnp.sum(kr * kr, axis=-1, keepdims=True) * (1.0 + 2.0 ** -6)
            k_ref[:, c0:c0 + AUG] = (kb + k_add + jnp.where(lane == NORM_LANE, norm2, 0.0)).astype(BF16)
    for cb in range(SSM_WIDTH // LANES):
        ubuf[cb] = up[:, cb * LANES:(cb + 1) * LANES]
    per_half = HALF_GROUPS * SSM_GROUP // LANES
    for s in range(CHUNK):
        for cb in range(SSM_WIDTH // LANES):
            c0 = ((cb // per_half) * CHUNK + s) * HALF_GROUPS * SSM_GROUP + (cb % per_half) * LANES
            u_ref[:, c0:c0 + LANES] = ubuf[cb, pl.ds(s, tile // CHUNK, stride=CHUNK), :].astype(BF16)


def _in_proj(x2, gain, wqv_t, wku, seq, tile=512):
    t = x2.shape[0]
    qk = 2 * N_HEADS * AUG
    kern = functools.partial(_inproj_kernel, tile=tile, seq=seq)
    full = lambda shape: pl.BlockSpec(shape, lambda i: (0, 0))
    return pl.pallas_call(
        kern, name="in_proj",
        grid=(t // tile,),
        in_specs=[pl.BlockSpec((tile, D_MODEL), lambda i: (i, 0)),
                  full((1, D_MODEL)), full((2 * ATTN_WIDTH, D_MODEL)), full((D_MODEL, qk + SSM_WIDTH))],
        out_specs=[pl.BlockSpec((qk, tile), lambda i: (0, i)),
                   pl.BlockSpec((tile, qk), lambda i: (i, 0)),
                   pl.BlockSpec((N_HEADS * V_ROWS, tile), lambda i: (0, i)),
                   pl.BlockSpec((tile // CHUNK, CHUNK * SSM_WIDTH), lambda i: (i, 0))],
        out_shape=[jax.ShapeDtypeStruct((qk, t), BF16),
                   jax.ShapeDtypeStruct((t, qk), BF16),
                   jax.ShapeDtypeStruct((N_HEADS * V_ROWS, t), BF16),
                   jax.ShapeDtypeStruct((t // CHUNK, CHUNK * SSM_WIDTH), BF16)],
        scratch_shapes=[pltpu.VMEM((SSM_WIDTH // LANES, tile, LANES), F32)],
        compiler_params=_params(("arbitrary",), 48),
    )(x2, gain, wqv_t, wku)


def _attn_kernel(lq1, lk1, lq2, lk2, sub_ref, q_ref, k_ref, v_ref, o_ref, m_sc, acc_sc, kn_sc, *, tq, kb, strip):
    hd = pl.program_id(1)
    qi = pl.program_id(2)
    lam = (jnp.exp(jnp.sum(lq1[...] * lk1[...], axis=-1, keepdims=True))
           - jnp.exp(jnp.sum(lq2[...] * lk2[...], axis=-1, keepdims=True)) + LAMBDA_INIT)

    @pl.when(qi == 0)
    def _():
        for m in range(2):
            kn_sc[m] = jnp.max(k_ref[:, m * AUG:(m + 1) * AUG].astype(F32), axis=0, keepdims=True)

    def score(start, nkeys, align, q_lo=0, nq=tq):
        r0 = pl.multiple_of(start, align)
        kblk = k_ref[pl.ds(r0, nkeys), :]
        return [jnp.dot(kblk[:, m * AUG:(m + 1) * AUG], q_ref[m * AUG:(m + 1) * AUG, q_lo:q_lo + nq],
                        preferred_element_type=F32) for m in range(2)]

    def step(start, nkeys, align, q_lo=0, nq=tq, key_lo=None, first=False, scores=None):
        if scores is None:
            scores = score(start, nkeys, align, q_lo, nq)
        vblk = v_ref[:, pl.ds(pl.multiple_of(start, align), nkeys)]
        for m in range(2):
            s = scores[m]
            if key_lo is not None:
                key = lax.broadcasted_iota(jnp.int32, (nkeys, nq), 0) + key_lo
                qry = lax.broadcasted_iota(jnp.int32, (nkeys, nq), 1) + q_lo
                s = jnp.where(key <= qry, s, -jnp.inf)
            if first:
                m_new = jnp.max(s, axis=0, keepdims=True)
                pv = jnp.dot(vblk, jnp.exp2(s - m_new).astype(BF16), preferred_element_type=F32)
            else:
                m_prev = m_sc[m, :, q_lo:q_lo + nq]
                m_new = jnp.maximum(m_prev, jnp.max(s, axis=0, keepdims=True))
                pv = (jnp.exp2(m_prev - m_new) * acc_sc[m, :, q_lo:q_lo + nq]
                      + jnp.dot(vblk, jnp.exp2(s - m_new).astype(BF16), preferred_element_type=F32))
            acc_sc[m, :, q_lo:q_lo + nq] = pv
            m_sc[m, :, q_lo:q_lo + nq] = m_new

    for lo in range(0, tq, strip):
        step(qi * tq + lo, strip, strip, q_lo=lo, nq=tq - lo, key_lo=lo, first=lo == 0)

    lane = lax.broadcasted_iota(jnp.int32, (1, AUG), 1)
    sub = lax.broadcasted_iota(jnp.int32, (16, 1), 0)
    slope = LOG2E * jnp.exp2(-2.0 * (jnp.full((1, 1), hd, jnp.int32) + 1).astype(F32))
    reach = jnp.zeros((1, 1), F32)
    for m in range(2):
        qrows = jnp.max(q_ref[m * AUG + HEAD_DIM:m * AUG + HEAD_DIM + 16, :].astype(F32), axis=-1, keepdims=True)
        q2 = jnp.max(jnp.where(sub == QNORM_ROW - HEAD_DIM, qrows, 0.0), axis=0, keepdims=True)
        k2 = jnp.max(jnp.where(lane == NORM_LANE, kn_sc[m], 0.0), axis=-1, keepdims=True)
        m_min = jnp.min(m_sc[m], axis=-1, keepdims=True)
        reach = jnp.maximum(reach, (jnp.sqrt(q2 * k2) * 1.001 + (UNDERFLOW_LOG2 + 0.5) - m_min) / slope)
    need = jnp.ceil((reach + (kb - 1)) / kb).astype(jnp.int32) - 1
    n_below = jnp.minimum(jnp.max(jnp.maximum(need, 0)), qi * (tq // kb))
    top = qi * tq

    def quad(i, carry):
        a, b = top - (2 + 4 * i) * kb, top - (4 + 4 * i) * kb
        sa, sb = score(a, 2 * kb, kb), score(b, 2 * kb, kb)
        step(a, 2 * kb, kb, scores=sa)
        step(b, 2 * kb, kb, scores=sb)
        return carry

    lax.fori_loop(0, n_below // 4, quad, 0)
    rest = n_below % 4
    done = n_below - rest

    @pl.when(rest >= 2)
    def _():
        step(top - (done + 2) * kb, 2 * kb, kb)

    @pl.when(rest % 2 == 1)
    def _():
        step(top - n_below * kb, kb, kb)

    l0 = acc_sc[0, VALUE_DIM:VALUE_DIM + 1, :]
    l1 = acc_sc[1, VALUE_DIM:VALUE_DIM + 1, :]
    o = acc_sc[0, :VALUE_DIM, :] / l0 - lam * (acc_sc[1, :VALUE_DIM, :] / l1)
    o = o * lax.rsqrt(jnp.mean(o * o, axis=0, keepdims=True) + RMS_EPS) * sub_ref[...] * (1.0 - LAMBDA_INIT)
    o_ref[...] = o.T.astype(BF16)


def _attention(q_t, k_aug, v_t, lq1, lk1, lq2, lk2, subln_col, batch, seq, tq=1024, kb=512, strip=256):
    tq = min(tq, seq)
    nq = seq // tq
    t = batch * seq
    small = pl.BlockSpec((1, HEAD_DIM), lambda b, h, i: (0, 0))
    kern = functools.partial(_attn_kernel, tq=tq, kb=kb, strip=strip)
    return pl.pallas_call(
        kern, name="attention",
        grid=(batch, N_HEADS, nq),
        in_specs=[small, small, small, small,
                  pl.BlockSpec((VALUE_DIM, 1), lambda b, h, i: (0, 0)),
                  pl.BlockSpec((2 * AUG, tq), lambda b, h, i: (h, b * nq + i)),
                  pl.BlockSpec((seq, 2 * AUG), lambda b, h, i: (b, h)),
                  pl.BlockSpec((V_ROWS, seq), lambda b, h, i: (h, b))],
        out_specs=pl.BlockSpec((tq, VALUE_DIM), lambda b, h, i: (b * nq + i, h)),
        out_shape=jax.ShapeDtypeStruct((t, ATTN_WIDTH), BF16),
        scratch_shapes=[pltpu.VMEM((2, 1, tq), F32), pltpu.VMEM((2, V_ROWS, tq), F32),
                        pltpu.VMEM((2, 1, AUG), F32)],
        compiler_params=_params(("arbitrary", "arbitrary", "arbitrary"), 48),
    )(lq1, lk1, lq2, lk2, subln_col, q_t, k_aug, v_t)


SCAN_COLS = 256


def _ssm_state_kernel(u_ref, bre_ref, bim_ref, are_ref, aim_ref, pre_ref, pim_ref, hre_ref, him_ref,
                      wre, wim, sre, sim, *, nchunk, nsteps):
    i = pl.program_id(1)

    @pl.when(i == 0)
    def _():
        wre[...] = bre_ref[0]
        wim[...] = bim_ref[0]
        sre[...] = jnp.zeros(sre.shape, F32)
        sim[...] = jnp.zeros(sim.shape, F32)

    @pl.when(i > 0)
    def _():
        ar, ai = are_ref[0], aim_ref[0]
        wr, wi = wre[...], wim[...]
        wre[...] = wr * ar - wi * ai
        wim[...] = wr * ai + wi * ar

    u = u_ref[...]
    sre[...] += jnp.dot(u, wre[...].astype(BF16), preferred_element_type=F32)
    sim[...] += jnp.dot(u, wim[...].astype(BF16), preferred_element_type=F32)

    @pl.when(i == CHUNK - 1)
    def _():
        row = lax.rem(lax.broadcasted_iota(jnp.int32, (sre.shape[0], SCAN_COLS), 0), nchunk)
        for c0 in range(0, sre.shape[1], SCAN_COLS):
            cols = slice(c0, c0 + SCAN_COLS)
            hr, hi = sre[:, cols], sim[:, cols]
            for k in range(nsteps):
                d = 1 << k
                ar, ai = pre_ref[0, k:k + 1, cols], pim_ref[0, k:k + 1, cols]
                keep = row >= d
                pr = jnp.where(keep, pltpu.roll(hr, d, 0), 0.0)
                pi = jnp.where(keep, pltpu.roll(hi, d, 0), 0.0)
                hr, hi = hr + ar * pr - ai * pi, hi + ar * pi + ai * pr
            keep = row >= 1
            hre_ref[:, cols] = jnp.where(keep, pltpu.roll(hr, 1, 0), 0.0).astype(BF16)
            him_ref[:, cols] = jnp.where(keep, pltpu.roll(hi, 1, 0), 0.0).astype(BF16)


def _ssm_state(u3, b_re, b_im, a_row_re, a_row_im, p_re, p_im, nchunk):
    nch = u3.shape[0]
    nsteps = p_re.shape[1]
    hw = HALF_GROUPS * SSM_GROUP
    sw = HALF_GROUPS * SSM_STATE
    kern = functools.partial(_ssm_state_kernel, nchunk=nchunk, nsteps=nsteps)
    per_half = lambda shape: pl.BlockSpec((1,) + shape, lambda hf, i: (hf, 0, 0))
    return pl.pallas_call(
        kern, name="ssm_state",
        grid=(2, CHUNK),
        in_specs=[pl.BlockSpec((nch, hw), lambda hf, i: (0, hf * CHUNK + CHUNK - 1 - i)),
                  per_half((hw, sw)), per_half((hw, sw)), per_half((1, sw)), per_half((1, sw)),
                  per_half((nsteps, sw)), per_half((nsteps, sw))],
        out_specs=[pl.BlockSpec((nch, sw), lambda hf, i: (0, hf)),
                   pl.BlockSpec((nch, sw), lambda hf, i: (0, hf))],
        out_shape=[jax.ShapeDtypeStruct((nch, 2 * sw), BF16)] * 2,
        scratch_shapes=[pltpu.VMEM((hw, sw), F32), pltpu.VMEM((hw, sw), F32),
                        pltpu.VMEM((nch, sw), F32), pltpu.VMEM((nch, sw), F32)],
        compiler_params=_params(("arbitrary", "arbitrary"), 48),
    )(u3, b_re, b_im, a_row_re, a_row_im, p_re, p_im)


def _ssm_out_kernel(u_ref, kt_ref, hre_ref, him_ref, cre_ref, cim_ref, are_ref, aim_ref, y_ref, wre, wim, toep):
    t = pl.program_id(1)
    ar, ai = are_ref[0], aim_ref[0]

    @pl.when(t == 0)
    def _():
        cr, ci = cre_ref[0], cim_ref[0]
        wre[...] = cr * ar - ci * ai
        wim[...] = cr * ai + ci * ar

    @pl.when(t > 0)
    def _():
        wr, wi = wre[...], wim[...]
        wre[...] = wr * ar - wi * ai
        wim[...] = wr * ai + wi * ar

    hw = kt_ref.shape[-1]
    for s in range(CHUNK):
        tile = kt_ref[jnp.maximum(t - s, 0), 0]
        toep[s * hw:(s + 1) * hw, :] = jnp.where(s <= t, tile, jnp.zeros_like(tile))

    span = TOEP_SPAN * hw
    y_ref[0] = (jnp.dot(u_ref[:, :span], toep[:span, :], preferred_element_type=F32)
                + jnp.dot(hre_ref[...], wre[...].astype(BF16), preferred_element_type=F32)
                - jnp.dot(him_ref[...], wim[...].astype(BF16), preferred_element_type=F32))
    for piece in range(1, CHUNK // TOEP_SPAN):
        @pl.when(t >= piece * TOEP_SPAN)
        def _():
            lo = piece * span
            y_ref[0] += jnp.dot(u_ref[:, lo:lo + span], toep[lo:lo + span, :], preferred_element_type=F32)


def _ssm_out(u3, kt, h_re, h_im, c_re, c_im, a_col_re, a_col_im):
    nch = u3.shape[0]
    hw = HALF_GROUPS * SSM_GROUP
    sw = HALF_GROUPS * SSM_STATE
    return pl.pallas_call(
        _ssm_out_kernel, name="ssm_out",
        grid=(2, CHUNK),
        in_specs=[pl.BlockSpec((nch, CHUNK * hw), lambda hf, t: (0, hf)),
                  pl.BlockSpec((CHUNK, 1, hw, hw), lambda hf, t: (0, hf, 0, 0)),
                  pl.BlockSpec((nch, sw), lambda hf, t: (0, hf)),
                  pl.BlockSpec((nch, sw), lambda hf, t: (0, hf)),
                  pl.BlockSpec((1, sw, hw), lambda hf, t: (hf, 0, 0)),
                  pl.BlockSpec((1, sw, hw), lambda hf, t: (hf, 0, 0)),
                  pl.BlockSpec((1, sw, 1), lambda hf, t: (hf, 0, 0)),
                  pl.BlockSpec((1, sw, 1), lambda hf, t: (hf, 0, 0))],
        out_specs=pl.BlockSpec((1, nch, hw), lambda hf, t: (t, 0, hf)),
        out_shape=jax.ShapeDtypeStruct((CHUNK, nch, SSM_WIDTH), F32),
        scratch_shapes=[pltpu.VMEM((sw, hw), F32), pltpu.VMEM((sw, hw), F32), pltpu.VMEM((CHUNK * hw, hw), BF16)],
        compiler_params=_params(("arbitrary", "arbitrary"), 48),
    )(u3, kt, h_re, h_im, c_re, c_im, a_col_re, a_col_im)


def _ssm_constants(lam_re, lam_im, log_dt, b_re, b_im, c_re, c_im, d_skip, nchunk):
    lr, li = lam_re.astype(F32), lam_im.astype(F32)
    dt = jnp.exp(log_dt.astype(F32))[:, None]

    def lam_bar_pow(k):
        mag = jnp.exp(k * lr * dt)
        return mag * jnp.cos(k * li * dt), mag * jnp.sin(k * li * dt)

    a_re, a_im = lam_bar_pow(1.0)
    den = lr * lr + li * li
    coef_re = ((a_re - 1.0) * lr + a_im * li) / den
    coef_im = (a_im * lr - (a_re - 1.0) * li) / den
    bb_re = coef_re[..., None] * b_re.astype(F32) - coef_im[..., None] * b_im.astype(F32)
    bb_im = coef_re[..., None] * b_im.astype(F32) + coef_im[..., None] * b_re.astype(F32)
    cc_re, cc_im = c_re.astype(F32), c_im.astype(F32)
    lags = jnp.arange(CHUNK, dtype=F32)[:, None, None]
    pw_re, pw_im = lam_bar_pow(lags)
    pb_re = pw_re[..., None] * bb_re[None] - pw_im[..., None] * bb_im[None]
    pb_im = pw_re[..., None] * bb_im[None] + pw_im[..., None] * bb_re[None]
    kt = jnp.einsum('gcn,jgnd->jgdc', cc_re, pb_re) - jnp.einsum('gcn,jgnd->jgdc', cc_im, pb_im)
    skip = jnp.einsum('gc,dc->gdc', d_skip.astype(F32), jnp.eye(SSM_GROUP, dtype=F32))
    kt = jnp.concatenate([kt[:1] + skip[None], kt[1:]], axis=0)
    hw, sw = HALF_GROUPS * SSM_GROUP, HALF_GROUPS * SSM_STATE

    def block_diag(rows, row_group, col_group):
        wide = jnp.tile(rows, (1,) * (rows.ndim - 1) + (HALF_GROUPS,))
        r = lax.broadcasted_iota(jnp.int32, wide.shape, wide.ndim - 2) // row_group
        c = lax.broadcasted_iota(jnp.int32, wide.shape, wide.ndim - 1) // col_group
        return jnp.where(r == c, wide, 0.0)

    kt_t = block_diag(kt.reshape(CHUNK, 2, hw, SSM_GROUP), SSM_GROUP, SSM_GROUP).astype(BF16)

    def b_tiles(part):
        p = part.reshape(2, HALF_GROUPS, SSM_STATE, SSM_GROUP).transpose(0, 1, 3, 2)
        return block_diag(p.reshape(2, hw, SSM_STATE), SSM_GROUP, SSM_STATE)

    def c_tiles(part):
        p = part.reshape(2, HALF_GROUPS, SSM_GROUP, SSM_STATE).transpose(0, 1, 3, 2)
        return block_diag(p.reshape(2, sw, SSM_GROUP), SSM_STATE, SSM_GROUP)

    nsteps = max(int(math.log2(nchunk)), 1)
    steps = (CHUNK * 2.0 ** jnp.arange(nsteps, dtype=F32))[:, None, None]
    st_re, st_im = lam_bar_pow(steps)
    by_half = lambda p: p.reshape(nsteps, 2, sw).transpose(1, 0, 2)
    ar_h, ai_h = a_re.reshape(2, sw), a_im.reshape(2, sw)
    return dict(kt=kt_t, b_re=b_tiles(bb_re), b_im=b_tiles(bb_im),
                c_re=c_tiles(cc_re), c_im=c_tiles(cc_im),
                a_row_re=ar_h[:, None, :], a_row_im=ai_h[:, None, :],
                a_col_re=ar_h[:, :, None], a_col_im=ai_h[:, :, None],
                p_re=by_half(st_re), p_im=by_half(st_im))


def _mix_kernel(x_ref, attn_ref, y3_ref, wglu_ref, bglu_ref, gssm_ref, woa_ref, wos_ref, gmoe_ref,
                wr_ref, br_ref, x1_ref, h2_ref, info_ref, rt_ref, cnt_ref, ybuf, carry, *, tile):
    i = pl.program_id(0)

    @pl.when(i == 0)
    def _():
        carry[...] = jnp.zeros(carry.shape, F32)

    for s in range(CHUNK):
        for cb in range(SSM_WIDTH // LANES):
            ybuf[cb, pl.ds(s, tile // CHUNK, stride=CHUNK), :] = y3_ref[s, :, cb * LANES:(cb + 1) * LANES]
    y = jax.nn.gelu(jnp.concatenate([ybuf[cb] for cb in range(SSM_WIDTH // LANES)], axis=-1))
    z = jnp.dot(y.astype(BF16), wglu_ref[...], preferred_element_type=F32) + bglu_ref[...]
    y = y * jax.nn.sigmoid(z)
    ssm = _rms(y, gssm_ref[...])
    x1 = (x_ref[...] + jnp.dot(attn_ref[...], woa_ref[...], preferred_element_type=F32)
          + jnp.dot(ssm.astype(BF16), wos_ref[...], preferred_element_type=F32))
    x1_ref[...] = x1
    h2 = _rms(x1, gmoe_ref[...])
    _store_tile_rows(h2_ref, h2)

    h_hi = h2.astype(BF16)
    h_lo = (h2 - h_hi.astype(F32)).astype(BF16)
    wr = wr_ref[...]
    both = jnp.dot(h_hi, wr, preferred_element_type=F32)
    logits = (both[:, :LANES] + both[:, LANES:]
              + jnp.dot(h_lo, wr[:, :LANES], preferred_element_type=F32) + br_ref[...])
    lane = lax.broadcasted_iota(jnp.int32, logits.shape, 1)
    neg = -jnp.inf
    gl = jnp.where(lane < N_EXPERT_GROUPS, logits, neg)
    gmax = jnp.max(gl, axis=-1, keepdims=True)
    gsel = jnp.min(jnp.where(gl == gmax, lane, LANES), axis=-1, keepdims=True)
    p_group = 1.0 / jnp.sum(jnp.exp(gl - gmax), axis=-1, keepdims=True)
    elane = lane - N_EXPERT_GROUPS
    in_grp = (elane >= 0) & (elane < N_EXPERTS) & ((elane >> 3) == gsel)
    el = jnp.where(in_grp, logits, neg)
    m1 = jnp.max(el, axis=-1, keepdims=True)
    i1 = jnp.min(jnp.where(el == m1, lane, LANES), axis=-1, keepdims=True)
    den = jnp.sum(jnp.exp(el - m1), axis=-1, keepdims=True)
    el2 = jnp.where(lane == i1, neg, el)
    m2 = jnp.max(el2, axis=-1, keepdims=True)
    i2 = jnp.min(jnp.where(el2 == m2, lane, LANES), axis=-1, keepdims=True)
    g0 = p_group / den
    g1 = p_group * jnp.exp(m2 - m1) / den
    e0 = i1 - N_EXPERT_GROUPS
    e1 = i2 - N_EXPERT_GROUPS

    hit0 = lane == e0
    hit1 = lane == e1
    onehot = jnp.where(hit0 | hit1, 1.0, 0.0)
    r = lax.broadcasted_iota(jnp.int32, (tile, tile), 0)
    c = lax.broadcasted_iota(jnp.int32, (tile, tile), 1)
    tril = jnp.where(c < r, 1.0, 0.0).astype(BF16)
    before = jnp.dot(tril, onehot.astype(BF16), preferred_element_type=F32) + carry[...]
    rank0 = jnp.sum(jnp.where(hit0, before, 0.0), axis=-1, keepdims=True)
    rank1 = jnp.sum(jnp.where(hit1, before, 0.0), axis=-1, keepdims=True)
    carry[...] += jnp.sum(onehot, axis=0, keepdims=True)
    cnt_ref[...] = carry[...]
    info = jnp.where(lane == 0, e0.astype(F32),
                     jnp.where(lane == 1, e1.astype(F32),
                               jnp.where(lane == 2, g0,
                                         jnp.where(lane == 3, g1,
                                                   jnp.where(lane == 4, rank0, jnp.where(lane == 5, rank1, 0.0))))))
    info_ref[...] = info
    rt_ref[...] = info.T[:8]


def _mix(x2, attn, y3, wglu, bglu, gssm, wo_a, wo_s, gmoe, w_router, b_router, tile=512):
    t = x2.shape[0]
    kern = functools.partial(_mix_kernel, tile=tile)
    full = lambda shape: pl.BlockSpec(shape, lambda i: tuple(0 for _ in shape))
    return pl.pallas_call(
        kern, name="mix",
        grid=(t // tile,),
        in_specs=[pl.BlockSpec((tile, D_MODEL), lambda i: (i, 0)),
                  pl.BlockSpec((tile, ATTN_WIDTH), lambda i: (i, 0)),
                  pl.BlockSpec((CHUNK, tile // CHUNK, SSM_WIDTH), lambda i: (0, i, 0)),
                  full((SSM_WIDTH, SSM_WIDTH)), full((1, SSM_WIDTH)), full((1, SSM_WIDTH)),
                  full((ATTN_WIDTH, D_MODEL)), full((SSM_WIDTH, D_MODEL)), full((1, D_MODEL)),
                  full((D_MODEL, 2 * LANES)), full((1, LANES))],
        out_specs=[pl.BlockSpec((tile, D_MODEL), lambda i: (i, 0)),
                   pl.BlockSpec((tile * ROW_TILE, LANES), lambda i: (i, 0)),
                   pl.BlockSpec((tile, LANES), lambda i: (i, 0)),
                   pl.BlockSpec((8, tile), lambda i: (0, i)),
                   pl.BlockSpec((1, LANES), lambda i: (0, 0))],
        out_shape=[jax.ShapeDtypeStruct((t, D_MODEL), F32),
                   jax.ShapeDtypeStruct((t * ROW_TILE, LANES), F32),
                   jax.ShapeDtypeStruct((t, LANES), F32),
                   jax.ShapeDtypeStruct((8, t), F32),
                   jax.ShapeDtypeStruct((1, LANES), F32)],
        scratch_shapes=[pltpu.VMEM((SSM_WIDTH // LANES, tile, LANES), F32), pltpu.VMEM((1, LANES), F32)],
        compiler_params=_params(("arbitrary",), 48),
    )(x2, attn, y3, wglu, bglu, gssm, wo_a, wo_s, gmoe, w_router, b_router)


def _rows_at(ref, row, n_rows=1):
    return ref.at[pl.ds(pl.multiple_of(row * ROW_TILE, ROW_TILE), n_rows * ROW_TILE), :]


def _row_copy(src, s, dst, d, sem):
    return pltpu.make_async_copy(_rows_at(src, s), _rows_at(dst, d), sem)


UNROLL = 4
COMBINE_GROUPS = 16
PAD_SIZES = tuple(1 << b for b in reversed(range(ROW_BLOCK.bit_length() - 1)))


def _dispatch_kernel(dest_ref, pad_start_ref, pad_len_ref, h_ref, rows_out, zbuf, sem, zsem, *, tile, n_tok,
                     n_blocks):
    i = pl.program_id(0)
    base = i * tile

    def zero_fill(e, start):
        off, rem = pad_start_ref[e], pad_len_ref[e]
        for size in PAD_SIZES:
            @pl.when((rem & size) != 0)
            def _():
                cp = pltpu.make_async_copy(_rows_at(zbuf, 0, size), _rows_at(rows_out, off, size), zsem)
                cp.start() if start else cp.wait()
            off = off + (rem & size)

    def zero_tail(start):
        used = (pad_start_ref[N_EXPERTS - 1] + pad_len_ref[N_EXPERTS - 1]) // ROW_BLOCK

        def blk(b, carry):
            for half in range(ROW_BLOCK // PAD_SIZES[0]):
                cp = pltpu.make_async_copy(zbuf, _rows_at(rows_out, b * ROW_BLOCK + half * PAD_SIZES[0],
                                                          PAD_SIZES[0]), zsem)
                cp.start() if start else cp.wait()
            return carry

        lax.fori_loop(used, n_blocks, blk, 0)

    @pl.when(i == 0)
    def _():
        zbuf[...] = jnp.zeros(zbuf.shape, F32)
        lax.fori_loop(0, N_EXPERTS, lambda e, c: (zero_fill(e, True), c)[1], 0)
        zero_tail(True)

    def issue(tb, carry):
        for k in range(UNROLL):
            t = tb * UNROLL + k
            for j in range(2):
                _row_copy(h_ref, t, rows_out, dest_ref[j * n_tok + base + t], sem).start(priority=j)
        return carry

    def drain(tb, carry):
        for _ in range(2 * UNROLL):
            _row_copy(h_ref, 0, rows_out, 0, sem).wait()
        return carry

    lax.fori_loop(0, tile // UNROLL, issue, 0)
    lax.fori_loop(0, tile // UNROLL, drain, 0)

    @pl.when(i == 0)
    def _():
        lax.fori_loop(0, N_EXPERTS, lambda e, c: (zero_fill(e, False), c)[1], 0)
        zero_tail(False)


def _dispatch(dest_flat, pad_start, pad_len, h2, n_rows, tile=512):
    t = h2.shape[0] // ROW_TILE
    kern = functools.partial(_dispatch_kernel, tile=tile, n_tok=t, n_blocks=n_rows // ROW_BLOCK)
    grid_spec = pltpu.PrefetchScalarGridSpec(
        num_scalar_prefetch=3, grid=(t // tile,),
        in_specs=[pl.BlockSpec((tile * ROW_TILE, LANES), lambda i, *_: (i, 0))],
        out_specs=pl.BlockSpec(memory_space=pl.ANY),
        scratch_shapes=[pltpu.VMEM((PAD_SIZES[0] * ROW_TILE, LANES), F32),
                        pltpu.SemaphoreType.DMA(()), pltpu.SemaphoreType.DMA(())])
    return pl.pallas_call(
        kern, name="dispatch", grid_spec=grid_spec,
        out_shape=jax.ShapeDtypeStruct((n_rows * ROW_TILE, LANES), F32),
        compiler_params=_params(("arbitrary",), 32),
    )(dest_flat, pad_start, pad_len, h2)


def _expert_kernel(be_ref, nu_ref, first_ref, next_ref, slot_ref, rows_ref, wg_hbm, wu_hbm, wd_hbm, out_ref,
                   wg_f, wu_f, wd_f, wg_b, wu_b, wd_b, sems):
    i = pl.program_id(0)

    def fetch(e, slot):
        return (pltpu.make_async_copy(wg_hbm.at[e], wg_f.at[slot], sems.at[slot, 0]),
                pltpu.make_async_copy(wu_hbm.at[e], wu_f.at[slot], sems.at[slot, 1]),
                pltpu.make_async_copy(wd_hbm.at[e], wd_f.at[slot], sems.at[slot, 2]))

    @pl.when((i == 0) & (nu_ref[0] > 0))
    def _():
        for cp in fetch(be_ref[0], 0):
            cp.start()

    @pl.when((first_ref[i] == 1) & (i < nu_ref[0]))
    def _():
        slot = slot_ref[i]

        @pl.when(next_ref[i] >= 0)
        def _():
            for cp in fetch(next_ref[i], 1 - slot):
                cp.start()

        for cp in fetch(be_ref[i], slot):
            cp.wait()
        wg_b[...] = wg_f[slot].astype(BF16)
        wu_b[...] = wu_f[slot].astype(BF16)
        wd_b[...] = wd_f[slot].astype(BF16)

    @pl.when(i < nu_ref[0])
    def _():
        xb = _load_tile_rows(rows_ref).astype(BF16)
        gate = jnp.dot(xb, wg_b[...], preferred_element_type=F32)
        up = jnp.dot(xb, wu_b[...], preferred_element_type=F32)
        hid = (gate * jax.nn.sigmoid(gate) * up).astype(BF16)
        _store_tile_rows(out_ref, jnp.dot(hid, wd_b[...], preferred_element_type=F32))

    @pl.when(i >= nu_ref[0])
    def _():
        out_ref[...] = jnp.zeros(out_ref.shape, F32)


def _experts(block_e, n_used, seg_first, seg_next, seg_slot, rows, w_gate, w_up, w_down):
    nb = rows.shape[0] // (ROW_BLOCK * ROW_TILE)
    last = lambda i, be, nu, *_: jnp.maximum(jnp.minimum(i, nu[0] - 1), 0)
    hbm = pl.BlockSpec(memory_space=pl.ANY)
    grid_spec = pltpu.PrefetchScalarGridSpec(
        num_scalar_prefetch=5, grid=(nb,),
        in_specs=[pl.BlockSpec((ROW_BLOCK * ROW_TILE, LANES), lambda i, *s: (last(i, *s), 0)), hbm, hbm, hbm],
        out_specs=pl.BlockSpec((ROW_BLOCK * ROW_TILE, LANES), lambda i, *s: (i, 0)),
        scratch_shapes=[pltpu.VMEM((2, D_MODEL, EXPERT_FF), F32), pltpu.VMEM((2, D_MODEL, EXPERT_FF), F32),
                        pltpu.VMEM((2, EXPERT_FF, D_MODEL), F32),
                        pltpu.VMEM((D_MODEL, EXPERT_FF), BF16), pltpu.VMEM((D_MODEL, EXPERT_FF), BF16),
                        pltpu.VMEM((EXPERT_FF, D_MODEL), BF16), pltpu.SemaphoreType.DMA((2, 3))])
    return pl.pallas_call(
        _expert_kernel, name="experts", grid_spec=grid_spec,
        out_shape=jax.ShapeDtypeStruct(rows.shape, F32),
        compiler_params=_params(("arbitrary",), 48),
    )(block_e, n_used, seg_first, seg_next, seg_slot, rows, w_gate, w_up, w_down)


def _combine_kernel(dest_ref, x1_ref, info_ref, gfin_ref, rows_ref, o_ref, buf0, buf1, sems, *, tile, n_tok):
    i = pl.program_id(0)
    slot = i % 2

    def gather(step, to_slot):
        base = step * tile
        b0, b1, sem = buf0.at[to_slot], buf1.at[to_slot], sems.at[to_slot]

        def issue(tb, carry):
            for k in range(UNROLL):
                t = tb * UNROLL + k
                _row_copy(rows_ref, dest_ref[base + t], b0, t, sem).start(priority=0)
                _row_copy(rows_ref, dest_ref[n_tok + base + t], b1, t, sem).start(priority=1)
            return carry

        lax.fori_loop(0, tile // UNROLL, issue, 0)

    def drain(of_slot):
        b0, b1, sem = buf0.at[of_slot], buf1.at[of_slot], sems.at[of_slot]

        def wait(tb, carry):
            for _ in range(UNROLL):
                _row_copy(rows_ref, 0, b0, 0, sem).wait()
                _row_copy(rows_ref, 0, b1, 0, sem).wait()
            return carry

        lax.fori_loop(0, tile // UNROLL, wait, 0)

    @pl.when(i == 0)
    def _():
        gather(0, 0)

    drain(slot)

    last = pl.num_programs(0) - 1
    nxt_base = jnp.minimum(i + 1, last) * tile
    b0, b1 = buf0.at[slot], buf1.at[slot]
    n0, n1, nsem = buf0.at[1 - slot], buf1.at[1 - slot], sems.at[1 - slot]
    group = tile // COMBINE_GROUPS
    for g in range(COMBINE_GROUPS):
        lo = g * group
        for t in range(lo, lo + group):
            _row_copy(rows_ref, dest_ref[nxt_base + t], n0, t, nsem).start(priority=0)
            _row_copy(rows_ref, dest_ref[n_tok + nxt_base + t], n1, t, nsem).start(priority=1)
        info = info_ref[lo:lo + group, :]
        x2 = (x1_ref[lo:lo + group, :] + info[:, 2:3] * _load_tile_rows(b0, lo, group)
              + info[:, 3:4] * _load_tile_rows(b1, lo, group))
        o_ref[lo:lo + group, :] = _rms(x2, gfin_ref[...])

    @pl.when(i == last)
    def _():
        drain(1 - slot)


def _combine(dest_flat, x1, info, gfin, out_rows, tile=512):
    t = x1.shape[0]
    kern = functools.partial(_combine_kernel, tile=tile, n_tok=t)
    grid_spec = pltpu.PrefetchScalarGridSpec(
        num_scalar_prefetch=1, grid=(t // tile,),
        in_specs=[pl.BlockSpec((tile, D_MODEL), lambda i, d: (i, 0)),
                  pl.BlockSpec((tile, LANES), lambda i, d: (i, 0)),
                  pl.BlockSpec((1, D_MODEL), lambda i, d: (0, 0)),
                  pl.BlockSpec(memory_space=pl.ANY)],
        out_specs=pl.BlockSpec((tile, D_MODEL), lambda i, d: (i, 0)),
        scratch_shapes=[pltpu.VMEM((2, tile * ROW_TILE, LANES), F32), pltpu.VMEM((2, tile * ROW_TILE, LANES), F32),
                        pltpu.SemaphoreType.DMA((2,))])
    return pl.pallas_call(
        kern, name="combine", grid_spec=grid_spec,
        out_shape=jax.ShapeDtypeStruct((t, D_MODEL), F32),
        compiler_params=_params(("arbitrary",), 32),
    )(dest_flat, x1, info, gfin, out_rows)


def _split_w_in(w_in):
    w4 = w_in.astype(BF16).reshape(D_MODEL, 4, ATTN_WIDTH)
    wqv_t = w4[:, 0::2, :].reshape(D_MODEL, 2 * ATTN_WIDTH).T
    wk = jnp.pad(w4[:, 1, :].reshape(D_MODEL, N_HEADS * 2, HEAD_DIM), ((0, 0), (0, 0), (0, AUG - HEAD_DIM)))
    wku = jnp.concatenate([wk.reshape(D_MODEL, N_HEADS * 2 * AUG), w4[:, 3, :]], axis=1)
    return wqv_t, wku


def kernel(x, norm_attn, w_in, lambda_q1, lambda_k1, lambda_q2, lambda_k2, attn_subln, ssm_lam_re, ssm_lam_im, ssm_log_dt, ssm_b_re, ssm_b_im, ssm_c_re, ssm_c_im, ssm_d, w_glu, b_glu, ssm_norm, w_out, norm_moe, w_router_group, b_router_group, w_router_expert, b_router_expert, w_gate, w_up, w_down, norm_final):
    batch, seq, d = x.shape
    t = batch * seq
    nchunk = seq // CHUNK
    x2 = x.reshape(t, d)
    l = 0

    q_t, k_aug, v_t, u3 = _in_proj(x2, norm_attn[l][None], *_split_w_in(w_in[l]), seq)
    attn = _attention(q_t, k_aug, v_t, lambda_q1[l][None], lambda_k1[l][None], lambda_q2[l][None],
                      lambda_k2[l][None], attn_subln[l][:, None], batch, seq)

    sc = _ssm_constants(ssm_lam_re[l], ssm_lam_im[l], ssm_log_dt[l], ssm_b_re[l], ssm_b_im[l],
                        ssm_c_re[l], ssm_c_im[l], ssm_d[l], nchunk)
    h_re, h_im = _ssm_state(u3, sc["b_re"], sc["b_im"], sc["a_row_re"], sc["a_row_im"], sc["p_re"], sc["p_im"],
                            nchunk)
    y3 = _ssm_out(u3, sc["kt"], h_re, h_im, sc["c_re"], sc["c_im"], sc["a_col_re"], sc["a_col_im"])

    w_router = jnp.concatenate([w_router_group[l], w_router_expert[l]], axis=1).astype(F32)
    w_router = jnp.pad(w_router, ((0, 0), (0, LANES - w_router.shape[1])))
    w_router_hi = w_router.astype(BF16)
    w_router = jnp.concatenate([w_router_hi, (w_router - w_router_hi.astype(F32)).astype(BF16)], axis=1)
    b_router = jnp.concatenate([b_router_group[l], b_router_expert[l]]).astype(F32)
    b_router = jnp.pad(b_router, (0, LANES - b_router.shape[0]))[None]
    x1, h2, info, route_t, cnt = _mix(x2, attn, y3, w_glu[l].astype(BF16), b_glu[l][None], ssm_norm[l][None],
                                      w_out[l][:ATTN_WIDTH].astype(BF16), w_out[l][ATTN_WIDTH:].astype(BF16),
                                      norm_moe[l][None], w_router, b_router)

    experts = route_t[0:2].astype(jnp.int32)
    ranks = route_t[4:6].astype(jnp.int32)
    counts = cnt[0, :N_EXPERTS].astype(jnp.int32)
    padded = ((counts + ROW_BLOCK - 1) // ROW_BLOCK) * ROW_BLOCK
    ids = jnp.arange(N_EXPERTS, dtype=jnp.int32)
    pend = jnp.sum(jnp.where(ids[None, :] <= ids[:, None], padded[None, :], 0), axis=1)
    pstart = pend - padded
    dest = ranks + jnp.sum(jnp.where(experts[:, None, :] == ids[None, :, None], pstart[None, :, None], 0), axis=1)
    dest = dest.reshape(-1)
    n_rows = ((2 * t + N_EXPERTS * (ROW_BLOCK - 1) + ROW_BLOCK - 1) // ROW_BLOCK) * ROW_BLOCK
    nb = n_rows // ROW_BLOCK
    n_used = (pend[-1] // ROW_BLOCK).astype(jnp.int32)
    blk = jnp.minimum(jnp.arange(nb, dtype=jnp.int32), n_used - 1) * ROW_BLOCK
    block_e = jnp.minimum(jnp.sum((pend[None, :] <= blk[:, None]).astype(jnp.int32), axis=1), N_EXPERTS - 1)

    bidx = jnp.arange(nb, dtype=jnp.int32)
    seg_first = ((bidx == 0) | (block_e != jnp.concatenate([block_e[:1], block_e[:-1]]))).astype(jnp.int32)
    seg_slot = (jnp.sum(jnp.where(bidx[None, :] <= bidx[:, None], seg_first[None, :], 0), axis=1) - 1) & 1
    seg_end = jnp.sum(jnp.where(block_e[:, None] == ids[None, :], pend[None, :], 0), axis=1) // ROW_BLOCK
    after = jnp.sum(jnp.where(bidx[None, :] == seg_end[:, None], block_e[None, :], 0), axis=1)
    seg_next = jnp.where(seg_end < n_used, after, -1).astype(jnp.int32)

    rows = _dispatch(dest, pstart + counts, padded - counts, h2, n_rows)
    out_rows = _experts(block_e, n_used[None], seg_first, seg_next, seg_slot.astype(jnp.int32), rows,
                        w_gate[l], w_up[l], w_down[l])
    out = _combine(dest, x1, info, norm_final[None], out_rows)
    return out.reshape(batch, seq, d)
```

```python
import functools
import math

import jax
import jax.numpy as jnp
import numpy as np
from jax import lax
from jax.experimental import pallas as pl
from jax.experimental.pallas import tpu as pltpu

F32 = jnp.float32
BF16 = jnp.bfloat16

D_MODEL = 1024
N_HEADS = 4
HEAD_DIM = 64
VALUE_DIM = 128
ATTN_WIDTH = 512
SSM_WIDTH = 512
SSM_GROUP = 16
N_GROUPS = 32
SSM_STATE = 64
N_EXPERT_GROUPS = 4
EXPERTS_PER_GROUP = 8
N_EXPERTS = 32
EXPERT_FF = 512
RMS_EPS = 1e-6
LAMBDA_INIT = 0.8 - 0.6 * math.exp(-0.3 * 0)
LOG2E = math.log2(math.e)

CHUNK = 16
HALF_GROUPS = 16
TOEP_SPAN = 4
ROUTER_ROWS = 48
LANES = 128
AUG = 128
NORM_LANE = 70
QNORM_ROW = 71
V_ROWS = VALUE_DIM + 16
UNDERFLOW_LOG2 = 152.0
ROW_BLOCK = 256
ROW_TILE = D_MODEL // 128
VMEM_LIMIT_CAP = 56 * 1024 * 1024


def _params(dims, vmem_mb):
    return pltpu.CompilerParams(dimension_semantics=dims,
                                vmem_limit_bytes=min(vmem_mb * 1024 * 1024, VMEM_LIMIT_CAP))


def _rms(x, gain):
    return x * lax.rsqrt(jnp.mean(x * x, axis=-1, keepdims=True) + RMS_EPS) * gain


def _store_tile_rows(ref, val):
    n = val.shape[0]
    for c in range(ROW_TILE):
        ref[pl.ds(c, n, stride=ROW_TILE), :] = val[:, c * LANES:(c + 1) * LANES]


def _load_tile_rows(ref, first=0, n=None):
    n = ref.shape[0] // ROW_TILE if n is None else n
    return jnp.concatenate([ref[pl.ds(first * ROW_TILE + c, n, stride=ROW_TILE), :] for c in range(ROW_TILE)],
                           axis=-1)


def _split3(val):
    hi = val.astype(BF16).astype(F32)
    r1 = val - hi
    mid = r1.astype(BF16).astype(F32)
    lo = r1 - mid
    return hi, mid, lo


def _inproj_kernel(x_ref, g_ref, wqv_ref, wku_ref, q_ref, k_ref, v_ref, u_ref, ubuf, *, tile, seq):
    i = pl.program_id(0)
    h = _rms(x_ref[...], g_ref[...]).astype(BF16)
    nt = (((1,), (1,)), ((), ()))
    kcols = 2 * N_HEADS * AUG
    qt = lax.dot_general(wqv_ref[:ATTN_WIDTH, :], h, nt, preferred_element_type=F32)
    kp = jnp.dot(h, wku_ref[:, :kcols], preferred_element_type=F32)
    vt = lax.dot_general(wqv_ref[ATTN_WIDTH:, :], h, nt, preferred_element_type=F32)
    up = jnp.dot(h, wku_ref[:, kcols:], preferred_element_type=F32)
    ones_row = jnp.where(lax.broadcasted_iota(jnp.int32, (V_ROWS - VALUE_DIM, tile), 0) == 0, 1.0, 0.0)
    for hd in range(N_HEADS):
        v_ref[hd * V_ROWS:hd * V_ROWS + VALUE_DIM, :] = vt[hd * VALUE_DIM:(hd + 1) * VALUE_DIM, :].astype(BF16)
        v_ref[hd * V_ROWS + VALUE_DIM:(hd + 1) * V_ROWS, :] = ones_row.astype(BF16)
    pos0 = lax.rem(i * tile, seq)
    pos_k = (pos0 + lax.broadcasted_iota(jnp.int32, (tile, AUG), 0)).astype(F32)
    pos_q = (pos0 + lax.broadcasted_iota(jnp.int32, (1, tile), 1)).astype(F32)
    lane = lax.broadcasted_iota(jnp.int32, (tile, AUG), 1)
    srow = lax.broadcasted_iota(jnp.int32, (AUG - HEAD_DIM, tile), 0)
    qscale = HEAD_DIM ** -0.5 * LOG2E
    for hd in range(N_HEADS):
        slope = 2.0 ** (-8.0 * (hd + 1) / N_HEADS) * LOG2E
        hi, mid, lo = _split3(pos_k * slope)
        k_add = jnp.where(lane == 64, hi,
                          jnp.where(lane == 65, mid,
                                    jnp.where(lane == 66, lo, jnp.where((lane >= 67) & (lane < 70), 1.0, 0.0))))
        hi, mid, lo = _split3(pos_q * slope)
        q_add = jnp.where(srow < 3, 1.0,
                          jnp.where(srow == 3, -hi, jnp.where(srow == 4, -mid, jnp.where(srow == 5, -lo, 0.0))))
        for m in range(2):
            c0 = (hd * 2 + m) * AUG
            qb = (qt[c0 // 2:c0 // 2 + HEAD_DIM, :] * qscale).astype(BF16)
            q_ref[c0:c0 + HEAD_DIM, :] = qb
            qr = qb.astype(F32)
            qnorm2 = jnp.sum(qr * qr, axis=0, keepdims=True) * (1.0 + 2.0 ** -6)
            q_ref[c0 + HEAD_DIM:c0 + AUG, :] = (
                q_add + jnp.where(srow == QNORM_ROW - HEAD_DIM, qnorm2, 0.0)).astype(BF16)
            kb = kp[:, c0:c0 + AUG]
            kr = kb.astype(BF16).astype(F32)
            norm2 = jnp.sum(kr * kr, axis=-1, keepdims=True) * (1.0 + 2.0 ** -6)
            k_ref[:, c0:c0 + AUG] = (kb + k_add + jnp.where(lane == NORM_LANE, norm2, 0.0)).astype(BF16)
    for cb in range(SSM_WIDTH // LANES):
        ubuf[cb] = up[:, cb * LANES:(cb + 1) * LANES]
    per_half = HALF_GROUPS * SSM_GROUP // LANES
    for s in range(CHUNK):
        for cb in range(SSM_WIDTH // LANES):
            c0 = ((cb // per_half) * CHUNK + s) * HALF_GROUPS * SSM_GROUP + (cb % per_half) * LANES
            u_ref[:, c0:c0 + LANES] = ubuf[cb, pl.ds(s, tile // CHUNK, stride=CHUNK), :].astype(BF16)


def _in_proj(x2, gain, wqv_t, wku, seq, tile=512):
    t = x2.shape[0]
    qk = 2 * N_HEADS * AUG
    kern = functools.partial(_inproj_kernel, tile=tile, seq=seq)
    full = lambda shape: pl.BlockSpec(shape, lambda i: (0, 0))
    return pl.pallas_call(
        kern, name="in_proj",
        grid=(t // tile,),
        in_specs=[pl.BlockSpec((tile, D_MODEL), lambda i: (i, 0)),
                  full((1, D_MODEL)), full((2 * ATTN_WIDTH, D_MODEL)), full((D_MODEL, qk + SSM_WIDTH))],
        out_specs=[pl.BlockSpec((qk, tile), lambda i: (0, i)),
                   pl.BlockSpec((tile, qk), lambda i: (i, 0)),
                   pl.BlockSpec((N_HEADS * V_ROWS, tile), lambda i: (0, i)),
                   pl.BlockSpec((tile // CHUNK, CHUNK * SSM_WIDTH), lambda i: (i, 0))],
        out_shape=[jax.ShapeDtypeStruct((qk, t), BF16),
                   jax.ShapeDtypeStruct((t, qk), BF16),
                   jax.ShapeDtypeStruct((N_HEADS * V_ROWS, t), BF16),
                   jax.ShapeDtypeStruct((t // CHUNK, CHUNK * SSM_WIDTH), BF16)],
        scratch_shapes=[pltpu.VMEM((SSM_WIDTH // LANES, tile, LANES), F32)],
        compiler_params=_params(("arbitrary",), 48),
    )(x2, gain, wqv_t, wku)


def _attn_kernel(lq1, lk1, lq2, lk2, sub_ref, q_ref, k_ref, v_ref, o_ref, m_sc, acc_sc, kn_sc, *, tq, kb, strip):
    hd = pl.program_id(1)
    qi = pl.program_id(2)
    lam = (jnp.exp(jnp.sum(lq1[...] * lk1[...], axis=-1, keepdims=True))
           - jnp.exp(jnp.sum(lq2[...] * lk2[...], axis=-1, keepdims=True)) + LAMBDA_INIT)

    @pl.when(qi == 0)
    def _():
        for m in range(2):
            kn_sc[m] = jnp.max(k_ref[:, m * AUG:(m + 1) * AUG].astype(F32), axis=0, keepdims=True)

    def score(start, nkeys, align, q_lo=0, nq=tq):
        r0 = pl.multiple_of(start, align)
        kblk = k_ref[pl.ds(r0, nkeys), :]
        return [jnp.dot(kblk[:, m * AUG:(m + 1) * AUG], q_ref[m * AUG:(m + 1) * AUG, q_lo:q_lo + nq],
                        preferred_element_type=F32) for m in range(2)]

    def step(start, nkeys, align, q_lo=0, nq=tq, key_lo=None, first=False, scores=None):
        if scores is None:
            scores = score(start, nkeys, align, q_lo, nq)
        vblk = v_ref[:, pl.ds(pl.multiple_of(start, align), nkeys)]
        for m in range(2):
            s = scores[m]
            if key_lo is not None:
                key = lax.broadcasted_iota(jnp.int32, (nkeys, nq), 0) + key_lo
                qry = lax.broadcasted_iota(jnp.int32, (nkeys, nq), 1) + q_lo
                s = jnp.where(key <= qry, s, -jnp.inf)
            if first:
                m_new = jnp.max(s, axis=0, keepdims=True)
                pv = jnp.dot(vblk, jnp.exp2(s - m_new).astype(BF16), preferred_element_type=F32)
            else:
                m_prev = m_sc[m, :, q_lo:q_lo + nq]
                m_new = jnp.maximum(m_prev, jnp.max(s, axis=0, keepdims=True))
                pv = (jnp.exp2(m_prev - m_new) * acc_sc[m, :, q_lo:q_lo + nq]
                      + jnp.dot(vblk, jnp.exp2(s - m_new).astype(BF16), preferred_element_type=F32))
            acc_sc[m, :, q_lo:q_lo + nq] = pv
            m_sc[m, :, q_lo:q_lo + nq] = m_new

    for lo in range(0, tq, strip):
        step(qi * tq + lo, strip, strip, q_lo=lo, nq=tq - lo, key_lo=lo, first=lo == 0)

    lane = lax.broadcasted_iota(jnp.int32, (1, AUG), 1)
    sub = lax.broadcasted_iota(jnp.int32, (16, 1), 0)
    slope = LOG2E * jnp.exp2(-2.0 * (jnp.full((1, 1), hd, jnp.int32) + 1).astype(F32))
    reach = jnp.zeros((1, 1), F32)
    for m in range(2):
        qrows = jnp.max(q_ref[m * AUG + HEAD_DIM:m * AUG + HEAD_DIM + 16, :].astype(F32), axis=-1, keepdims=True)
        q2 = jnp.max(jnp.where(sub == QNORM_ROW - HEAD_DIM, qrows, 0.0), axis=0, keepdims=True)
        k2 = jnp.max(jnp.where(lane == NORM_LANE, kn_sc[m], 0.0), axis=-1, keepdims=True)
        m_min = jnp.min(m_sc[m], axis=-1, keepdims=True)
        reach = jnp.maximum(reach, (jnp.sqrt(q2 * k2) * 1.001 + (UNDERFLOW_LOG2 + 0.5) - m_min) / slope)
    need = jnp.ceil((reach + (kb - 1)) / kb).astype(jnp.int32) - 1
    n_below = jnp.minimum(jnp.max(jnp.maximum(need, 0)), qi * (tq // kb))
    top = qi * tq

    def quad(i, carry):
        a, b = top - (2 + 4 * i) * kb, top - (4 + 4 * i) * kb
        sa, sb = score(a, 2 * kb, kb), score(b, 2 * kb, kb)
        step(a, 2 * kb, kb, scores=sa)
        step(b, 2 * kb, kb, scores=sb)
        return carry

    lax.fori_loop(0, n_below // 4, quad, 0)
    rest = n_below % 4
    done = n_below - rest

    @pl.when(rest >= 2)
    def _():
        step(top - (done + 2) * kb, 2 * kb, kb)

    @pl.when(rest % 2 == 1)
    def _():
        step(top - n_below * kb, kb, kb)

    l0 = acc_sc[0, VALUE_DIM:VALUE_DIM + 1, :]
    l1 = acc_sc[1, VALUE_DIM:VALUE_DIM + 1, :]
    o = acc_sc[0, :VALUE_DIM, :] / l0 - lam * (acc_sc[1, :VALUE_DIM, :] / l1)
    o = o * lax.rsqrt(jnp.mean(o * o, axis=0, keepdims=True) + RMS_EPS) * sub_ref[...] * (1.0 - LAMBDA_INIT)
    o_ref[...] = o.T.astype(BF16)


def _attention(q_t, k_aug, v_t, lq1, lk1, lq2, lk2, subln_col, batch, seq, tq=1024, kb=512, strip=256):
    tq = min(tq, seq)
    nq = seq // tq
    t = batch * seq
    small = pl.BlockSpec((1, HEAD_DIM), lambda b, h, i: (0, 0))
    kern = functools.partial(_attn_kernel, tq=tq, kb=kb, strip=strip)
    return pl.pallas_call(
        kern, name="attention",
        grid=(batch, N_HEADS, nq),
        in_specs=[small, small, small, small,
                  pl.BlockSpec((VALUE_DIM, 1), lambda b, h, i: (0, 0)),
                  pl.BlockSpec((2 * AUG, tq), lambda b, h, i: (h, b * nq + i)),
                  pl.BlockSpec((seq, 2 * AUG), lambda b, h, i: (b, h)),
                  pl.BlockSpec((V_ROWS, seq), lambda b, h, i: (h, b))],
        out_specs=pl.BlockSpec((tq, VALUE_DIM), lambda b, h, i: (b * nq + i, h)),
        out_shape=jax.ShapeDtypeStruct((t, ATTN_WIDTH), BF16),
        scratch_shapes=[pltpu.VMEM((2, 1, tq), F32), pltpu.VMEM((2, V_ROWS, tq), F32),
                        pltpu.VMEM((2, 1, AUG), F32)],
        compiler_params=_params(("arbitrary", "arbitrary", "arbitrary"), 48),
    )(lq1, lk1, lq2, lk2, subln_col, q_t, k_aug, v_t)


SCAN_COLS = 256


def _ssm_state_kernel(u_ref, bre_ref, bim_ref, are_ref, aim_ref, pre_ref, pim_ref, hre_ref, him_ref,
                      wre, wim, sre, sim, *, nchunk, nsteps):
    i = pl.program_id(1)

    @pl.when(i == 0)
    def _():
        wre[...] = bre_ref[0]
        wim[...] = bim_ref[0]
        sre[...] = jnp.zeros(sre.shape, F32)
        sim[...] = jnp.zeros(sim.shape, F32)

    @pl.when(i > 0)
    def _():
        ar, ai = are_ref[0], aim_ref[0]
        wr, wi = wre[...], wim[...]
        wre[...] = wr * ar - wi * ai
        wim[...] = wr * ai + wi * ar

    u = u_ref[...]
    sre[...] += jnp.dot(u, wre[...].astype(BF16), preferred_element_type=F32)
    sim[...] += jnp.dot(u, wim[...].astype(BF16), preferred_element_type=F32)

    @pl.when(i == CHUNK - 1)
    def _():
        row = lax.rem(lax.broadcasted_iota(jnp.int32, (sre.shape[0], SCAN_COLS), 0), nchunk)
        for c0 in range(0, sre.shape[1], SCAN_COLS):
            cols = slice(c0, c0 + SCAN_COLS)
            hr, hi = sre[:, cols], sim[:, cols]
            for k in range(nsteps):
                d = 1 << k
                ar, ai = pre_ref[0, k:k + 1, cols], pim_ref[0, k:k + 1, cols]
                keep = row >= d
                pr = jnp.where(keep, pltpu.roll(hr, d, 0), 0.0)
                pi = jnp.where(keep, pltpu.roll(hi, d, 0), 0.0)
                hr, hi = hr + ar * pr - ai * pi, hi + ar * pi + ai * pr
            keep = row >= 1
            hre_ref[:, cols] = jnp.where(keep, pltpu.roll(hr, 1, 0), 0.0).astype(BF16)
            him_ref[:, cols] = jnp.where(keep, pltpu.roll(hi, 1, 0), 0.0).astype(BF16)


def _ssm_state(u3, b_re, b_im, a_row_re, a_row_im, p_re, p_im, nchunk):
    nch = u3.shape[0]
    nsteps = p_re.shape[1]
    hw = HALF_GROUPS * SSM_GROUP
    sw = HALF_GROUPS * SSM_STATE
    kern = functools.partial(_ssm_state_kernel, nchunk=nchunk, nsteps=nsteps)
    per_half = lambda shape: pl.BlockSpec((1,) + shape, lambda hf, i: (hf, 0, 0))
    return pl.pallas_call(
        kern, name="ssm_state",
        grid=(2, CHUNK),
        in_specs=[pl.BlockSpec((nch, hw), lambda hf, i: (0, hf * CHUNK + CHUNK - 1 - i)),
                  per_half((hw, sw)), per_half((hw, sw)), per_half((1, sw)), per_half((1, sw)),
                  per_half((nsteps, sw)), per_half((nsteps, sw))],
        out_specs=[pl.BlockSpec((nch, sw), lambda hf, i: (0, hf)),
                   pl.BlockSpec((nch, sw), lambda hf, i: (0, hf))],
        out_shape=[jax.ShapeDtypeStruct((nch, 2 * sw), BF16)] * 2,
        scratch_shapes=[pltpu.VMEM((hw, sw), F32), pltpu.VMEM((hw, sw), F32),
                        pltpu.VMEM((nch, sw), F32), pltpu.VMEM((nch, sw), F32)],
        compiler_params=_params(("arbitrary", "arbitrary"), 48),
    )(u3, b_re, b_im, a_row_re, a_row_im, p_re, p_im)


def _ssm_out_kernel(u_ref, kt_ref, hre_ref, him_ref, cre_ref, cim_ref, are_ref, aim_ref, y_ref, wre, wim, toep):
    t = pl.program_id(1)
    ar, ai = are_ref[0], aim_ref[0]

    @pl.when(t == 0)
    def _():
        cr, ci = cre_ref[0], cim_ref[0]
        wre[...] = cr * ar - ci * ai
        wim[...] = cr * ai + ci * ar

    @pl.when(t > 0)
    def _():
        wr, wi = wre[...], wim[...]
        wre[...] = wr * ar - wi * ai
        wim[...] = wr * ai + wi * ar

    hw = kt_ref.shape[-1]
    for s in range(CHUNK):
        tile = kt_ref[jnp.maximum(t - s, 0), 0]
        toep[s * hw:(s + 1) * hw, :] = jnp.where(s <= t, tile, jnp.zeros_like(tile))

    span = TOEP_SPAN * hw
    y_ref[0] = (jnp.dot(u_ref[:, :span], toep[:span, :], preferred_element_type=F32)
                + jnp.dot(hre_ref[...], wre[...].astype(BF16), preferred_element_type=F32)
                - jnp.dot(him_ref[...], wim[...].astype(BF16), preferred_element_type=F32))
    for piece in range(1, CHUNK // TOEP_SPAN):
        @pl.when(t >= piece * TOEP_SPAN)
        def _():
            lo = piece * span
            y_ref[0] += jnp.dot(u_ref[:, lo:lo + span], toep[lo:lo + span, :], preferred_element_type=F32)


def _ssm_out(u3, kt, h_re, h_im, c_re, c_im, a_col_re, a_col_im):
    nch = u3.shape[0]
    hw = HALF_GROUPS * SSM_GROUP
    sw = HALF_GROUPS * SSM_STATE
    return pl.pallas_call(
        _ssm_out_kernel, name="ssm_out",
        grid=(2, CHUNK),
        in_specs=[pl.BlockSpec((nch, CHUNK * hw), lambda hf, t: (0, hf)),
                  pl.BlockSpec((CHUNK, 1, hw, hw), lambda hf, t: (0, hf, 0, 0)),
                  pl.BlockSpec((nch, sw), lambda hf, t: (0, hf)),
                  pl.BlockSpec((nch, sw), lambda hf, t: (0, hf)),
                  pl.BlockSpec((1, sw, hw), lambda hf, t: (hf, 0, 0)),
                  pl.BlockSpec((1, sw, hw), lambda hf, t: (hf, 0, 0)),
                  pl.BlockSpec((1, sw, 1), lambda hf, t: (hf, 0, 0)),
                  pl.BlockSpec((1, sw, 1), lambda hf, t: (hf, 0, 0))],
        out_specs=pl.BlockSpec((1, nch, hw), lambda hf, t: (t, 0, hf)),
        out_shape=jax.ShapeDtypeStruct((CHUNK, nch, SSM_WIDTH), F32),
        scratch_shapes=[pltpu.VMEM((sw, hw), F32), pltpu.VMEM((sw, hw), F32), pltpu.VMEM((CHUNK * hw, hw), BF16)],
        compiler_params=_params(("arbitrary", "arbitrary"), 48),
    )(u3, kt, h_re, h_im, c_re, c_im, a_col_re, a_col_im)


def _ssm_constants(lam_re, lam_im, log_dt, b_re, b_im, c_re, c_im, d_skip, nchunk):
    lr, li = lam_re.astype(F32), lam_im.astype(F32)
    dt = jnp.exp(log_dt.astype(F32))[:, None]

    def lam_bar_pow(k):
        mag = jnp.exp(k * lr * dt)
        return mag * jnp.cos(k * li * dt), mag * jnp.sin(k * li * dt)

    a_re, a_im = lam_bar_pow(1.0)
    den = lr * lr + li * li
    coef_re = ((a_re - 1.0) * lr + a_im * li) / den
    coef_im = (a_im * lr - (a_re - 1.0) * li) / den
    bb_re = coef_re[..., None] * b_re.astype(F32) - coef_im[..., None] * b_im.astype(F32)
    bb_im = coef_re[..., None] * b_im.astype(F32) + coef_im[..., None] * b_re.astype(F32)
    cc_re, cc_im = c_re.astype(F32), c_im.astype(F32)
    lags = jnp.arange(CHUNK, dtype=F32)[:, None, None]
    pw_re, pw_im = lam_bar_pow(lags)
    pb_re = pw_re[..., None] * bb_re[None] - pw_im[..., None] * bb_im[None]
    pb_im = pw_re[..., None] * bb_im[None] + pw_im[..., None] * bb_re[None]
    kt = jnp.einsum('gcn,jgnd->jgdc', cc_re, pb_re) - jnp.einsum('gcn,jgnd->jgdc', cc_im, pb_im)
    skip = jnp.einsum('gc,dc->gdc', d_skip.astype(F32), jnp.eye(SSM_GROUP, dtype=F32))
    kt = jnp.concatenate([kt[:1] + skip[None], kt[1:]], axis=0)
    hw, sw = HALF_GROUPS * SSM_GROUP, HALF_GROUPS * SSM_STATE

    def block_diag(rows, row_group, col_group):
        wide = jnp.tile(rows, (1,) * (rows.ndim - 1) + (HALF_GROUPS,))
        r = lax.broadcasted_iota(jnp.int32, wide.shape, wide.ndim - 2) // row_group
        c = lax.broadcasted_iota(jnp.int32, wide.shape, wide.ndim - 1) // col_group
        return jnp.where(r == c, wide, 0.0)

    kt_t = block_diag(kt.reshape(CHUNK, 2, hw, SSM_GROUP), SSM_GROUP, SSM_GROUP).astype(BF16)

    def b_tiles(part):
        p = part.reshape(2, HALF_GROUPS, SSM_STATE, SSM_GROUP).transpose(0, 1, 3, 2)
        return block_diag(p.reshape(2, hw, SSM_STATE), SSM_GROUP, SSM_STATE)

    def c_tiles(part):
        p = part.reshape(2, HALF_GROUPS, SSM_GROUP, SSM_STATE).transpose(0, 1, 3, 2)
        return block_diag(p.reshape(2, sw, SSM_GROUP), SSM_STATE, SSM_GROUP)

    nsteps = max(int(math.log2(nchunk)), 1)
    steps = (CHUNK * 2.0 ** jnp.arange(nsteps, dtype=F32))[:, None, None]
    st_re, st_im = lam_bar_pow(steps)
    by_half = lambda p: p.reshape(nsteps, 2, sw).transpose(1, 0, 2)
    ar_h, ai_h = a_re.reshape(2, sw), a_im.reshape(2, sw)
    return dict(kt=kt_t, b_re=b_tiles(bb_re), b_im=b_tiles(bb_im),
                c_re=c_tiles(cc_re), c_im=c_tiles(cc_im),
                a_row_re=ar_h[:, None, :], a_row_im=ai_h[:, None, :],
                a_col_re=ar_h[:, :, None], a_col_im=ai_h[:, :, None],
                p_re=by_half(st_re), p_im=by_half(st_im))


def _mix_kernel(x_ref, attn_ref, y3_ref, wglu_ref, bglu_ref, gssm_ref, woa_ref, wos_ref, gmoe_ref,
                wr_ref, br_ref, x1_ref, h2_ref, info_ref, rt_ref, cnt_ref, ybuf, carry, *, tile):
    i = pl.program_id(0)

    @pl.when(i == 0)
    def _():
        carry[...] = jnp.zeros(carry.shape, F32)

    for s in range(CHUNK):
        for cb in range(SSM_WIDTH // LANES):
            ybuf[cb, pl.ds(s, tile // CHUNK, stride=CHUNK), :] = y3_ref[s, :, cb * LANES:(cb + 1) * LANES]
    y = jax.nn.gelu(jnp.concatenate([ybuf[cb] for cb in range(SSM_WIDTH // LANES)], axis=-1))
    z = jnp.dot(y.astype(BF16), wglu_ref[...], preferred_element_type=F32) + bglu_ref[...]
    y = y * jax.nn.sigmoid(z)
    ssm = _rms(y, gssm_ref[...])
    x1 = (x_ref[...] + jnp.dot(attn_ref[...], woa_ref[...], preferred_element_type=F32)
          + jnp.dot(ssm.astype(BF16), wos_ref[...], preferred_element_type=F32))
    x1_ref[...] = x1
    h2 = _rms(x1, gmoe_ref[...])
    _store_tile_rows(h2_ref, h2)

    nt = (((1,), (1,)), ((), ()))
    h_hi = h2.astype(BF16)
    h_lo = (h2 - h_hi.astype(F32)).astype(BF16)
    both = lax.dot_general(wr_ref[...], h_hi, nt, preferred_element_type=F32)
    logits = (both[:ROUTER_ROWS] + both[LANES:LANES + ROUTER_ROWS]
              + lax.dot_general(wr_ref[:ROUTER_ROWS, :], h_lo, nt, preferred_element_type=F32) + br_ref[...])
    row = lax.broadcasted_iota(jnp.int32, logits.shape, 0)
    neg = -jnp.inf
    gl = jnp.where(row < N_EXPERT_GROUPS, logits, neg)
    gmax = jnp.max(gl, axis=0, keepdims=True)
    gsel = jnp.min(jnp.where(gl == gmax, row, LANES), axis=0, keepdims=True)
    p_group = 1.0 / jnp.sum(jnp.exp(gl - gmax), axis=0, keepdims=True)
    erow = row - N_EXPERT_GROUPS
    in_grp = (erow >= 0) & (erow < N_EXPERTS) & ((erow >> 3) == gsel)
    el = jnp.where(in_grp, logits, neg)
    m1 = jnp.max(el, axis=0, keepdims=True)
    i1 = jnp.min(jnp.where(el == m1, row, LANES), axis=0, keepdims=True)
    den = jnp.sum(jnp.exp(el - m1), axis=0, keepdims=True)
    el2 = jnp.where(row == i1, neg, el)
    m2 = jnp.max(el2, axis=0, keepdims=True)
    i2 = jnp.min(jnp.where(el2 == m2, row, LANES), axis=0, keepdims=True)
    g0 = p_group / den
    g1 = p_group * jnp.exp(m2 - m1) / den

    hit0 = row == i1
    hit1 = row == i2
    onehot = jnp.where(hit0 | hit1, 1.0, 0.0)
    r = lax.broadcasted_iota(jnp.int32, (tile, tile), 0)
    c = lax.broadcasted_iota(jnp.int32, (tile, tile), 1)
    earlier = jnp.where(r < c, 1.0, 0.0).astype(BF16)
    before = jnp.dot(onehot.astype(BF16), earlier, preferred_element_type=F32) + carry[...]
    rank0 = jnp.sum(jnp.where(hit0, before, 0.0), axis=0, keepdims=True)
    rank1 = jnp.sum(jnp.where(hit1, before, 0.0), axis=0, keepdims=True)
    carry[...] += jnp.sum(onehot, axis=1, keepdims=True)
    cnt_ref[...] = jnp.broadcast_to(carry[...], cnt_ref.shape)
    r8 = lax.broadcasted_iota(jnp.int32, (8, tile), 0)
    route = jnp.where(r8 == 0, (i1 - N_EXPERT_GROUPS).astype(F32),
                      jnp.where(r8 == 1, (i2 - N_EXPERT_GROUPS).astype(F32),
                                jnp.where(r8 == 2, g0,
                                          jnp.where(r8 == 3, g1,
                                                    jnp.where(r8 == 4, rank0, jnp.where(r8 == 5, rank1, 0.0))))))
    rt_ref[...] = route
    info_ref[...] = route.T


def _mix(x2, attn, y3, wglu, bglu, gssm, wo_a, wo_s, gmoe, w_router, b_router, tile=512):
    t = x2.shape[0]
    kern = functools.partial(_mix_kernel, tile=tile)
    full = lambda shape: pl.BlockSpec(shape, lambda i: tuple(0 for _ in shape))
    return pl.pallas_call(
        kern, name="mix",
        grid=(t // tile,),
        in_specs=[pl.BlockSpec((tile, D_MODEL), lambda i: (i, 0)),
                  pl.BlockSpec((tile, ATTN_WIDTH), lambda i: (i, 0)),
                  pl.BlockSpec((CHUNK, tile // CHUNK, SSM_WIDTH), lambda i: (0, i, 0)),
                  full((SSM_WIDTH, SSM_WIDTH)), full((1, SSM_WIDTH)), full((1, SSM_WIDTH)),
                  full((ATTN_WIDTH, D_MODEL)), full((SSM_WIDTH, D_MODEL)), full((1, D_MODEL)),
                  full((2 * LANES, D_MODEL)), full((ROUTER_ROWS, 1))],
        out_specs=[pl.BlockSpec((tile, D_MODEL), lambda i: (i, 0)),
                   pl.BlockSpec((tile * ROW_TILE, LANES), lambda i: (i, 0)),
                   pl.BlockSpec((tile, 8), lambda i: (i, 0)),
                   pl.BlockSpec((8, tile), lambda i: (0, i)),
                   pl.BlockSpec((ROUTER_ROWS, LANES), lambda i: (0, 0))],
        out_shape=[jax.ShapeDtypeStruct((t, D_MODEL), F32),
                   jax.ShapeDtypeStruct((t * ROW_TILE, LANES), F32),
                   jax.ShapeDtypeStruct((t, 8), F32),
                   jax.ShapeDtypeStruct((8, t), F32),
                   jax.ShapeDtypeStruct((ROUTER_ROWS, LANES), F32)],
        scratch_shapes=[pltpu.VMEM((SSM_WIDTH // LANES, tile, LANES), F32), pltpu.VMEM((ROUTER_ROWS, 1), F32)],
        compiler_params=_params(("arbitrary",), 48),
    )(x2, attn, y3, wglu, bglu, gssm, wo_a, wo_s, gmoe, w_router, b_router)


def _rows_at(ref, row, n_rows=1):
    return ref.at[pl.ds(pl.multiple_of(row * ROW_TILE, ROW_TILE), n_rows * ROW_TILE), :]


def _row_copy(src, s, dst, d, sem):
    return pltpu.make_async_copy(_rows_at(src, s), _rows_at(dst, d), sem)


UNROLL = 4
COMBINE_GROUPS = 16
PAD_SIZES = tuple(1 << b for b in reversed(range(ROW_BLOCK.bit_length() - 1)))


def _dispatch_kernel(dest_ref, pad_start_ref, pad_len_ref, h_ref, rows_out, zbuf, sem, zsem, *, tile, n_tok,
                     n_blocks):
    i = pl.program_id(0)
    base = i * tile

    def zero_fill(e, start):
        off, rem = pad_start_ref[e], pad_len_ref[e]
        for size in PAD_SIZES:
            @pl.when((rem & size) != 0)
            def _():
                cp = pltpu.make_async_copy(_rows_at(zbuf, 0, size), _rows_at(rows_out, off, size), zsem)
                cp.start() if start else cp.wait()
            off = off + (rem & size)

    def zero_tail(start):
        used = (pad_start_ref[N_EXPERTS - 1] + pad_len_ref[N_EXPERTS - 1]) // ROW_BLOCK

        def blk(b, carry):
            for half in range(ROW_BLOCK // PAD_SIZES[0]):
                cp = pltpu.make_async_copy(zbuf, _rows_at(rows_out, b * ROW_BLOCK + half * PAD_SIZES[0],
                                                          PAD_SIZES[0]), zsem)
                cp.start() if start else cp.wait()
            return carry

        lax.fori_loop(used, n_blocks, blk, 0)

    @pl.when(i == 0)
    def _():
        zbuf[...] = jnp.zeros(zbuf.shape, F32)
        lax.fori_loop(0, N_EXPERTS, lambda e, c: (zero_fill(e, True), c)[1], 0)
        zero_tail(True)

    def issue(tb, carry):
        for k in range(UNROLL):
            t = tb * UNROLL + k
            for j in range(2):
                _row_copy(h_ref, t, rows_out, dest_ref[j * n_tok + base + t], sem).start(priority=j)
        return carry

    def drain(tb, carry):
        for _ in range(2 * UNROLL):
            _row_copy(h_ref, 0, rows_out, 0, sem).wait()
        return carry

    lax.fori_loop(0, tile // UNROLL, issue, 0)
    lax.fori_loop(0, tile // UNROLL, drain, 0)

    @pl.when(i == 0)
    def _():
        lax.fori_loop(0, N_EXPERTS, lambda e, c: (zero_fill(e, False), c)[1], 0)
        zero_tail(False)


def _dispatch(dest_flat, pad_start, pad_len, h2, n_rows, tile=512):
    t = h2.shape[0] // ROW_TILE
    kern = functools.partial(_dispatch_kernel, tile=tile, n_tok=t, n_blocks=n_rows // ROW_BLOCK)
    grid_spec = pltpu.PrefetchScalarGridSpec(
        num_scalar_prefetch=3, grid=(t // tile,),
        in_specs=[pl.BlockSpec((tile * ROW_TILE, LANES), lambda i, *_: (i, 0))],
        out_specs=pl.BlockSpec(memory_space=pl.ANY),
        scratch_shapes=[pltpu.VMEM((PAD_SIZES[0] * ROW_TILE, LANES), F32),
                        pltpu.SemaphoreType.DMA(()), pltpu.SemaphoreType.DMA(())])
    return pl.pallas_call(
        kern, name="dispatch", grid_spec=grid_spec,
        out_shape=jax.ShapeDtypeStruct((n_rows * ROW_TILE, LANES), F32),
        compiler_params=_params(("arbitrary",), 32),
    )(dest_flat, pad_start, pad_len, h2)


def _expert_kernel(be_ref, nu_ref, first_ref, next_ref, slot_ref, rows_ref, wg_hbm, wu_hbm, wd_hbm, out_ref,
                   wg_f, wu_f, wd_f, wg_b, wu_b, wd_b, sems):
    i = pl.program_id(0)

    def fetch(e, slot):
        return (pltpu.make_async_copy(wg_hbm.at[e], wg_f.at[slot], sems.at[slot, 0]),
                pltpu.make_async_copy(wu_hbm.at[e], wu_f.at[slot], sems.at[slot, 1]),
                pltpu.make_async_copy(wd_hbm.at[e], wd_f.at[slot], sems.at[slot, 2]))

    @pl.when((i == 0) & (nu_ref[0] > 0))
    def _():
        for cp in fetch(be_ref[0], 0):
            cp.start()

    @pl.when((first_ref[i] == 1) & (i < nu_ref[0]))
    def _():
        slot = slot_ref[i]

        @pl.when(next_ref[i] >= 0)
        def _():
            for cp in fetch(next_ref[i], 1 - slot):
                cp.start()

        for cp in fetch(be_ref[i], slot):
            cp.wait()
        wg_b[...] = wg_f[slot].astype(BF16)
        wu_b[...] = wu_f[slot].astype(BF16)
        wd_b[...] = wd_f[slot].astype(BF16)

    @pl.when(i < nu_ref[0])
    def _():
        xb = _load_tile_rows(rows_ref).astype(BF16)
        gate = jnp.dot(xb, wg_b[...], preferred_element_type=F32)
        up = jnp.dot(xb, wu_b[...], preferred_element_type=F32)
        hid = (gate * jax.nn.sigmoid(gate) * up).astype(BF16)
        _store_tile_rows(out_ref, jnp.dot(hid, wd_b[...], preferred_element_type=F32))

    @pl.when(i >= nu_ref[0])
    def _():
        out_ref[...] = jnp.zeros(out_ref.shape, F32)


def _experts(block_e, n_used, seg_first, seg_next, seg_slot, rows, w_gate, w_up, w_down):
    nb = rows.shape[0] // (ROW_BLOCK * ROW_TILE)
    last = lambda i, be, nu, *_: jnp.maximum(jnp.minimum(i, nu[0] - 1), 0)
    hbm = pl.BlockSpec(memory_space=pl.ANY)
    grid_spec = pltpu.PrefetchScalarGridSpec(
        num_scalar_prefetch=5, grid=(nb,),
        in_specs=[pl.BlockSpec((ROW_BLOCK * ROW_TILE, LANES), lambda i, *s: (last(i, *s), 0)), hbm, hbm, hbm],
        out_specs=pl.BlockSpec((ROW_BLOCK * ROW_TILE, LANES), lambda i, *s: (i, 0)),
        scratch_shapes=[pltpu.VMEM((2, D_MODEL, EXPERT_FF), F32), pltpu.VMEM((2, D_MODEL, EXPERT_FF), F32),
                        pltpu.VMEM((2, EXPERT_FF, D_MODEL), F32),
                        pltpu.VMEM((D_MODEL, EXPERT_FF), BF16), pltpu.VMEM((D_MODEL, EXPERT_FF), BF16),
                        pltpu.VMEM((EXPERT_FF, D_MODEL), BF16), pltpu.SemaphoreType.DMA((2, 3))])
    return pl.pallas_call(
        _expert_kernel, name="experts", grid_spec=grid_spec,
        out_shape=jax.ShapeDtypeStruct(rows.shape, F32),
        compiler_params=_params(("arbitrary",), 48),
    )(block_e, n_used, seg_first, seg_next, seg_slot, rows, w_gate, w_up, w_down)


def _combine_kernel(dest_ref, x1_ref, info_ref, gfin_ref, rows_ref, o_ref, buf0, buf1, sems, *, tile, n_tok):
    i = pl.program_id(0)
    slot = i % 2

    def gather(step, to_slot):
        base = step * tile
        b0, b1, sem = buf0.at[to_slot], buf1.at[to_slot], sems.at[to_slot]

        def issue(tb, carry):
            for k in range(UNROLL):
                t = tb * UNROLL + k
                _row_copy(rows_ref, dest_ref[base + t], b0, t, sem).start(priority=0)
                _row_copy(rows_ref, dest_ref[n_tok + base + t], b1, t, sem).start(priority=1)
            return carry

        lax.fori_loop(0, tile // UNROLL, issue, 0)

    def drain(of_slot):
        b0, b1, sem = buf0.at[of_slot], buf1.at[of_slot], sems.at[of_slot]

        def wait(tb, carry):
            for _ in range(UNROLL):
                _row_copy(rows_ref, 0, b0, 0, sem).wait()
                _row_copy(rows_ref, 0, b1, 0, sem).wait()
            return carry

        lax.fori_loop(0, tile // UNROLL, wait, 0)

    @pl.when(i == 0)
    def _():
        gather(0, 0)

    drain(slot)

    last = pl.num_programs(0) - 1
    nxt_base = jnp.minimum(i + 1, last) * tile
    b0, b1 = buf0.at[slot], buf1.at[slot]
    n0, n1, nsem = buf0.at[1 - slot], buf1.at[1 - slot], sems.at[1 - slot]
    group = tile // COMBINE_GROUPS
    for g in range(COMBINE_GROUPS):
        lo = g * group
        for t in range(lo, lo + group):
            _row_copy(rows_ref, dest_ref[nxt_base + t], n0, t, nsem).start(priority=0)
            _row_copy(rows_ref, dest_ref[n_tok + nxt_base + t], n1, t, nsem).start(priority=1)
        info = info_ref[lo:lo + group, :]
        x2 = (x1_ref[lo:lo + group, :] + info[:, 2:3] * _load_tile_rows(b0, lo, group)
              + info[:, 3:4] * _load_tile_rows(b1, lo, group))
        o_ref[lo:lo + group, :] = _rms(x2, gfin_ref[...])

    @pl.when(i == last)
    def _():
        drain(1 - slot)


def _combine(dest_flat, x1, info, gfin, out_rows, tile=512):
    t = x1.shape[0]
    kern = functools.partial(_combine_kernel, tile=tile, n_tok=t)
    grid_spec = pltpu.PrefetchScalarGridSpec(
        num_scalar_prefetch=1, grid=(t // tile,),
        in_specs=[pl.BlockSpec((tile, D_MODEL), lambda i, d: (i, 0)),
                  pl.BlockSpec((tile, 8), lambda i, d: (i, 0)),
                  pl.BlockSpec((1, D_MODEL), lambda i, d: (0, 0)),
                  pl.BlockSpec(memory_space=pl.ANY)],
        out_specs=pl.BlockSpec((tile, D_MODEL), lambda i, d: (i, 0)),
        scratch_shapes=[pltpu.VMEM((2, tile * ROW_TILE, LANES), F32), pltpu.VMEM((2, tile * ROW_TILE, LANES), F32),
                        pltpu.SemaphoreType.DMA((2,))])
    return pl.pallas_call(
        kern, name="combine", grid_spec=grid_spec,
        out_shape=jax.ShapeDtypeStruct((t, D_MODEL), F32),
        compiler_params=_params(("arbitrary",), 32),
    )(dest_flat, x1, info, gfin, out_rows)


def _split_w_in(w_in):
    w4 = w_in.astype(BF16).reshape(D_MODEL, 4, ATTN_WIDTH)
    wqv_t = w4[:, 0::2, :].reshape(D_MODEL, 2 * ATTN_WIDTH).T
    wk = jnp.pad(w4[:, 1, :].reshape(D_MODEL, N_HEADS * 2, HEAD_DIM), ((0, 0), (0, 0), (0, AUG - HEAD_DIM)))
    wku = jnp.concatenate([wk.reshape(D_MODEL, N_HEADS * 2 * AUG), w4[:, 3, :]], axis=1)
    return wqv_t, wku


def kernel(x, norm_attn, w_in, lambda_q1, lambda_k1, lambda_q2, lambda_k2, attn_subln, ssm_lam_re, ssm_lam_im, ssm_log_dt, ssm_b_re, ssm_b_im, ssm_c_re, ssm_c_im, ssm_d, w_glu, b_glu, ssm_norm, w_out, norm_moe, w_router_group, b_router_group, w_router_expert, b_router_expert, w_gate, w_up, w_down, norm_final):
    batch, seq, d = x.shape
    t = batch * seq
    nchunk = seq // CHUNK
    x2 = x.reshape(t, d)
    l = 0

    q_t, k_aug, v_t, u3 = _in_proj(x2, norm_attn[l][None], *_split_w_in(w_in[l]), seq)
    attn = _attention(q_t, k_aug, v_t, lambda_q1[l][None], lambda_k1[l][None], lambda_q2[l][None],
                      lambda_k2[l][None], attn_subln[l][:, None], batch, seq)

    sc = _ssm_constants(ssm_lam_re[l], ssm_lam_im[l], ssm_log_dt[l], ssm_b_re[l], ssm_b_im[l],
                        ssm_c_re[l], ssm_c_im[l], ssm_d[l], nchunk)
    h_re, h_im = _ssm_state(u3, sc["b_re"], sc["b_im"], sc["a_row_re"], sc["a_row_im"], sc["p_re"], sc["p_im"],
                            nchunk)
    y3 = _ssm_out(u3, sc["kt"], h_re, h_im, sc["c_re"], sc["c_im"], sc["a_col_re"], sc["a_col_im"])

    w_router = jnp.concatenate([w_router_group[l], w_router_expert[l]], axis=1).astype(F32).T
    w_router = jnp.pad(w_router, ((0, LANES - w_router.shape[0]), (0, 0)))
    w_router_hi = w_router.astype(BF16)
    w_router = jnp.concatenate([w_router_hi, (w_router - w_router_hi.astype(F32)).astype(BF16)], axis=0)
    b_router = jnp.concatenate([b_router_group[l], b_router_expert[l]]).astype(F32)
    b_router = jnp.pad(b_router, (0, ROUTER_ROWS - b_router.shape[0]))[:, None]
    x1, h2, info, route_t, cnt = _mix(x2, attn, y3, w_glu[l].astype(BF16), b_glu[l][None], ssm_norm[l][None],
                                      w_out[l][:ATTN_WIDTH].astype(BF16), w_out[l][ATTN_WIDTH:].astype(BF16),
                                      norm_moe[l][None], w_router, b_router)

    experts = route_t[0:2].astype(jnp.int32)
    ranks = route_t[4:6].astype(jnp.int32)
    counts = cnt[N_EXPERT_GROUPS:N_EXPERT_GROUPS + N_EXPERTS, 0].astype(jnp.int32)
    padded = ((counts + ROW_BLOCK - 1) // ROW_BLOCK) * ROW_BLOCK
    ids = jnp.arange(N_EXPERTS, dtype=jnp.int32)
    pend = jnp.sum(jnp.where(ids[None, :] <= ids[:, None], padded[None, :], 0), axis=1)
    pstart = pend - padded
    dest = ranks + jnp.sum(jnp.where(experts[:, None, :] == ids[None, :, None], pstart[None, :, None], 0), axis=1)
    dest = dest.reshape(-1)
    n_rows = ((2 * t + N_EXPERTS * (ROW_BLOCK - 1) + ROW_BLOCK - 1) // ROW_BLOCK) * ROW_BLOCK
    nb = n_rows // ROW_BLOCK
    n_used = (pend[-1] // ROW_BLOCK).astype(jnp.int32)
    blk = jnp.minimum(jnp.arange(nb, dtype=jnp.int32), n_used - 1) * ROW_BLOCK
    block_e = jnp.minimum(jnp.sum((pend[None, :] <= blk[:, None]).astype(jnp.int32), axis=1), N_EXPERTS - 1)

    bidx = jnp.arange(nb, dtype=jnp.int32)
    seg_first = ((bidx == 0) | (block_e != jnp.concatenate([block_e[:1], block_e[:-1]]))).astype(jnp.int32)
    seg_slot = (jnp.sum(jnp.where(bidx[None, :] <= bidx[:, None], seg_first[None, :], 0), axis=1) - 1) & 1
    seg_end = jnp.sum(jnp.where(block_e[:, None] == ids[None, :], pend[None, :], 0), axis=1) // ROW_BLOCK
    after = jnp.sum(jnp.where(bidx[None, :] == seg_end[:, None], block_e[None, :], 0), axis=1)
    seg_next = jnp.where(seg_end < n_used, after, -1).astype(jnp.int32)

    rows = _dispatch(dest, pstart + counts, padded - counts, h2, n_rows)
    out_rows = _experts(block_e, n_used[None], seg_first, seg_next, seg_slot.astype(jnp.int32), rows,
                        w_gate[l], w_up[l], w_down[l])
    out = _combine(dest, x1, info, norm_final[None], out_rows)
    return out.reshape(batch, seq, d)
```

```python
import functools
import math

import jax
import jax.numpy as jnp
import numpy as np
from jax import lax
from jax.experimental import pallas as pl
from jax.experimental.pallas import tpu as pltpu

F32 = jnp.float32
BF16 = jnp.bfloat16

D_MODEL = 1024
N_HEADS = 4
HEAD_DIM = 64
VALUE_DIM = 128
ATTN_WIDTH = 512
SSM_WIDTH = 512
SSM_GROUP = 16
N_GROUPS = 32
SSM_STATE = 64
N_EXPERT_GROUPS = 4
EXPERTS_PER_GROUP = 8
N_EXPERTS = 32
EXPERT_FF = 512
RMS_EPS = 1e-6
LAMBDA_INIT = 0.8 - 0.6 * math.exp(-0.3 * 0)
LOG2E = math.log2(math.e)

CHUNK = 16
HALF_GROUPS = 16
TOEP_SPAN = 4
ROUTER_ROWS = 48
LANES = 128
AUG = 128
NORM_LANE = 70
QNORM_ROW = 71
V_ROWS = VALUE_DIM + 16
UNDERFLOW_LOG2 = 152.0
ROW_BLOCK = 256
ROW_TILE = D_MODEL // 128
VMEM_LIMIT_CAP = 56 * 1024 * 1024


def _params(dims, vmem_mb):
    return pltpu.CompilerParams(dimension_semantics=dims,
                                vmem_limit_bytes=min(vmem_mb * 1024 * 1024, VMEM_LIMIT_CAP))


def _rms(x, gain):
    return x * lax.rsqrt(jnp.mean(x * x, axis=-1, keepdims=True) + RMS_EPS) * gain


def _store_tile_rows(ref, val):
    n = val.shape[0]
    for c in range(ROW_TILE):
        ref[pl.ds(c, n, stride=ROW_TILE), :] = val[:, c * LANES:(c + 1) * LANES]


def _load_tile_rows(ref, first=0, n=None):
    n = ref.shape[0] // ROW_TILE if n is None else n
    return jnp.concatenate([ref[pl.ds(first * ROW_TILE + c, n, stride=ROW_TILE), :] for c in range(ROW_TILE)],
                           axis=-1)


def _split3(val):
    hi = val.astype(BF16).astype(F32)
    r1 = val - hi
    mid = r1.astype(BF16).astype(F32)
    lo = r1 - mid
    return hi, mid, lo


def _inproj_kernel(x_ref, g_ref, wqv_ref, wku_ref, q_ref, k_ref, v_ref, u_ref, ubuf, *, tile, seq):
    i = pl.program_id(0)
    h = _rms(x_ref[...], g_ref[...]).astype(BF16)
    nt = (((1,), (1,)), ((), ()))
    kcols = 2 * N_HEADS * AUG
    qt = lax.dot_general(wqv_ref[:ATTN_WIDTH, :], h, nt, preferred_element_type=F32)
    kp = jnp.dot(h, wku_ref[:, :kcols], preferred_element_type=F32)
    vt = lax.dot_general(wqv_ref[ATTN_WIDTH:, :], h, nt, preferred_element_type=F32)
    up = jnp.dot(h, wku_ref[:, kcols:], preferred_element_type=F32)
    ones_row = jnp.where(lax.broadcasted_iota(jnp.int32, (V_ROWS - VALUE_DIM, tile), 0) == 0, 1.0, 0.0)
    for hd in range(N_HEADS):
        v_ref[hd * V_ROWS:hd * V_ROWS + VALUE_DIM, :] = vt[hd * VALUE_DIM:(hd + 1) * VALUE_DIM, :].astype(BF16)
        v_ref[hd * V_ROWS + VALUE_DIM:(hd + 1) * V_ROWS, :] = ones_row.astype(BF16)
    pos0 = lax.rem(i * tile, seq)
    pos_k = (pos0 + lax.broadcasted_iota(jnp.int32, (tile, AUG), 0)).astype(F32)
    pos_q = (pos0 + lax.broadcasted_iota(jnp.int32, (1, tile), 1)).astype(F32)
    lane = lax.broadcasted_iota(jnp.int32, (tile, AUG), 1)
    srow = lax.broadcasted_iota(jnp.int32, (AUG - HEAD_DIM, tile), 0)
    qscale = HEAD_DIM ** -0.5 * LOG2E
    for hd in range(N_HEADS):
        slope = 2.0 ** (-8.0 * (hd + 1) / N_HEADS) * LOG2E
        hi, mid, lo = _split3(pos_k * slope)
        k_add = jnp.where(lane == 64, hi,
                          jnp.where(lane == 65, mid,
                                    jnp.where(lane == 66, lo, jnp.where((lane >= 67) & (lane < 70), 1.0, 0.0))))
        hi, mid, lo = _split3(pos_q * slope)
        q_add = jnp.where(srow < 3, 1.0,
                          jnp.where(srow == 3, -hi, jnp.where(srow == 4, -mid, jnp.where(srow == 5, -lo, 0.0))))
        for m in range(2):
            c0 = (hd * 2 + m) * AUG
            qb = (qt[c0 // 2:c0 // 2 + HEAD_DIM, :] * qscale).astype(BF16)
            q_ref[c0:c0 + HEAD_DIM, :] = qb
            qr = qb.astype(F32)
            qnorm2 = jnp.sum(qr * qr, axis=0, keepdims=True) * (1.0 + 2.0 ** -6)
            q_ref[c0 + HEAD_DIM:c0 + AUG, :] = (
                q_add + jnp.where(srow == QNORM_ROW - HEAD_DIM, qnorm2, 0.0)).astype(BF16)
            kb = kp[:, c0:c0 + AUG]
            kr = kb.astype(BF16).astype(F32)
            norm2 = jnp.sum(kr * kr, axis=-1, keepdims=True) * (1.0 + 2.0 ** -6)
            k_ref[:, c0:c0 + AUG] = (kb + k_add + jnp.where(lane == NORM_LANE, norm2, 0.0)).astype(BF16)
    for cb in range(SSM_WIDTH // LANES):
        ubuf[cb] = up[:, cb * LANES:(cb + 1) * LANES]
    per_half = HALF_GROUPS * SSM_GROUP // LANES
    for s in range(CHUNK):
        for cb in range(SSM_WIDTH // LANES):
            c0 = ((cb // per_half) * CHUNK + s) * HALF_GROUPS * SSM_GROUP + (cb % per_half) * LANES
            u_ref[:, c0:c0 + LANES] = ubuf[cb, pl.ds(s, tile // CHUNK, stride=CHUNK), :].astype(BF16)


def _in_proj(x2, gain, wqv_t, wku, seq, tile=512):
    t = x2.shape[0]
    qk = 2 * N_HEADS * AUG
    kern = functools.partial(_inproj_kernel, tile=tile, seq=seq)
    full = lambda shape: pl.BlockSpec(shape, lambda i: (0, 0))
    return pl.pallas_call(
        kern, name="in_proj",
        grid=(t // tile,),
        in_specs=[pl.BlockSpec((tile, D_MODEL), lambda i: (i, 0)),
                  full((1, D_MODEL)), full((2 * ATTN_WIDTH, D_MODEL)), full((D_MODEL, qk + SSM_WIDTH))],
        out_specs=[pl.BlockSpec((qk, tile), lambda i: (0, i)),
                   pl.BlockSpec((tile, qk), lambda i: (i, 0)),
                   pl.BlockSpec((N_HEADS * V_ROWS, tile), lambda i: (0, i)),
                   pl.BlockSpec((tile // CHUNK, CHUNK * SSM_WIDTH), lambda i: (i, 0))],
        out_shape=[jax.ShapeDtypeStruct((qk, t), BF16),
                   jax.ShapeDtypeStruct((t, qk), BF16),
                   jax.ShapeDtypeStruct((N_HEADS * V_ROWS, t), BF16),
                   jax.ShapeDtypeStruct((t // CHUNK, CHUNK * SSM_WIDTH), BF16)],
        scratch_shapes=[pltpu.VMEM((SSM_WIDTH // LANES, tile, LANES), F32)],
        compiler_params=_params(("arbitrary",), 48),
    )(x2, gain, wqv_t, wku)


def _attn_kernel(lq1, lk1, lq2, lk2, sub_ref, q_ref, k_ref, v_ref, o_ref, m_sc, acc_sc, kn_sc, *, tq, kb, strip):
    hd = pl.program_id(1)
    qi = pl.program_id(2)
    lam = (jnp.exp(jnp.sum(lq1[...] * lk1[...], axis=-1, keepdims=True))
           - jnp.exp(jnp.sum(lq2[...] * lk2[...], axis=-1, keepdims=True)) + LAMBDA_INIT)

    @pl.when(qi == 0)
    def _():
        for m in range(2):
            kn_sc[m] = jnp.max(k_ref[:, m * AUG:(m + 1) * AUG].astype(F32), axis=0, keepdims=True)

    def score(start, nkeys, align, q_lo=0, nq=tq):
        r0 = pl.multiple_of(start, align)
        kblk = k_ref[pl.ds(r0, nkeys), :]
        return [jnp.dot(kblk[:, m * AUG:(m + 1) * AUG], q_ref[m * AUG:(m + 1) * AUG, q_lo:q_lo + nq],
                        preferred_element_type=F32) for m in range(2)]

    def step(start, nkeys, align, q_lo=0, nq=tq, key_lo=None, first=False, scores=None):
        if scores is None:
            scores = score(start, nkeys, align, q_lo, nq)
        vblk = v_ref[:, pl.ds(pl.multiple_of(start, align), nkeys)]
        for m in range(2):
            s = scores[m]
            if key_lo is not None:
                key = lax.broadcasted_iota(jnp.int32, (nkeys, nq), 0) + key_lo
                qry = lax.broadcasted_iota(jnp.int32, (nkeys, nq), 1) + q_lo
                s = jnp.where(key <= qry, s, -jnp.inf)
            if first:
                m_new = jnp.max(s, axis=0, keepdims=True)
                pv = jnp.dot(vblk, jnp.exp2(s - m_new).astype(BF16), preferred_element_type=F32)
            else:
                m_prev = m_sc[m, :, q_lo:q_lo + nq]
                m_new = jnp.maximum(m_prev, jnp.max(s, axis=0, keepdims=True))
                pv = (jnp.exp2(m_prev - m_new) * acc_sc[m, :, q_lo:q_lo + nq]
                      + jnp.dot(vblk, jnp.exp2(s - m_new).astype(BF16), preferred_element_type=F32))
            acc_sc[m, :, q_lo:q_lo + nq] = pv
            m_sc[m, :, q_lo:q_lo + nq] = m_new

    for lo in range(0, tq, strip):
        step(qi * tq + lo, strip, strip, q_lo=lo, nq=tq - lo, key_lo=lo, first=lo == 0)

    lane = lax.broadcasted_iota(jnp.int32, (1, AUG), 1)
    sub = lax.broadcasted_iota(jnp.int32, (16, 1), 0)
    slope = LOG2E * jnp.exp2(-2.0 * (jnp.full((1, 1), hd, jnp.int32) + 1).astype(F32))
    reach = jnp.zeros((1, 1), F32)
    for m in range(2):
        qrows = jnp.max(q_ref[m * AUG + HEAD_DIM:m * AUG + HEAD_DIM + 16, :].astype(F32), axis=-1, keepdims=True)
        q2 = jnp.max(jnp.where(sub == QNORM_ROW - HEAD_DIM, qrows, 0.0), axis=0, keepdims=True)
        k2 = jnp.max(jnp.where(lane == NORM_LANE, kn_sc[m], 0.0), axis=-1, keepdims=True)
        m_min = jnp.min(m_sc[m], axis=-1, keepdims=True)
        reach = jnp.maximum(reach, (jnp.sqrt(q2 * k2) * 1.001 + (UNDERFLOW_LOG2 + 0.5) - m_min) / slope)
    need = jnp.ceil((reach + (kb - 1)) / kb).astype(jnp.int32) - 1
    n_below = jnp.minimum(jnp.max(jnp.maximum(need, 0)), qi * (tq // kb))
    top = qi * tq

    def quad(i, carry):
        a, b = top - (2 + 4 * i) * kb, top - (4 + 4 * i) * kb
        sa, sb = score(a, 2 * kb, kb), score(b, 2 * kb, kb)
        step(a, 2 * kb, kb, scores=sa)
        step(b, 2 * kb, kb, scores=sb)
        return carry

    lax.fori_loop(0, n_below // 4, quad, 0)
    rest = n_below % 4
    done = n_below - rest

    @pl.when(rest >= 2)
    def _():
        step(top - (done + 2) * kb, 2 * kb, kb)

    @pl.when(rest % 2 == 1)
    def _():
        step(top - n_below * kb, kb, kb)

    l0 = acc_sc[0, VALUE_DIM:VALUE_DIM + 1, :]
    l1 = acc_sc[1, VALUE_DIM:VALUE_DIM + 1, :]
    o = acc_sc[0, :VALUE_DIM, :] / l0 - lam * (acc_sc[1, :VALUE_DIM, :] / l1)
    o = o * lax.rsqrt(jnp.mean(o * o, axis=0, keepdims=True) + RMS_EPS) * sub_ref[...] * (1.0 - LAMBDA_INIT)
    o_ref[...] = o.T.astype(BF16)


def _attention(q_t, k_aug, v_t, lq1, lk1, lq2, lk2, subln_col, batch, seq, tq=1024, kb=512, strip=512):
    tq = min(tq, seq)
    nq = seq // tq
    t = batch * seq
    small = pl.BlockSpec((1, HEAD_DIM), lambda b, h, i: (0, 0))
    kern = functools.partial(_attn_kernel, tq=tq, kb=kb, strip=strip)
    return pl.pallas_call(
        kern, name="attention",
        grid=(batch, N_HEADS, nq),
        in_specs=[small, small, small, small,
                  pl.BlockSpec((VALUE_DIM, 1), lambda b, h, i: (0, 0)),
                  pl.BlockSpec((2 * AUG, tq), lambda b, h, i: (h, b * nq + i)),
                  pl.BlockSpec((seq, 2 * AUG), lambda b, h, i: (b, h)),
                  pl.BlockSpec((V_ROWS, seq), lambda b, h, i: (h, b))],
        out_specs=pl.BlockSpec((tq, VALUE_DIM), lambda b, h, i: (b * nq + i, h)),
        out_shape=jax.ShapeDtypeStruct((t, ATTN_WIDTH), BF16),
        scratch_shapes=[pltpu.VMEM((2, 1, tq), F32), pltpu.VMEM((2, V_ROWS, tq), F32),
                        pltpu.VMEM((2, 1, AUG), F32)],
        compiler_params=_params(("arbitrary", "arbitrary", "arbitrary"), 48),
    )(lq1, lk1, lq2, lk2, subln_col, q_t, k_aug, v_t)


SCAN_COLS = 256


def _ssm_state_kernel(u_ref, bre_ref, bim_ref, are_ref, aim_ref, pre_ref, pim_ref, cre_ref, cim_ref, d_ref,
                      hre_ref, him_ref, kt_ref, wre, wim, sre, sim, *, nchunk, nsteps):
    i = pl.program_id(1)

    @pl.when(i == 0)
    def _():
        wre[...] = bre_ref[0]
        wim[...] = bim_ref[0]
        sre[...] = jnp.zeros(sre.shape, F32)
        sim[...] = jnp.zeros(sim.shape, F32)

    @pl.when(i > 0)
    def _():
        ar, ai = are_ref[0], aim_ref[0]
        wr, wi = wre[...], wim[...]
        wre[...] = wr * ar - wi * ai
        wim[...] = wr * ai + wi * ar

    u = u_ref[...]
    wr_b, wi_b = wre[...].astype(BF16), wim[...].astype(BF16)
    sre[...] += jnp.dot(u, wr_b, preferred_element_type=F32)
    sim[...] += jnp.dot(u, wi_b, preferred_element_type=F32)

    lag = (jnp.dot(wr_b, cre_ref[0].astype(BF16), preferred_element_type=F32)
           - jnp.dot(wi_b, cim_ref[0].astype(BF16), preferred_element_type=F32))
    r = lax.broadcasted_iota(jnp.int32, lag.shape, 0)
    c = lax.broadcasted_iota(jnp.int32, lag.shape, 1)
    skip = jnp.where((r == c) & (i == 0), d_ref[0], 0.0)
    kt_ref[0, 0] = (lag + skip).astype(BF16)

    @pl.when(i == CHUNK - 1)
    def _():
        row = lax.rem(lax.broadcasted_iota(jnp.int32, (sre.shape[0], SCAN_COLS), 0), nchunk)
        for c0 in range(0, sre.shape[1], SCAN_COLS):
            cols = slice(c0, c0 + SCAN_COLS)
            hr, hi = sre[:, cols], sim[:, cols]
            for k in range(nsteps):
                d = 1 << k
                ar, ai = pre_ref[0, k:k + 1, cols], pim_ref[0, k:k + 1, cols]
                keep = row >= d
                pr = jnp.where(keep, pltpu.roll(hr, d, 0), 0.0)
                pi = jnp.where(keep, pltpu.roll(hi, d, 0), 0.0)
                hr, hi = hr + ar * pr - ai * pi, hi + ar * pi + ai * pr
            keep = row >= 1
            hre_ref[:, cols] = jnp.where(keep, pltpu.roll(hr, 1, 0), 0.0).astype(BF16)
            him_ref[:, cols] = jnp.where(keep, pltpu.roll(hi, 1, 0), 0.0).astype(BF16)


def _ssm_state(u3, b_re, b_im, a_row_re, a_row_im, p_re, p_im, c_re, c_im, d_row, nchunk):
    nch = u3.shape[0]
    nsteps = p_re.shape[1]
    hw = HALF_GROUPS * SSM_GROUP
    sw = HALF_GROUPS * SSM_STATE
    kern = functools.partial(_ssm_state_kernel, nchunk=nchunk, nsteps=nsteps)
    per_half = lambda shape: pl.BlockSpec((1,) + shape, lambda hf, i: (hf, 0, 0))
    return pl.pallas_call(
        kern, name="ssm_state",
        grid=(2, CHUNK),
        in_specs=[pl.BlockSpec((nch, hw), lambda hf, i: (0, hf * CHUNK + CHUNK - 1 - i)),
                  per_half((hw, sw)), per_half((hw, sw)), per_half((1, sw)), per_half((1, sw)),
                  per_half((nsteps, sw)), per_half((nsteps, sw)),
                  per_half((sw, hw)), per_half((sw, hw)), per_half((1, hw))],
        out_specs=[pl.BlockSpec((nch, sw), lambda hf, i: (0, hf)),
                   pl.BlockSpec((nch, sw), lambda hf, i: (0, hf)),
                   pl.BlockSpec((1, 1, hw, hw), lambda hf, i: (i, hf, 0, 0))],
        out_shape=[jax.ShapeDtypeStruct((nch, 2 * sw), BF16)] * 2
        + [jax.ShapeDtypeStruct((CHUNK, 2, hw, hw), BF16)],
        scratch_shapes=[pltpu.VMEM((hw, sw), F32), pltpu.VMEM((hw, sw), F32),
                        pltpu.VMEM((nch, sw), F32), pltpu.VMEM((nch, sw), F32)],
        compiler_params=_params(("arbitrary", "arbitrary"), 48),
    )(u3, b_re, b_im, a_row_re, a_row_im, p_re, p_im, c_re, c_im, d_row)


def _ssm_out_kernel(u_ref, kt_ref, hre_ref, him_ref, cre_ref, cim_ref, are_ref, aim_ref, y_ref, wre, wim, toep):
    t = pl.program_id(1)
    ar, ai = are_ref[0], aim_ref[0]

    @pl.when(t == 0)
    def _():
        cr, ci = cre_ref[0], cim_ref[0]
        wre[...] = cr * ar - ci * ai
        wim[...] = cr * ai + ci * ar

    @pl.when(t > 0)
    def _():
        wr, wi = wre[...], wim[...]
        wre[...] = wr * ar - wi * ai
        wim[...] = wr * ai + wi * ar

    hw = kt_ref.shape[-1]
    for s in range(CHUNK):
        tile = kt_ref[jnp.maximum(t - s, 0), 0]
        toep[s * hw:(s + 1) * hw, :] = jnp.where(s <= t, tile, jnp.zeros_like(tile))

    span = TOEP_SPAN * hw
    y_ref[0] = (jnp.dot(u_ref[:, :span], toep[:span, :], preferred_element_type=F32)
                + jnp.dot(hre_ref[...], wre[...].astype(BF16), preferred_element_type=F32)
                - jnp.dot(him_ref[...], wim[...].astype(BF16), preferred_element_type=F32))
    for piece in range(1, CHUNK // TOEP_SPAN):
        @pl.when(t >= piece * TOEP_SPAN)
        def _():
            lo = piece * span
            y_ref[0] += jnp.dot(u_ref[:, lo:lo + span], toep[lo:lo + span, :], preferred_element_type=F32)


def _ssm_out(u3, kt, h_re, h_im, c_re, c_im, a_col_re, a_col_im):
    nch = u3.shape[0]
    hw = HALF_GROUPS * SSM_GROUP
    sw = HALF_GROUPS * SSM_STATE
    return pl.pallas_call(
        _ssm_out_kernel, name="ssm_out",
        grid=(2, CHUNK),
        in_specs=[pl.BlockSpec((nch, CHUNK * hw), lambda hf, t: (0, hf)),
                  pl.BlockSpec((CHUNK, 1, hw, hw), lambda hf, t: (0, hf, 0, 0)),
                  pl.BlockSpec((nch, sw), lambda hf, t: (0, hf)),
                  pl.BlockSpec((nch, sw), lambda hf, t: (0, hf)),
                  pl.BlockSpec((1, sw, hw), lambda hf, t: (hf, 0, 0)),
                  pl.BlockSpec((1, sw, hw), lambda hf, t: (hf, 0, 0)),
                  pl.BlockSpec((1, sw, 1), lambda hf, t: (hf, 0, 0)),
                  pl.BlockSpec((1, sw, 1), lambda hf, t: (hf, 0, 0))],
        out_specs=pl.BlockSpec((1, nch, hw), lambda hf, t: (t, 0, hf)),
        out_shape=jax.ShapeDtypeStruct((CHUNK, nch, SSM_WIDTH), F32),
        scratch_shapes=[pltpu.VMEM((sw, hw), F32), pltpu.VMEM((sw, hw), F32), pltpu.VMEM((CHUNK * hw, hw), BF16)],
        compiler_params=_params(("arbitrary", "arbitrary"), 48),
    )(u3, kt, h_re, h_im, c_re, c_im, a_col_re, a_col_im)


def _ssm_constants(lam_re, lam_im, log_dt, b_re, b_im, c_re, c_im, d_skip, nchunk):
    lr, li = lam_re.astype(F32), lam_im.astype(F32)
    dt = jnp.exp(log_dt.astype(F32))[:, None]

    def lam_bar_pow(k):
        mag = jnp.exp(k * lr * dt)
        return mag * jnp.cos(k * li * dt), mag * jnp.sin(k * li * dt)

    a_re, a_im = lam_bar_pow(1.0)
    den = lr * lr + li * li
    coef_re = ((a_re - 1.0) * lr + a_im * li) / den
    coef_im = (a_im * lr - (a_re - 1.0) * li) / den
    bb_re = coef_re[..., None] * b_re.astype(F32) - coef_im[..., None] * b_im.astype(F32)
    bb_im = coef_re[..., None] * b_im.astype(F32) + coef_im[..., None] * b_re.astype(F32)
    cc_re, cc_im = c_re.astype(F32), c_im.astype(F32)
    hw, sw = HALF_GROUPS * SSM_GROUP, HALF_GROUPS * SSM_STATE

    def block_diag(rows, row_group, col_group):
        wide = jnp.tile(rows, (1,) * (rows.ndim - 1) + (HALF_GROUPS,))
        r = lax.broadcasted_iota(jnp.int32, wide.shape, wide.ndim - 2) // row_group
        c = lax.broadcasted_iota(jnp.int32, wide.shape, wide.ndim - 1) // col_group
        return jnp.where(r == c, wide, 0.0)

    def b_tiles(part):
        p = part.reshape(2, HALF_GROUPS, SSM_STATE, SSM_GROUP).transpose(0, 1, 3, 2)
        return block_diag(p.reshape(2, hw, SSM_STATE), SSM_GROUP, SSM_STATE)

    def c_tiles(part):
        p = part.reshape(2, HALF_GROUPS, SSM_GROUP, SSM_STATE).transpose(0, 1, 3, 2)
        return block_diag(p.reshape(2, sw, SSM_GROUP), SSM_STATE, SSM_GROUP)

    nsteps = max(int(math.log2(nchunk)), 1)
    steps = (CHUNK * 2.0 ** jnp.arange(nsteps, dtype=F32))[:, None, None]
    st_re, st_im = lam_bar_pow(steps)
    by_half = lambda p: p.reshape(nsteps, 2, sw).transpose(1, 0, 2)
    ar_h, ai_h = a_re.reshape(2, sw), a_im.reshape(2, sw)
    return dict(d_row=d_skip.astype(F32).reshape(2, 1, hw), b_re=b_tiles(bb_re), b_im=b_tiles(bb_im),
                c_re=c_tiles(cc_re), c_im=c_tiles(cc_im),
                a_row_re=ar_h[:, None, :], a_row_im=ai_h[:, None, :],
                a_col_re=ar_h[:, :, None], a_col_im=ai_h[:, :, None],
                p_re=by_half(st_re), p_im=by_half(st_im))


def _mix_kernel(x_ref, attn_ref, y3_ref, wglu_ref, bglu_ref, gssm_ref, woa_ref, wos_ref, gmoe_ref,
                wr_ref, br_ref, x1_ref, h2_ref, info_ref, rt_ref, cnt_ref, ybuf, carry, *, tile):
    i = pl.program_id(0)

    @pl.when(i == 0)
    def _():
        carry[...] = jnp.zeros(carry.shape, F32)

    for s in range(CHUNK):
        for cb in range(SSM_WIDTH // LANES):
            ybuf[cb, pl.ds(s, tile // CHUNK, stride=CHUNK), :] = y3_ref[s, :, cb * LANES:(cb + 1) * LANES]
    y = jax.nn.gelu(jnp.concatenate([ybuf[cb] for cb in range(SSM_WIDTH // LANES)], axis=-1))
    z = jnp.dot(y.astype(BF16), wglu_ref[...], preferred_element_type=F32) + bglu_ref[...]
    y = y * jax.nn.sigmoid(z)
    ssm = _rms(y, gssm_ref[...])
    x1 = (x_ref[...] + jnp.dot(attn_ref[...], woa_ref[...], preferred_element_type=F32)
          + jnp.dot(ssm.astype(BF16), wos_ref[...], preferred_element_type=F32))
    x1_ref[...] = x1
    h2 = _rms(x1, gmoe_ref[...])
    _store_tile_rows(h2_ref, h2)

    nt = (((1,), (1,)), ((), ()))
    h_hi = h2.astype(BF16)
    h_lo = (h2 - h_hi.astype(F32)).astype(BF16)
    both = lax.dot_general(wr_ref[...], h_hi, nt, preferred_element_type=F32)
    logits = (both[:ROUTER_ROWS] + both[LANES:LANES + ROUTER_ROWS]
              + lax.dot_general(wr_ref[:ROUTER_ROWS, :], h_lo, nt, preferred_element_type=F32) + br_ref[...])
    row = lax.broadcasted_iota(jnp.int32, logits.shape, 0)
    neg = -jnp.inf
    gl = jnp.where(row < N_EXPERT_GROUPS, logits, neg)
    gmax = jnp.max(gl, axis=0, keepdims=True)
    gsel = jnp.min(jnp.where(gl == gmax, row, LANES), axis=0, keepdims=True)
    p_group = 1.0 / jnp.sum(jnp.exp(gl - gmax), axis=0, keepdims=True)
    erow = row - N_EXPERT_GROUPS
    in_grp = (erow >= 0) & (erow < N_EXPERTS) & ((erow >> 3) == gsel)
    el = jnp.where(in_grp, logits, neg)
    m1 = jnp.max(el, axis=0, keepdims=True)
    i1 = jnp.min(jnp.where(el == m1, row, LANES), axis=0, keepdims=True)
    den = jnp.sum(jnp.exp(el - m1), axis=0, keepdims=True)
    el2 = jnp.where(row == i1, neg, el)
    m2 = jnp.max(el2, axis=0, keepdims=True)
    i2 = jnp.min(jnp.where(el2 == m2, row, LANES), axis=0, keepdims=True)
    g0 = p_group / den
    g1 = p_group * jnp.exp(m2 - m1) / den

    hit0 = row == i1
    hit1 = row == i2
    onehot = jnp.where(hit0 | hit1, 1.0, 0.0)
    r = lax.broadcasted_iota(jnp.int32, (tile, tile), 0)
    c = lax.broadcasted_iota(jnp.int32, (tile, tile), 1)
    earlier = jnp.where(r < c, 1.0, 0.0).astype(BF16)
    before = jnp.dot(onehot.astype(BF16), earlier, preferred_element_type=F32) + carry[...]
    rank0 = jnp.sum(jnp.where(hit0, before, 0.0), axis=0, keepdims=True)
    rank1 = jnp.sum(jnp.where(hit1, before, 0.0), axis=0, keepdims=True)
    carry[...] += jnp.sum(onehot, axis=1, keepdims=True)
    cnt_ref[...] = jnp.broadcast_to(carry[...], cnt_ref.shape)
    r8 = lax.broadcasted_iota(jnp.int32, (8, tile), 0)
    route = jnp.where(r8 == 0, (i1 - N_EXPERT_GROUPS).astype(F32),
                      jnp.where(r8 == 1, (i2 - N_EXPERT_GROUPS).astype(F32),
                                jnp.where(r8 == 2, g0,
                                          jnp.where(r8 == 3, g1,
                                                    jnp.where(r8 == 4, rank0, jnp.where(r8 == 5, rank1, 0.0))))))
    rt_ref[...] = route
    info_ref[...] = route.T


def _mix(x2, attn, y3, wglu, bglu, gssm, wo_a, wo_s, gmoe, w_router, b_router, tile=512):
    t = x2.shape[0]
    kern = functools.partial(_mix_kernel, tile=tile)
    full = lambda shape: pl.BlockSpec(shape, lambda i: tuple(0 for _ in shape))
    return pl.pallas_call(
        kern, name="mix",
        grid=(t // tile,),
        in_specs=[pl.BlockSpec((tile, D_MODEL), lambda i: (i, 0)),
                  pl.BlockSpec((tile, ATTN_WIDTH), lambda i: (i, 0)),
                  pl.BlockSpec((CHUNK, tile // CHUNK, SSM_WIDTH), lambda i: (0, i, 0)),
                  full((SSM_WIDTH, SSM_WIDTH)), full((1, SSM_WIDTH)), full((1, SSM_WIDTH)),
                  full((ATTN_WIDTH, D_MODEL)), full((SSM_WIDTH, D_MODEL)), full((1, D_MODEL)),
                  full((2 * LANES, D_MODEL)), full((ROUTER_ROWS, 1))],
        out_specs=[pl.BlockSpec((tile, D_MODEL), lambda i: (i, 0)),
                   pl.BlockSpec((tile * ROW_TILE, LANES), lambda i: (i, 0)),
                   pl.BlockSpec((tile, 8), lambda i: (i, 0)),
                   pl.BlockSpec((8, tile), lambda i: (0, i)),
                   pl.BlockSpec((ROUTER_ROWS, LANES), lambda i: (0, 0))],
        out_shape=[jax.ShapeDtypeStruct((t, D_MODEL), F32),
                   jax.ShapeDtypeStruct((t * ROW_TILE, LANES), F32),
                   jax.ShapeDtypeStruct((t, 8), F32),
                   jax.ShapeDtypeStruct((8, t), F32),
                   jax.ShapeDtypeStruct((ROUTER_ROWS, LANES), F32)],
        scratch_shapes=[pltpu.VMEM((SSM_WIDTH // LANES, tile, LANES), F32), pltpu.VMEM((ROUTER_ROWS, 1), F32)],
        compiler_params=_params(("arbitrary",), 48),
    )(x2, attn, y3, wglu, bglu, gssm, wo_a, wo_s, gmoe, w_router, b_router)


def _rows_at(ref, row, n_rows=1):
    return ref.at[pl.ds(pl.multiple_of(row * ROW_TILE, ROW_TILE), n_rows * ROW_TILE), :]


def _row_copy(src, s, dst, d, sem):
    return pltpu.make_async_copy(_rows_at(src, s), _rows_at(dst, d), sem)


UNROLL = 4
COMBINE_GROUPS = 16
PAD_SIZES = tuple(1 << b for b in reversed(range(ROW_BLOCK.bit_length() - 1)))


def _dispatch_kernel(dest_ref, pad_start_ref, pad_len_ref, h_ref, rows_out, zbuf, sem, zsem, *, tile, n_tok,
                     n_blocks):
    i = pl.program_id(0)
    base = i * tile

    def zero_fill(e, start):
        off, rem = pad_start_ref[e], pad_len_ref[e]
        for size in PAD_SIZES:
            @pl.when((rem & size) != 0)
            def _():
                cp = pltpu.make_async_copy(_rows_at(zbuf, 0, size), _rows_at(rows_out, off, size), zsem)
                cp.start() if start else cp.wait()
            off = off + (rem & size)

    def zero_tail(start):
        used = (pad_start_ref[N_EXPERTS - 1] + pad_len_ref[N_EXPERTS - 1]) // ROW_BLOCK

        def blk(b, carry):
            for half in range(ROW_BLOCK // PAD_SIZES[0]):
                cp = pltpu.make_async_copy(zbuf, _rows_at(rows_out, b * ROW_BLOCK + half * PAD_SIZES[0],
                                                          PAD_SIZES[0]), zsem)
                cp.start() if start else cp.wait()
            return carry

        lax.fori_loop(used, n_blocks, blk, 0)

    @pl.when(i == 0)
    def _():
        zbuf[...] = jnp.zeros(zbuf.shape, F32)
        lax.fori_loop(0, N_EXPERTS, lambda e, c: (zero_fill(e, True), c)[1], 0)
        zero_tail(True)

    def issue(tb, carry):
        for k in range(UNROLL):
            t = tb * UNROLL + k
            for j in range(2):
                _row_copy(h_ref, t, rows_out, dest_ref[j * n_tok + base + t], sem).start(priority=j)
        return carry

    def drain(tb, carry):
        for _ in range(2 * UNROLL):
            _row_copy(h_ref, 0, rows_out, 0, sem).wait()
        return carry

    lax.fori_loop(0, tile // UNROLL, issue, 0)
    lax.fori_loop(0, tile // UNROLL, drain, 0)

    @pl.when(i == 0)
    def _():
        lax.fori_loop(0, N_EXPERTS, lambda e, c: (zero_fill(e, False), c)[1], 0)
        zero_tail(False)


def _dispatch(dest_flat, pad_start, pad_len, h2, n_rows, tile=512):
    t = h2.shape[0] // ROW_TILE
    kern = functools.partial(_dispatch_kernel, tile=tile, n_tok=t, n_blocks=n_rows // ROW_BLOCK)
    grid_spec = pltpu.PrefetchScalarGridSpec(
        num_scalar_prefetch=3, grid=(t // tile,),
        in_specs=[pl.BlockSpec((tile * ROW_TILE, LANES), lambda i, *_: (i, 0))],
        out_specs=pl.BlockSpec(memory_space=pl.ANY),
        scratch_shapes=[pltpu.VMEM((PAD_SIZES[0] * ROW_TILE, LANES), F32),
                        pltpu.SemaphoreType.DMA(()), pltpu.SemaphoreType.DMA(())])
    return pl.pallas_call(
        kern, name="dispatch", grid_spec=grid_spec,
        out_shape=jax.ShapeDtypeStruct((n_rows * ROW_TILE, LANES), F32),
        compiler_params=_params(("arbitrary",), 32),
    )(dest_flat, pad_start, pad_len, h2)


def _expert_kernel(be_ref, nu_ref, first_ref, next_ref, slot_ref, rows_ref, wg_hbm, wu_hbm, wd_hbm, out_ref,
                   wg_f, wu_f, wd_f, wg_b, wu_b, wd_b, sems):
    i = pl.program_id(0)

    def fetch(e, slot):
        return (pltpu.make_async_copy(wg_hbm.at[e], wg_f.at[slot], sems.at[slot, 0]),
                pltpu.make_async_copy(wu_hbm.at[e], wu_f.at[slot], sems.at[slot, 1]),
                pltpu.make_async_copy(wd_hbm.at[e], wd_f.at[slot], sems.at[slot, 2]))

    @pl.when((i == 0) & (nu_ref[0] > 0))
    def _():
        for cp in fetch(be_ref[0], 0):
            cp.start()

    @pl.when((first_ref[i] == 1) & (i < nu_ref[0]))
    def _():
        slot = slot_ref[i]

        @pl.when(next_ref[i] >= 0)
        def _():
            for cp in fetch(next_ref[i], 1 - slot):
                cp.start()

        for cp in fetch(be_ref[i], slot):
            cp.wait()
        wg_b[...] = wg_f[slot].astype(BF16)
        wu_b[...] = wu_f[slot].astype(BF16)
        wd_b[...] = wd_f[slot].astype(BF16)

    @pl.when(i < nu_ref[0])
    def _():
        xb = _load_tile_rows(rows_ref).astype(BF16)
        gate = jnp.dot(xb, wg_b[...], preferred_element_type=F32)
        up = jnp.dot(xb, wu_b[...], preferred_element_type=F32)
        hid = (gate * jax.nn.sigmoid(gate) * up).astype(BF16)
        _store_tile_rows(out_ref, jnp.dot(hid, wd_b[...], preferred_element_type=F32))

    @pl.when(i >= nu_ref[0])
    def _():
        out_ref[...] = jnp.zeros(out_ref.shape, F32)


def _experts(block_e, n_used, seg_first, seg_next, seg_slot, rows, w_gate, w_up, w_down):
    nb = rows.shape[0] // (ROW_BLOCK * ROW_TILE)
    last = lambda i, be, nu, *_: jnp.maximum(jnp.minimum(i, nu[0] - 1), 0)
    hbm = pl.BlockSpec(memory_space=pl.ANY)
    grid_spec = pltpu.PrefetchScalarGridSpec(
        num_scalar_prefetch=5, grid=(nb,),
        in_specs=[pl.BlockSpec((ROW_BLOCK * ROW_TILE, LANES), lambda i, *s: (last(i, *s), 0)), hbm, hbm, hbm],
        out_specs=pl.BlockSpec((ROW_BLOCK * ROW_TILE, LANES), lambda i, *s: (i, 0)),
        scratch_shapes=[pltpu.VMEM((2, D_MODEL, EXPERT_FF), F32), pltpu.VMEM((2, D_MODEL, EXPERT_FF), F32),
                        pltpu.VMEM((2, EXPERT_FF, D_MODEL), F32),
                        pltpu.VMEM((D_MODEL, EXPERT_FF), BF16), pltpu.VMEM((D_MODEL, EXPERT_FF), BF16),
                        pltpu.VMEM((EXPERT_FF, D_MODEL), BF16), pltpu.SemaphoreType.DMA((2, 3))])
    return pl.pallas_call(
        _expert_kernel, name="experts", grid_spec=grid_spec,
        out_shape=jax.ShapeDtypeStruct(rows.shape, F32),
        compiler_params=_params(("arbitrary",), 48),
    )(block_e, n_used, seg_first, seg_next, seg_slot, rows, w_gate, w_up, w_down)


def _combine_kernel(dest_ref, x1_ref, info_ref, gfin_ref, rows_ref, o_ref, buf0, buf1, sems, *, tile, n_tok):
    i = pl.program_id(0)
    slot = i % 2

    def gather(step, to_slot):
        base = step * tile
        b0, b1, sem = buf0.at[to_slot], buf1.at[to_slot], sems.at[to_slot]

        def issue(tb, carry):
            for k in range(UNROLL):
                t = tb * UNROLL + k
                _row_copy(rows_ref, dest_ref[base + t], b0, t, sem).start(priority=0)
                _row_copy(rows_ref, dest_ref[n_tok + base + t], b1, t, sem).start(priority=1)
            return carry

        lax.fori_loop(0, tile // UNROLL, issue, 0)

    def drain(of_slot):
        b0, b1, sem = buf0.at[of_slot], buf1.at[of_slot], sems.at[of_slot]

        def wait(tb, carry):
            for _ in range(UNROLL):
                _row_copy(rows_ref, 0, b0, 0, sem).wait()
                _row_copy(rows_ref, 0, b1, 0, sem).wait()
            return carry

        lax.fori_loop(0, tile // UNROLL, wait, 0)

    @pl.when(i == 0)
    def _():
        gather(0, 0)

    drain(slot)

    last = pl.num_programs(0) - 1
    nxt_base = jnp.minimum(i + 1, last) * tile
    b0, b1 = buf0.at[slot], buf1.at[slot]
    n0, n1, nsem = buf0.at[1 - slot], buf1.at[1 - slot], sems.at[1 - slot]
    group = tile // COMBINE_GROUPS
    for g in range(COMBINE_GROUPS):
        lo = g * group
        for t in range(lo, lo + group):
            _row_copy(rows_ref, dest_ref[nxt_base + t], n0, t, nsem).start(priority=0)
            _row_copy(rows_ref, dest_ref[n_tok + nxt_base + t], n1, t, nsem).start(priority=1)
        info = info_ref[lo:lo + group, :]
        x2 = (x1_ref[lo:lo + group, :] + info[:, 2:3] * _load_tile_rows(b0, lo, group)
              + info[:, 3:4] * _load_tile_rows(b1, lo, group))
        o_ref[lo:lo + group, :] = _rms(x2, gfin_ref[...])

    @pl.when(i == last)
    def _():
        drain(1 - slot)


def _combine(dest_flat, x1, info, gfin, out_rows, tile=512):
    t = x1.shape[0]
    kern = functools.partial(_combine_kernel, tile=tile, n_tok=t)
    grid_spec = pltpu.PrefetchScalarGridSpec(
        num_scalar_prefetch=1, grid=(t // tile,),
        in_specs=[pl.BlockSpec((tile, D_MODEL), lambda i, d: (i, 0)),
                  pl.BlockSpec((tile, 8), lambda i, d: (i, 0)),
                  pl.BlockSpec((1, D_MODEL), lambda i, d: (0, 0)),
                  pl.BlockSpec(memory_space=pl.ANY)],
        out_specs=pl.BlockSpec((tile, D_MODEL), lambda i, d: (i, 0)),
        scratch_shapes=[pltpu.VMEM((2, tile * ROW_TILE, LANES), F32), pltpu.VMEM((2, tile * ROW_TILE, LANES), F32),
                        pltpu.SemaphoreType.DMA((2,))])
    return pl.pallas_call(
        kern, name="combine", grid_spec=grid_spec,
        out_shape=jax.ShapeDtypeStruct((t, D_MODEL), F32),
        compiler_params=_params(("arbitrary",), 32),
    )(dest_flat, x1, info, gfin, out_rows)


def _split_w_in(w_in):
    w4 = w_in.astype(BF16).reshape(D_MODEL, 4, ATTN_WIDTH)
    wqv_t = w4[:, 0::2, :].reshape(D_MODEL, 2 * ATTN_WIDTH).T
    wk = jnp.pad(w4[:, 1, :].reshape(D_MODEL, N_HEADS * 2, HEAD_DIM), ((0, 0), (0, 0), (0, AUG - HEAD_DIM)))
    wku = jnp.concatenate([wk.reshape(D_MODEL, N_HEADS * 2 * AUG), w4[:, 3, :]], axis=1)
    return wqv_t, wku


def kernel(x, norm_attn, w_in, lambda_q1, lambda_k1, lambda_q2, lambda_k2, attn_subln, ssm_lam_re, ssm_lam_im, ssm_log_dt, ssm_b_re, ssm_b_im, ssm_c_re, ssm_c_im, ssm_d, w_glu, b_glu, ssm_norm, w_out, norm_moe, w_router_group, b_router_group, w_router_expert, b_router_expert, w_gate, w_up, w_down, norm_final):
    batch, seq, d = x.shape
    t = batch * seq
    nchunk = seq // CHUNK
    x2 = x.reshape(t, d)
    l = 0

    q_t, k_aug, v_t, u3 = _in_proj(x2, norm_attn[l][None], *_split_w_in(w_in[l]), seq)
    attn = _attention(q_t, k_aug, v_t, lambda_q1[l][None], lambda_k1[l][None], lambda_q2[l][None],
                      lambda_k2[l][None], attn_subln[l][:, None], batch, seq)

    sc = _ssm_constants(ssm_lam_re[l], ssm_lam_im[l], ssm_log_dt[l], ssm_b_re[l], ssm_b_im[l],
                        ssm_c_re[l], ssm_c_im[l], ssm_d[l], nchunk)
    h_re, h_im, kt = _ssm_state(u3, sc["b_re"], sc["b_im"], sc["a_row_re"], sc["a_row_im"], sc["p_re"], sc["p_im"],
                                sc["c_re"], sc["c_im"], sc["d_row"], nchunk)
    y3 = _ssm_out(u3, kt, h_re, h_im, sc["c_re"], sc["c_im"], sc["a_col_re"], sc["a_col_im"])

    w_router = jnp.concatenate([w_router_group[l], w_router_expert[l]], axis=1).astype(F32).T
    w_router = jnp.pad(w_router, ((0, LANES - w_router.shape[0]), (0, 0)))
    w_router_hi = w_router.astype(BF16)
    w_router = jnp.concatenate([w_router_hi, (w_router - w_router_hi.astype(F32)).astype(BF16)], axis=0)
    b_router = jnp.concatenate([b_router_group[l], b_router_expert[l]]).astype(F32)
    b_router = jnp.pad(b_router, (0, ROUTER_ROWS - b_router.shape[0]))[:, None]
    x1, h2, info, route_t, cnt = _mix(x2, attn, y3, w_glu[l].astype(BF16), b_glu[l][None], ssm_norm[l][None],
                                      w_out[l][:ATTN_WIDTH].astype(BF16), w_out[l][ATTN_WIDTH:].astype(BF16),
                                      norm_moe[l][None], w_router, b_router)

    experts = route_t[0:2].astype(jnp.int32)
    ranks = route_t[4:6].astype(jnp.int32)
    counts = cnt[N_EXPERT_GROUPS:N_EXPERT_GROUPS + N_EXPERTS, 0].astype(jnp.int32)
    padded = ((counts + ROW_BLOCK - 1) // ROW_BLOCK) * ROW_BLOCK
    ids = jnp.arange(N_EXPERTS, dtype=jnp.int32)
    pend = jnp.sum(jnp.where(ids[None, :] <= ids[:, None], padded[None, :], 0), axis=1)
    pstart = pend - padded
    dest = ranks + jnp.sum(jnp.where(experts[:, None, :] == ids[None, :, None], pstart[None, :, None], 0), axis=1)
    dest = dest.reshape(-1)
    n_rows = ((2 * t + N_EXPERTS * (ROW_BLOCK - 1) + ROW_BLOCK - 1) // ROW_BLOCK) * ROW_BLOCK
    nb = n_rows // ROW_BLOCK
    n_used = (pend[-1] // ROW_BLOCK).astype(jnp.int32)
    blk = jnp.minimum(jnp.arange(nb, dtype=jnp.int32), n_used - 1) * ROW_BLOCK
    block_e = jnp.minimum(jnp.sum((pend[None, :] <= blk[:, None]).astype(jnp.int32), axis=1), N_EXPERTS - 1)

    bidx = jnp.arange(nb, dtype=jnp.int32)
    seg_first = ((bidx == 0) | (block_e != jnp.concatenate([block_e[:1], block_e[:-1]]))).astype(jnp.int32)
    seg_slot = (jnp.sum(jnp.where(bidx[None, :] <= bidx[:, None], seg_first[None, :], 0), axis=1) - 1) & 1
    seg_end = jnp.sum(jnp.where(block_e[:, None] == ids[None, :], pend[None, :], 0), axis=1) // ROW_BLOCK
    after = jnp.sum(jnp.where(bidx[None, :] == seg_end[:, None], block_e[None, :], 0), axis=1)
    seg_next = jnp.where(seg_end < n_used, after, -1).astype(jnp.int32)

    rows = _dispatch(dest, pstart + counts, padded - counts, h2, n_rows)
    out_rows = _experts(block_e, n_used[None], seg_first, seg_next, seg_slot.astype(jnp.int32), rows,
                        w_gate[l], w_up[l], w_down[l])
    out = _combine(dest, x1, info, norm_final[None], out_rows)
    return out.reshape(batch, seq, d)
```

```python
import functools
import math

import jax
import jax.numpy as jnp
import numpy as np
from jax import lax
from jax.experimental import pallas as pl
from jax.experimental.pallas import tpu as pltpu

F32 = jnp.float32
BF16 = jnp.bfloat16

D_MODEL = 1024
N_HEADS = 4
HEAD_DIM = 64
VALUE_DIM = 128
ATTN_WIDTH = 512
SSM_WIDTH = 512
SSM_GROUP = 16
N_GROUPS = 32
SSM_STATE = 64
N_EXPERT_GROUPS = 4
EXPERTS_PER_GROUP = 8
N_EXPERTS = 32
EXPERT_FF = 512
RMS_EPS = 1e-6
LAMBDA_INIT = 0.8 - 0.6 * math.exp(-0.3 * 0)
LOG2E = math.log2(math.e)

CHUNK = 16
HALF_GROUPS = 16
TOEP_SPAN = 4
ROUTER_ROWS = 48
LANES = 128
AUG = 128
NORM_LANE = 70
QNORM_ROW = 71
V_ROWS = VALUE_DIM + 16
UNDERFLOW_LOG2 = 152.0
ROW_BLOCK = 256
ROW_TILE = D_MODEL // 128
VMEM_LIMIT_CAP = 56 * 1024 * 1024


def _params(dims, vmem_mb):
    return pltpu.CompilerParams(dimension_semantics=dims,
                                vmem_limit_bytes=min(vmem_mb * 1024 * 1024, VMEM_LIMIT_CAP))


def _rms(x, gain):
    return x * lax.rsqrt(jnp.mean(x * x, axis=-1, keepdims=True) + RMS_EPS) * gain


def _store_tile_rows(ref, val):
    n = val.shape[0]
    for c in range(ROW_TILE):
        ref[pl.ds(c, n, stride=ROW_TILE), :] = val[:, c * LANES:(c + 1) * LANES]


def _load_tile_rows(ref, first=0, n=None):
    n = ref.shape[0] // ROW_TILE if n is None else n
    return jnp.concatenate([ref[pl.ds(first * ROW_TILE + c, n, stride=ROW_TILE), :] for c in range(ROW_TILE)],
                           axis=-1)


def _split3(val):
    hi = val.astype(BF16).astype(F32)
    r1 = val - hi
    mid = r1.astype(BF16).astype(F32)
    lo = r1 - mid
    return hi, mid, lo


def _inproj_kernel(x_ref, g_ref, wqv_ref, wku_ref, q_ref, k_ref, v_ref, u_ref, ubuf, *, tile, seq):
    i = pl.program_id(0)
    h = _rms(x_ref[...], g_ref[...]).astype(BF16)
    nt = (((1,), (1,)), ((), ()))
    kcols = 2 * N_HEADS * AUG
    qt = lax.dot_general(wqv_ref[:ATTN_WIDTH, :], h, nt, preferred_element_type=F32)
    kp = jnp.dot(h, wku_ref[:, :kcols], preferred_element_type=F32)
    vt = lax.dot_general(wqv_ref[ATTN_WIDTH:, :], h, nt, preferred_element_type=F32)
    up = jnp.dot(h, wku_ref[:, kcols:], preferred_element_type=F32)
    ones_row = jnp.where(lax.broadcasted_iota(jnp.int32, (V_ROWS - VALUE_DIM, tile), 0) == 0, 1.0, 0.0)
    for hd in range(N_HEADS):
        v_ref[hd * V_ROWS:hd * V_ROWS + VALUE_DIM, :] = vt[hd * VALUE_DIM:(hd + 1) * VALUE_DIM, :].astype(BF16)
        v_ref[hd * V_ROWS + VALUE_DIM:(hd + 1) * V_ROWS, :] = ones_row.astype(BF16)
    pos0 = lax.rem(i * tile, seq)
    pos_k = (pos0 + lax.broadcasted_iota(jnp.int32, (tile, AUG), 0)).astype(F32)
    pos_q = (pos0 + lax.broadcasted_iota(jnp.int32, (1, tile), 1)).astype(F32)
    lane = lax.broadcasted_iota(jnp.int32, (tile, AUG), 1)
    srow = lax.broadcasted_iota(jnp.int32, (AUG - HEAD_DIM, tile), 0)
    qscale = HEAD_DIM ** -0.5 * LOG2E
    for hd in range(N_HEADS):
        slope = 2.0 ** (-8.0 * (hd + 1) / N_HEADS) * LOG2E
        hi, mid, lo = _split3(pos_k * slope)
        k_add = jnp.where(lane == 64, hi,
                          jnp.where(lane == 65, mid,
                                    jnp.where(lane == 66, lo, jnp.where((lane >= 67) & (lane < 70), 1.0, 0.0))))
        hi, mid, lo = _split3(pos_q * slope)
        q_add = jnp.where(srow < 3, 1.0,
                          jnp.where(srow == 3, -hi, jnp.where(srow == 4, -mid, jnp.where(srow == 5, -lo, 0.0))))
        for m in range(2):
            c0 = (hd * 2 + m) * AUG
            qb = (qt[c0 // 2:c0 // 2 + HEAD_DIM, :] * qscale).astype(BF16)
            q_ref[c0:c0 + HEAD_DIM, :] = qb
            qr = qb.astype(F32)
            qnorm2 = jnp.sum(qr * qr, axis=0, keepdims=True) * (1.0 + 2.0 ** -6)
            q_ref[c0 + HEAD_DIM:c0 + AUG, :] = (
                q_add + jnp.where(srow == QNORM_ROW - HEAD_DIM, qnorm2, 0.0)).astype(BF16)
            kb = kp[:, c0:c0 + AUG]
            kr = kb.astype(BF16).astype(F32)
            norm2 = jnp.sum(kr * kr, axis=-1, keepdims=True) * (1.0 + 2.0 ** -6)
            k_ref[:, c0:c0 + AUG] = (kb + k_add + jnp.where(lane == NORM_LANE, norm2, 0.0)).astype(BF16)
    for cb in range(SSM_WIDTH // LANES):
        ubuf[cb] = up[:, cb * LANES:(cb + 1) * LANES]
    per_half = HALF_GROUPS * SSM_GROUP // LANES
    for s in range(CHUNK):
        for cb in range(SSM_WIDTH // LANES):
            c0 = ((cb // per_half) * CHUNK + s) * HALF_GROUPS * SSM_GROUP + (cb % per_half) * LANES
            u_ref[:, c0:c0 + LANES] = ubuf[cb, pl.ds(s, tile // CHUNK, stride=CHUNK), :].astype(BF16)


def _in_proj(x2, gain, wqv_t, wku, seq, tile=512):
    t = x2.shape[0]
    qk = 2 * N_HEADS * AUG
    kern = functools.partial(_inproj_kernel, tile=tile, seq=seq)
    full = lambda shape: pl.BlockSpec(shape, lambda i: (0, 0))
    return pl.pallas_call(
        kern, name="in_proj",
        grid=(t // tile,),
        in_specs=[pl.BlockSpec((tile, D_MODEL), lambda i: (i, 0)),
                  full((1, D_MODEL)), full((2 * ATTN_WIDTH, D_MODEL)), full((D_MODEL, qk + SSM_WIDTH))],
        out_specs=[pl.BlockSpec((qk, tile), lambda i: (0, i)),
                   pl.BlockSpec((tile, qk), lambda i: (i, 0)),
                   pl.BlockSpec((N_HEADS * V_ROWS, tile), lambda i: (0, i)),
                   pl.BlockSpec((tile // CHUNK, CHUNK * SSM_WIDTH), lambda i: (i, 0))],
        out_shape=[jax.ShapeDtypeStruct((qk, t), BF16),
                   jax.ShapeDtypeStruct((t, qk), BF16),
                   jax.ShapeDtypeStruct((N_HEADS * V_ROWS, t), BF16),
                   jax.ShapeDtypeStruct((t // CHUNK, CHUNK * SSM_WIDTH), BF16)],
        scratch_shapes=[pltpu.VMEM((SSM_WIDTH // LANES, tile, LANES), F32)],
        compiler_params=_params(("arbitrary",), 48),
    )(x2, gain, wqv_t, wku)


def _attn_kernel(lq1, lk1, lq2, lk2, sub_ref, q_ref, k_ref, v_ref, o_ref, m_sc, acc_sc, kn_sc, *, tq, kb, strip):
    hd = pl.program_id(1)
    qi = pl.program_id(2)
    lam = (jnp.exp(jnp.sum(lq1[...] * lk1[...], axis=-1, keepdims=True))
           - jnp.exp(jnp.sum(lq2[...] * lk2[...], axis=-1, keepdims=True)) + LAMBDA_INIT)

    @pl.when(qi == 0)
    def _():
        for m in range(2):
            kn_sc[m] = jnp.max(k_ref[:, m * AUG:(m + 1) * AUG].astype(F32), axis=0, keepdims=True)

    def score(start, nkeys, align, q_lo=0, nq=tq):
        r0 = pl.multiple_of(start, align)
        kblk = k_ref[pl.ds(r0, nkeys), :]
        return [jnp.dot(kblk[:, m * AUG:(m + 1) * AUG], q_ref[m * AUG:(m + 1) * AUG, q_lo:q_lo + nq],
                        preferred_element_type=F32) for m in range(2)]

    def step(start, nkeys, align, q_lo=0, nq=tq, key_lo=None, first=False, scores=None):
        if scores is None:
            scores = score(start, nkeys, align, q_lo, nq)
        vblk = v_ref[:, pl.ds(pl.multiple_of(start, align), nkeys)]
        for m in range(2):
            s = scores[m]
            if key_lo is not None:
                key = lax.broadcasted_iota(jnp.int32, (nkeys, nq), 0) + key_lo
                qry = lax.broadcasted_iota(jnp.int32, (nkeys, nq), 1) + q_lo
                s = jnp.where(key <= qry, s, -jnp.inf)
            if first:
                m_new = jnp.max(s, axis=0, keepdims=True)
                pv = jnp.dot(vblk, jnp.exp2(s - m_new).astype(BF16), preferred_element_type=F32)
            else:
                m_prev = m_sc[m, :, q_lo:q_lo + nq]
                m_new = jnp.maximum(m_prev, jnp.max(s, axis=0, keepdims=True))
                pv = (jnp.exp2(m_prev - m_new) * acc_sc[m, :, q_lo:q_lo + nq]
                      + jnp.dot(vblk, jnp.exp2(s - m_new).astype(BF16), preferred_element_type=F32))
            acc_sc[m, :, q_lo:q_lo + nq] = pv
            m_sc[m, :, q_lo:q_lo + nq] = m_new

    for lo in range(0, tq, strip):
        step(qi * tq + lo, strip, strip, q_lo=lo, nq=tq - lo, key_lo=lo, first=lo == 0)

    lane = lax.broadcasted_iota(jnp.int32, (1, AUG), 1)
    sub = lax.broadcasted_iota(jnp.int32, (16, 1), 0)
    slope = LOG2E * jnp.exp2(-2.0 * (jnp.full((1, 1), hd, jnp.int32) + 1).astype(F32))
    reach = jnp.zeros((1, 1), F32)
    for m in range(2):
        qrows = jnp.max(q_ref[m * AUG + HEAD_DIM:m * AUG + HEAD_DIM + 16, :].astype(F32), axis=-1, keepdims=True)
        q2 = jnp.max(jnp.where(sub == QNORM_ROW - HEAD_DIM, qrows, 0.0), axis=0, keepdims=True)
        k2 = jnp.max(jnp.where(lane == NORM_LANE, kn_sc[m], 0.0), axis=-1, keepdims=True)
        m_min = jnp.min(m_sc[m], axis=-1, keepdims=True)
        reach = jnp.maximum(reach, (jnp.sqrt(q2 * k2) * 1.001 + (UNDERFLOW_LOG2 + 0.5) - m_min) / slope)
    need = jnp.ceil((reach + (kb - 1)) / kb).astype(jnp.int32) - 1
    n_below = jnp.minimum(jnp.max(jnp.maximum(need, 0)), qi * (tq // kb))
    top = qi * tq

    def quad(i, carry):
        a, b = top - (2 + 4 * i) * kb, top - (4 + 4 * i) * kb
        sa, sb = score(a, 2 * kb, kb), score(b, 2 * kb, kb)
        step(a, 2 * kb, kb, scores=sa)
        step(b, 2 * kb, kb, scores=sb)
        return carry

    lax.fori_loop(0, n_below // 4, quad, 0)
    rest = n_below % 4
    done = n_below - rest

    @pl.when(rest >= 2)
    def _():
        step(top - (done + 2) * kb, 2 * kb, kb)

    @pl.when(rest % 2 == 1)
    def _():
        step(top - n_below * kb, kb, kb)

    l0 = acc_sc[0, VALUE_DIM:VALUE_DIM + 1, :]
    l1 = acc_sc[1, VALUE_DIM:VALUE_DIM + 1, :]
    o = acc_sc[0, :VALUE_DIM, :] / l0 - lam * (acc_sc[1, :VALUE_DIM, :] / l1)
    o = o * lax.rsqrt(jnp.mean(o * o, axis=0, keepdims=True) + RMS_EPS) * sub_ref[...] * (1.0 - LAMBDA_INIT)
    o_ref[...] = o.T.astype(BF16)


def _attention(q_t, k_aug, v_t, lq1, lk1, lq2, lk2, subln_col, batch, seq, tq=1024, kb=512, strip=512):
    tq = min(tq, seq)
    nq = seq // tq
    t = batch * seq
    small = pl.BlockSpec((1, HEAD_DIM), lambda b, h, i: (0, 0))
    kern = functools.partial(_attn_kernel, tq=tq, kb=kb, strip=strip)
    return pl.pallas_call(
        kern, name="attention",
        grid=(batch, N_HEADS, nq),
        in_specs=[small, small, small, small,
                  pl.BlockSpec((VALUE_DIM, 1), lambda b, h, i: (0, 0)),
                  pl.BlockSpec((2 * AUG, tq), lambda b, h, i: (h, b * nq + i)),
                  pl.BlockSpec((seq, 2 * AUG), lambda b, h, i: (b, h)),
                  pl.BlockSpec((V_ROWS, seq), lambda b, h, i: (h, b))],
        out_specs=pl.BlockSpec((tq, VALUE_DIM), lambda b, h, i: (b * nq + i, h)),
        out_shape=jax.ShapeDtypeStruct((t, ATTN_WIDTH), BF16),
        scratch_shapes=[pltpu.VMEM((2, 1, tq), F32), pltpu.VMEM((2, V_ROWS, tq), F32),
                        pltpu.VMEM((2, 1, AUG), F32)],
        compiler_params=_params(("arbitrary", "arbitrary", "arbitrary"), 48),
    )(lq1, lk1, lq2, lk2, subln_col, q_t, k_aug, v_t)


SCAN_COLS = 256


def _ssm_state_kernel(u_ref, bre_ref, bim_ref, are_ref, aim_ref, pre_ref, pim_ref, cre_ref, cim_ref, d_ref,
                      hre_ref, him_ref, kt_ref, wre, wim, sre, sim, *, nchunk, nsteps):
    i = pl.program_id(1)

    @pl.when(i == 0)
    def _():
        wre[...] = bre_ref[0]
        wim[...] = bim_ref[0]
        sre[...] = jnp.zeros(sre.shape, F32)
        sim[...] = jnp.zeros(sim.shape, F32)

    wr, wi = wre[...], wim[...]
    ar, ai = are_ref[0], aim_ref[0]
    wre[...] = wr * ar - wi * ai
    wim[...] = wr * ai + wi * ar

    u = u_ref[...]
    wr_b, wi_b = wr.astype(BF16), wi.astype(BF16)
    sre[...] += jnp.dot(u, wr_b, preferred_element_type=F32)
    sim[...] += jnp.dot(u, wi_b, preferred_element_type=F32)

    lag = (jnp.dot(wr_b, cre_ref[0].astype(BF16), preferred_element_type=F32)
           - jnp.dot(wi_b, cim_ref[0].astype(BF16), preferred_element_type=F32))
    r = lax.broadcasted_iota(jnp.int32, lag.shape, 0)
    c = lax.broadcasted_iota(jnp.int32, lag.shape, 1)
    skip = jnp.where((r == c) & (i == 0), d_ref[0], 0.0)
    kt_ref[0, 0] = (lag + skip).astype(BF16)

    @pl.when(i == CHUNK - 1)
    def _():
        row = lax.rem(lax.broadcasted_iota(jnp.int32, (sre.shape[0], SCAN_COLS), 0), nchunk)
        for c0 in range(0, sre.shape[1], SCAN_COLS):
            cols = slice(c0, c0 + SCAN_COLS)
            hr, hi = sre[:, cols], sim[:, cols]
            for k in range(nsteps):
                d = 1 << k
                ar, ai = pre_ref[0, k:k + 1, cols], pim_ref[0, k:k + 1, cols]
                keep = row >= d
                pr = jnp.where(keep, pltpu.roll(hr, d, 0), 0.0)
                pi = jnp.where(keep, pltpu.roll(hi, d, 0), 0.0)
                hr, hi = hr + ar * pr - ai * pi, hi + ar * pi + ai * pr
            keep = row >= 1
            hre_ref[:, cols] = jnp.where(keep, pltpu.roll(hr, 1, 0), 0.0).astype(BF16)
            him_ref[:, cols] = jnp.where(keep, pltpu.roll(hi, 1, 0), 0.0).astype(BF16)


def _ssm_state(u3, b_re, b_im, a_row_re, a_row_im, p_re, p_im, c_re, c_im, d_row, nchunk):
    nch = u3.shape[0]
    nsteps = p_re.shape[1]
    hw = HALF_GROUPS * SSM_GROUP
    sw = HALF_GROUPS * SSM_STATE
    kern = functools.partial(_ssm_state_kernel, nchunk=nchunk, nsteps=nsteps)
    per_half = lambda shape: pl.BlockSpec((1,) + shape, lambda hf, i: (hf, 0, 0))
    return pl.pallas_call(
        kern, name="ssm_state",
        grid=(2, CHUNK),
        in_specs=[pl.BlockSpec((nch, hw), lambda hf, i: (0, hf * CHUNK + CHUNK - 1 - i)),
                  per_half((hw, sw)), per_half((hw, sw)), per_half((1, sw)), per_half((1, sw)),
                  per_half((nsteps, sw)), per_half((nsteps, sw)),
                  per_half((sw, hw)), per_half((sw, hw)), per_half((1, hw))],
        out_specs=[pl.BlockSpec((nch, sw), lambda hf, i: (0, hf)),
                   pl.BlockSpec((nch, sw), lambda hf, i: (0, hf)),
                   pl.BlockSpec((1, 1, hw, hw), lambda hf, i: (i, hf, 0, 0))],
        out_shape=[jax.ShapeDtypeStruct((nch, 2 * sw), BF16)] * 2
        + [jax.ShapeDtypeStruct((CHUNK, 2, hw, hw), BF16)],
        scratch_shapes=[pltpu.VMEM((hw, sw), F32), pltpu.VMEM((hw, sw), F32),
                        pltpu.VMEM((nch, sw), F32), pltpu.VMEM((nch, sw), F32)],
        compiler_params=_params(("arbitrary", "arbitrary"), 48),
    )(u3, b_re, b_im, a_row_re, a_row_im, p_re, p_im, c_re, c_im, d_row)


def _ssm_out_kernel(u_ref, kt_ref, hre_ref, him_ref, cre_ref, cim_ref, are_ref, aim_ref, y_ref, wre, wim, toep):
    t = pl.program_id(1)
    ar, ai = are_ref[0], aim_ref[0]

    @pl.when(t == 0)
    def _():
        cr, ci = cre_ref[0], cim_ref[0]
        wre[...] = cr * ar - ci * ai
        wim[...] = cr * ai + ci * ar

    wr, wi = wre[...], wim[...]
    wre[...] = wr * ar - wi * ai
    wim[...] = wr * ai + wi * ar

    hw = kt_ref.shape[-1]
    for s in range(CHUNK):
        tile = kt_ref[jnp.maximum(t - s, 0), 0]
        toep[s * hw:(s + 1) * hw, :] = jnp.where(s <= t, tile, jnp.zeros_like(tile))

    span = TOEP_SPAN * hw
    y_ref[0] = (jnp.dot(u_ref[:, :span], toep[:span, :], preferred_element_type=F32)
                + jnp.dot(hre_ref[...], wr.astype(BF16), preferred_element_type=F32)
                - jnp.dot(him_ref[...], wi.astype(BF16), preferred_element_type=F32))
    for piece in range(1, CHUNK // TOEP_SPAN):
        @pl.when(t >= piece * TOEP_SPAN)
        def _():
            lo = piece * span
            y_ref[0] += jnp.dot(u_ref[:, lo:lo + span], toep[lo:lo + span, :], preferred_element_type=F32)


def _ssm_out(u3, kt, h_re, h_im, c_re, c_im, a_col_re, a_col_im):
    nch = u3.shape[0]
    hw = HALF_GROUPS * SSM_GROUP
    sw = HALF_GROUPS * SSM_STATE
    return pl.pallas_call(
        _ssm_out_kernel, name="ssm_out",
        grid=(2, CHUNK),
        in_specs=[pl.BlockSpec((nch, CHUNK * hw), lambda hf, t: (0, hf)),
                  pl.BlockSpec((CHUNK, 1, hw, hw), lambda hf, t: (0, hf, 0, 0)),
                  pl.BlockSpec((nch, sw), lambda hf, t: (0, hf)),
                  pl.BlockSpec((nch, sw), lambda hf, t: (0, hf)),
                  pl.BlockSpec((1, sw, hw), lambda hf, t: (hf, 0, 0)),
                  pl.BlockSpec((1, sw, hw), lambda hf, t: (hf, 0, 0)),
                  pl.BlockSpec((1, sw, 1), lambda hf, t: (hf, 0, 0)),
                  pl.BlockSpec((1, sw, 1), lambda hf, t: (hf, 0, 0))],
        out_specs=pl.BlockSpec((1, nch, hw), lambda hf, t: (t, 0, hf)),
        out_shape=jax.ShapeDtypeStruct((CHUNK, nch, SSM_WIDTH), F32),
        scratch_shapes=[pltpu.VMEM((sw, hw), F32), pltpu.VMEM((sw, hw), F32), pltpu.VMEM((CHUNK * hw, hw), BF16)],
        compiler_params=_params(("arbitrary", "arbitrary"), 48),
    )(u3, kt, h_re, h_im, c_re, c_im, a_col_re, a_col_im)


def _ssm_constants(lam_re, lam_im, log_dt, b_re, b_im, c_re, c_im, d_skip, nchunk):
    lr, li = lam_re.astype(F32), lam_im.astype(F32)
    dt = jnp.exp(log_dt.astype(F32))[:, None]

    def lam_bar_pow(k):
        mag = jnp.exp(k * lr * dt)
        return mag * jnp.cos(k * li * dt), mag * jnp.sin(k * li * dt)

    a_re, a_im = lam_bar_pow(1.0)
    den = lr * lr + li * li
    coef_re = ((a_re - 1.0) * lr + a_im * li) / den
    coef_im = (a_im * lr - (a_re - 1.0) * li) / den
    bb_re = coef_re[..., None] * b_re.astype(F32) - coef_im[..., None] * b_im.astype(F32)
    bb_im = coef_re[..., None] * b_im.astype(F32) + coef_im[..., None] * b_re.astype(F32)
    cc_re, cc_im = c_re.astype(F32), c_im.astype(F32)
    hw, sw = HALF_GROUPS * SSM_GROUP, HALF_GROUPS * SSM_STATE

    def block_diag(rows, row_group, col_group):
        wide = jnp.tile(rows, (1,) * (rows.ndim - 1) + (HALF_GROUPS,))
        r = lax.broadcasted_iota(jnp.int32, wide.shape, wide.ndim - 2) // row_group
        c = lax.broadcasted_iota(jnp.int32, wide.shape, wide.ndim - 1) // col_group
        return jnp.where(r == c, wide, 0.0)

    def b_tiles(part):
        p = part.reshape(2, HALF_GROUPS, SSM_STATE, SSM_GROUP).transpose(0, 1, 3, 2)
        return block_diag(p.reshape(2, hw, SSM_STATE), SSM_GROUP, SSM_STATE)

    def c_tiles(part):
        p = part.reshape(2, HALF_GROUPS, SSM_GROUP, SSM_STATE).transpose(0, 1, 3, 2)
        return block_diag(p.reshape(2, sw, SSM_GROUP), SSM_STATE, SSM_GROUP)

    nsteps = max(int(math.log2(nchunk)), 1)
    steps = (CHUNK * 2.0 ** jnp.arange(nsteps, dtype=F32))[:, None, None]
    st_re, st_im = lam_bar_pow(steps)
    by_half = lambda p: p.reshape(nsteps, 2, sw).transpose(1, 0, 2)
    ar_h, ai_h = a_re.reshape(2, sw), a_im.reshape(2, sw)
    return dict(d_row=d_skip.astype(F32).reshape(2, 1, hw), b_re=b_tiles(bb_re), b_im=b_tiles(bb_im),
                c_re=c_tiles(cc_re), c_im=c_tiles(cc_im),
                a_row_re=ar_h[:, None, :], a_row_im=ai_h[:, None, :],
                a_col_re=ar_h[:, :, None], a_col_im=ai_h[:, :, None],
                p_re=by_half(st_re), p_im=by_half(st_im))


def _mix_kernel(x_ref, attn_ref, y3_ref, wglu_ref, bglu_ref, gssm_ref, woa_ref, wos_ref, gmoe_ref,
                wr_ref, br_ref, x1_ref, h2_ref, info_ref, rt_ref, cnt_ref, ybuf, carry, *, tile):
    i = pl.program_id(0)

    @pl.when(i == 0)
    def _():
        carry[...] = jnp.zeros(carry.shape, F32)

    for s in range(CHUNK):
        for cb in range(SSM_WIDTH // LANES):
            ybuf[cb, pl.ds(s, tile // CHUNK, stride=CHUNK), :] = y3_ref[s, :, cb * LANES:(cb + 1) * LANES]
    y = jax.nn.gelu(jnp.concatenate([ybuf[cb] for cb in range(SSM_WIDTH // LANES)], axis=-1))
    z = jnp.dot(y.astype(BF16), wglu_ref[...], preferred_element_type=F32) + bglu_ref[...]
    y = y * jax.nn.sigmoid(z)
    ssm = _rms(y, gssm_ref[...])
    x1 = (x_ref[...] + jnp.dot(attn_ref[...], woa_ref[...], preferred_element_type=F32)
          + jnp.dot(ssm.astype(BF16), wos_ref[...], preferred_element_type=F32))
    x1_ref[...] = x1
    h2 = _rms(x1, gmoe_ref[...])
    _store_tile_rows(h2_ref, h2)

    nt = (((1,), (1,)), ((), ()))
    h_hi = h2.astype(BF16)
    h_lo = (h2 - h_hi.astype(F32)).astype(BF16)
    both = lax.dot_general(wr_ref[...], h_hi, nt, preferred_element_type=F32)
    logits = (both[:ROUTER_ROWS] + both[LANES:LANES + ROUTER_ROWS]
              + lax.dot_general(wr_ref[:ROUTER_ROWS, :], h_lo, nt, preferred_element_type=F32) + br_ref[...])
    row = lax.broadcasted_iota(jnp.int32, logits.shape, 0)
    neg = -jnp.inf
    gl = jnp.where(row < N_EXPERT_GROUPS, logits, neg)
    gmax = jnp.max(gl, axis=0, keepdims=True)
    gsel = jnp.min(jnp.where(gl == gmax, row, LANES), axis=0, keepdims=True)
    p_group = 1.0 / jnp.sum(jnp.exp(gl - gmax), axis=0, keepdims=True)
    erow = row - N_EXPERT_GROUPS
    in_grp = (erow >= 0) & (erow < N_EXPERTS) & ((erow >> 3) == gsel)
    el = jnp.where(in_grp, logits, neg)
    m1 = jnp.max(el, axis=0, keepdims=True)
    i1 = jnp.min(jnp.where(el == m1, row, LANES), axis=0, keepdims=True)
    den = jnp.sum(jnp.exp(el - m1), axis=0, keepdims=True)
    el2 = jnp.where(row == i1, neg, el)
    m2 = jnp.max(el2, axis=0, keepdims=True)
    i2 = jnp.min(jnp.where(el2 == m2, row, LANES), axis=0, keepdims=True)
    g0 = p_group / den
    g1 = p_group * jnp.exp(m2 - m1) / den

    hit0 = row == i1
    hit1 = row == i2
    onehot = jnp.where(hit0 | hit1, 1.0, 0.0)
    r = lax.broadcasted_iota(jnp.int32, (tile, tile), 0)
    c = lax.broadcasted_iota(jnp.int32, (tile, tile), 1)
    earlier = jnp.where(r < c, 1.0, 0.0).astype(BF16)
    before = jnp.dot(onehot.astype(BF16), earlier, preferred_element_type=F32) + carry[...]
    rank0 = jnp.sum(jnp.where(hit0, before, 0.0), axis=0, keepdims=True)
    rank1 = jnp.sum(jnp.where(hit1, before, 0.0), axis=0, keepdims=True)
    carry[...] += jnp.sum(onehot, axis=1, keepdims=True)
    cnt_ref[...] = jnp.broadcast_to(carry[...], cnt_ref.shape)
    r8 = lax.broadcasted_iota(jnp.int32, (8, tile), 0)
    route = jnp.where(r8 == 0, (i1 - N_EXPERT_GROUPS).astype(F32),
                      jnp.where(r8 == 1, (i2 - N_EXPERT_GROUPS).astype(F32),
                                jnp.where(r8 == 2, g0,
                                          jnp.where(r8 == 3, g1,
                                                    jnp.where(r8 == 4, rank0, jnp.where(r8 == 5, rank1, 0.0))))))
    rt_ref[...] = route
    info_ref[...] = route.T


def _mix(x2, attn, y3, wglu, bglu, gssm, wo_a, wo_s, gmoe, w_router, b_router, tile=512):
    t = x2.shape[0]
    kern = functools.partial(_mix_kernel, tile=tile)
    full = lambda shape: pl.BlockSpec(shape, lambda i: tuple(0 for _ in shape))
    return pl.pallas_call(
        kern, name="mix",
        grid=(t // tile,),
        in_specs=[pl.BlockSpec((tile, D_MODEL), lambda i: (i, 0)),
                  pl.BlockSpec((tile, ATTN_WIDTH), lambda i: (i, 0)),
                  pl.BlockSpec((CHUNK, tile // CHUNK, SSM_WIDTH), lambda i: (0, i, 0)),
                  full((SSM_WIDTH, SSM_WIDTH)), full((1, SSM_WIDTH)), full((1, SSM_WIDTH)),
                  full((ATTN_WIDTH, D_MODEL)), full((SSM_WIDTH, D_MODEL)), full((1, D_MODEL)),
                  full((2 * LANES, D_MODEL)), full((ROUTER_ROWS, 1))],
        out_specs=[pl.BlockSpec((tile, D_MODEL), lambda i: (i, 0)),
                   pl.BlockSpec((tile * ROW_TILE, LANES), lambda i: (i, 0)),
                   pl.BlockSpec((tile, 8), lambda i: (i, 0)),
                   pl.BlockSpec((8, tile), lambda i: (0, i)),
                   pl.BlockSpec((ROUTER_ROWS, LANES), lambda i: (0, 0))],
        out_shape=[jax.ShapeDtypeStruct((t, D_MODEL), F32),
                   jax.ShapeDtypeStruct((t * ROW_TILE, LANES), F32),
                   jax.ShapeDtypeStruct((t, 8), F32),
                   jax.ShapeDtypeStruct((8, t), F32),
                   jax.ShapeDtypeStruct((ROUTER_ROWS, LANES), F32)],
        scratch_shapes=[pltpu.VMEM((SSM_WIDTH // LANES, tile, LANES), F32), pltpu.VMEM((ROUTER_ROWS, 1), F32)],
        compiler_params=_params(("arbitrary",), 48),
    )(x2, attn, y3, wglu, bglu, gssm, wo_a, wo_s, gmoe, w_router, b_router)


def _rows_at(ref, row, n_rows=1):
    return ref.at[pl.ds(pl.multiple_of(row * ROW_TILE, ROW_TILE), n_rows * ROW_TILE), :]


def _row_copy(src, s, dst, d, sem):
    return pltpu.make_async_copy(_rows_at(src, s), _rows_at(dst, d), sem)


UNROLL = 4
COMBINE_GROUPS = 16
PAD_SIZES = tuple(1 << b for b in reversed(range(ROW_BLOCK.bit_length() - 1)))


def _dispatch_kernel(dest_ref, pad_start_ref, pad_len_ref, h_ref, rows_out, zbuf, sem, zsem, *, tile, n_tok,
                     n_blocks):
    i = pl.program_id(0)
    base = i * tile

    def zero_fill(e, start):
        off, rem = pad_start_ref[e], pad_len_ref[e]
        for size in PAD_SIZES:
            @pl.when((rem & size) != 0)
            def _():
                cp = pltpu.make_async_copy(_rows_at(zbuf, 0, size), _rows_at(rows_out, off, size), zsem)
                cp.start() if start else cp.wait()
            off = off + (rem & size)

    def zero_tail(start):
        used = (pad_start_ref[N_EXPERTS - 1] + pad_len_ref[N_EXPERTS - 1]) // ROW_BLOCK

        def blk(b, carry):
            for half in range(ROW_BLOCK // PAD_SIZES[0]):
                cp = pltpu.make_async_copy(zbuf, _rows_at(rows_out, b * ROW_BLOCK + half * PAD_SIZES[0],
                                                          PAD_SIZES[0]), zsem)
                cp.start() if start else cp.wait()
            return carry

        lax.fori_loop(used, n_blocks, blk, 0)

    @pl.when(i == 0)
    def _():
        zbuf[...] = jnp.zeros(zbuf.shape, F32)
        lax.fori_loop(0, N_EXPERTS, lambda e, c: (zero_fill(e, True), c)[1], 0)
        zero_tail(True)

    def issue(tb, carry):
        for k in range(UNROLL):
            t = tb * UNROLL + k
            for j in range(2):
                _row_copy(h_ref, t, rows_out, dest_ref[j * n_tok + base + t], sem).start(priority=j)
        return carry

    def drain(tb, carry):
        for _ in range(2 * UNROLL):
            _row_copy(h_ref, 0, rows_out, 0, sem).wait()
        return carry

    lax.fori_loop(0, tile // UNROLL, issue, 0)
    lax.fori_loop(0, tile // UNROLL, drain, 0)

    @pl.when(i == 0)
    def _():
        lax.fori_loop(0, N_EXPERTS, lambda e, c: (zero_fill(e, False), c)[1], 0)
        zero_tail(False)


def _dispatch(dest_flat, pad_start, pad_len, h2, n_rows, tile=1024):
    t = h2.shape[0] // ROW_TILE
    kern = functools.partial(_dispatch_kernel, tile=tile, n_tok=t, n_blocks=n_rows // ROW_BLOCK)
    grid_spec = pltpu.PrefetchScalarGridSpec(
        num_scalar_prefetch=3, grid=(t // tile,),
        in_specs=[pl.BlockSpec((tile * ROW_TILE, LANES), lambda i, *_: (i, 0))],
        out_specs=pl.BlockSpec(memory_space=pl.ANY),
        scratch_shapes=[pltpu.VMEM((PAD_SIZES[0] * ROW_TILE, LANES), F32),
                        pltpu.SemaphoreType.DMA(()), pltpu.SemaphoreType.DMA(())])
    return pl.pallas_call(
        kern, name="dispatch", grid_spec=grid_spec,
        out_shape=jax.ShapeDtypeStruct((n_rows * ROW_TILE, LANES), F32),
        compiler_params=_params(("arbitrary",), 32),
    )(dest_flat, pad_start, pad_len, h2)


def _expert_kernel(be_ref, nu_ref, first_ref, next_ref, slot_ref, rows_ref, wg_hbm, wu_hbm, wd_hbm, out_ref,
                   wg_f, wu_f, wd_f, wg_b, wu_b, wd_b, sems):
    i = pl.program_id(0)

    def fetch(e, slot):
        return (pltpu.make_async_copy(wg_hbm.at[e], wg_f.at[slot], sems.at[slot, 0]),
                pltpu.make_async_copy(wu_hbm.at[e], wu_f.at[slot], sems.at[slot, 1]),
                pltpu.make_async_copy(wd_hbm.at[e], wd_f.at[slot], sems.at[slot, 2]))

    @pl.when((i == 0) & (nu_ref[0] > 0))
    def _():
        for cp in fetch(be_ref[0], 0):
            cp.start()

    @pl.when((first_ref[i] == 1) & (i < nu_ref[0]))
    def _():
        slot = slot_ref[i]

        @pl.when(next_ref[i] >= 0)
        def _():
            for cp in fetch(next_ref[i], 1 - slot):
                cp.start()

        for cp in fetch(be_ref[i], slot):
            cp.wait()
        wg_b[...] = wg_f[slot].astype(BF16)
        wu_b[...] = wu_f[slot].astype(BF16)
        wd_b[...] = wd_f[slot].astype(BF16)

    @pl.when(i < nu_ref[0])
    def _():
        xb = _load_tile_rows(rows_ref).astype(BF16)
        gate = jnp.dot(xb, wg_b[...], preferred_element_type=F32)
        up = jnp.dot(xb, wu_b[...], preferred_element_type=F32)
        hid = (gate * jax.nn.sigmoid(gate) * up).astype(BF16)
        _store_tile_rows(out_ref, jnp.dot(hid, wd_b[...], preferred_element_type=F32))

    @pl.when(i >= nu_ref[0])
    def _():
        out_ref[...] = jnp.zeros(out_ref.shape, F32)


def _experts(block_e, n_used, seg_first, seg_next, seg_slot, rows, w_gate, w_up, w_down):
    nb = rows.shape[0] // (ROW_BLOCK * ROW_TILE)
    last = lambda i, be, nu, *_: jnp.maximum(jnp.minimum(i, nu[0] - 1), 0)
    hbm = pl.BlockSpec(memory_space=pl.ANY)
    grid_spec = pltpu.PrefetchScalarGridSpec(
        num_scalar_prefetch=5, grid=(nb,),
        in_specs=[pl.BlockSpec((ROW_BLOCK * ROW_TILE, LANES), lambda i, *s: (last(i, *s), 0)), hbm, hbm, hbm],
        out_specs=pl.BlockSpec((ROW_BLOCK * ROW_TILE, LANES), lambda i, *s: (i, 0)),
        scratch_shapes=[pltpu.VMEM((2, D_MODEL, EXPERT_FF), F32), pltpu.VMEM((2, D_MODEL, EXPERT_FF), F32),
                        pltpu.VMEM((2, EXPERT_FF, D_MODEL), F32),
                        pltpu.VMEM((D_MODEL, EXPERT_FF), BF16), pltpu.VMEM((D_MODEL, EXPERT_FF), BF16),
                        pltpu.VMEM((EXPERT_FF, D_MODEL), BF16), pltpu.SemaphoreType.DMA((2, 3))])
    return pl.pallas_call(
        _expert_kernel, name="experts", grid_spec=grid_spec,
        out_shape=jax.ShapeDtypeStruct(rows.shape, F32),
        compiler_params=_params(("arbitrary",), 48),
    )(block_e, n_used, seg_first, seg_next, seg_slot, rows, w_gate, w_up, w_down)


def _combine_kernel(dest_ref, x1_ref, info_ref, gfin_ref, rows_ref, o_ref, buf0, buf1, sems, *, tile, n_tok):
    i = pl.program_id(0)
    slot = i % 2

    def gather(step, to_slot):
        base = step * tile
        b0, b1, sem = buf0.at[to_slot], buf1.at[to_slot], sems.at[to_slot]

        def issue(tb, carry):
            for k in range(UNROLL):
                t = tb * UNROLL + k
                _row_copy(rows_ref, dest_ref[base + t], b0, t, sem).start(priority=0)
                _row_copy(rows_ref, dest_ref[n_tok + base + t], b1, t, sem).start(priority=1)
            return carry

        lax.fori_loop(0, tile // UNROLL, issue, 0)

    def drain(of_slot):
        b0, b1, sem = buf0.at[of_slot], buf1.at[of_slot], sems.at[of_slot]

        def wait(tb, carry):
            for _ in range(UNROLL):
                _row_copy(rows_ref, 0, b0, 0, sem).wait()
                _row_copy(rows_ref, 0, b1, 0, sem).wait()
            return carry

        lax.fori_loop(0, tile // UNROLL, wait, 0)

    @pl.when(i == 0)
    def _():
        gather(0, 0)

    drain(slot)

    last = pl.num_programs(0) - 1
    nxt_base = jnp.minimum(i + 1, last) * tile
    b0, b1 = buf0.at[slot], buf1.at[slot]
    n0, n1, nsem = buf0.at[1 - slot], buf1.at[1 - slot], sems.at[1 - slot]
    group = tile // COMBINE_GROUPS
    for g in range(COMBINE_GROUPS):
        lo = g * group
        for t in range(lo, lo + group):
            _row_copy(rows_ref, dest_ref[nxt_base + t], n0, t, nsem).start(priority=0)
            _row_copy(rows_ref, dest_ref[n_tok + nxt_base + t], n1, t, nsem).start(priority=1)
        info = info_ref[lo:lo + group, :]
        x2 = (x1_ref[lo:lo + group, :] + info[:, 2:3] * _load_tile_rows(b0, lo, group)
              + info[:, 3:4] * _load_tile_rows(b1, lo, group))
        o_ref[lo:lo + group, :] = _rms(x2, gfin_ref[...])

    @pl.when(i == last)
    def _():
        drain(1 - slot)


def _combine(dest_flat, x1, info, gfin, out_rows, tile=512):
    t = x1.shape[0]
    kern = functools.partial(_combine_kernel, tile=tile, n_tok=t)
    grid_spec = pltpu.PrefetchScalarGridSpec(
        num_scalar_prefetch=1, grid=(t // tile,),
        in_specs=[pl.BlockSpec((tile, D_MODEL), lambda i, d: (i, 0)),
                  pl.BlockSpec((tile, 8), lambda i, d: (i, 0)),
                  pl.BlockSpec((1, D_MODEL), lambda i, d: (0, 0)),
                  pl.BlockSpec(memory_space=pl.ANY)],
        out_specs=pl.BlockSpec((tile, D_MODEL), lambda i, d: (i, 0)),
        scratch_shapes=[pltpu.VMEM((2, tile * ROW_TILE, LANES), F32), pltpu.VMEM((2, tile * ROW_TILE, LANES), F32),
                        pltpu.SemaphoreType.DMA((2,))])
    return pl.pallas_call(
        kern, name="combine", grid_spec=grid_spec,
        out_shape=jax.ShapeDtypeStruct((t, D_MODEL), F32),
        compiler_params=_params(("arbitrary",), 32),
    )(dest_flat, x1, info, gfin, out_rows)


def _split_w_in(w_in):
    w4 = w_in.astype(BF16).reshape(D_MODEL, 4, ATTN_WIDTH)
    wqv_t = w4[:, 0::2, :].reshape(D_MODEL, 2 * ATTN_WIDTH).T
    wk = jnp.pad(w4[:, 1, :].reshape(D_MODEL, N_HEADS * 2, HEAD_DIM), ((0, 0), (0, 0), (0, AUG - HEAD_DIM)))
    wku = jnp.concatenate([wk.reshape(D_MODEL, N_HEADS * 2 * AUG), w4[:, 3, :]], axis=1)
    return wqv_t, wku


def kernel(x, norm_attn, w_in, lambda_q1, lambda_k1, lambda_q2, lambda_k2, attn_subln, ssm_lam_re, ssm_lam_im, ssm_log_dt, ssm_b_re, ssm_b_im, ssm_c_re, ssm_c_im, ssm_d, w_glu, b_glu, ssm_norm, w_out, norm_moe, w_router_group, b_router_group, w_router_expert, b_router_expert, w_gate, w_up, w_down, norm_final):
    batch, seq, d = x.shape
    t = batch * seq
    nchunk = seq // CHUNK
    x2 = x.reshape(t, d)
    l = 0

    q_t, k_aug, v_t, u3 = _in_proj(x2, norm_attn[l][None], *_split_w_in(w_in[l]), seq)
    attn = _attention(q_t, k_aug, v_t, lambda_q1[l][None], lambda_k1[l][None], lambda_q2[l][None],
                      lambda_k2[l][None], attn_subln[l][:, None], batch, seq)

    sc = _ssm_constants(ssm_lam_re[l], ssm_lam_im[l], ssm_log_dt[l], ssm_b_re[l], ssm_b_im[l],
                        ssm_c_re[l], ssm_c_im[l], ssm_d[l], nchunk)
    h_re, h_im, kt = _ssm_state(u3, sc["b_re"], sc["b_im"], sc["a_row_re"], sc["a_row_im"], sc["p_re"], sc["p_im"],
                                sc["c_re"], sc["c_im"], sc["d_row"], nchunk)
    y3 = _ssm_out(u3, kt, h_re, h_im, sc["c_re"], sc["c_im"], sc["a_col_re"], sc["a_col_im"])

    w_router = jnp.concatenate([w_router_group[l], w_router_expert[l]], axis=1).astype(F32).T
    w_router = jnp.pad(w_router, ((0, LANES - w_router.shape[0]), (0, 0)))
    w_router_hi = w_router.astype(BF16)
    w_router = jnp.concatenate([w_router_hi, (w_router - w_router_hi.astype(F32)).astype(BF16)], axis=0)
    b_router = jnp.concatenate([b_router_group[l], b_router_expert[l]]).astype(F32)
    b_router = jnp.pad(b_router, (0, ROUTER_ROWS - b_router.shape[0]))[:, None]
    x1, h2, info, route_t, cnt = _mix(x2, attn, y3, w_glu[l].astype(BF16), b_glu[l][None], ssm_norm[l][None],
                                      w_out[l][:ATTN_WIDTH].astype(BF16), w_out[l][ATTN_WIDTH:].astype(BF16),
                                      norm_moe[l][None], w_router, b_router)

    experts = route_t[0:2].astype(jnp.int32)
    ranks = route_t[4:6].astype(jnp.int32)
    counts = cnt[N_EXPERT_GROUPS:N_EXPERT_GROUPS + N_EXPERTS, 0].astype(jnp.int32)
    padded = ((counts + ROW_BLOCK - 1) // ROW_BLOCK) * ROW_BLOCK
    ids = jnp.arange(N_EXPERTS, dtype=jnp.int32)
    pend = jnp.sum(jnp.where(ids[None, :] <= ids[:, None], padded[None, :], 0), axis=1)
    pstart = pend - padded
    dest = ranks + jnp.sum(jnp.where(experts[:, None, :] == ids[None, :, None], pstart[None, :, None], 0), axis=1)
    dest = dest.reshape(-1)
    n_rows = ((2 * t + N_EXPERTS * (ROW_BLOCK - 1) + ROW_BLOCK - 1) // ROW_BLOCK) * ROW_BLOCK
    nb = n_rows // ROW_BLOCK
    n_used = (pend[-1] // ROW_BLOCK).astype(jnp.int32)
    blk = jnp.minimum(jnp.arange(nb, dtype=jnp.int32), n_used - 1) * ROW_BLOCK
    block_e = jnp.minimum(jnp.sum((pend[None, :] <= blk[:, None]).astype(jnp.int32), axis=1), N_EXPERTS - 1)

    bidx = jnp.arange(nb, dtype=jnp.int32)
    seg_first = ((bidx == 0) | (block_e != jnp.concatenate([block_e[:1], block_e[:-1]]))).astype(jnp.int32)
    seg_slot = (jnp.sum(jnp.where(bidx[None, :] <= bidx[:, None], seg_first[None, :], 0), axis=1) - 1) & 1
    seg_end = jnp.sum(jnp.where(block_e[:, None] == ids[None, :], pend[None, :], 0), axis=1) // ROW_BLOCK
    after = jnp.sum(jnp.where(bidx[None, :] == seg_end[:, None], block_e[None, :], 0), axis=1)
    seg_next = jnp.where(seg_end < n_used, after, -1).astype(jnp.int32)

    rows = _dispatch(dest, pstart + counts, padded - counts, h2, n_rows)
    out_rows = _experts(block_e, n_used[None], seg_first, seg_next, seg_slot.astype(jnp.int32), rows,
                        w_gate[l], w_up[l], w_down[l])
    out = _combine(dest, x1, info, norm_final[None], out_rows)
    return out.reshape(batch, seq, d)
```

```python
import functools
import math

import jax
import jax.numpy as jnp
import numpy as np
from jax import lax
from jax.experimental import pallas as pl
from jax.experimental.pallas import tpu as pltpu

F32 = jnp.float32
BF16 = jnp.bfloat16

D_MODEL = 1024
N_HEADS = 4
HEAD_DIM = 64
VALUE_DIM = 128
ATTN_WIDTH = 512
SSM_WIDTH = 512
SSM_GROUP = 16
N_GROUPS = 32
SSM_STATE = 64
N_EXPERT_GROUPS = 4
EXPERTS_PER_GROUP = 8
N_EXPERTS = 32
EXPERT_FF = 512
RMS_EPS = 1e-6
LAMBDA_INIT = 0.8 - 0.6 * math.exp(-0.3 * 0)
LOG2E = math.log2(math.e)

CHUNK = 16
HALF_GROUPS = 16
TOEP_SPAN = 4
ROUTER_ROWS = 48
LANES = 128
AUG = 128
NORM_LANE = 70
QNORM_ROW = 71
V_ROWS = VALUE_DIM + 16
UNDERFLOW_LOG2 = 152.0
ROW_BLOCK = 256
ROW_TILE = D_MODEL // 128
VMEM_LIMIT_CAP = 56 * 1024 * 1024


def _params(dims, vmem_mb):
    return pltpu.CompilerParams(dimension_semantics=dims,
                                vmem_limit_bytes=min(vmem_mb * 1024 * 1024, VMEM_LIMIT_CAP))


def _rms(x, gain):
    return x * lax.rsqrt(jnp.mean(x * x, axis=-1, keepdims=True) + RMS_EPS) * gain


def _store_tile_rows(ref, val):
    n = val.shape[0]
    for c in range(ROW_TILE):
        ref[pl.ds(c, n, stride=ROW_TILE), :] = val[:, c * LANES:(c + 1) * LANES]


def _load_tile_rows(ref, first=0, n=None):
    n = ref.shape[0] // ROW_TILE if n is None else n
    return jnp.concatenate([ref[pl.ds(first * ROW_TILE + c, n, stride=ROW_TILE), :] for c in range(ROW_TILE)],
                           axis=-1)


def _split3(val):
    hi = val.astype(BF16).astype(F32)
    r1 = val - hi
    mid = r1.astype(BF16).astype(F32)
    lo = r1 - mid
    return hi, mid, lo


def _inproj_kernel(x_ref, g_ref, w_ref, q_ref, k_ref, v_ref, u_ref, ubuf, wqv_ref, wku_ref, *, tile, seq):
    i = pl.program_id(0)

    @pl.when(i == 0)
    def _():
        for part, col0 in ((0, 0), (1, 2 * ATTN_WIDTH)):
            for cb in range(ATTN_WIDTH // LANES):
                r0, c0 = part * ATTN_WIDTH + cb * LANES, col0 + cb * LANES
                wqv_ref[r0:r0 + LANES, :] = w_ref[:, c0:c0 + LANES].T.astype(BF16)
        for blk in range(2 * N_HEADS):
            src = ATTN_WIDTH + blk * HEAD_DIM
            wku_ref[:, blk * AUG:blk * AUG + HEAD_DIM] = w_ref[:, src:src + HEAD_DIM].astype(BF16)
            wku_ref[:, blk * AUG + HEAD_DIM:(blk + 1) * AUG] = jnp.zeros((D_MODEL, AUG - HEAD_DIM), BF16)
        wku_ref[:, 2 * N_HEADS * AUG:] = w_ref[:, 3 * ATTN_WIDTH:].astype(BF16)

    h = _rms(x_ref[...], g_ref[...]).astype(BF16)
    nt = (((1,), (1,)), ((), ()))
    kcols = 2 * N_HEADS * AUG
    qt = lax.dot_general(wqv_ref[:ATTN_WIDTH, :], h, nt, preferred_element_type=F32)
    kp = jnp.dot(h, wku_ref[:, :kcols], preferred_element_type=F32)
    vt = lax.dot_general(wqv_ref[ATTN_WIDTH:, :], h, nt, preferred_element_type=F32)
    up = jnp.dot(h, wku_ref[:, kcols:], preferred_element_type=F32)
    ones_row = jnp.where(lax.broadcasted_iota(jnp.int32, (V_ROWS - VALUE_DIM, tile), 0) == 0, 1.0, 0.0)
    for hd in range(N_HEADS):
        v_ref[hd * V_ROWS:hd * V_ROWS + VALUE_DIM, :] = vt[hd * VALUE_DIM:(hd + 1) * VALUE_DIM, :].astype(BF16)
        v_ref[hd * V_ROWS + VALUE_DIM:(hd + 1) * V_ROWS, :] = ones_row.astype(BF16)
    pos0 = lax.rem(i * tile, seq)
    pos_k = (pos0 + lax.broadcasted_iota(jnp.int32, (tile, AUG), 0)).astype(F32)
    pos_q = (pos0 + lax.broadcasted_iota(jnp.int32, (1, tile), 1)).astype(F32)
    lane = lax.broadcasted_iota(jnp.int32, (tile, AUG), 1)
    srow = lax.broadcasted_iota(jnp.int32, (AUG - HEAD_DIM, tile), 0)
    qscale = HEAD_DIM ** -0.5 * LOG2E
    for hd in range(N_HEADS):
        slope = 2.0 ** (-8.0 * (hd + 1) / N_HEADS) * LOG2E
        hi, mid, lo = _split3(pos_k * slope)
        k_add = jnp.where(lane == 64, hi,
                          jnp.where(lane == 65, mid,
                                    jnp.where(lane == 66, lo, jnp.where((lane >= 67) & (lane < 70), 1.0, 0.0))))
        hi, mid, lo = _split3(pos_q * slope)
        q_add = jnp.where(srow < 3, 1.0,
                          jnp.where(srow == 3, -hi, jnp.where(srow == 4, -mid, jnp.where(srow == 5, -lo, 0.0))))
        for m in range(2):
            c0 = (hd * 2 + m) * AUG
            qb = (qt[c0 // 2:c0 // 2 + HEAD_DIM, :] * qscale).astype(BF16)
            q_ref[c0:c0 + HEAD_DIM, :] = qb
            qr = qb.astype(F32)
            qnorm2 = jnp.sum(qr * qr, axis=0, keepdims=True) * (1.0 + 2.0 ** -6)
            q_ref[c0 + HEAD_DIM:c0 + AUG, :] = (
                q_add + jnp.where(srow == QNORM_ROW - HEAD_DIM, qnorm2, 0.0)).astype(BF16)
            kb = kp[:, c0:c0 + AUG]
            kr = kb.astype(BF16).astype(F32)
            norm2 = jnp.sum(kr * kr, axis=-1, keepdims=True) * (1.0 + 2.0 ** -6)
            k_ref[:, c0:c0 + AUG] = (kb + k_add + jnp.where(lane == NORM_LANE, norm2, 0.0)).astype(BF16)
    for cb in range(SSM_WIDTH // LANES):
        ubuf[cb] = up[:, cb * LANES:(cb + 1) * LANES]
    per_half = HALF_GROUPS * SSM_GROUP // LANES
    for s in range(CHUNK):
        for cb in range(SSM_WIDTH // LANES):
            c0 = ((cb // per_half) * CHUNK + s) * HALF_GROUPS * SSM_GROUP + (cb % per_half) * LANES
            u_ref[:, c0:c0 + LANES] = ubuf[cb, pl.ds(s, tile // CHUNK, stride=CHUNK), :].astype(BF16)


def _in_proj(x2, gain, w_in, seq, tile=512):
    t = x2.shape[0]
    qk = 2 * N_HEADS * AUG
    kern = functools.partial(_inproj_kernel, tile=tile, seq=seq)
    full = lambda shape: pl.BlockSpec(shape, lambda i: (0, 0))
    return pl.pallas_call(
        kern, name="in_proj",
        grid=(t // tile,),
        in_specs=[pl.BlockSpec((tile, D_MODEL), lambda i: (i, 0)),
                  full((1, D_MODEL)), full(w_in.shape)],
        out_specs=[pl.BlockSpec((qk, tile), lambda i: (0, i)),
                   pl.BlockSpec((tile, qk), lambda i: (i, 0)),
                   pl.BlockSpec((N_HEADS * V_ROWS, tile), lambda i: (0, i)),
                   pl.BlockSpec((tile // CHUNK, CHUNK * SSM_WIDTH), lambda i: (i, 0))],
        out_shape=[jax.ShapeDtypeStruct((qk, t), BF16),
                   jax.ShapeDtypeStruct((t, qk), BF16),
                   jax.ShapeDtypeStruct((N_HEADS * V_ROWS, t), BF16),
                   jax.ShapeDtypeStruct((t // CHUNK, CHUNK * SSM_WIDTH), BF16)],
        scratch_shapes=[pltpu.VMEM((SSM_WIDTH // LANES, tile, LANES), F32),
                        pltpu.VMEM((2 * ATTN_WIDTH, D_MODEL), BF16), pltpu.VMEM((D_MODEL, qk + SSM_WIDTH), BF16)],
        compiler_params=_params(("arbitrary",), 48),
    )(x2, gain, w_in)


def _attn_kernel(lq1, lk1, lq2, lk2, sub_ref, q_ref, k_ref, v_ref, o_ref, m_sc, acc_sc, kn_sc, *, tq, kb, strip):
    hd = pl.program_id(1)
    qi = pl.program_id(2)
    lam = (jnp.exp(jnp.sum(lq1[...] * lk1[...], axis=-1, keepdims=True))
           - jnp.exp(jnp.sum(lq2[...] * lk2[...], axis=-1, keepdims=True)) + LAMBDA_INIT)

    @pl.when(qi == 0)
    def _():
        for m in range(2):
            kn_sc[m] = jnp.max(k_ref[:, m * AUG:(m + 1) * AUG].astype(F32), axis=0, keepdims=True)

    def score(start, nkeys, align, q_lo=0, nq=tq):
        r0 = pl.multiple_of(start, align)
        kblk = k_ref[pl.ds(r0, nkeys), :]
        return [jnp.dot(kblk[:, m * AUG:(m + 1) * AUG], q_ref[m * AUG:(m + 1) * AUG, q_lo:q_lo + nq],
                        preferred_element_type=F32) for m in range(2)]

    def step(start, nkeys, align, q_lo=0, nq=tq, key_lo=None, first=False, scores=None):
        if scores is None:
            scores = score(start, nkeys, align, q_lo, nq)
        vblk = v_ref[:, pl.ds(pl.multiple_of(start, align), nkeys)]
        for m in range(2):
            s = scores[m]
            if key_lo is not None:
                key = lax.broadcasted_iota(jnp.int32, (nkeys, nq), 0) + key_lo
                qry = lax.broadcasted_iota(jnp.int32, (nkeys, nq), 1) + q_lo
                s = jnp.where(key <= qry, s, -jnp.inf)
            if first:
                m_new = jnp.max(s, axis=0, keepdims=True)
                pv = jnp.dot(vblk, jnp.exp2(s - m_new).astype(BF16), preferred_element_type=F32)
            else:
                m_prev = m_sc[m, :, q_lo:q_lo + nq]
                m_new = jnp.maximum(m_prev, jnp.max(s, axis=0, keepdims=True))
                pv = (jnp.exp2(m_prev - m_new) * acc_sc[m, :, q_lo:q_lo + nq]
                      + jnp.dot(vblk, jnp.exp2(s - m_new).astype(BF16), preferred_element_type=F32))
            acc_sc[m, :, q_lo:q_lo + nq] = pv
            m_sc[m, :, q_lo:q_lo + nq] = m_new

    for lo in range(0, tq, strip):
        step(qi * tq + lo, strip, strip, q_lo=lo, nq=tq - lo, key_lo=lo, first=lo == 0)

    lane = lax.broadcasted_iota(jnp.int32, (1, AUG), 1)
    sub = lax.broadcasted_iota(jnp.int32, (16, 1), 0)
    slope = LOG2E * jnp.exp2(-2.0 * (jnp.full((1, 1), hd, jnp.int32) + 1).astype(F32))
    reach = jnp.zeros((1, 1), F32)
    for m in range(2):
        qrows = jnp.max(q_ref[m * AUG + HEAD_DIM:m * AUG + HEAD_DIM + 16, :].astype(F32), axis=-1, keepdims=True)
        q2 = jnp.max(jnp.where(sub == QNORM_ROW - HEAD_DIM, qrows, 0.0), axis=0, keepdims=True)
        k2 = jnp.max(jnp.where(lane == NORM_LANE, kn_sc[m], 0.0), axis=-1, keepdims=True)
        m_min = jnp.min(m_sc[m], axis=-1, keepdims=True)
        reach = jnp.maximum(reach, (jnp.sqrt(q2 * k2) * 1.001 + (UNDERFLOW_LOG2 + 0.5) - m_min) / slope)
    need = jnp.ceil((reach + (kb - 1)) / kb).astype(jnp.int32) - 1
    n_below = jnp.minimum(jnp.max(jnp.maximum(need, 0)), qi * (tq // kb))
    top = qi * tq

    def quad(i, carry):
        a, b = top - (2 + 4 * i) * kb, top - (4 + 4 * i) * kb
        sa, sb = score(a, 2 * kb, kb), score(b, 2 * kb, kb)
        step(a, 2 * kb, kb, scores=sa)
        step(b, 2 * kb, kb, scores=sb)
        return carry

    lax.fori_loop(0, n_below // 4, quad, 0)
    rest = n_below % 4
    done = n_below - rest

    @pl.when(rest >= 2)
    def _():
        step(top - (done + 2) * kb, 2 * kb, kb)

    @pl.when(rest % 2 == 1)
    def _():
        step(top - n_below * kb, kb, kb)

    l0 = acc_sc[0, VALUE_DIM:VALUE_DIM + 1, :]
    l1 = acc_sc[1, VALUE_DIM:VALUE_DIM + 1, :]
    o = acc_sc[0, :VALUE_DIM, :] / l0 - lam * (acc_sc[1, :VALUE_DIM, :] / l1)
    o = o * lax.rsqrt(jnp.mean(o * o, axis=0, keepdims=True) + RMS_EPS) * sub_ref[...] * (1.0 - LAMBDA_INIT)
    o_ref[...] = o.T.astype(BF16)


def _attention(q_t, k_aug, v_t, lq1, lk1, lq2, lk2, subln_col, batch, seq, tq=1024, kb=512, strip=512):
    tq = min(tq, seq)
    nq = seq // tq
    t = batch * seq
    small = pl.BlockSpec((1, HEAD_DIM), lambda b, h, i: (0, 0))
    kern = functools.partial(_attn_kernel, tq=tq, kb=kb, strip=strip)
    return pl.pallas_call(
        kern, name="attention",
        grid=(batch, N_HEADS, nq),
        in_specs=[small, small, small, small,
                  pl.BlockSpec((VALUE_DIM, 1), lambda b, h, i: (0, 0)),
                  pl.BlockSpec((2 * AUG, tq), lambda b, h, i: (h, b * nq + i)),
                  pl.BlockSpec((seq, 2 * AUG), lambda b, h, i: (b, h)),
                  pl.BlockSpec((V_ROWS, seq), lambda b, h, i: (h, b))],
        out_specs=pl.BlockSpec((tq, VALUE_DIM), lambda b, h, i: (b * nq + i, h)),
        out_shape=jax.ShapeDtypeStruct((t, ATTN_WIDTH), BF16),
        scratch_shapes=[pltpu.VMEM((2, 1, tq), F32), pltpu.VMEM((2, V_ROWS, tq), F32),
                        pltpu.VMEM((2, 1, AUG), F32)],
        compiler_params=_params(("arbitrary", "arbitrary", "arbitrary"), 48),
    )(lq1, lk1, lq2, lk2, subln_col, q_t, k_aug, v_t)


SCAN_COLS = 256


def _ssm_state_kernel(u_ref, bre_ref, bim_ref, are_ref, aim_ref, pre_ref, pim_ref, cre_ref, cim_ref, d_ref,
                      hre_ref, him_ref, kt_ref, wre, wim, sre, sim, *, nchunk, nsteps):
    i = pl.program_id(1)

    @pl.when(i == 0)
    def _():
        wre[...] = bre_ref[0]
        wim[...] = bim_ref[0]
        sre[...] = jnp.zeros(sre.shape, F32)
        sim[...] = jnp.zeros(sim.shape, F32)

    wr, wi = wre[...], wim[...]
    ar, ai = are_ref[0], aim_ref[0]
    wre[...] = wr * ar - wi * ai
    wim[...] = wr * ai + wi * ar

    u = u_ref[...]
    wr_b, wi_b = wr.astype(BF16), wi.astype(BF16)
    sre[...] += jnp.dot(u, wr_b, preferred_element_type=F32)
    sim[...] += jnp.dot(u, wi_b, preferred_element_type=F32)

    lag = (jnp.dot(wr_b, cre_ref[0].astype(BF16), preferred_element_type=F32)
           - jnp.dot(wi_b, cim_ref[0].astype(BF16), preferred_element_type=F32))
    r = lax.broadcasted_iota(jnp.int32, lag.shape, 0)
    c = lax.broadcasted_iota(jnp.int32, lag.shape, 1)
    skip = jnp.where((r == c) & (i == 0), d_ref[0], 0.0)
    kt_ref[0, 0] = (lag + skip).astype(BF16)

    @pl.when(i == CHUNK - 1)
    def _():
        row = lax.rem(lax.broadcasted_iota(jnp.int32, (sre.shape[0], SCAN_COLS), 0), nchunk)
        for c0 in range(0, sre.shape[1], SCAN_COLS):
            cols = slice(c0, c0 + SCAN_COLS)
            hr, hi = sre[:, cols], sim[:, cols]
            for k in range(nsteps):
                d = 1 << k
                ar, ai = pre_ref[0, k:k + 1, cols], pim_ref[0, k:k + 1, cols]
                keep = row >= d
                pr = jnp.where(keep, pltpu.roll(hr, d, 0), 0.0)
                pi = jnp.where(keep, pltpu.roll(hi, d, 0), 0.0)
                hr, hi = hr + ar * pr - ai * pi, hi + ar * pi + ai * pr
            keep = row >= 1
            hre_ref[:, cols] = jnp.where(keep, pltpu.roll(hr, 1, 0), 0.0).astype(BF16)
            him_ref[:, cols] = jnp.where(keep, pltpu.roll(hi, 1, 0), 0.0).astype(BF16)


def _ssm_state(u3, b_re, b_im, a_row_re, a_row_im, p_re, p_im, c_re, c_im, d_row, nchunk):
    nch = u3.shape[0]
    nsteps = p_re.shape[1]
    hw = HALF_GROUPS * SSM_GROUP
    sw = HALF_GROUPS * SSM_STATE
    kern = functools.partial(_ssm_state_kernel, nchunk=nchunk, nsteps=nsteps)
    per_half = lambda shape: pl.BlockSpec((1,) + shape, lambda hf, i: (hf, 0, 0))
    return pl.pallas_call(
        kern, name="ssm_state",
        grid=(2, CHUNK),
        in_specs=[pl.BlockSpec((nch, hw), lambda hf, i: (0, hf * CHUNK + CHUNK - 1 - i)),
                  per_half((hw, sw)), per_half((hw, sw)), per_half((1, sw)), per_half((1, sw)),
                  per_half((nsteps, sw)), per_half((nsteps, sw)),
                  per_half((sw, hw)), per_half((sw, hw)), per_half((1, hw))],
        out_specs=[pl.BlockSpec((nch, sw), lambda hf, i: (0, hf)),
                   pl.BlockSpec((nch, sw), lambda hf, i: (0, hf)),
                   pl.BlockSpec((1, 1, hw, hw), lambda hf, i: (i, hf, 0, 0))],
        out_shape=[jax.ShapeDtypeStruct((nch, 2 * sw), BF16)] * 2
        + [jax.ShapeDtypeStruct((CHUNK, 2, hw, hw), BF16)],
        scratch_shapes=[pltpu.VMEM((hw, sw), F32), pltpu.VMEM((hw, sw), F32),
                        pltpu.VMEM((nch, sw), F32), pltpu.VMEM((nch, sw), F32)],
        compiler_params=_params(("arbitrary", "arbitrary"), 48),
    )(u3, b_re, b_im, a_row_re, a_row_im, p_re, p_im, c_re, c_im, d_row)


def _ssm_out_kernel(u_ref, kt_ref, hre_ref, him_ref, cre_ref, cim_ref, are_ref, aim_ref, y_ref, wre, wim, toep):
    t = pl.program_id(1)
    ar, ai = are_ref[0], aim_ref[0]

    @pl.when(t == 0)
    def _():
        cr, ci = cre_ref[0], cim_ref[0]
        wre[...] = cr * ar - ci * ai
        wim[...] = cr * ai + ci * ar

    wr, wi = wre[...], wim[...]
    wre[...] = wr * ar - wi * ai
    wim[...] = wr * ai + wi * ar

    hw = kt_ref.shape[-1]
    for s in range(CHUNK):
        tile = kt_ref[jnp.maximum(t - s, 0), 0]
        toep[s * hw:(s + 1) * hw, :] = jnp.where(s <= t, tile, jnp.zeros_like(tile))

    span = TOEP_SPAN * hw
    y_ref[0] = (jnp.dot(u_ref[:, :span], toep[:span, :], preferred_element_type=F32)
                + jnp.dot(hre_ref[...], wr.astype(BF16), preferred_element_type=F32)
                - jnp.dot(him_ref[...], wi.astype(BF16), preferred_element_type=F32))
    for piece in range(1, CHUNK // TOEP_SPAN):
        @pl.when(t >= piece * TOEP_SPAN)
        def _():
            lo = piece * span
            y_ref[0] += jnp.dot(u_ref[:, lo:lo + span], toep[lo:lo + span, :], preferred_element_type=F32)


def _ssm_out(u3, kt, h_re, h_im, c_re, c_im, a_col_re, a_col_im):
    nch = u3.shape[0]
    hw = HALF_GROUPS * SSM_GROUP
    sw = HALF_GROUPS * SSM_STATE
    return pl.pallas_call(
        _ssm_out_kernel, name="ssm_out",
        grid=(2, CHUNK),
        in_specs=[pl.BlockSpec((nch, CHUNK * hw), lambda hf, t: (0, hf)),
                  pl.BlockSpec((CHUNK, 1, hw, hw), lambda hf, t: (0, hf, 0, 0)),
                  pl.BlockSpec((nch, sw), lambda hf, t: (0, hf)),
                  pl.BlockSpec((nch, sw), lambda hf, t: (0, hf)),
                  pl.BlockSpec((1, sw, hw), lambda hf, t: (hf, 0, 0)),
                  pl.BlockSpec((1, sw, hw), lambda hf, t: (hf, 0, 0)),
                  pl.BlockSpec((1, sw, 1), lambda hf, t: (hf, 0, 0)),
                  pl.BlockSpec((1, sw, 1), lambda hf, t: (hf, 0, 0))],
        out_specs=pl.BlockSpec((1, nch, hw), lambda hf, t: (t, 0, hf)),
        out_shape=jax.ShapeDtypeStruct((CHUNK, nch, SSM_WIDTH), F32),
        scratch_shapes=[pltpu.VMEM((sw, hw), F32), pltpu.VMEM((sw, hw), F32), pltpu.VMEM((CHUNK * hw, hw), BF16)],
        compiler_params=_params(("arbitrary", "arbitrary"), 48),
    )(u3, kt, h_re, h_im, c_re, c_im, a_col_re, a_col_im)


def _ssm_constants(lam_re, lam_im, log_dt, b_re, b_im, c_re, c_im, d_skip, nchunk):
    lr, li = lam_re.astype(F32), lam_im.astype(F32)
    dt = jnp.exp(log_dt.astype(F32))[:, None]

    def lam_bar_pow(k):
        mag = jnp.exp(k * lr * dt)
        return mag * jnp.cos(k * li * dt), mag * jnp.sin(k * li * dt)

    a_re, a_im = lam_bar_pow(1.0)
    den = lr * lr + li * li
    coef_re = ((a_re - 1.0) * lr + a_im * li) / den
    coef_im = (a_im * lr - (a_re - 1.0) * li) / den
    bb_re = coef_re[..., None] * b_re.astype(F32) - coef_im[..., None] * b_im.astype(F32)
    bb_im = coef_re[..., None] * b_im.astype(F32) + coef_im[..., None] * b_re.astype(F32)
    cc_re, cc_im = c_re.astype(F32), c_im.astype(F32)
    hw, sw = HALF_GROUPS * SSM_GROUP, HALF_GROUPS * SSM_STATE

    def block_diag(rows, row_group, col_group):
        wide = jnp.tile(rows, (1,) * (rows.ndim - 1) + (HALF_GROUPS,))
        r = lax.broadcasted_iota(jnp.int32, wide.shape, wide.ndim - 2) // row_group
        c = lax.broadcasted_iota(jnp.int32, wide.shape, wide.ndim - 1) // col_group
        return jnp.where(r == c, wide, 0.0)

    def b_tiles(part):
        p = part.reshape(2, HALF_GROUPS, SSM_STATE, SSM_GROUP).transpose(0, 1, 3, 2)
        return block_diag(p.reshape(2, hw, SSM_STATE), SSM_GROUP, SSM_STATE)

    def c_tiles(part):
        p = part.reshape(2, HALF_GROUPS, SSM_GROUP, SSM_STATE).transpose(0, 1, 3, 2)
        return block_diag(p.reshape(2, sw, SSM_GROUP), SSM_STATE, SSM_GROUP)

    nsteps = max(int(math.log2(nchunk)), 1)
    steps = (CHUNK * 2.0 ** jnp.arange(nsteps, dtype=F32))[:, None, None]
    st_re, st_im = lam_bar_pow(steps)
    by_half = lambda p: p.reshape(nsteps, 2, sw).transpose(1, 0, 2)
    ar_h, ai_h = a_re.reshape(2, sw), a_im.reshape(2, sw)
    return dict(d_row=d_skip.astype(F32).reshape(2, 1, hw), b_re=b_tiles(bb_re), b_im=b_tiles(bb_im),
                c_re=c_tiles(cc_re), c_im=c_tiles(cc_im),
                a_row_re=ar_h[:, None, :], a_row_im=ai_h[:, None, :],
                a_col_re=ar_h[:, :, None], a_col_im=ai_h[:, :, None],
                p_re=by_half(st_re), p_im=by_half(st_im))


def _mix_kernel(x_ref, attn_ref, y3_ref, wglu_ref, bglu_ref, gssm_ref, woa_ref, wos_ref, gmoe_ref,
                wr_ref, br_ref, x1_ref, h2_ref, info_ref, rt_ref, cnt_ref, ybuf, carry, *, tile):
    i = pl.program_id(0)

    @pl.when(i == 0)
    def _():
        carry[...] = jnp.zeros(carry.shape, F32)

    for s in range(CHUNK):
        for cb in range(SSM_WIDTH // LANES):
            ybuf[cb, pl.ds(s, tile // CHUNK, stride=CHUNK), :] = y3_ref[s, :, cb * LANES:(cb + 1) * LANES]
    y = jax.nn.gelu(jnp.concatenate([ybuf[cb] for cb in range(SSM_WIDTH // LANES)], axis=-1))
    z = jnp.dot(y.astype(BF16), wglu_ref[...], preferred_element_type=F32) + bglu_ref[...]
    y = y * jax.nn.sigmoid(z)
    ssm = _rms(y, gssm_ref[...])
    x1 = (x_ref[...] + jnp.dot(attn_ref[...], woa_ref[...], preferred_element_type=F32)
          + jnp.dot(ssm.astype(BF16), wos_ref[...], preferred_element_type=F32))
    x1_ref[...] = x1
    h2 = _rms(x1, gmoe_ref[...])
    _store_tile_rows(h2_ref, h2)

    nt = (((1,), (1,)), ((), ()))
    h_hi = h2.astype(BF16)
    h_lo = (h2 - h_hi.astype(F32)).astype(BF16)
    both = lax.dot_general(wr_ref[...], h_hi, nt, preferred_element_type=F32)
    logits = (both[:ROUTER_ROWS] + both[LANES:LANES + ROUTER_ROWS]
              + lax.dot_general(wr_ref[:ROUTER_ROWS, :], h_lo, nt, preferred_element_type=F32) + br_ref[...])
    row = lax.broadcasted_iota(jnp.int32, logits.shape, 0)
    neg = -jnp.inf
    gl = jnp.where(row < N_EXPERT_GROUPS, logits, neg)
    gmax = jnp.max(gl, axis=0, keepdims=True)
    gsel = jnp.min(jnp.where(gl == gmax, row, LANES), axis=0, keepdims=True)
    p_group = 1.0 / jnp.sum(jnp.exp(gl - gmax), axis=0, keepdims=True)
    erow = row - N_EXPERT_GROUPS
    in_grp = (erow >= 0) & (erow < N_EXPERTS) & ((erow >> 3) == gsel)
    el = jnp.where(in_grp, logits, neg)
    m1 = jnp.max(el, axis=0, keepdims=True)
    i1 = jnp.min(jnp.where(el == m1, row, LANES), axis=0, keepdims=True)
    den = jnp.sum(jnp.exp(el - m1), axis=0, keepdims=True)
    el2 = jnp.where(row == i1, neg, el)
    m2 = jnp.max(el2, axis=0, keepdims=True)
    i2 = jnp.min(jnp.where(el2 == m2, row, LANES), axis=0, keepdims=True)
    g0 = p_group / den
    g1 = p_group * jnp.exp(m2 - m1) / den

    hit0 = row == i1
    hit1 = row == i2
    onehot = jnp.where(hit0 | hit1, 1.0, 0.0)
    r = lax.broadcasted_iota(jnp.int32, (tile, tile), 0)
    c = lax.broadcasted_iota(jnp.int32, (tile, tile), 1)
    earlier = jnp.where(r < c, 1.0, 0.0).astype(BF16)
    before = jnp.dot(onehot.astype(BF16), earlier, preferred_element_type=F32) + carry[...]
    rank0 = jnp.sum(jnp.where(hit0, before, 0.0), axis=0, keepdims=True)
    rank1 = jnp.sum(jnp.where(hit1, before, 0.0), axis=0, keepdims=True)
    carry[...] += jnp.sum(onehot, axis=1, keepdims=True)
    cnt_ref[...] = jnp.broadcast_to(carry[...], cnt_ref.shape)
    r8 = lax.broadcasted_iota(jnp.int32, (8, tile), 0)
    route = jnp.where(r8 == 0, (i1 - N_EXPERT_GROUPS).astype(F32),
                      jnp.where(r8 == 1, (i2 - N_EXPERT_GROUPS).astype(F32),
                                jnp.where(r8 == 2, g0,
                                          jnp.where(r8 == 3, g1,
                                                    jnp.where(r8 == 4, rank0, jnp.where(r8 == 5, rank1, 0.0))))))
    rt_ref[...] = route
    info_ref[...] = route.T


def _mix(x2, attn, y3, wglu, bglu, gssm, wo_a, wo_s, gmoe, w_router, b_router, tile=512):
    t = x2.shape[0]
    kern = functools.partial(_mix_kernel, tile=tile)
    full = lambda shape: pl.BlockSpec(shape, lambda i: tuple(0 for _ in shape))
    return pl.pallas_call(
        kern, name="mix",
        grid=(t // tile,),
        in_specs=[pl.BlockSpec((tile, D_MODEL), lambda i: (i, 0)),
                  pl.BlockSpec((tile, ATTN_WIDTH), lambda i: (i, 0)),
                  pl.BlockSpec((CHUNK, tile // CHUNK, SSM_WIDTH), lambda i: (0, i, 0)),
                  full((SSM_WIDTH, SSM_WIDTH)), full((1, SSM_WIDTH)), full((1, SSM_WIDTH)),
                  full((ATTN_WIDTH, D_MODEL)), full((SSM_WIDTH, D_MODEL)), full((1, D_MODEL)),
                  full((2 * LANES, D_MODEL)), full((ROUTER_ROWS, 1))],
        out_specs=[pl.BlockSpec((tile, D_MODEL), lambda i: (i, 0)),
                   pl.BlockSpec((tile * ROW_TILE, LANES), lambda i: (i, 0)),
                   pl.BlockSpec((tile, 8), lambda i: (i, 0)),
                   pl.BlockSpec((8, tile), lambda i: (0, i)),
                   pl.BlockSpec((ROUTER_ROWS, LANES), lambda i: (0, 0))],
        out_shape=[jax.ShapeDtypeStruct((t, D_MODEL), F32),
                   jax.ShapeDtypeStruct((t * ROW_TILE, LANES), F32),
                   jax.ShapeDtypeStruct((t, 8), F32),
                   jax.ShapeDtypeStruct((8, t), F32),
                   jax.ShapeDtypeStruct((ROUTER_ROWS, LANES), F32)],
        scratch_shapes=[pltpu.VMEM((SSM_WIDTH // LANES, tile, LANES), F32), pltpu.VMEM((ROUTER_ROWS, 1), F32)],
        compiler_params=_params(("arbitrary",), 48),
    )(x2, attn, y3, wglu, bglu, gssm, wo_a, wo_s, gmoe, w_router, b_router)


def _rows_at(ref, row, n_rows=1):
    return ref.at[pl.ds(pl.multiple_of(row * ROW_TILE, ROW_TILE), n_rows * ROW_TILE), :]


def _row_copy(src, s, dst, d, sem):
    return pltpu.make_async_copy(_rows_at(src, s), _rows_at(dst, d), sem)


UNROLL = 4
COMBINE_GROUPS = 16
PAD_SIZES = tuple(1 << b for b in reversed(range(ROW_BLOCK.bit_length() - 1)))


def _dispatch_kernel(dest_ref, pad_start_ref, pad_len_ref, h_ref, rows_out, zbuf, sem, zsem, *, tile, n_tok,
                     n_blocks):
    i = pl.program_id(0)
    base = i * tile

    def zero_fill(e, start):
        off, rem = pad_start_ref[e], pad_len_ref[e]
        for size in PAD_SIZES:
            @pl.when((rem & size) != 0)
            def _():
                cp = pltpu.make_async_copy(_rows_at(zbuf, 0, size), _rows_at(rows_out, off, size), zsem)
                cp.start() if start else cp.wait()
            off = off + (rem & size)

    def zero_tail(start):
        used = (pad_start_ref[N_EXPERTS - 1] + pad_len_ref[N_EXPERTS - 1]) // ROW_BLOCK

        def blk(b, carry):
            for half in range(ROW_BLOCK // PAD_SIZES[0]):
                cp = pltpu.make_async_copy(zbuf, _rows_at(rows_out, b * ROW_BLOCK + half * PAD_SIZES[0],
                                                          PAD_SIZES[0]), zsem)
                cp.start() if start else cp.wait()
            return carry

        lax.fori_loop(used, n_blocks, blk, 0)

    @pl.when(i == 0)
    def _():
        zbuf[...] = jnp.zeros(zbuf.shape, F32)
        lax.fori_loop(0, N_EXPERTS, lambda e, c: (zero_fill(e, True), c)[1], 0)
        zero_tail(True)

    def issue(tb, carry):
        for k in range(UNROLL):
            t = tb * UNROLL + k
            for j in range(2):
                _row_copy(h_ref, t, rows_out, dest_ref[j * n_tok + base + t], sem).start(priority=j)
        return carry

    def drain(tb, carry):
        for _ in range(2 * UNROLL):
            _row_copy(h_ref, 0, rows_out, 0, sem).wait()
        return carry

    lax.fori_loop(0, tile // UNROLL, issue, 0)
    lax.fori_loop(0, tile // UNROLL, drain, 0)

    @pl.when(i == 0)
    def _():
        lax.fori_loop(0, N_EXPERTS, lambda e, c: (zero_fill(e, False), c)[1], 0)
        zero_tail(False)


def _dispatch(dest_flat, pad_start, pad_len, h2, n_rows, tile=1024):
    t = h2.shape[0] // ROW_TILE
    kern = functools.partial(_dispatch_kernel, tile=tile, n_tok=t, n_blocks=n_rows // ROW_BLOCK)
    grid_spec = pltpu.PrefetchScalarGridSpec(
        num_scalar_prefetch=3, grid=(t // tile,),
        in_specs=[pl.BlockSpec((tile * ROW_TILE, LANES), lambda i, *_: (i, 0))],
        out_specs=pl.BlockSpec(memory_space=pl.ANY),
        scratch_shapes=[pltpu.VMEM((PAD_SIZES[0] * ROW_TILE, LANES), F32),
                        pltpu.SemaphoreType.DMA(()), pltpu.SemaphoreType.DMA(())])
    return pl.pallas_call(
        kern, name="dispatch", grid_spec=grid_spec,
        out_shape=jax.ShapeDtypeStruct((n_rows * ROW_TILE, LANES), F32),
        compiler_params=_params(("arbitrary",), 32),
    )(dest_flat, pad_start, pad_len, h2)


def _expert_kernel(be_ref, nu_ref, first_ref, next_ref, slot_ref, rows_ref, wg_hbm, wu_hbm, wd_hbm, out_ref,
                   wg_f, wu_f, wd_f, wg_b, wu_b, wd_b, sems):
    i = pl.program_id(0)

    def fetch(e, slot):
        return (pltpu.make_async_copy(wg_hbm.at[e], wg_f.at[slot], sems.at[slot, 0]),
                pltpu.make_async_copy(wu_hbm.at[e], wu_f.at[slot], sems.at[slot, 1]),
                pltpu.make_async_copy(wd_hbm.at[e], wd_f.at[slot], sems.at[slot, 2]))

    @pl.when((i == 0) & (nu_ref[0] > 0))
    def _():
        for cp in fetch(be_ref[0], 0):
            cp.start()

    @pl.when((first_ref[i] == 1) & (i < nu_ref[0]))
    def _():
        slot = slot_ref[i]

        @pl.when(next_ref[i] >= 0)
        def _():
            for cp in fetch(next_ref[i], 1 - slot):
                cp.start()

        for cp in fetch(be_ref[i], slot):
            cp.wait()
        wg_b[...] = wg_f[slot].astype(BF16)
        wu_b[...] = wu_f[slot].astype(BF16)
        wd_b[...] = wd_f[slot].astype(BF16)

    @pl.when(i < nu_ref[0])
    def _():
        xb = _load_tile_rows(rows_ref).astype(BF16)
        gate = jnp.dot(xb, wg_b[...], preferred_element_type=F32)
        up = jnp.dot(xb, wu_b[...], preferred_element_type=F32)
        hid = (gate * jax.nn.sigmoid(gate) * up).astype(BF16)
        _store_tile_rows(out_ref, jnp.dot(hid, wd_b[...], preferred_element_type=F32))

    @pl.when(i >= nu_ref[0])
    def _():
        out_ref[...] = jnp.zeros(out_ref.shape, F32)


def _experts(block_e, n_used, seg_first, seg_next, seg_slot, rows, w_gate, w_up, w_down):
    nb = rows.shape[0] // (ROW_BLOCK * ROW_TILE)
    last = lambda i, be, nu, *_: jnp.maximum(jnp.minimum(i, nu[0] - 1), 0)
    hbm = pl.BlockSpec(memory_space=pl.ANY)
    grid_spec = pltpu.PrefetchScalarGridSpec(
        num_scalar_prefetch=5, grid=(nb,),
        in_specs=[pl.BlockSpec((ROW_BLOCK * ROW_TILE, LANES), lambda i, *s: (last(i, *s), 0)), hbm, hbm, hbm],
        out_specs=pl.BlockSpec((ROW_BLOCK * ROW_TILE, LANES), lambda i, *s: (i, 0)),
        scratch_shapes=[pltpu.VMEM((2, D_MODEL, EXPERT_FF), F32), pltpu.VMEM((2, D_MODEL, EXPERT_FF), F32),
                        pltpu.VMEM((2, EXPERT_FF, D_MODEL), F32),
                        pltpu.VMEM((D_MODEL, EXPERT_FF), BF16), pltpu.VMEM((D_MODEL, EXPERT_FF), BF16),
                        pltpu.VMEM((EXPERT_FF, D_MODEL), BF16), pltpu.SemaphoreType.DMA((2, 3))])
    return pl.pallas_call(
        _expert_kernel, name="experts", grid_spec=grid_spec,
        out_shape=jax.ShapeDtypeStruct(rows.shape, F32),
        compiler_params=_params(("arbitrary",), 48),
    )(block_e, n_used, seg_first, seg_next, seg_slot, rows, w_gate, w_up, w_down)


def _combine_kernel(dest_ref, x1_ref, info_ref, gfin_ref, rows_ref, o_ref, buf0, buf1, sems, *, tile, n_tok):
    i = pl.program_id(0)
    slot = i % 2

    def gather(step, to_slot):
        base = step * tile
        b0, b1, sem = buf0.at[to_slot], buf1.at[to_slot], sems.at[to_slot]

        def issue(tb, carry):
            for k in range(UNROLL):
                t = tb * UNROLL + k
                _row_copy(rows_ref, dest_ref[base + t], b0, t, sem).start(priority=0)
                _row_copy(rows_ref, dest_ref[n_tok + base + t], b1, t, sem).start(priority=1)
            return carry

        lax.fori_loop(0, tile // UNROLL, issue, 0)

    def drain(of_slot):
        b0, b1, sem = buf0.at[of_slot], buf1.at[of_slot], sems.at[of_slot]

        def wait(tb, carry):
            for _ in range(UNROLL):
                _row_copy(rows_ref, 0, b0, 0, sem).wait()
                _row_copy(rows_ref, 0, b1, 0, sem).wait()
            return carry

        lax.fori_loop(0, tile // UNROLL, wait, 0)

    @pl.when(i == 0)
    def _():
        gather(0, 0)

    drain(slot)

    last = pl.num_programs(0) - 1
    nxt_base = jnp.minimum(i + 1, last) * tile
    b0, b1 = buf0.at[slot], buf1.at[slot]
    n0, n1, nsem = buf0.at[1 - slot], buf1.at[1 - slot], sems.at[1 - slot]
    group = tile // COMBINE_GROUPS
    for g in range(COMBINE_GROUPS):
        lo = g * group
        for t in range(lo, lo + group):
            _row_copy(rows_ref, dest_ref[nxt_base + t], n0, t, nsem).start(priority=0)
            _row_copy(rows_ref, dest_ref[n_tok + nxt_base + t], n1, t, nsem).start(priority=1)
        info = info_ref[lo:lo + group, :]
        x2 = (x1_ref[lo:lo + group, :] + info[:, 2:3] * _load_tile_rows(b0, lo, group)
              + info[:, 3:4] * _load_tile_rows(b1, lo, group))
        o_ref[lo:lo + group, :] = _rms(x2, gfin_ref[...])

    @pl.when(i == last)
    def _():
        drain(1 - slot)


def _combine(dest_flat, x1, info, gfin, out_rows, tile=512):
    t = x1.shape[0]
    kern = functools.partial(_combine_kernel, tile=tile, n_tok=t)
    grid_spec = pltpu.PrefetchScalarGridSpec(
        num_scalar_prefetch=1, grid=(t // tile,),
        in_specs=[pl.BlockSpec((tile, D_MODEL), lambda i, d: (i, 0)),
                  pl.BlockSpec((tile, 8), lambda i, d: (i, 0)),
                  pl.BlockSpec((1, D_MODEL), lambda i, d: (0, 0)),
                  pl.BlockSpec(memory_space=pl.ANY)],
        out_specs=pl.BlockSpec((tile, D_MODEL), lambda i, d: (i, 0)),
        scratch_shapes=[pltpu.VMEM((2, tile * ROW_TILE, LANES), F32), pltpu.VMEM((2, tile * ROW_TILE, LANES), F32),
                        pltpu.SemaphoreType.DMA((2,))])
    return pl.pallas_call(
        kern, name="combine", grid_spec=grid_spec,
        out_shape=jax.ShapeDtypeStruct((t, D_MODEL), F32),
        compiler_params=_params(("arbitrary",), 32),
    )(dest_flat, x1, info, gfin, out_rows)


def kernel(x, norm_attn, w_in, lambda_q1, lambda_k1, lambda_q2, lambda_k2, attn_subln, ssm_lam_re, ssm_lam_im, ssm_log_dt, ssm_b_re, ssm_b_im, ssm_c_re, ssm_c_im, ssm_d, w_glu, b_glu, ssm_norm, w_out, norm_moe, w_router_group, b_router_group, w_router_expert, b_router_expert, w_gate, w_up, w_down, norm_final):
    batch, seq, d = x.shape
    t = batch * seq
    nchunk = seq // CHUNK
    x2 = x.reshape(t, d)
    l = 0

    q_t, k_aug, v_t, u3 = _in_proj(x2, norm_attn[l][None], w_in[l], seq)
    attn = _attention(q_t, k_aug, v_t, lambda_q1[l][None], lambda_k1[l][None], lambda_q2[l][None],
                      lambda_k2[l][None], attn_subln[l][:, None], batch, seq)

    sc = _ssm_constants(ssm_lam_re[l], ssm_lam_im[l], ssm_log_dt[l], ssm_b_re[l], ssm_b_im[l],
                        ssm_c_re[l], ssm_c_im[l], ssm_d[l], nchunk)
    h_re, h_im, kt = _ssm_state(u3, sc["b_re"], sc["b_im"], sc["a_row_re"], sc["a_row_im"], sc["p_re"], sc["p_im"],
                                sc["c_re"], sc["c_im"], sc["d_row"], nchunk)
    y3 = _ssm_out(u3, kt, h_re, h_im, sc["c_re"], sc["c_im"], sc["a_col_re"], sc["a_col_im"])

    w_router = jnp.concatenate([w_router_group[l], w_router_expert[l]], axis=1).astype(F32).T
    w_router = jnp.pad(w_router, ((0, LANES - w_router.shape[0]), (0, 0)))
    w_router_hi = w_router.astype(BF16)
    w_router = jnp.concatenate([w_router_hi, (w_router - w_router_hi.astype(F32)).astype(BF16)], axis=0)
    b_router = jnp.concatenate([b_router_group[l], b_router_expert[l]]).astype(F32)
    b_router = jnp.pad(b_router, (0, ROUTER_ROWS - b_router.shape[0]))[:, None]
    x1, h2, info, route_t, cnt = _mix(x2, attn, y3, w_glu[l].astype(BF16), b_glu[l][None], ssm_norm[l][None],
                                      w_out[l][:ATTN_WIDTH].astype(BF16), w_out[l][ATTN_WIDTH:].astype(BF16),
                                      norm_moe[l][None], w_router, b_router)

    experts = route_t[0:2].astype(jnp.int32)
    ranks = route_t[4:6].astype(jnp.int32)
    counts = cnt[N_EXPERT_GROUPS:N_EXPERT_GROUPS + N_EXPERTS, 0].astype(jnp.int32)
    padded = ((counts + ROW_BLOCK - 1) // ROW_BLOCK) * ROW_BLOCK
    ids = jnp.arange(N_EXPERTS, dtype=jnp.int32)
    pend = jnp.sum(jnp.where(ids[None, :] <= ids[:, None], padded[None, :], 0), axis=1)
    pstart = pend - padded
    dest = ranks + jnp.sum(jnp.where(experts[:, None, :] == ids[None, :, None], pstart[None, :, None], 0), axis=1)
    dest = dest.reshape(-1)
    n_rows = ((2 * t + N_EXPERTS * (ROW_BLOCK - 1) + ROW_BLOCK - 1) // ROW_BLOCK) * ROW_BLOCK
    nb = n_rows // ROW_BLOCK
    n_used = (pend[-1] // ROW_BLOCK).astype(jnp.int32)
    blk = jnp.minimum(jnp.arange(nb, dtype=jnp.int32), n_used - 1) * ROW_BLOCK
    block_e = jnp.minimum(jnp.sum((pend[None, :] <= blk[:, None]).astype(jnp.int32), axis=1), N_EXPERTS - 1)

    bidx = jnp.arange(nb, dtype=jnp.int32)
    seg_first = ((bidx == 0) | (block_e != jnp.concatenate([block_e[:1], block_e[:-1]]))).astype(jnp.int32)
    seg_slot = (jnp.sum(jnp.where(bidx[None, :] <= bidx[:, None], seg_first[None, :], 0), axis=1) - 1) & 1
    seg_end = jnp.sum(jnp.where(block_e[:, None] == ids[None, :], pend[None, :], 0), axis=1) // ROW_BLOCK
    after = jnp.sum(jnp.where(bidx[None, :] == seg_end[:, None], block_e[None, :], 0), axis=1)
    seg_next = jnp.where(seg_end < n_used, after, -1).astype(jnp.int32)

    rows = _dispatch(dest, pstart + counts, padded - counts, h2, n_rows)
    out_rows = _experts(block_e, n_used[None], seg_first, seg_next, seg_slot.astype(jnp.int32), rows,
                        w_gate[l], w_up[l], w_down[l])
    out = _combine(dest, x1, info, norm_final[None], out_rows)
    return out.reshape(batch, seq, d)
```

```python
import functools
import math

import jax
import jax.numpy as jnp
import numpy as np
from jax import lax
from jax.experimental import pallas as pl
from jax.experimental.pallas import tpu as pltpu

F32 = jnp.float32
BF16 = jnp.bfloat16

D_MODEL = 1024
N_HEADS = 4
HEAD_DIM = 64
VALUE_DIM = 128
ATTN_WIDTH = 512
SSM_WIDTH = 512
SSM_GROUP = 16
N_GROUPS = 32
SSM_STATE = 64
N_EXPERT_GROUPS = 4
EXPERTS_PER_GROUP = 8
N_EXPERTS = 32
EXPERT_FF = 512
RMS_EPS = 1e-6
LAMBDA_INIT = 0.8 - 0.6 * math.exp(-0.3 * 0)
LOG2E = math.log2(math.e)

CHUNK = 16
HALF_GROUPS = 16
TOEP_SPAN = 4
ROUTER_ROWS = 48
LANES = 128
AUG = 128
NORM_LANE = 70
QNORM_ROW = 71
V_ROWS = VALUE_DIM + 16
UNDERFLOW_LOG2 = 152.0
ROW_BLOCK = 256
ROW_TILE = D_MODEL // 128
VMEM_LIMIT_CAP = 56 * 1024 * 1024


def _params(dims, vmem_mb):
    return pltpu.CompilerParams(dimension_semantics=dims,
                                vmem_limit_bytes=min(vmem_mb * 1024 * 1024, VMEM_LIMIT_CAP))


def _rms(x, gain):
    return x * lax.rsqrt(jnp.mean(x * x, axis=-1, keepdims=True) + RMS_EPS) * gain


def _store_tile_rows(ref, val):
    n = val.shape[0]
    for c in range(ROW_TILE):
        ref[pl.ds(c, n, stride=ROW_TILE), :] = val[:, c * LANES:(c + 1) * LANES]


def _load_tile_rows(ref, first=0, n=None):
    n = ref.shape[0] // ROW_TILE if n is None else n
    return jnp.concatenate([ref[pl.ds(first * ROW_TILE + c, n, stride=ROW_TILE), :] for c in range(ROW_TILE)],
                           axis=-1)


def _split3(val):
    hi = val.astype(BF16).astype(F32)
    r1 = val - hi
    mid = r1.astype(BF16).astype(F32)
    lo = r1 - mid
    return hi, mid, lo


def _inproj_kernel(x_ref, g_ref, w_ref, q_ref, k_ref, v_ref, u_ref, ubuf, wqv_ref, wku_ref, *, tile, seq):
    i = pl.program_id(0)

    @pl.when(i == 0)
    def _():
        for part, col0 in ((0, 0), (1, 2 * ATTN_WIDTH)):
            for cb in range(ATTN_WIDTH // LANES):
                r0, c0 = part * ATTN_WIDTH + cb * LANES, col0 + cb * LANES
                wqv_ref[r0:r0 + LANES, :] = w_ref[:, c0:c0 + LANES].T.astype(BF16)
        for blk in range(2 * N_HEADS):
            src = ATTN_WIDTH + blk * HEAD_DIM
            wku_ref[:, blk * AUG:blk * AUG + HEAD_DIM] = w_ref[:, src:src + HEAD_DIM].astype(BF16)
            wku_ref[:, blk * AUG + HEAD_DIM:(blk + 1) * AUG] = jnp.zeros((D_MODEL, AUG - HEAD_DIM), BF16)
        wku_ref[:, 2 * N_HEADS * AUG:] = w_ref[:, 3 * ATTN_WIDTH:].astype(BF16)

    h = _rms(x_ref[...], g_ref[...]).astype(BF16)
    nt = (((1,), (1,)), ((), ()))
    kcols = 2 * N_HEADS * AUG
    qt = lax.dot_general(wqv_ref[:ATTN_WIDTH, :], h, nt, preferred_element_type=F32)
    kp = jnp.dot(h, wku_ref[:, :kcols], preferred_element_type=F32)
    vt = lax.dot_general(wqv_ref[ATTN_WIDTH:, :], h, nt, preferred_element_type=F32)
    up = jnp.dot(h, wku_ref[:, kcols:], preferred_element_type=F32)
    ones_row = jnp.where(lax.broadcasted_iota(jnp.int32, (V_ROWS - VALUE_DIM, tile), 0) == 0, 1.0, 0.0)
    for hd in range(N_HEADS):
        v_ref[hd * V_ROWS:hd * V_ROWS + VALUE_DIM, :] = vt[hd * VALUE_DIM:(hd + 1) * VALUE_DIM, :].astype(BF16)
        v_ref[hd * V_ROWS + VALUE_DIM:(hd + 1) * V_ROWS, :] = ones_row.astype(BF16)
    pos0 = lax.rem(i * tile, seq)
    pos_k = (pos0 + lax.broadcasted_iota(jnp.int32, (tile, AUG), 0)).astype(F32)
    pos_q = (pos0 + lax.broadcasted_iota(jnp.int32, (1, tile), 1)).astype(F32)
    lane = lax.broadcasted_iota(jnp.int32, (tile, AUG), 1)
    srow = lax.broadcasted_iota(jnp.int32, (AUG - HEAD_DIM, tile), 0)
    qscale = HEAD_DIM ** -0.5 * LOG2E
    for hd in range(N_HEADS):
        slope = 2.0 ** (-8.0 * (hd + 1) / N_HEADS) * LOG2E
        hi, mid, lo = _split3(pos_k * slope)
        k_add = jnp.where(lane == 64, hi,
                          jnp.where(lane == 65, mid,
                                    jnp.where(lane == 66, lo, jnp.where((lane >= 67) & (lane < 70), 1.0, 0.0))))
        hi, mid, lo = _split3(pos_q * slope)
        q_add = jnp.where(srow < 3, 1.0,
                          jnp.where(srow == 3, -hi, jnp.where(srow == 4, -mid, jnp.where(srow == 5, -lo, 0.0))))
        for m in range(2):
            c0 = (hd * 2 + m) * AUG
            qb = (qt[c0 // 2:c0 // 2 + HEAD_DIM, :] * qscale).astype(BF16)
            q_ref[c0:c0 + HEAD_DIM, :] = qb
            qr = qb.astype(F32)
            qnorm2 = jnp.sum(qr * qr, axis=0, keepdims=True) * (1.0 + 2.0 ** -6)
            q_ref[c0 + HEAD_DIM:c0 + AUG, :] = (
                q_add + jnp.where(srow == QNORM_ROW - HEAD_DIM, qnorm2, 0.0)).astype(BF16)
            kb = kp[:, c0:c0 + AUG]
            kr = kb.astype(BF16).astype(F32)
            norm2 = jnp.sum(kr * kr, axis=-1, keepdims=True) * (1.0 + 2.0 ** -6)
            k_ref[:, c0:c0 + AUG] = (kb + k_add + jnp.where(lane == NORM_LANE, norm2, 0.0)).astype(BF16)
    for cb in range(SSM_WIDTH // LANES):
        ubuf[cb] = up[:, cb * LANES:(cb + 1) * LANES]
    per_half = HALF_GROUPS * SSM_GROUP // LANES
    for s in range(CHUNK):
        for cb in range(SSM_WIDTH // LANES):
            c0 = ((cb // per_half) * CHUNK + s) * HALF_GROUPS * SSM_GROUP + (cb % per_half) * LANES
            u_ref[:, c0:c0 + LANES] = ubuf[cb, pl.ds(s, tile // CHUNK, stride=CHUNK), :].astype(BF16)


def _in_proj(x2, gain, w_in, seq, tile=512):
    t = x2.shape[0]
    qk = 2 * N_HEADS * AUG
    kern = functools.partial(_inproj_kernel, tile=tile, seq=seq)
    full = lambda shape: pl.BlockSpec(shape, lambda i: (0, 0))
    return pl.pallas_call(
        kern, name="in_proj",
        grid=(t // tile,),
        in_specs=[pl.BlockSpec((tile, D_MODEL), lambda i: (i, 0)),
                  full((1, D_MODEL)), full(w_in.shape)],
        out_specs=[pl.BlockSpec((qk, tile), lambda i: (0, i)),
                   pl.BlockSpec((tile, qk), lambda i: (i, 0)),
                   pl.BlockSpec((N_HEADS * V_ROWS, tile), lambda i: (0, i)),
                   pl.BlockSpec((tile // CHUNK, CHUNK * SSM_WIDTH), lambda i: (i, 0))],
        out_shape=[jax.ShapeDtypeStruct((qk, t), BF16),
                   jax.ShapeDtypeStruct((t, qk), BF16),
                   jax.ShapeDtypeStruct((N_HEADS * V_ROWS, t), BF16),
                   jax.ShapeDtypeStruct((t // CHUNK, CHUNK * SSM_WIDTH), BF16)],
        scratch_shapes=[pltpu.VMEM((SSM_WIDTH // LANES, tile, LANES), F32),
                        pltpu.VMEM((2 * ATTN_WIDTH, D_MODEL), BF16), pltpu.VMEM((D_MODEL, qk + SSM_WIDTH), BF16)],
        compiler_params=_params(("arbitrary",), 48),
    )(x2, gain, w_in)


def _attn_kernel(lq1, lk1, lq2, lk2, sub_ref, q_ref, k_ref, v_ref, o_ref, m_sc, acc_sc, kn_sc, *, tq, kb, strip):
    hd = pl.program_id(1)
    qi = pl.program_id(2)
    lam = (jnp.exp(jnp.sum(lq1[...] * lk1[...], axis=-1, keepdims=True))
           - jnp.exp(jnp.sum(lq2[...] * lk2[...], axis=-1, keepdims=True)) + LAMBDA_INIT)

    @pl.when(qi == 0)
    def _():
        for m in range(2):
            kn_sc[m] = jnp.max(k_ref[:, m * AUG:(m + 1) * AUG].astype(F32), axis=0, keepdims=True)

    def score(start, nkeys, align, q_lo=0, nq=tq):
        r0 = pl.multiple_of(start, align)
        kblk = k_ref[pl.ds(r0, nkeys), :]
        return [jnp.dot(kblk[:, m * AUG:(m + 1) * AUG], q_ref[m * AUG:(m + 1) * AUG, q_lo:q_lo + nq],
                        preferred_element_type=F32) for m in range(2)]

    def step(start, nkeys, align, q_lo=0, nq=tq, key_lo=None, first=False, scores=None):
        if scores is None:
            scores = score(start, nkeys, align, q_lo, nq)
        vblk = v_ref[:, pl.ds(pl.multiple_of(start, align), nkeys)]
        for m in range(2):
            s = scores[m]
            if key_lo is not None:
                key = lax.broadcasted_iota(jnp.int32, (nkeys, nq), 0) + key_lo
                qry = lax.broadcasted_iota(jnp.int32, (nkeys, nq), 1) + q_lo
                s = jnp.where(key <= qry, s, -jnp.inf)
            if first:
                m_new = jnp.max(s, axis=0, keepdims=True)
                pv = jnp.dot(vblk, jnp.exp2(s - m_new).astype(BF16), preferred_element_type=F32)
            else:
                m_prev = m_sc[m, :, q_lo:q_lo + nq]
                m_new = jnp.maximum(m_prev, jnp.max(s, axis=0, keepdims=True))
                pv = (jnp.exp2(m_prev - m_new) * acc_sc[m, :, q_lo:q_lo + nq]
                      + jnp.dot(vblk, jnp.exp2(s - m_new).astype(BF16), preferred_element_type=F32))
            acc_sc[m, :, q_lo:q_lo + nq] = pv
            m_sc[m, :, q_lo:q_lo + nq] = m_new

    for lo in range(0, tq, strip):
        step(qi * tq + lo, strip, strip, q_lo=lo, nq=tq - lo, key_lo=lo, first=lo == 0)

    lane = lax.broadcasted_iota(jnp.int32, (1, AUG), 1)
    sub = lax.broadcasted_iota(jnp.int32, (16, 1), 0)
    slope = LOG2E * jnp.exp2(-2.0 * (jnp.full((1, 1), hd, jnp.int32) + 1).astype(F32))
    reach = jnp.zeros((1, 1), F32)
    for m in range(2):
        qrows = jnp.max(q_ref[m * AUG + HEAD_DIM:m * AUG + HEAD_DIM + 16, :].astype(F32), axis=-1, keepdims=True)
        q2 = jnp.max(jnp.where(sub == QNORM_ROW - HEAD_DIM, qrows, 0.0), axis=0, keepdims=True)
        k2 = jnp.max(jnp.where(lane == NORM_LANE, kn_sc[m], 0.0), axis=-1, keepdims=True)
        m_min = jnp.min(m_sc[m], axis=-1, keepdims=True)
        reach = jnp.maximum(reach, (jnp.sqrt(q2 * k2) * 1.001 + (UNDERFLOW_LOG2 + 0.5) - m_min) / slope)
    need = jnp.ceil((reach + (kb - 1)) / kb).astype(jnp.int32) - 1
    n_below = jnp.minimum(jnp.max(jnp.maximum(need, 0)), qi * (tq // kb))
    top = qi * tq

    def quad(i, carry):
        a, b = top - (2 + 4 * i) * kb, top - (4 + 4 * i) * kb
        sa, sb = score(a, 2 * kb, kb), score(b, 2 * kb, kb)
        step(a, 2 * kb, kb, scores=sa)
        step(b, 2 * kb, kb, scores=sb)
        return carry

    lax.fori_loop(0, n_below // 4, quad, 0)
    rest = n_below % 4
    done = n_below - rest

    @pl.when(rest >= 2)
    def _():
        step(top - (done + 2) * kb, 2 * kb, kb)

    @pl.when(rest % 2 == 1)
    def _():
        step(top - n_below * kb, kb, kb)

    l0 = acc_sc[0, VALUE_DIM:VALUE_DIM + 1, :]
    l1 = acc_sc[1, VALUE_DIM:VALUE_DIM + 1, :]
    o = acc_sc[0, :VALUE_DIM, :] / l0 - lam * (acc_sc[1, :VALUE_DIM, :] / l1)
    o = o * lax.rsqrt(jnp.mean(o * o, axis=0, keepdims=True) + RMS_EPS) * sub_ref[...] * (1.0 - LAMBDA_INIT)
    o_ref[...] = o.T.astype(BF16)


def _attention(q_t, k_aug, v_t, lq1, lk1, lq2, lk2, subln_col, batch, seq, tq=1024, kb=512, strip=512):
    tq = min(tq, seq)
    nq = seq // tq
    t = batch * seq
    small = pl.BlockSpec((1, HEAD_DIM), lambda b, h, i: (0, 0))
    kern = functools.partial(_attn_kernel, tq=tq, kb=kb, strip=strip)
    return pl.pallas_call(
        kern, name="attention",
        grid=(batch, N_HEADS, nq),
        in_specs=[small, small, small, small,
                  pl.BlockSpec((VALUE_DIM, 1), lambda b, h, i: (0, 0)),
                  pl.BlockSpec((2 * AUG, tq), lambda b, h, i: (h, b * nq + i)),
                  pl.BlockSpec((seq, 2 * AUG), lambda b, h, i: (b, h)),
                  pl.BlockSpec((V_ROWS, seq), lambda b, h, i: (h, b))],
        out_specs=pl.BlockSpec((tq, VALUE_DIM), lambda b, h, i: (b * nq + i, h)),
        out_shape=jax.ShapeDtypeStruct((t, ATTN_WIDTH), BF16),
        scratch_shapes=[pltpu.VMEM((2, 1, tq), F32), pltpu.VMEM((2, V_ROWS, tq), F32),
                        pltpu.VMEM((2, 1, AUG), F32)],
        compiler_params=_params(("arbitrary", "arbitrary", "arbitrary"), 48),
    )(lq1, lk1, lq2, lk2, subln_col, q_t, k_aug, v_t)


SCAN_COLS = 256


def _ssm_state_kernel(u_ref, bre_ref, bim_ref, are_ref, aim_ref, pre_ref, pim_ref, cre_ref, cim_ref, d_ref,
                      hre_ref, him_ref, kt_ref, wre, wim, sre, sim, *, nchunk, nsteps):
    i = pl.program_id(1)

    @pl.when(i == 0)
    def _():
        wre[...] = bre_ref[0]
        wim[...] = bim_ref[0]
        sre[...] = jnp.zeros(sre.shape, F32)
        sim[...] = jnp.zeros(sim.shape, F32)

    wr, wi = wre[...], wim[...]
    ar, ai = are_ref[0], aim_ref[0]
    wre[...] = wr * ar - wi * ai
    wim[...] = wr * ai + wi * ar

    u = u_ref[...]
    wr_b, wi_b = wr.astype(BF16), wi.astype(BF16)
    sre[...] += jnp.dot(u, wr_b, preferred_element_type=F32)
    sim[...] += jnp.dot(u, wi_b, preferred_element_type=F32)

    lag = (jnp.dot(wr_b, cre_ref[0].astype(BF16), preferred_element_type=F32)
           - jnp.dot(wi_b, cim_ref[0].astype(BF16), preferred_element_type=F32))
    r = lax.broadcasted_iota(jnp.int32, lag.shape, 0)
    c = lax.broadcasted_iota(jnp.int32, lag.shape, 1)
    skip = jnp.where((r == c) & (i == 0), d_ref[0], 0.0)
    kt_ref[0, 0] = (lag + skip).astype(BF16)

    @pl.when(i == CHUNK - 1)
    def _():
        row = lax.rem(lax.broadcasted_iota(jnp.int32, (sre.shape[0], SCAN_COLS), 0), nchunk)
        for c0 in range(0, sre.shape[1], SCAN_COLS):
            cols = slice(c0, c0 + SCAN_COLS)
            hr, hi = sre[:, cols], sim[:, cols]
            for k in range(nsteps):
                d = 1 << k
                ar, ai = pre_ref[0, k:k + 1, cols], pim_ref[0, k:k + 1, cols]
                keep = row >= d
                pr = jnp.where(keep, pltpu.roll(hr, d, 0), 0.0)
                pi = jnp.where(keep, pltpu.roll(hi, d, 0), 0.0)
                hr, hi = hr + ar * pr - ai * pi, hi + ar * pi + ai * pr
            keep = row >= 1
            hre_ref[:, cols] = jnp.where(keep, pltpu.roll(hr, 1, 0), 0.0).astype(BF16)
            him_ref[:, cols] = jnp.where(keep, pltpu.roll(hi, 1, 0), 0.0).astype(BF16)


def _ssm_state(u3, b_re, b_im, a_row_re, a_row_im, p_re, p_im, c_re, c_im, d_row, nchunk):
    nch = u3.shape[0]
    nsteps = p_re.shape[1]
    hw = HALF_GROUPS * SSM_GROUP
    sw = HALF_GROUPS * SSM_STATE
    kern = functools.partial(_ssm_state_kernel, nchunk=nchunk, nsteps=nsteps)
    per_half = lambda shape: pl.BlockSpec((1,) + shape, lambda hf, i: (hf, 0, 0))
    return pl.pallas_call(
        kern, name="ssm_state",
        grid=(2, CHUNK),
        in_specs=[pl.BlockSpec((nch, hw), lambda hf, i: (0, hf * CHUNK + CHUNK - 1 - i)),
                  per_half((hw, sw)), per_half((hw, sw)), per_half((1, sw)), per_half((1, sw)),
                  per_half((nsteps, sw)), per_half((nsteps, sw)),
                  per_half((sw, hw)), per_half((sw, hw)), per_half((1, hw))],
        out_specs=[pl.BlockSpec((nch, sw), lambda hf, i: (0, hf)),
                   pl.BlockSpec((nch, sw), lambda hf, i: (0, hf)),
                   pl.BlockSpec((1, 1, hw, hw), lambda hf, i: (i, hf, 0, 0))],
        out_shape=[jax.ShapeDtypeStruct((nch, 2 * sw), BF16)] * 2
        + [jax.ShapeDtypeStruct((CHUNK, 2, hw, hw), BF16)],
        scratch_shapes=[pltpu.VMEM((hw, sw), F32), pltpu.VMEM((hw, sw), F32),
                        pltpu.VMEM((nch, sw), F32), pltpu.VMEM((nch, sw), F32)],
        compiler_params=_params(("arbitrary", "arbitrary"), 48),
    )(u3, b_re, b_im, a_row_re, a_row_im, p_re, p_im, c_re, c_im, d_row)


def _ssm_out_kernel(u_ref, kt_ref, hre_ref, him_ref, cre_ref, cim_ref, are_ref, aim_ref, y_ref, wre, wim, toep):
    t = pl.program_id(1)
    ar, ai = are_ref[0], aim_ref[0]

    @pl.when(t == 0)
    def _():
        cr, ci = cre_ref[0], cim_ref[0]
        wre[...] = cr * ar - ci * ai
        wim[...] = cr * ai + ci * ar

    wr, wi = wre[...], wim[...]
    wre[...] = wr * ar - wi * ai
    wim[...] = wr * ai + wi * ar

    hw = kt_ref.shape[-1]
    for s in range(CHUNK):
        tile = kt_ref[jnp.maximum(t - s, 0), 0]
        toep[s * hw:(s + 1) * hw, :] = jnp.where(s <= t, tile, jnp.zeros_like(tile))

    span = TOEP_SPAN * hw
    y_ref[0] = (jnp.dot(u_ref[:, :span], toep[:span, :], preferred_element_type=F32)
                + jnp.dot(hre_ref[...], wr.astype(BF16), preferred_element_type=F32)
                - jnp.dot(him_ref[...], wi.astype(BF16), preferred_element_type=F32))
    for piece in range(1, CHUNK // TOEP_SPAN):
        @pl.when(t >= piece * TOEP_SPAN)
        def _():
            lo = piece * span
            y_ref[0] += jnp.dot(u_ref[:, lo:lo + span], toep[lo:lo + span, :], preferred_element_type=F32)


def _ssm_out(u3, kt, h_re, h_im, c_re, c_im, a_col_re, a_col_im):
    nch = u3.shape[0]
    hw = HALF_GROUPS * SSM_GROUP
    sw = HALF_GROUPS * SSM_STATE
    return pl.pallas_call(
        _ssm_out_kernel, name="ssm_out",
        grid=(2, CHUNK),
        in_specs=[pl.BlockSpec((nch, CHUNK * hw), lambda hf, t: (0, hf)),
                  pl.BlockSpec((CHUNK, 1, hw, hw), lambda hf, t: (0, hf, 0, 0)),
                  pl.BlockSpec((nch, sw), lambda hf, t: (0, hf)),
                  pl.BlockSpec((nch, sw), lambda hf, t: (0, hf)),
                  pl.BlockSpec((1, sw, hw), lambda hf, t: (hf, 0, 0)),
                  pl.BlockSpec((1, sw, hw), lambda hf, t: (hf, 0, 0)),
                  pl.BlockSpec((1, sw, 1), lambda hf, t: (hf, 0, 0)),
                  pl.BlockSpec((1, sw, 1), lambda hf, t: (hf, 0, 0))],
        out_specs=pl.BlockSpec((1, nch, hw), lambda hf, t: (t, 0, hf)),
        out_shape=jax.ShapeDtypeStruct((CHUNK, nch, SSM_WIDTH), F32),
        scratch_shapes=[pltpu.VMEM((sw, hw), F32), pltpu.VMEM((sw, hw), F32), pltpu.VMEM((CHUNK * hw, hw), BF16)],
        compiler_params=_params(("arbitrary", "arbitrary"), 48),
    )(u3, kt, h_re, h_im, c_re, c_im, a_col_re, a_col_im)


def _ssm_constants(lam_re, lam_im, log_dt, b_re, b_im, c_re, c_im, d_skip, nchunk):
    lr, li = lam_re.astype(F32), lam_im.astype(F32)
    dt = jnp.exp(log_dt.astype(F32))[:, None]

    def lam_bar_pow(k):
        mag = jnp.exp(k * lr * dt)
        return mag * jnp.cos(k * li * dt), mag * jnp.sin(k * li * dt)

    a_re, a_im = lam_bar_pow(1.0)
    den = lr * lr + li * li
    coef_re = ((a_re - 1.0) * lr + a_im * li) / den
    coef_im = (a_im * lr - (a_re - 1.0) * li) / den
    bb_re = coef_re[..., None] * b_re.astype(F32) - coef_im[..., None] * b_im.astype(F32)
    bb_im = coef_re[..., None] * b_im.astype(F32) + coef_im[..., None] * b_re.astype(F32)
    cc_re, cc_im = c_re.astype(F32), c_im.astype(F32)
    hw, sw = HALF_GROUPS * SSM_GROUP, HALF_GROUPS * SSM_STATE

    def block_diag(rows, row_group, col_group):
        wide = jnp.tile(rows, (1,) * (rows.ndim - 1) + (HALF_GROUPS,))
        r = lax.broadcasted_iota(jnp.int32, wide.shape, wide.ndim - 2) // row_group
        c = lax.broadcasted_iota(jnp.int32, wide.shape, wide.ndim - 1) // col_group
        return jnp.where(r == c, wide, 0.0)

    def b_tiles(part):
        p = part.reshape(2, HALF_GROUPS, SSM_STATE, SSM_GROUP).transpose(0, 1, 3, 2)
        return block_diag(p.reshape(2, hw, SSM_STATE), SSM_GROUP, SSM_STATE)

    def c_tiles(part):
        p = part.reshape(2, HALF_GROUPS, SSM_GROUP, SSM_STATE).transpose(0, 1, 3, 2)
        return block_diag(p.reshape(2, sw, SSM_GROUP), SSM_STATE, SSM_GROUP)

    nsteps = max(int(math.log2(nchunk)), 1)
    steps = (CHUNK * 2.0 ** jnp.arange(nsteps, dtype=F32))[:, None, None]
    st_re, st_im = lam_bar_pow(steps)
    by_half = lambda p: p.reshape(nsteps, 2, sw).transpose(1, 0, 2)
    ar_h, ai_h = a_re.reshape(2, sw), a_im.reshape(2, sw)
    return dict(d_row=d_skip.astype(F32).reshape(2, 1, hw), b_re=b_tiles(bb_re), b_im=b_tiles(bb_im),
                c_re=c_tiles(cc_re), c_im=c_tiles(cc_im),
                a_row_re=ar_h[:, None, :], a_row_im=ai_h[:, None, :],
                a_col_re=ar_h[:, :, None], a_col_im=ai_h[:, :, None],
                p_re=by_half(st_re), p_im=by_half(st_im))


def _mix_kernel(x_ref, attn_ref, y3_ref, wglu_ref, bglu_ref, gssm_ref, woa_ref, wos_ref, gmoe_ref,
                wr_ref, br_ref, x1_ref, h2_ref, info_ref, rt_ref, cnt_ref, ybuf, carry, *, tile):
    i = pl.program_id(0)

    @pl.when(i == 0)
    def _():
        carry[...] = jnp.zeros(carry.shape, F32)

    for s in range(CHUNK):
        for cb in range(SSM_WIDTH // LANES):
            ybuf[cb, pl.ds(s, tile // CHUNK, stride=CHUNK), :] = y3_ref[s, :, cb * LANES:(cb + 1) * LANES]
    y = jax.nn.gelu(jnp.concatenate([ybuf[cb] for cb in range(SSM_WIDTH // LANES)], axis=-1))
    z = jnp.dot(y.astype(BF16), wglu_ref[...], preferred_element_type=F32) + bglu_ref[...]
    y = y * jax.nn.sigmoid(z)
    ssm = _rms(y, gssm_ref[...])
    x1 = (x_ref[...] + jnp.dot(attn_ref[...], woa_ref[...], preferred_element_type=F32)
          + jnp.dot(ssm.astype(BF16), wos_ref[...], preferred_element_type=F32))
    x1_ref[...] = x1
    h2 = _rms(x1, gmoe_ref[...])
    _store_tile_rows(h2_ref, h2)

    nt = (((1,), (1,)), ((), ()))
    h_hi = h2.astype(BF16)
    h_lo = (h2 - h_hi.astype(F32)).astype(BF16)
    both = lax.dot_general(wr_ref[...], h_hi, nt, preferred_element_type=F32)
    logits = (both[:ROUTER_ROWS] + both[LANES:LANES + ROUTER_ROWS]
              + lax.dot_general(wr_ref[:ROUTER_ROWS, :], h_lo, nt, preferred_element_type=F32) + br_ref[...])
    row = lax.broadcasted_iota(jnp.int32, logits.shape, 0)
    neg = -jnp.inf
    gl = jnp.where(row < N_EXPERT_GROUPS, logits, neg)
    gmax = jnp.max(gl, axis=0, keepdims=True)
    gsel = jnp.min(jnp.where(gl == gmax, row, LANES), axis=0, keepdims=True)
    p_group = 1.0 / jnp.sum(jnp.exp(gl - gmax), axis=0, keepdims=True)
    erow = row - N_EXPERT_GROUPS
    in_grp = (erow >= 0) & (erow < N_EXPERTS) & ((erow >> 3) == gsel)
    el = jnp.where(in_grp, logits, neg)
    m1 = jnp.max(el, axis=0, keepdims=True)
    i1 = jnp.min(jnp.where(el == m1, row, LANES), axis=0, keepdims=True)
    den = jnp.sum(jnp.exp(el - m1), axis=0, keepdims=True)
    el2 = jnp.where(row == i1, neg, el)
    m2 = jnp.max(el2, axis=0, keepdims=True)
    i2 = jnp.min(jnp.where(el2 == m2, row, LANES), axis=0, keepdims=True)
    g0 = p_group / den
    g1 = p_group * jnp.exp(m2 - m1) / den

    hit0 = row == i1
    hit1 = row == i2
    onehot = jnp.where(hit0 | hit1, 1.0, 0.0)
    r = lax.broadcasted_iota(jnp.int32, (tile, tile), 0)
    c = lax.broadcasted_iota(jnp.int32, (tile, tile), 1)
    earlier = jnp.where(r < c, 1.0, 0.0).astype(BF16)
    before = jnp.dot(onehot.astype(BF16), earlier, preferred_element_type=F32) + carry[...]
    rank0 = jnp.sum(jnp.where(hit0, before, 0.0), axis=0, keepdims=True)
    rank1 = jnp.sum(jnp.where(hit1, before, 0.0), axis=0, keepdims=True)
    carry[...] += jnp.sum(onehot, axis=1, keepdims=True)
    cnt_ref[...] = jnp.broadcast_to(carry[...], cnt_ref.shape)
    r8 = lax.broadcasted_iota(jnp.int32, (8, tile), 0)
    route = jnp.where(r8 == 0, (i1 - N_EXPERT_GROUPS).astype(F32),
                      jnp.where(r8 == 1, (i2 - N_EXPERT_GROUPS).astype(F32),
                                jnp.where(r8 == 2, g0,
                                          jnp.where(r8 == 3, g1,
                                                    jnp.where(r8 == 4, rank0, jnp.where(r8 == 5, rank1, 0.0))))))
    rt_ref[...] = route
    info_ref[...] = route.T


def _mix(x2, attn, y3, wglu, bglu, gssm, wo_a, wo_s, gmoe, w_router, b_router, tile=512):
    t = x2.shape[0]
    kern = functools.partial(_mix_kernel, tile=tile)
    full = lambda shape: pl.BlockSpec(shape, lambda i: tuple(0 for _ in shape))
    return pl.pallas_call(
        kern, name="mix",
        grid=(t // tile,),
        in_specs=[pl.BlockSpec((tile, D_MODEL), lambda i: (i, 0)),
                  pl.BlockSpec((tile, ATTN_WIDTH), lambda i: (i, 0)),
                  pl.BlockSpec((CHUNK, tile // CHUNK, SSM_WIDTH), lambda i: (0, i, 0)),
                  full((SSM_WIDTH, SSM_WIDTH)), full((1, SSM_WIDTH)), full((1, SSM_WIDTH)),
                  full((ATTN_WIDTH, D_MODEL)), full((SSM_WIDTH, D_MODEL)), full((1, D_MODEL)),
                  full((2 * LANES, D_MODEL)), full((ROUTER_ROWS, 1))],
        out_specs=[pl.BlockSpec((tile, D_MODEL), lambda i: (i, 0)),
                   pl.BlockSpec((tile * ROW_TILE, LANES), lambda i: (i, 0)),
                   pl.BlockSpec((tile, 8), lambda i: (i, 0)),
                   pl.BlockSpec((8, tile), lambda i: (0, i)),
                   pl.BlockSpec((ROUTER_ROWS, LANES), lambda i: (0, 0))],
        out_shape=[jax.ShapeDtypeStruct((t, D_MODEL), F32),
                   jax.ShapeDtypeStruct((t * ROW_TILE, LANES), F32),
                   jax.ShapeDtypeStruct((t, 8), F32),
                   jax.ShapeDtypeStruct((8, t), F32),
                   jax.ShapeDtypeStruct((ROUTER_ROWS, LANES), F32)],
        scratch_shapes=[pltpu.VMEM((SSM_WIDTH // LANES, tile, LANES), F32), pltpu.VMEM((ROUTER_ROWS, 1), F32)],
        compiler_params=_params(("arbitrary",), 48),
    )(x2, attn, y3, wglu, bglu, gssm, wo_a, wo_s, gmoe, w_router, b_router)


def _rows_at(ref, row, n_rows=1):
    return ref.at[pl.ds(pl.multiple_of(row * ROW_TILE, ROW_TILE), n_rows * ROW_TILE), :]


def _row_copy(src, s, dst, d, sem):
    return pltpu.make_async_copy(_rows_at(src, s), _rows_at(dst, d), sem)


UNROLL = 4
DRAIN_UNROLL = 32
COMBINE_GROUPS = 16
PAD_SIZES = tuple(1 << b for b in reversed(range(ROW_BLOCK.bit_length() - 1)))


def _dispatch_kernel(dest_ref, pad_start_ref, pad_len_ref, h_ref, rows_out, zbuf, sem, zsem, *, tile, n_tok,
                     n_blocks):
    i = pl.program_id(0)
    base = i * tile

    def zero_fill(e, start):
        off, rem = pad_start_ref[e], pad_len_ref[e]
        for size in PAD_SIZES:
            @pl.when((rem & size) != 0)
            def _():
                cp = pltpu.make_async_copy(_rows_at(zbuf, 0, size), _rows_at(rows_out, off, size), zsem)
                cp.start() if start else cp.wait()
            off = off + (rem & size)

    def zero_tail(start):
        used = (pad_start_ref[N_EXPERTS - 1] + pad_len_ref[N_EXPERTS - 1]) // ROW_BLOCK

        def blk(b, carry):
            for half in range(ROW_BLOCK // PAD_SIZES[0]):
                cp = pltpu.make_async_copy(zbuf, _rows_at(rows_out, b * ROW_BLOCK + half * PAD_SIZES[0],
                                                          PAD_SIZES[0]), zsem)
                cp.start() if start else cp.wait()
            return carry

        lax.fori_loop(used, n_blocks, blk, 0)

    @pl.when(i == 0)
    def _():
        zbuf[...] = jnp.zeros(zbuf.shape, F32)
        lax.fori_loop(0, N_EXPERTS, lambda e, c: (zero_fill(e, True), c)[1], 0)
        zero_tail(True)

    def issue(tb, carry):
        for k in range(UNROLL):
            t = tb * UNROLL + k
            for j in range(2):
                _row_copy(h_ref, t, rows_out, dest_ref[j * n_tok + base + t], sem).start(priority=j)
        return carry

    def drain(tb, carry):
        for _ in range(2 * DRAIN_UNROLL):
            _row_copy(h_ref, 0, rows_out, 0, sem).wait()
        return carry

    lax.fori_loop(0, tile // UNROLL, issue, 0)
    lax.fori_loop(0, tile // DRAIN_UNROLL, drain, 0)

    @pl.when(i == 0)
    def _():
        lax.fori_loop(0, N_EXPERTS, lambda e, c: (zero_fill(e, False), c)[1], 0)
        zero_tail(False)


def _dispatch(dest_flat, pad_start, pad_len, h2, n_rows, tile=1024):
    t = h2.shape[0] // ROW_TILE
    kern = functools.partial(_dispatch_kernel, tile=tile, n_tok=t, n_blocks=n_rows // ROW_BLOCK)
    grid_spec = pltpu.PrefetchScalarGridSpec(
        num_scalar_prefetch=3, grid=(t // tile,),
        in_specs=[pl.BlockSpec((tile * ROW_TILE, LANES), lambda i, *_: (i, 0))],
        out_specs=pl.BlockSpec(memory_space=pl.ANY),
        scratch_shapes=[pltpu.VMEM((PAD_SIZES[0] * ROW_TILE, LANES), F32),
                        pltpu.SemaphoreType.DMA(()), pltpu.SemaphoreType.DMA(())])
    return pl.pallas_call(
        kern, name="dispatch", grid_spec=grid_spec,
        out_shape=jax.ShapeDtypeStruct((n_rows * ROW_TILE, LANES), F32),
        compiler_params=_params(("arbitrary",), 32),
    )(dest_flat, pad_start, pad_len, h2)


def _expert_kernel(be_ref, nu_ref, first_ref, next_ref, slot_ref, rows_ref, wg_hbm, wu_hbm, wd_hbm, out_ref,
                   wg_f, wu_f, wd_f, wg_b, wu_b, wd_b, sems):
    i = pl.program_id(0)

    def fetch(e, slot):
        return (pltpu.make_async_copy(wg_hbm.at[e], wg_f.at[slot], sems.at[slot, 0]),
                pltpu.make_async_copy(wu_hbm.at[e], wu_f.at[slot], sems.at[slot, 1]),
                pltpu.make_async_copy(wd_hbm.at[e], wd_f.at[slot], sems.at[slot, 2]))

    @pl.when((i == 0) & (nu_ref[0] > 0))
    def _():
        for cp in fetch(be_ref[0], 0):
            cp.start()

    @pl.when((first_ref[i] == 1) & (i < nu_ref[0]))
    def _():
        slot = slot_ref[i]

        @pl.when(next_ref[i] >= 0)
        def _():
            for cp in fetch(next_ref[i], 1 - slot):
                cp.start()

        for cp in fetch(be_ref[i], slot):
            cp.wait()
        wg_b[...] = wg_f[slot].astype(BF16)
        wu_b[...] = wu_f[slot].astype(BF16)
        wd_b[...] = wd_f[slot].astype(BF16)

    @pl.when(i < nu_ref[0])
    def _():
        xb = _load_tile_rows(rows_ref).astype(BF16)
        gate = jnp.dot(xb, wg_b[...], preferred_element_type=F32)
        up = jnp.dot(xb, wu_b[...], preferred_element_type=F32)
        hid = (gate * jax.nn.sigmoid(gate) * up).astype(BF16)
        _store_tile_rows(out_ref, jnp.dot(hid, wd_b[...], preferred_element_type=F32))

    @pl.when(i >= nu_ref[0])
    def _():
        out_ref[...] = jnp.zeros(out_ref.shape, F32)


def _experts(block_e, n_used, seg_first, seg_next, seg_slot, rows, w_gate, w_up, w_down):
    nb = rows.shape[0] // (ROW_BLOCK * ROW_TILE)
    last = lambda i, be, nu, *_: jnp.maximum(jnp.minimum(i, nu[0] - 1), 0)
    hbm = pl.BlockSpec(memory_space=pl.ANY)
    grid_spec = pltpu.PrefetchScalarGridSpec(
        num_scalar_prefetch=5, grid=(nb,),
        in_specs=[pl.BlockSpec((ROW_BLOCK * ROW_TILE, LANES), lambda i, *s: (last(i, *s), 0)), hbm, hbm, hbm],
        out_specs=pl.BlockSpec((ROW_BLOCK * ROW_TILE, LANES), lambda i, *s: (i, 0)),
        scratch_shapes=[pltpu.VMEM((2, D_MODEL, EXPERT_FF), F32), pltpu.VMEM((2, D_MODEL, EXPERT_FF), F32),
                        pltpu.VMEM((2, EXPERT_FF, D_MODEL), F32),
                        pltpu.VMEM((D_MODEL, EXPERT_FF), BF16), pltpu.VMEM((D_MODEL, EXPERT_FF), BF16),
                        pltpu.VMEM((EXPERT_FF, D_MODEL), BF16), pltpu.SemaphoreType.DMA((2, 3))])
    return pl.pallas_call(
        _expert_kernel, name="experts", grid_spec=grid_spec,
        out_shape=jax.ShapeDtypeStruct(rows.shape, F32),
        compiler_params=_params(("arbitrary",), 48),
    )(block_e, n_used, seg_first, seg_next, seg_slot, rows, w_gate, w_up, w_down)


def _combine_kernel(dest_ref, x1_ref, info_ref, gfin_ref, rows_ref, o_ref, buf0, buf1, sems, *, tile, n_tok):
    i = pl.program_id(0)
    slot = i % 2

    def gather(step, to_slot):
        base = step * tile
        b0, b1, sem = buf0.at[to_slot], buf1.at[to_slot], sems.at[to_slot]

        def issue(tb, carry):
            for k in range(UNROLL):
                t = tb * UNROLL + k
                _row_copy(rows_ref, dest_ref[base + t], b0, t, sem).start(priority=0)
                _row_copy(rows_ref, dest_ref[n_tok + base + t], b1, t, sem).start(priority=1)
            return carry

        lax.fori_loop(0, tile // UNROLL, issue, 0)

    def drain(of_slot):
        b0, b1, sem = buf0.at[of_slot], buf1.at[of_slot], sems.at[of_slot]

        def wait(tb, carry):
            for _ in range(DRAIN_UNROLL):
                _row_copy(rows_ref, 0, b0, 0, sem).wait()
                _row_copy(rows_ref, 0, b1, 0, sem).wait()
            return carry

        lax.fori_loop(0, tile // DRAIN_UNROLL, wait, 0)

    @pl.when(i == 0)
    def _():
        gather(0, 0)

    drain(slot)

    last = pl.num_programs(0) - 1
    nxt_base = jnp.minimum(i + 1, last) * tile
    b0, b1 = buf0.at[slot], buf1.at[slot]
    n0, n1, nsem = buf0.at[1 - slot], buf1.at[1 - slot], sems.at[1 - slot]
    group = tile // COMBINE_GROUPS
    for g in range(COMBINE_GROUPS):
        lo = g * group
        for t in range(lo, lo + group):
            _row_copy(rows_ref, dest_ref[nxt_base + t], n0, t, nsem).start(priority=0)
            _row_copy(rows_ref, dest_ref[n_tok + nxt_base + t], n1, t, nsem).start(priority=1)
        info = info_ref[lo:lo + group, :]
        x2 = (x1_ref[lo:lo + group, :] + info[:, 2:3] * _load_tile_rows(b0, lo, group)
              + info[:, 3:4] * _load_tile_rows(b1, lo, group))
        o_ref[lo:lo + group, :] = _rms(x2, gfin_ref[...])

    @pl.when(i == last)
    def _():
        drain(1 - slot)


def _combine(dest_flat, x1, info, gfin, out_rows, tile=512):
    t = x1.shape[0]
    kern = functools.partial(_combine_kernel, tile=tile, n_tok=t)
    grid_spec = pltpu.PrefetchScalarGridSpec(
        num_scalar_prefetch=1, grid=(t // tile,),
        in_specs=[pl.BlockSpec((tile, D_MODEL), lambda i, d: (i, 0)),
                  pl.BlockSpec((tile, 8), lambda i, d: (i, 0)),
                  pl.BlockSpec((1, D_MODEL), lambda i, d: (0, 0)),
                  pl.BlockSpec(memory_space=pl.ANY)],
        out_specs=pl.BlockSpec((tile, D_MODEL), lambda i, d: (i, 0)),
        scratch_shapes=[pltpu.VMEM((2, tile * ROW_TILE, LANES), F32), pltpu.VMEM((2, tile * ROW_TILE, LANES), F32),
                        pltpu.SemaphoreType.DMA((2,))])
    return pl.pallas_call(
        kern, name="combine", grid_spec=grid_spec,
        out_shape=jax.ShapeDtypeStruct((t, D_MODEL), F32),
        compiler_params=_params(("arbitrary",), 32),
    )(dest_flat, x1, info, gfin, out_rows)


def kernel(x, norm_attn, w_in, lambda_q1, lambda_k1, lambda_q2, lambda_k2, attn_subln, ssm_lam_re, ssm_lam_im, ssm_log_dt, ssm_b_re, ssm_b_im, ssm_c_re, ssm_c_im, ssm_d, w_glu, b_glu, ssm_norm, w_out, norm_moe, w_router_group, b_router_group, w_router_expert, b_router_expert, w_gate, w_up, w_down, norm_final):
    batch, seq, d = x.shape
    t = batch * seq
    nchunk = seq // CHUNK
    x2 = x.reshape(t, d)
    l = 0

    q_t, k_aug, v_t, u3 = _in_proj(x2, norm_attn[l][None], w_in[l], seq)
    attn = _attention(q_t, k_aug, v_t, lambda_q1[l][None], lambda_k1[l][None], lambda_q2[l][None],
                      lambda_k2[l][None], attn_subln[l][:, None], batch, seq)

    sc = _ssm_constants(ssm_lam_re[l], ssm_lam_im[l], ssm_log_dt[l], ssm_b_re[l], ssm_b_im[l],
                        ssm_c_re[l], ssm_c_im[l], ssm_d[l], nchunk)
    h_re, h_im, kt = _ssm_state(u3, sc["b_re"], sc["b_im"], sc["a_row_re"], sc["a_row_im"], sc["p_re"], sc["p_im"],
                                sc["c_re"], sc["c_im"], sc["d_row"], nchunk)
    y3 = _ssm_out(u3, kt, h_re, h_im, sc["c_re"], sc["c_im"], sc["a_col_re"], sc["a_col_im"])

    w_router = jnp.concatenate([w_router_group[l], w_router_expert[l]], axis=1).astype(F32).T
    w_router = jnp.pad(w_router, ((0, LANES - w_router.shape[0]), (0, 0)))
    w_router_hi = w_router.astype(BF16)
    w_router = jnp.concatenate([w_router_hi, (w_router - w_router_hi.astype(F32)).astype(BF16)], axis=0)
    b_router = jnp.concatenate([b_router_group[l], b_router_expert[l]]).astype(F32)
    b_router = jnp.pad(b_router, (0, ROUTER_ROWS - b_router.shape[0]))[:, None]
    x1, h2, info, route_t, cnt = _mix(x2, attn, y3, w_glu[l].astype(BF16), b_glu[l][None], ssm_norm[l][None],
                                      w_out[l][:ATTN_WIDTH].astype(BF16), w_out[l][ATTN_WIDTH:].astype(BF16),
                                      norm_moe[l][None], w_router, b_router)

    experts = route_t[0:2].astype(jnp.int32)
    ranks = route_t[4:6].astype(jnp.int32)
    counts = cnt[N_EXPERT_GROUPS:N_EXPERT_GROUPS + N_EXPERTS, 0].astype(jnp.int32)
    padded = ((counts + ROW_BLOCK - 1) // ROW_BLOCK) * ROW_BLOCK
    ids = jnp.arange(N_EXPERTS, dtype=jnp.int32)
    pend = jnp.sum(jnp.where(ids[None, :] <= ids[:, None], padded[None, :], 0), axis=1)
    pstart = pend - padded
    dest = ranks + jnp.sum(jnp.where(experts[:, None, :] == ids[None, :, None], pstart[None, :, None], 0), axis=1)
    dest = dest.reshape(-1)
    n_rows = ((2 * t + N_EXPERTS * (ROW_BLOCK - 1) + ROW_BLOCK - 1) // ROW_BLOCK) * ROW_BLOCK
    nb = n_rows // ROW_BLOCK
    n_used = (pend[-1] // ROW_BLOCK).astype(jnp.int32)
    blk = jnp.minimum(jnp.arange(nb, dtype=jnp.int32), n_used - 1) * ROW_BLOCK
    block_e = jnp.minimum(jnp.sum((pend[None, :] <= blk[:, None]).astype(jnp.int32), axis=1), N_EXPERTS - 1)

    bidx = jnp.arange(nb, dtype=jnp.int32)
    seg_first = ((bidx == 0) | (block_e != jnp.concatenate([block_e[:1], block_e[:-1]]))).astype(jnp.int32)
    seg_slot = (jnp.sum(jnp.where(bidx[None, :] <= bidx[:, None], seg_first[None, :], 0), axis=1) - 1) & 1
    seg_end = jnp.sum(jnp.where(block_e[:, None] == ids[None, :], pend[None, :], 0), axis=1) // ROW_BLOCK
    after = jnp.sum(jnp.where(bidx[None, :] == seg_end[:, None], block_e[None, :], 0), axis=1)
    seg_next = jnp.where(seg_end < n_used, after, -1).astype(jnp.int32)

    rows = _dispatch(dest, pstart + counts, padded - counts, h2, n_rows)
    out_rows = _experts(block_e, n_used[None], seg_first, seg_next, seg_slot.astype(jnp.int32), rows,
                        w_gate[l], w_up[l], w_down[l])
    out = _combine(dest, x1, info, norm_final[None], out_rows)
    return out.reshape(batch, seq, d)
```

```python
import functools
import math

import jax
import jax.numpy as jnp
from jax import lax
from jax.experimental import pallas as pl
from jax.experimental.pallas import tpu as pltpu

F32 = jnp.float32
BF16 = jnp.bfloat16

D_MODEL = 1024
N_HEADS = 4
HEAD_DIM = 64
VALUE_DIM = 128
ATTN_WIDTH = 512
SSM_WIDTH = 512
SSM_GROUP = 16
SSM_STATE = 64
N_EXPERT_GROUPS = 4
EXPERTS_PER_GROUP = 8
N_EXPERTS = 32
EXPERT_FF = 512
RMS_EPS = 1e-6
LAMBDA_INIT = 0.8 - 0.6 * math.exp(-0.3 * 0)
LOG2E = math.log2(math.e)

CHUNK = 16
HALF_GROUPS = 16
TOEP_SPAN = 4
ROUTER_ROWS = 48
LANES = 128
AUG = 128
NORM_LANE = 70
QNORM_ROW = 71
V_ROWS = VALUE_DIM + 16
UNDERFLOW_LOG2 = 152.0
ROW_BLOCK = 256
ROW_TILE = D_MODEL // 128
VMEM_LIMIT_CAP = 56 * 1024 * 1024


def _params(dims, vmem_mb):
    return pltpu.CompilerParams(dimension_semantics=dims,
                                vmem_limit_bytes=min(vmem_mb * 1024 * 1024, VMEM_LIMIT_CAP))


def _rms(x, gain):
    return x * lax.rsqrt(jnp.mean(x * x, axis=-1, keepdims=True) + RMS_EPS) * gain


def _store_tile_rows(ref, val):
    n = val.shape[0]
    for c in range(ROW_TILE):
        ref[pl.ds(c, n, stride=ROW_TILE), :] = val[:, c * LANES:(c + 1) * LANES]


def _load_tile_rows(ref, first=0, n=None):
    n = ref.shape[0] // ROW_TILE if n is None else n
    return jnp.concatenate([ref[pl.ds(first * ROW_TILE + c, n, stride=ROW_TILE), :] for c in range(ROW_TILE)],
                           axis=-1)


def _split3(val):
    hi = val.astype(BF16).astype(F32)
    r1 = val - hi
    mid = r1.astype(BF16).astype(F32)
    lo = r1 - mid
    return hi, mid, lo


def _inproj_kernel(x_ref, g_ref, w_ref, q_ref, k_ref, v_ref, u_ref, ubuf, wqv_ref, wku_ref, *, tile, seq):
    i = pl.program_id(0)

    @pl.when(i == 0)
    def _():
        for part, col0 in ((0, 0), (1, 2 * ATTN_WIDTH)):
            for cb in range(ATTN_WIDTH // LANES):
                r0, c0 = part * ATTN_WIDTH + cb * LANES, col0 + cb * LANES
                wqv_ref[r0:r0 + LANES, :] = w_ref[:, c0:c0 + LANES].T.astype(BF16)
        for blk in range(2 * N_HEADS):
            src = ATTN_WIDTH + blk * HEAD_DIM
            wku_ref[:, blk * AUG:blk * AUG + HEAD_DIM] = w_ref[:, src:src + HEAD_DIM].astype(BF16)
            wku_ref[:, blk * AUG + HEAD_DIM:(blk + 1) * AUG] = jnp.zeros((D_MODEL, AUG - HEAD_DIM), BF16)
        wku_ref[:, 2 * N_HEADS * AUG:] = w_ref[:, 3 * ATTN_WIDTH:].astype(BF16)

    h = _rms(x_ref[...], g_ref[...]).astype(BF16)
    nt = (((1,), (1,)), ((), ()))
    kcols = 2 * N_HEADS * AUG
    qt = lax.dot_general(wqv_ref[:ATTN_WIDTH, :], h, nt, preferred_element_type=F32)
    kp = jnp.dot(h, wku_ref[:, :kcols], preferred_element_type=F32)
    vt = lax.dot_general(wqv_ref[ATTN_WIDTH:, :], h, nt, preferred_element_type=F32)
    up = jnp.dot(h, wku_ref[:, kcols:], preferred_element_type=F32)
    ones_row = jnp.where(lax.broadcasted_iota(jnp.int32, (V_ROWS - VALUE_DIM, tile), 0) == 0, 1.0, 0.0)
    for hd in range(N_HEADS):
        v_ref[hd * V_ROWS:hd * V_ROWS + VALUE_DIM, :] = vt[hd * VALUE_DIM:(hd + 1) * VALUE_DIM, :].astype(BF16)
        v_ref[hd * V_ROWS + VALUE_DIM:(hd + 1) * V_ROWS, :] = ones_row.astype(BF16)
    pos0 = lax.rem(i * tile, seq)
    pos_k = (pos0 + lax.broadcasted_iota(jnp.int32, (tile, AUG), 0)).astype(F32)
    pos_q = (pos0 + lax.broadcasted_iota(jnp.int32, (1, tile), 1)).astype(F32)
    lane = lax.broadcasted_iota(jnp.int32, (tile, AUG), 1)
    srow = lax.broadcasted_iota(jnp.int32, (AUG - HEAD_DIM, tile), 0)
    qscale = HEAD_DIM ** -0.5 * LOG2E
    for hd in range(N_HEADS):
        slope = 2.0 ** (-8.0 * (hd + 1) / N_HEADS) * LOG2E
        hi, mid, lo = _split3(pos_k * slope)
        k_add = jnp.where(lane == 64, hi,
                          jnp.where(lane == 65, mid,
                                    jnp.where(lane == 66, lo, jnp.where((lane >= 67) & (lane < 70), 1.0, 0.0))))
        hi, mid, lo = _split3(pos_q * slope)
        q_add = jnp.where(srow < 3, 1.0,
                          jnp.where(srow == 3, -hi, jnp.where(srow == 4, -mid, jnp.where(srow == 5, -lo, 0.0))))
        for m in range(2):
            c0 = (hd * 2 + m) * AUG
            qb = (qt[c0 // 2:c0 // 2 + HEAD_DIM, :] * qscale).astype(BF16)
            q_ref[c0:c0 + HEAD_DIM, :] = qb
            qr = qb.astype(F32)
            qnorm2 = jnp.sum(qr * qr, axis=0, keepdims=True) * (1.0 + 2.0 ** -6)
            q_ref[c0 + HEAD_DIM:c0 + AUG, :] = (
                q_add + jnp.where(srow == QNORM_ROW - HEAD_DIM, qnorm2, 0.0)).astype(BF16)
            kb = kp[:, c0:c0 + AUG]
            kr = kb.astype(BF16).astype(F32)
            norm2 = jnp.sum(kr * kr, axis=-1, keepdims=True) * (1.0 + 2.0 ** -6)
            k_ref[:, c0:c0 + AUG] = (kb + k_add + jnp.where(lane == NORM_LANE, norm2, 0.0)).astype(BF16)
    for cb in range(SSM_WIDTH // LANES):
        ubuf[cb] = up[:, cb * LANES:(cb + 1) * LANES]
    per_half = HALF_GROUPS * SSM_GROUP // LANES
    for s in range(CHUNK):
        for cb in range(SSM_WIDTH // LANES):
            c0 = ((cb // per_half) * CHUNK + s) * HALF_GROUPS * SSM_GROUP + (cb % per_half) * LANES
            u_ref[:, c0:c0 + LANES] = ubuf[cb, pl.ds(s, tile // CHUNK, stride=CHUNK), :].astype(BF16)


def _in_proj(x2, gain, w_in, seq, tile=512):
    t = x2.shape[0]
    qk = 2 * N_HEADS * AUG
    kern = functools.partial(_inproj_kernel, tile=tile, seq=seq)
    full = lambda shape: pl.BlockSpec(shape, lambda i: (0, 0))
    return pl.pallas_call(
        kern, name="in_proj",
        grid=(t // tile,),
        in_specs=[pl.BlockSpec((tile, D_MODEL), lambda i: (i, 0)),
                  full((1, D_MODEL)), full(w_in.shape)],
        out_specs=[pl.BlockSpec((qk, tile), lambda i: (0, i)),
                   pl.BlockSpec((tile, qk), lambda i: (i, 0)),
                   pl.BlockSpec((N_HEADS * V_ROWS, tile), lambda i: (0, i)),
                   pl.BlockSpec((tile // CHUNK, CHUNK * SSM_WIDTH), lambda i: (i, 0))],
        out_shape=[jax.ShapeDtypeStruct((qk, t), BF16),
                   jax.ShapeDtypeStruct((t, qk), BF16),
                   jax.ShapeDtypeStruct((N_HEADS * V_ROWS, t), BF16),
                   jax.ShapeDtypeStruct((t // CHUNK, CHUNK * SSM_WIDTH), BF16)],
        scratch_shapes=[pltpu.VMEM((SSM_WIDTH // LANES, tile, LANES), F32),
                        pltpu.VMEM((2 * ATTN_WIDTH, D_MODEL), BF16), pltpu.VMEM((D_MODEL, qk + SSM_WIDTH), BF16)],
        compiler_params=_params(("arbitrary",), 48),
    )(x2, gain, w_in)


def _attn_kernel(lq1, lk1, lq2, lk2, sub_ref, q_ref, k_ref, v_ref, o_ref, m_sc, acc_sc, kn_sc, *, tq, kb, strip):
    hd = pl.program_id(1)
    qi = pl.program_id(2)
    lam = (jnp.exp(jnp.sum(lq1[...] * lk1[...], axis=-1, keepdims=True))
           - jnp.exp(jnp.sum(lq2[...] * lk2[...], axis=-1, keepdims=True)) + LAMBDA_INIT)

    @pl.when(qi == 0)
    def _():
        for m in range(2):
            kn_sc[m] = jnp.max(k_ref[:, m * AUG:(m + 1) * AUG].astype(F32), axis=0, keepdims=True)

    def score(start, nkeys, align, q_lo=0, nq=tq):
        r0 = pl.multiple_of(start, align)
        kblk = k_ref[pl.ds(r0, nkeys), :]
        return [jnp.dot(kblk[:, m * AUG:(m + 1) * AUG], q_ref[m * AUG:(m + 1) * AUG, q_lo:q_lo + nq],
                        preferred_element_type=F32) for m in range(2)]

    def step(start, nkeys, align, q_lo=0, nq=tq, key_lo=None, first=False, scores=None):
        if scores is None:
            scores = score(start, nkeys, align, q_lo, nq)
        vblk = v_ref[:, pl.ds(pl.multiple_of(start, align), nkeys)]
        for m in range(2):
            s = scores[m]
            if key_lo is not None:
                key = lax.broadcasted_iota(jnp.int32, (nkeys, nq), 0) + key_lo
                qry = lax.broadcasted_iota(jnp.int32, (nkeys, nq), 1) + q_lo
                s = jnp.where(key <= qry, s, -jnp.inf)
            if first:
                m_new = jnp.max(s, axis=0, keepdims=True)
                pv = jnp.dot(vblk, jnp.exp2(s - m_new).astype(BF16), preferred_element_type=F32)
            else:
                m_prev = m_sc[m, :, q_lo:q_lo + nq]
                m_new = jnp.maximum(m_prev, jnp.max(s, axis=0, keepdims=True))
                pv = (jnp.exp2(m_prev - m_new) * acc_sc[m, :, q_lo:q_lo + nq]
                      + jnp.dot(vblk, jnp.exp2(s - m_new).astype(BF16), preferred_element_type=F32))
            acc_sc[m, :, q_lo:q_lo + nq] = pv
            m_sc[m, :, q_lo:q_lo + nq] = m_new

    for lo in range(0, tq, strip):
        step(qi * tq + lo, strip, strip, q_lo=lo, nq=tq - lo, key_lo=lo, first=lo == 0)

    lane = lax.broadcasted_iota(jnp.int32, (1, AUG), 1)
    sub = lax.broadcasted_iota(jnp.int32, (16, 1), 0)
    slope = LOG2E * jnp.exp2(-2.0 * (jnp.full((1, 1), hd, jnp.int32) + 1).astype(F32))
    reach = jnp.zeros((1, 1), F32)
    for m in range(2):
        qrows = jnp.max(q_ref[m * AUG + HEAD_DIM:m * AUG + HEAD_DIM + 16, :].astype(F32), axis=-1, keepdims=True)
        q2 = jnp.max(jnp.where(sub == QNORM_ROW - HEAD_DIM, qrows, 0.0), axis=0, keepdims=True)
        k2 = jnp.max(jnp.where(lane == NORM_LANE, kn_sc[m], 0.0), axis=-1, keepdims=True)
        m_min = jnp.min(m_sc[m], axis=-1, keepdims=True)
        reach = jnp.maximum(reach, (jnp.sqrt(q2 * k2) * 1.001 + (UNDERFLOW_LOG2 + 0.5) - m_min) / slope)
    need = jnp.ceil((reach + (kb - 1)) / kb).astype(jnp.int32) - 1
    n_below = jnp.minimum(jnp.max(jnp.maximum(need, 0)), qi * (tq // kb))
    top = qi * tq

    def quad(i, carry):
        a, b = top - (2 + 4 * i) * kb, top - (4 + 4 * i) * kb
        sa, sb = score(a, 2 * kb, kb), score(b, 2 * kb, kb)
        step(a, 2 * kb, kb, scores=sa)
        step(b, 2 * kb, kb, scores=sb)
        return carry

    lax.fori_loop(0, n_below // 4, quad, 0)
    rest = n_below % 4
    done = n_below - rest

    @pl.when(rest >= 2)
    def _():
        step(top - (done + 2) * kb, 2 * kb, kb)

    @pl.when(rest % 2 == 1)
    def _():
        step(top - n_below * kb, kb, kb)

    l0 = acc_sc[0, VALUE_DIM:VALUE_DIM + 1, :]
    l1 = acc_sc[1, VALUE_DIM:VALUE_DIM + 1, :]
    o = acc_sc[0, :VALUE_DIM, :] / l0 - lam * (acc_sc[1, :VALUE_DIM, :] / l1)
    o = o * lax.rsqrt(jnp.mean(o * o, axis=0, keepdims=True) + RMS_EPS) * sub_ref[...] * (1.0 - LAMBDA_INIT)
    o_ref[...] = o.T.astype(BF16)


def _attention(q_t, k_aug, v_t, lq1, lk1, lq2, lk2, subln_col, batch, seq, tq=1024, kb=512, strip=512):
    tq = min(tq, seq)
    nq = seq // tq
    t = batch * seq
    small = pl.BlockSpec((1, HEAD_DIM), lambda b, h, i: (0, 0))
    kern = functools.partial(_attn_kernel, tq=tq, kb=kb, strip=strip)
    return pl.pallas_call(
        kern, name="attention",
        grid=(batch, N_HEADS, nq),
        in_specs=[small, small, small, small,
                  pl.BlockSpec((VALUE_DIM, 1), lambda b, h, i: (0, 0)),
                  pl.BlockSpec((2 * AUG, tq), lambda b, h, i: (h, b * nq + i)),
                  pl.BlockSpec((seq, 2 * AUG), lambda b, h, i: (b, h)),
                  pl.BlockSpec((V_ROWS, seq), lambda b, h, i: (h, b))],
        out_specs=pl.BlockSpec((tq, VALUE_DIM), lambda b, h, i: (b * nq + i, h)),
        out_shape=jax.ShapeDtypeStruct((t, ATTN_WIDTH), BF16),
        scratch_shapes=[pltpu.VMEM((2, 1, tq), F32), pltpu.VMEM((2, V_ROWS, tq), F32),
                        pltpu.VMEM((2, 1, AUG), F32)],
        compiler_params=_params(("arbitrary", "arbitrary", "arbitrary"), 48),
    )(lq1, lk1, lq2, lk2, subln_col, q_t, k_aug, v_t)


SCAN_COLS = 256


def _ssm_state_kernel(u_ref, bre_ref, bim_ref, are_ref, aim_ref, pre_ref, pim_ref, cre_ref, cim_ref, d_ref,
                      hre_ref, him_ref, kt_ref, wre, wim, sre, sim, *, nchunk, nsteps):
    i = pl.program_id(1)

    @pl.when(i == 0)
    def _():
        wre[...] = bre_ref[0]
        wim[...] = bim_ref[0]
        sre[...] = jnp.zeros(sre.shape, F32)
        sim[...] = jnp.zeros(sim.shape, F32)

    wr, wi = wre[...], wim[...]
    ar, ai = are_ref[0], aim_ref[0]
    wre[...] = wr * ar - wi * ai
    wim[...] = wr * ai + wi * ar

    u = u_ref[...]
    wr_b, wi_b = wr.astype(BF16), wi.astype(BF16)
    sre[...] += jnp.dot(u, wr_b, preferred_element_type=F32)
    sim[...] += jnp.dot(u, wi_b, preferred_element_type=F32)

    lag = (jnp.dot(wr_b, cre_ref[0].astype(BF16), preferred_element_type=F32)
           - jnp.dot(wi_b, cim_ref[0].astype(BF16), preferred_element_type=F32))
    r = lax.broadcasted_iota(jnp.int32, lag.shape, 0)
    c = lax.broadcasted_iota(jnp.int32, lag.shape, 1)
    skip = jnp.where((r == c) & (i == 0), d_ref[0], 0.0)
    kt_ref[0, 0] = (lag + skip).astype(BF16)

    @pl.when(i == CHUNK - 1)
    def _():
        row = lax.rem(lax.broadcasted_iota(jnp.int32, (sre.shape[0], SCAN_COLS), 0), nchunk)
        for c0 in range(0, sre.shape[1], SCAN_COLS):
            cols = slice(c0, c0 + SCAN_COLS)
            hr, hi = sre[:, cols], sim[:, cols]
            for k in range(nsteps):
                d = 1 << k
                ar, ai = pre_ref[0, k:k + 1, cols], pim_ref[0, k:k + 1, cols]
                keep = row >= d
                pr = jnp.where(keep, pltpu.roll(hr, d, 0), 0.0)
                pi = jnp.where(keep, pltpu.roll(hi, d, 0), 0.0)
                hr, hi = hr + ar * pr - ai * pi, hi + ar * pi + ai * pr
            keep = row >= 1
            hre_ref[:, cols] = jnp.where(keep, pltpu.roll(hr, 1, 0), 0.0).astype(BF16)
            him_ref[:, cols] = jnp.where(keep, pltpu.roll(hi, 1, 0), 0.0).astype(BF16)


def _ssm_state(u3, b_re, b_im, a_row_re, a_row_im, p_re, p_im, c_re, c_im, d_row, nchunk):
    nch = u3.shape[0]
    nsteps = p_re.shape[1]
    hw = HALF_GROUPS * SSM_GROUP
    sw = HALF_GROUPS * SSM_STATE
    kern = functools.partial(_ssm_state_kernel, nchunk=nchunk, nsteps=nsteps)
    per_half = lambda shape: pl.BlockSpec((1,) + shape, lambda hf, i: (hf, 0, 0))
    return pl.pallas_call(
        kern, name="ssm_state",
        grid=(2, CHUNK),
        in_specs=[pl.BlockSpec((nch, hw), lambda hf, i: (0, hf * CHUNK + CHUNK - 1 - i)),
                  per_half((hw, sw)), per_half((hw, sw)), per_half((1, sw)), per_half((1, sw)),
                  per_half((nsteps, sw)), per_half((nsteps, sw)),
                  per_half((sw, hw)), per_half((sw, hw)), per_half((1, hw))],
        out_specs=[pl.BlockSpec((nch, sw), lambda hf, i: (0, hf)),
                   pl.BlockSpec((nch, sw), lambda hf, i: (0, hf)),
                   pl.BlockSpec((1, 1, hw, hw), lambda hf, i: (i, hf, 0, 0))],
        out_shape=[jax.ShapeDtypeStruct((nch, 2 * sw), BF16)] * 2
        + [jax.ShapeDtypeStruct((CHUNK, 2, hw, hw), BF16)],
        scratch_shapes=[pltpu.VMEM((hw, sw), F32), pltpu.VMEM((hw, sw), F32),
                        pltpu.VMEM((nch, sw), F32), pltpu.VMEM((nch, sw), F32)],
        compiler_params=_params(("arbitrary", "arbitrary"), 48),
    )(u3, b_re, b_im, a_row_re, a_row_im, p_re, p_im, c_re, c_im, d_row)


def _ssm_out_kernel(u_ref, kt_ref, hre_ref, him_ref, cre_ref, cim_ref, are_ref, aim_ref, y_ref, wre, wim, toep):
    t = pl.program_id(1)
    ar, ai = are_ref[0], aim_ref[0]

    @pl.when(t == 0)
    def _():
        cr, ci = cre_ref[0], cim_ref[0]
        wre[...] = cr * ar - ci * ai
        wim[...] = cr * ai + ci * ar

    wr, wi = wre[...], wim[...]
    wre[...] = wr * ar - wi * ai
    wim[...] = wr * ai + wi * ar

    hw = kt_ref.shape[-1]
    for s in range(CHUNK):
        tile = kt_ref[jnp.maximum(t - s, 0), 0]
        toep[s * hw:(s + 1) * hw, :] = jnp.where(s <= t, tile, jnp.zeros_like(tile))

    span = TOEP_SPAN * hw
    y_ref[0] = (jnp.dot(u_ref[:, :span], toep[:span, :], preferred_element_type=F32)
                + jnp.dot(hre_ref[...], wr.astype(BF16), preferred_element_type=F32)
                - jnp.dot(him_ref[...], wi.astype(BF16), preferred_element_type=F32))
    for piece in range(1, CHUNK // TOEP_SPAN):
        @pl.when(t >= piece * TOEP_SPAN)
        def _():
            lo = piece * span
            y_ref[0] += jnp.dot(u_ref[:, lo:lo + span], toep[lo:lo + span, :], preferred_element_type=F32)


def _ssm_out(u3, kt, h_re, h_im, c_re, c_im, a_col_re, a_col_im):
    nch = u3.shape[0]
    hw = HALF_GROUPS * SSM_GROUP
    sw = HALF_GROUPS * SSM_STATE
    return pl.pallas_call(
        _ssm_out_kernel, name="ssm_out",
        grid=(2, CHUNK),
        in_specs=[pl.BlockSpec((nch, CHUNK * hw), lambda hf, t: (0, hf)),
                  pl.BlockSpec((CHUNK, 1, hw, hw), lambda hf, t: (0, hf, 0, 0)),
                  pl.BlockSpec((nch, sw), lambda hf, t: (0, hf)),
                  pl.BlockSpec((nch, sw), lambda hf, t: (0, hf)),
                  pl.BlockSpec((1, sw, hw), lambda hf, t: (hf, 0, 0)),
                  pl.BlockSpec((1, sw, hw), lambda hf, t: (hf, 0, 0)),
                  pl.BlockSpec((1, sw, 1), lambda hf, t: (hf, 0, 0)),
                  pl.BlockSpec((1, sw, 1), lambda hf, t: (hf, 0, 0))],
        out_specs=pl.BlockSpec((1, nch, hw), lambda hf, t: (t, 0, hf)),
        out_shape=jax.ShapeDtypeStruct((CHUNK, nch, SSM_WIDTH), F32),
        scratch_shapes=[pltpu.VMEM((sw, hw), F32), pltpu.VMEM((sw, hw), F32), pltpu.VMEM((CHUNK * hw, hw), BF16)],
        compiler_params=_params(("arbitrary", "arbitrary"), 48),
    )(u3, kt, h_re, h_im, c_re, c_im, a_col_re, a_col_im)


def _ssm_constants(lam_re, lam_im, log_dt, b_re, b_im, c_re, c_im, d_skip, nchunk):
    lr, li = lam_re.astype(F32), lam_im.astype(F32)
    dt = jnp.exp(log_dt.astype(F32))[:, None]

    def lam_bar_pow(k):
        mag = jnp.exp(k * lr * dt)
        return mag * jnp.cos(k * li * dt), mag * jnp.sin(k * li * dt)

    a_re, a_im = lam_bar_pow(1.0)
    den = lr * lr + li * li
    coef_re = ((a_re - 1.0) * lr + a_im * li) / den
    coef_im = (a_im * lr - (a_re - 1.0) * li) / den
    bb_re = coef_re[..., None] * b_re.astype(F32) - coef_im[..., None] * b_im.astype(F32)
    bb_im = coef_re[..., None] * b_im.astype(F32) + coef_im[..., None] * b_re.astype(F32)
    cc_re, cc_im = c_re.astype(F32), c_im.astype(F32)
    hw, sw = HALF_GROUPS * SSM_GROUP, HALF_GROUPS * SSM_STATE

    def block_diag(rows, row_group, col_group):
        wide = jnp.tile(rows, (1,) * (rows.ndim - 1) + (HALF_GROUPS,))
        r = lax.broadcasted_iota(jnp.int32, wide.shape, wide.ndim - 2) // row_group
        c = lax.broadcasted_iota(jnp.int32, wide.shape, wide.ndim - 1) // col_group
        return jnp.where(r == c, wide, 0.0)

    def b_tiles(part):
        p = part.reshape(2, HALF_GROUPS, SSM_STATE, SSM_GROUP).transpose(0, 1, 3, 2)
        return block_diag(p.reshape(2, hw, SSM_STATE), SSM_GROUP, SSM_STATE)

    def c_tiles(part):
        p = part.reshape(2, HALF_GROUPS, SSM_GROUP, SSM_STATE).transpose(0, 1, 3, 2)
        return block_diag(p.reshape(2, sw, SSM_GROUP), SSM_STATE, SSM_GROUP)

    nsteps = max(int(math.log2(nchunk)), 1)
    steps = (CHUNK * 2.0 ** jnp.arange(nsteps, dtype=F32))[:, None, None]
    st_re, st_im = lam_bar_pow(steps)
    by_half = lambda p: p.reshape(nsteps, 2, sw).transpose(1, 0, 2)
    ar_h, ai_h = a_re.reshape(2, sw), a_im.reshape(2, sw)
    return dict(d_row=d_skip.astype(F32).reshape(2, 1, hw), b_re=b_tiles(bb_re), b_im=b_tiles(bb_im),
                c_re=c_tiles(cc_re), c_im=c_tiles(cc_im),
                a_row_re=ar_h[:, None, :], a_row_im=ai_h[:, None, :],
                a_col_re=ar_h[:, :, None], a_col_im=ai_h[:, :, None],
                p_re=by_half(st_re), p_im=by_half(st_im))


def _mix_kernel(x_ref, attn_ref, y3_ref, wglu_ref, bglu_ref, gssm_ref, woa_ref, wos_ref, gmoe_ref,
                wr_ref, br_ref, x1_ref, h2_ref, info_ref, rt_ref, cnt_ref, ybuf, carry, *, tile):
    i = pl.program_id(0)

    @pl.when(i == 0)
    def _():
        carry[...] = jnp.zeros(carry.shape, F32)

    for s in range(CHUNK):
        for cb in range(SSM_WIDTH // LANES):
            ybuf[cb, pl.ds(s, tile // CHUNK, stride=CHUNK), :] = y3_ref[s, :, cb * LANES:(cb + 1) * LANES]
    y = jax.nn.gelu(jnp.concatenate([ybuf[cb] for cb in range(SSM_WIDTH // LANES)], axis=-1))
    z = jnp.dot(y.astype(BF16), wglu_ref[...], preferred_element_type=F32) + bglu_ref[...]
    y = y * jax.nn.sigmoid(z)
    ssm = _rms(y, gssm_ref[...])
    x1 = (x_ref[...] + jnp.dot(attn_ref[...], woa_ref[...], preferred_element_type=F32)
          + jnp.dot(ssm.astype(BF16), wos_ref[...], preferred_element_type=F32))
    x1_ref[...] = x1
    h2 = _rms(x1, gmoe_ref[...])
    _store_tile_rows(h2_ref, h2)

    nt = (((1,), (1,)), ((), ()))
    h_hi = h2.astype(BF16)
    h_lo = (h2 - h_hi.astype(F32)).astype(BF16)
    both = lax.dot_general(wr_ref[...], h_hi, nt, preferred_element_type=F32)
    logits = (both[:ROUTER_ROWS] + both[LANES:LANES + ROUTER_ROWS]
              + lax.dot_general(wr_ref[:ROUTER_ROWS, :], h_lo, nt, preferred_element_type=F32) + br_ref[...])
    row = lax.broadcasted_iota(jnp.int32, logits.shape, 0)
    neg = -jnp.inf
    gl = jnp.where(row < N_EXPERT_GROUPS, logits, neg)
    gmax = jnp.max(gl, axis=0, keepdims=True)
    gsel = jnp.min(jnp.where(gl == gmax, row, LANES), axis=0, keepdims=True)
    p_group = 1.0 / jnp.sum(jnp.exp(gl - gmax), axis=0, keepdims=True)
    erow = row - N_EXPERT_GROUPS
    in_grp = (erow >= 0) & (erow < N_EXPERTS) & ((erow >> (EXPERTS_PER_GROUP.bit_length() - 1)) == gsel)
    el = jnp.where(in_grp, logits, neg)
    m1 = jnp.max(el, axis=0, keepdims=True)
    i1 = jnp.min(jnp.where(el == m1, row, LANES), axis=0, keepdims=True)
    den = jnp.sum(jnp.exp(el - m1), axis=0, keepdims=True)
    el2 = jnp.where(row == i1, neg, el)
    m2 = jnp.max(el2, axis=0, keepdims=True)
    i2 = jnp.min(jnp.where(el2 == m2, row, LANES), axis=0, keepdims=True)
    g0 = p_group / den
    g1 = p_group * jnp.exp(m2 - m1) / den

    hit0 = row == i1
    hit1 = row == i2
    onehot = jnp.where(hit0 | hit1, 1.0, 0.0)
    r = lax.broadcasted_iota(jnp.int32, (tile, tile), 0)
    c = lax.broadcasted_iota(jnp.int32, (tile, tile), 1)
    earlier = jnp.where(r < c, 1.0, 0.0).astype(BF16)
    before = jnp.dot(onehot.astype(BF16), earlier, preferred_element_type=F32) + carry[...]
    rank0 = jnp.sum(jnp.where(hit0, before, 0.0), axis=0, keepdims=True)
    rank1 = jnp.sum(jnp.where(hit1, before, 0.0), axis=0, keepdims=True)
    carry[...] += jnp.sum(onehot, axis=1, keepdims=True)
    cnt_ref[...] = jnp.broadcast_to(carry[...], cnt_ref.shape)
    r8 = lax.broadcasted_iota(jnp.int32, (8, tile), 0)
    route = jnp.where(r8 == 0, (i1 - N_EXPERT_GROUPS).astype(F32),
                      jnp.where(r8 == 1, (i2 - N_EXPERT_GROUPS).astype(F32),
                                jnp.where(r8 == 2, g0,
                                          jnp.where(r8 == 3, g1,
                                                    jnp.where(r8 == 4, rank0, jnp.where(r8 == 5, rank1, 0.0))))))
    rt_ref[...] = route
    info_ref[...] = route.T


def _mix(x2, attn, y3, wglu, bglu, gssm, wo_a, wo_s, gmoe, w_router, b_router, tile=512):
    t = x2.shape[0]
    kern = functools.partial(_mix_kernel, tile=tile)
    full = lambda shape: pl.BlockSpec(shape, lambda i: tuple(0 for _ in shape))
    return pl.pallas_call(
        kern, name="mix",
        grid=(t // tile,),
        in_specs=[pl.BlockSpec((tile, D_MODEL), lambda i: (i, 0)),
                  pl.BlockSpec((tile, ATTN_WIDTH), lambda i: (i, 0)),
                  pl.BlockSpec((CHUNK, tile // CHUNK, SSM_WIDTH), lambda i: (0, i, 0)),
                  full((SSM_WIDTH, SSM_WIDTH)), full((1, SSM_WIDTH)), full((1, SSM_WIDTH)),
                  full((ATTN_WIDTH, D_MODEL)), full((SSM_WIDTH, D_MODEL)), full((1, D_MODEL)),
                  full((2 * LANES, D_MODEL)), full((ROUTER_ROWS, 1))],
        out_specs=[pl.BlockSpec((tile, D_MODEL), lambda i: (i, 0)),
                   pl.BlockSpec((tile * ROW_TILE, LANES), lambda i: (i, 0)),
                   pl.BlockSpec((tile, 8), lambda i: (i, 0)),
                   pl.BlockSpec((8, tile), lambda i: (0, i)),
                   pl.BlockSpec((ROUTER_ROWS, LANES), lambda i: (0, 0))],
        out_shape=[jax.ShapeDtypeStruct((t, D_MODEL), F32),
                   jax.ShapeDtypeStruct((t * ROW_TILE, LANES), F32),
                   jax.ShapeDtypeStruct((t, 8), F32),
                   jax.ShapeDtypeStruct((8, t), F32),
                   jax.ShapeDtypeStruct((ROUTER_ROWS, LANES), F32)],
        scratch_shapes=[pltpu.VMEM((SSM_WIDTH // LANES, tile, LANES), F32), pltpu.VMEM((ROUTER_ROWS, 1), F32)],
        compiler_params=_params(("arbitrary",), 48),
    )(x2, attn, y3, wglu, bglu, gssm, wo_a, wo_s, gmoe, w_router, b_router)


def _rows_at(ref, row, n_rows=1):
    return ref.at[pl.ds(pl.multiple_of(row * ROW_TILE, ROW_TILE), n_rows * ROW_TILE), :]


def _row_copy(src, s, dst, d, sem):
    return pltpu.make_async_copy(_rows_at(src, s), _rows_at(dst, d), sem)


UNROLL = 4
DRAIN_UNROLL = 32
COMBINE_GROUPS = 16
PAD_SIZES = tuple(1 << b for b in reversed(range(ROW_BLOCK.bit_length() - 1)))


def _dispatch_kernel(dest_ref, pad_start_ref, pad_len_ref, h_ref, rows_out, zbuf, sem, zsem, *, tile, n_tok,
                     n_blocks):
    i = pl.program_id(0)
    base = i * tile

    def zero_fill(e, start):
        off, rem = pad_start_ref[e], pad_len_ref[e]
        for size in PAD_SIZES:
            @pl.when((rem & size) != 0)
            def _():
                cp = pltpu.make_async_copy(_rows_at(zbuf, 0, size), _rows_at(rows_out, off, size), zsem)
                cp.start() if start else cp.wait()
            off = off + (rem & size)

    def zero_tail(start):
        used = (pad_start_ref[N_EXPERTS - 1] + pad_len_ref[N_EXPERTS - 1]) // ROW_BLOCK

        def blk(b, carry):
            for half in range(ROW_BLOCK // PAD_SIZES[0]):
                cp = pltpu.make_async_copy(zbuf, _rows_at(rows_out, b * ROW_BLOCK + half * PAD_SIZES[0],
                                                          PAD_SIZES[0]), zsem)
                cp.start() if start else cp.wait()
            return carry

        lax.fori_loop(used, n_blocks, blk, 0)

    @pl.when(i == 0)
    def _():
        zbuf[...] = jnp.zeros(zbuf.shape, F32)
        lax.fori_loop(0, N_EXPERTS, lambda e, c: (zero_fill(e, True), c)[1], 0)
        zero_tail(True)

    def issue(tb, carry):
        for k in range(UNROLL):
            t = tb * UNROLL + k
            for j in range(2):
                _row_copy(h_ref, t, rows_out, dest_ref[j * n_tok + base + t], sem).start(priority=j)
        return carry

    def drain(tb, carry):
        for _ in range(2 * DRAIN_UNROLL):
            _row_copy(h_ref, 0, rows_out, 0, sem).wait()
        return carry

    lax.fori_loop(0, tile // UNROLL, issue, 0)
    lax.fori_loop(0, tile // DRAIN_UNROLL, drain, 0)

    @pl.when(i == 0)
    def _():
        lax.fori_loop(0, N_EXPERTS, lambda e, c: (zero_fill(e, False), c)[1], 0)
        zero_tail(False)


def _dispatch(dest_flat, pad_start, pad_len, h2, n_rows, tile=1024):
    t = h2.shape[0] // ROW_TILE
    kern = functools.partial(_dispatch_kernel, tile=tile, n_tok=t, n_blocks=n_rows // ROW_BLOCK)
    grid_spec = pltpu.PrefetchScalarGridSpec(
        num_scalar_prefetch=3, grid=(t // tile,),
        in_specs=[pl.BlockSpec((tile * ROW_TILE, LANES), lambda i, *_: (i, 0))],
        out_specs=pl.BlockSpec(memory_space=pl.ANY),
        scratch_shapes=[pltpu.VMEM((PAD_SIZES[0] * ROW_TILE, LANES), F32),
                        pltpu.SemaphoreType.DMA(()), pltpu.SemaphoreType.DMA(())])
    return pl.pallas_call(
        kern, name="dispatch", grid_spec=grid_spec,
        out_shape=jax.ShapeDtypeStruct((n_rows * ROW_TILE, LANES), F32),
        compiler_params=_params(("arbitrary",), 32),
    )(dest_flat, pad_start, pad_len, h2)


def _expert_kernel(be_ref, nu_ref, first_ref, next_ref, slot_ref, rows_ref, wg_hbm, wu_hbm, wd_hbm, out_ref,
                   wg_f, wu_f, wd_f, wg_b, wu_b, wd_b, sems):
    i = pl.program_id(0)

    def fetch(e, slot):
        return (pltpu.make_async_copy(wg_hbm.at[e], wg_f.at[slot], sems.at[slot, 0]),
                pltpu.make_async_copy(wu_hbm.at[e], wu_f.at[slot], sems.at[slot, 1]),
                pltpu.make_async_copy(wd_hbm.at[e], wd_f.at[slot], sems.at[slot, 2]))

    @pl.when((i == 0) & (nu_ref[0] > 0))
    def _():
        for cp in fetch(be_ref[0], 0):
            cp.start()

    @pl.when((first_ref[i] == 1) & (i < nu_ref[0]))
    def _():
        slot = slot_ref[i]

        @pl.when(next_ref[i] >= 0)
        def _():
            for cp in fetch(next_ref[i], 1 - slot):
                cp.start()

        for cp in fetch(be_ref[i], slot):
            cp.wait()
        wg_b[...] = wg_f[slot].astype(BF16)
        wu_b[...] = wu_f[slot].astype(BF16)
        wd_b[...] = wd_f[slot].astype(BF16)

    @pl.when(i < nu_ref[0])
    def _():
        xb = _load_tile_rows(rows_ref).astype(BF16)
        gate = jnp.dot(xb, wg_b[...], preferred_element_type=F32)
        up = jnp.dot(xb, wu_b[...], preferred_element_type=F32)
        hid = (gate * jax.nn.sigmoid(gate) * up).astype(BF16)
        _store_tile_rows(out_ref, jnp.dot(hid, wd_b[...], preferred_element_type=F32))

    @pl.when(i >= nu_ref[0])
    def _():
        out_ref[...] = jnp.zeros(out_ref.shape, F32)


def _experts(block_e, n_used, seg_first, seg_next, seg_slot, rows, w_gate, w_up, w_down):
    nb = rows.shape[0] // (ROW_BLOCK * ROW_TILE)
    last = lambda i, be, nu, *_: jnp.maximum(jnp.minimum(i, nu[0] - 1), 0)
    hbm = pl.BlockSpec(memory_space=pl.ANY)
    grid_spec = pltpu.PrefetchScalarGridSpec(
        num_scalar_prefetch=5, grid=(nb,),
        in_specs=[pl.BlockSpec((ROW_BLOCK * ROW_TILE, LANES), lambda i, *s: (last(i, *s), 0)), hbm, hbm, hbm],
        out_specs=pl.BlockSpec((ROW_BLOCK * ROW_TILE, LANES), lambda i, *s: (i, 0)),
        scratch_shapes=[pltpu.VMEM((2, D_MODEL, EXPERT_FF), F32), pltpu.VMEM((2, D_MODEL, EXPERT_FF), F32),
                        pltpu.VMEM((2, EXPERT_FF, D_MODEL), F32),
                        pltpu.VMEM((D_MODEL, EXPERT_FF), BF16), pltpu.VMEM((D_MODEL, EXPERT_FF), BF16),
                        pltpu.VMEM((EXPERT_FF, D_MODEL), BF16), pltpu.SemaphoreType.DMA((2, 3))])
    return pl.pallas_call(
        _expert_kernel, name="experts", grid_spec=grid_spec,
        out_shape=jax.ShapeDtypeStruct(rows.shape, F32),
        compiler_params=_params(("arbitrary",), 48),
    )(block_e, n_used, seg_first, seg_next, seg_slot, rows, w_gate, w_up, w_down)


def _combine_kernel(dest_ref, x1_ref, info_ref, gfin_ref, rows_ref, o_ref, buf0, buf1, sems, *, tile, n_tok):
    i = pl.program_id(0)
    slot = i % 2

    def gather(step, to_slot):
        base = step * tile
        b0, b1, sem = buf0.at[to_slot], buf1.at[to_slot], sems.at[to_slot]

        def issue(tb, carry):
            for k in range(UNROLL):
                t = tb * UNROLL + k
                _row_copy(rows_ref, dest_ref[base + t], b0, t, sem).start(priority=0)
                _row_copy(rows_ref, dest_ref[n_tok + base + t], b1, t, sem).start(priority=1)
            return carry

        lax.fori_loop(0, tile // UNROLL, issue, 0)

    def drain(of_slot):
        b0, b1, sem = buf0.at[of_slot], buf1.at[of_slot], sems.at[of_slot]

        def wait(tb, carry):
            for _ in range(DRAIN_UNROLL):
                _row_copy(rows_ref, 0, b0, 0, sem).wait()
                _row_copy(rows_ref, 0, b1, 0, sem).wait()
            return carry

        lax.fori_loop(0, tile // DRAIN_UNROLL, wait, 0)

    @pl.when(i == 0)
    def _():
        gather(0, 0)

    drain(slot)

    last = pl.num_programs(0) - 1
    nxt_base = jnp.minimum(i + 1, last) * tile
    b0, b1 = buf0.at[slot], buf1.at[slot]
    n0, n1, nsem = buf0.at[1 - slot], buf1.at[1 - slot], sems.at[1 - slot]
    group = tile // COMBINE_GROUPS
    for g in range(COMBINE_GROUPS):
        lo = g * group
        for t in range(lo, lo + group):
            _row_copy(rows_ref, dest_ref[nxt_base + t], n0, t, nsem).start(priority=0)
            _row_copy(rows_ref, dest_ref[n_tok + nxt_base + t], n1, t, nsem).start(priority=1)
        info = info_ref[lo:lo + group, :]
        x2 = (x1_ref[lo:lo + group, :] + info[:, 2:3] * _load_tile_rows(b0, lo, group)
              + info[:, 3:4] * _load_tile_rows(b1, lo, group))
        o_ref[lo:lo + group, :] = _rms(x2, gfin_ref[...])

    @pl.when(i == last)
    def _():
        drain(1 - slot)


def _combine(dest_flat, x1, info, gfin, out_rows, tile=512):
    t = x1.shape[0]
    kern = functools.partial(_combine_kernel, tile=tile, n_tok=t)
    grid_spec = pltpu.PrefetchScalarGridSpec(
        num_scalar_prefetch=1, grid=(t // tile,),
        in_specs=[pl.BlockSpec((tile, D_MODEL), lambda i, d: (i, 0)),
                  pl.BlockSpec((tile, 8), lambda i, d: (i, 0)),
                  pl.BlockSpec((1, D_MODEL), lambda i, d: (0, 0)),
                  pl.BlockSpec(memory_space=pl.ANY)],
        out_specs=pl.BlockSpec((tile, D_MODEL), lambda i, d: (i, 0)),
        scratch_shapes=[pltpu.VMEM((2, tile * ROW_TILE, LANES), F32), pltpu.VMEM((2, tile * ROW_TILE, LANES), F32),
                        pltpu.SemaphoreType.DMA((2,))])
    return pl.pallas_call(
        kern, name="combine", grid_spec=grid_spec,
        out_shape=jax.ShapeDtypeStruct((t, D_MODEL), F32),
        compiler_params=_params(("arbitrary",), 32),
    )(dest_flat, x1, info, gfin, out_rows)


def kernel(x, norm_attn, w_in, lambda_q1, lambda_k1, lambda_q2, lambda_k2, attn_subln, ssm_lam_re, ssm_lam_im, ssm_log_dt, ssm_b_re, ssm_b_im, ssm_c_re, ssm_c_im, ssm_d, w_glu, b_glu, ssm_norm, w_out, norm_moe, w_router_group, b_router_group, w_router_expert, b_router_expert, w_gate, w_up, w_down, norm_final):
    batch, seq, d = x.shape
    t = batch * seq
    nchunk = seq // CHUNK
    x2 = x.reshape(t, d)
    l = 0

    q_t, k_aug, v_t, u3 = _in_proj(x2, norm_attn[l][None], w_in[l], seq)
    attn = _attention(q_t, k_aug, v_t, lambda_q1[l][None], lambda_k1[l][None], lambda_q2[l][None],
                      lambda_k2[l][None], attn_subln[l][:, None], batch, seq)

    sc = _ssm_constants(ssm_lam_re[l], ssm_lam_im[l], ssm_log_dt[l], ssm_b_re[l], ssm_b_im[l],
                        ssm_c_re[l], ssm_c_im[l], ssm_d[l], nchunk)
    h_re, h_im, kt = _ssm_state(u3, sc["b_re"], sc["b_im"], sc["a_row_re"], sc["a_row_im"], sc["p_re"], sc["p_im"],
                                sc["c_re"], sc["c_im"], sc["d_row"], nchunk)
    y3 = _ssm_out(u3, kt, h_re, h_im, sc["c_re"], sc["c_im"], sc["a_col_re"], sc["a_col_im"])

    w_router = jnp.concatenate([w_router_group[l], w_router_expert[l]], axis=1).astype(F32).T
    w_router = jnp.pad(w_router, ((0, LANES - w_router.shape[0]), (0, 0)))
    w_router_hi = w_router.astype(BF16)
    w_router = jnp.concatenate([w_router_hi, (w_router - w_router_hi.astype(F32)).astype(BF16)], axis=0)
    b_router = jnp.concatenate([b_router_group[l], b_router_expert[l]]).astype(F32)
    b_router = jnp.pad(b_router, (0, ROUTER_ROWS - b_router.shape[0]))[:, None]
    x1, h2, info, route_t, cnt = _mix(x2, attn, y3, w_glu[l].astype(BF16), b_glu[l][None], ssm_norm[l][None],
                                      w_out[l][:ATTN_WIDTH].astype(BF16), w_out[l][ATTN_WIDTH:].astype(BF16),
                                      norm_moe[l][None], w_router, b_router)

    experts = route_t[0:2].astype(jnp.int32)
    ranks = route_t[4:6].astype(jnp.int32)
    counts = cnt[N_EXPERT_GROUPS:N_EXPERT_GROUPS + N_EXPERTS, 0].astype(jnp.int32)
    padded = ((counts + ROW_BLOCK - 1) // ROW_BLOCK) * ROW_BLOCK
    ids = jnp.arange(N_EXPERTS, dtype=jnp.int32)
    pend = jnp.sum(jnp.where(ids[None, :] <= ids[:, None], padded[None, :], 0), axis=1)
    pstart = pend - padded
    dest = ranks + jnp.sum(jnp.where(experts[:, None, :] == ids[None, :, None], pstart[None, :, None], 0), axis=1)
    dest = dest.reshape(-1)
    n_rows = ((2 * t + N_EXPERTS * (ROW_BLOCK - 1) + ROW_BLOCK - 1) // ROW_BLOCK) * ROW_BLOCK
    nb = n_rows // ROW_BLOCK
    n_used = (pend[-1] // ROW_BLOCK).astype(jnp.int32)
    blk = jnp.minimum(jnp.arange(nb, dtype=jnp.int32), n_used - 1) * ROW_BLOCK
    block_e = jnp.minimum(jnp.sum((pend[None, :] <= blk[:, None]).astype(jnp.int32), axis=1), N_EXPERTS - 1)

    bidx = jnp.arange(nb, dtype=jnp.int32)
    seg_first = ((bidx == 0) | (block_e != jnp.concatenate([block_e[:1], block_e[:-1]]))).astype(jnp.int32)
    seg_slot = (jnp.sum(jnp.where(bidx[None, :] <= bidx[:, None], seg_first[None, :], 0), axis=1) - 1) & 1
    seg_end = jnp.sum(jnp.where(block_e[:, None] == ids[None, :], pend[None, :], 0), axis=1) // ROW_BLOCK
    after = jnp.sum(jnp.where(bidx[None, :] == seg_end[:, None], block_e[None, :], 0), axis=1)
    seg_next = jnp.where(seg_end < n_used, after, -1).astype(jnp.int32)

    rows = _dispatch(dest, pstart + counts, padded - counts, h2, n_rows)
    out_rows = _experts(block_e, n_used[None], seg_first, seg_next, seg_slot.astype(jnp.int32), rows,
                        w_gate[l], w_up[l], w_down[l])
    out = _combine(dest, x1, info, norm_final[None], out_rows)
    return out.reshape(batch, seq, d)
```
